```python
import math
import jax, jax.numpy as jnp
from jax import lax
import numpy as np

D_MODEL = 1024
BATCH = 8
SEQ = 8192
DEPTH = 4

CHUNK = 64
N_META = 16
N_A = DEPTH // 2
N_B = DEPTH - N_A
D_FF = 2816
POOL_WINDOWS = (2, 4, 8, 16)
N_POOL_GROUPS = len(POOL_WINDOWS)
POOL_GROUP = D_MODEL // N_POOL_GROUPS
N_HEADS = 8
QK_NOPE = 64
QK_ROPE = 32
V_HEAD = 64
KV_RANK = 256
Q_RANK = 384
ROPE_THETA = 10000.0
Q_BLOCK = 128
EPS = 1e-6

kernel_name = "yoco_pool_mla_macaron_trunk"


def rmsnorm(x, g):
    xf = x.astype(jnp.float32)
    xf = xf * lax.rsqrt(jnp.mean(xf * xf, axis=-1, keepdims=True) + EPS)
    return xf.astype(x.dtype) * g


def swiglu(h, w_gate, w_up, w_down):
    return (jax.nn.silu(h @ w_gate) * (h @ w_up)) @ w_down


def chunk_ids(n):
    pos = jnp.arange(n)
    return jnp.where(pos < N_META, 0, (pos - N_META) // CHUNK + 1)


def rope_tables(n):
    inv = 1.0 / (ROPE_THETA ** (jnp.arange(0, QK_ROPE, 2, dtype=jnp.float32) / QK_ROPE))
    ang = jnp.arange(n, dtype=jnp.float32)[:, None] * inv[None, :]
    return jnp.cos(ang), jnp.sin(ang)


def apply_rope(x, cos, sin):
    xf = x.astype(jnp.float32)
    x1, x2 = xf[..., : QK_ROPE // 2], xf[..., QK_ROPE // 2:]
    out = jnp.concatenate([x1 * cos - x2 * sin, x2 * cos + x1 * sin], axis=-1)
    return out.astype(x.dtype)


def pool_mixer(h, w_group, scale):
    L = h.shape[1]
    hf = h.astype(jnp.float32)
    cs = jnp.concatenate([jnp.zeros_like(hf[:, :1]), jnp.cumsum(hf, axis=1)], axis=1)
    hi = jnp.arange(1, L + 1)
    outs = []
    for g, w in enumerate(POOL_WINDOWS):
        sl = slice(g * POOL_GROUP, (g + 1) * POOL_GROUP)
        lo = jnp.maximum(hi - w, 0)
        c = cs[..., sl]
        count = (hi - lo).astype(jnp.float32)[None, :, None]
        mean = (jnp.take(c, hi, axis=1) - jnp.take(c, lo, axis=1)) / count
        outs.append(mean - hf[..., sl])
    pooled = jnp.stack(outs, axis=2).astype(h.dtype)
    y = jnp.einsum('blgc,gcd->blgd', pooled, w_group)
    return y.reshape(h.shape) * scale


def mla_shared_kv(h, w_dkv, kv_latent_norm, w_uk, w_uv, cos, sin):
    B, L, _ = h.shape
    ckr = h @ w_dkv
    c_kv = rmsnorm(ckr[..., :KV_RANK], kv_latent_norm)
    k_rope = apply_rope(ckr[..., KV_RANK:], cos, sin)
    k_nope = (c_kv @ w_uk).reshape(B, L, N_HEADS, QK_NOPE)
    v = (c_kv @ w_uv).reshape(B, L, N_HEADS, V_HEAD)
    return k_nope, k_rope, v


def mla_attention(h, w_dq, q_latent_norm, w_uq, w_o, k_nope, k_rope, v, cos, sin):
    B, L, _ = h.shape
    cq = rmsnorm(h @ w_dq, q_latent_norm)
    q = (cq @ w_uq).reshape(B, L, N_HEADS, QK_NOPE + QK_ROPE)
    q_nope = q[..., :QK_NOPE]
    q_rope = apply_rope(q[..., QK_NOPE:], cos[:, None, :], sin[:, None, :])
    n_blk = -(-L // Q_BLOCK)
    Lp = n_blk * Q_BLOCK
    pad = ((0, 0), (0, Lp - L), (0, 0), (0, 0))
    qn_b = jnp.pad(q_nope, pad).reshape(B, n_blk, Q_BLOCK, N_HEADS, QK_NOPE).transpose(1, 0, 2, 3, 4)
    qr_b = jnp.pad(q_rope, pad).reshape(B, n_blk, Q_BLOCK, N_HEADS, QK_ROPE).transpose(1, 0, 2, 3, 4)
    qid_b = chunk_ids(Lp).reshape(n_blk, Q_BLOCK)
    kid = chunk_ids(L)
    sm_scale = 1.0 / math.sqrt(QK_NOPE + QK_ROPE)

    def block(args):
        qn, qr, qid = args
        s = (jnp.einsum('bqhd,bkhd->bhqk', qn, k_nope)
             + jnp.einsum('bqhr,bkr->bhqk', qr, k_rope)).astype(jnp.float32) * sm_scale
        mask = kid[None, :] <= qid[:, None]
        s = jnp.where(mask[None, None], s, jnp.finfo(jnp.float32).min)
        p = jax.nn.softmax(s, axis=-1).astype(v.dtype)
        return jnp.einsum('bhqk,bkhd->bqhd', p, v)

    o = lax.map(block, (qn_b, qr_b, qid_b))
    o = o.transpose(1, 0, 2, 3, 4).reshape(B, Lp, N_HEADS * V_HEAD)[:, :L]
    return o @ w_o


def _fwd_setup_inputs(seed: int = 0) -> dict:
    key = jax.random.key(seed)
    ks = iter(jax.random.split(key, 40))
    f32 = jnp.float32

    def nrm(shape, fan_in):
        return jax.random.normal(next(ks), shape, f32) * (fan_in ** -0.5)

    def gain(shape):
        return 1.0 + 0.1 * jax.random.normal(next(ks), shape, f32)

    return {
        "x": jax.random.normal(next(ks), (BATCH, SEQ, D_MODEL), f32),
        "meta_tokens": jax.random.normal(next(ks), (N_META, D_MODEL), f32),
        "ffn1_norm": gain((DEPTH, D_MODEL)),
        "ffn1_w_gate": nrm((DEPTH, D_MODEL, D_FF), D_MODEL),
        "ffn1_w_up": nrm((DEPTH, D_MODEL, D_FF), D_MODEL),
        "ffn1_w_down": nrm((DEPTH, D_FF, D_MODEL), D_FF),
        "mix_norm": gain((DEPTH, D_MODEL)),
        "ffn2_norm": gain((DEPTH, D_MODEL)),
        "ffn2_w_gate": nrm((DEPTH, D_MODEL, D_FF), D_MODEL),
        "ffn2_w_up": nrm((DEPTH, D_MODEL, D_FF), D_MODEL),
        "ffn2_w_down": nrm((DEPTH, D_FF, D_MODEL), D_FF),
        "pool_w": nrm((N_A, N_POOL_GROUPS, POOL_GROUP, POOL_GROUP), POOL_GROUP),
        "pool_scale": gain((N_A, D_MODEL)),
        "kv_in_norm": gain((D_MODEL,)),
        "w_dkv": nrm((D_MODEL, KV_RANK + QK_ROPE), D_MODEL),
        "kv_latent_norm": gain((KV_RANK,)),
        "w_uk": nrm((KV_RANK, N_HEADS * QK_NOPE), KV_RANK),
        "w_uv": nrm((KV_RANK, N_HEADS * V_HEAD), KV_RANK),
        "w_dq": nrm((N_B, D_MODEL, Q_RANK), D_MODEL),
        "q_latent_norm": gain((N_B, Q_RANK)),
        "w_uq": nrm((N_B, Q_RANK, N_HEADS * (QK_NOPE + QK_ROPE)), Q_RANK),
        "w_o": nrm((N_B, N_HEADS * V_HEAD, D_MODEL), N_HEADS * V_HEAD),
        "final_norm": gain((D_MODEL,)),
    }


def _fwd_reference(x, meta_tokens, ffn1_norm, ffn1_w_gate, ffn1_w_up, ffn1_w_down, mix_norm,
              ffn2_norm, ffn2_w_gate, ffn2_w_up, ffn2_w_down, pool_w, pool_scale,
              kv_in_norm, w_dkv, kv_latent_norm, w_uk, w_uv, w_dq, q_latent_norm, w_uq, w_o,
              final_norm):
    B = x.shape[0]
    meta = jnp.broadcast_to(meta_tokens[None].astype(x.dtype), (B, N_META, D_MODEL))
    h = jnp.concatenate([meta, x], axis=1)
    L = h.shape[1]
    cos, sin = rope_tables(L)
    shared = None
    for l in range(DEPTH):
        h = h + 0.5 * swiglu(rmsnorm(h, ffn1_norm[l]), ffn1_w_gate[l], ffn1_w_up[l], ffn1_w_down[l])
        u = rmsnorm(h, mix_norm[l])
        if l < N_A:
            h = h + pool_mixer(u, pool_w[l], pool_scale[l])
        else:
            j = l - N_A
            k_nope, k_rope, v = shared
            h = h + mla_attention(u, w_dq[j], q_latent_norm[j], w_uq[j], w_o[j],
                                  k_nope, k_rope, v, cos, sin)
        h = h + 0.5 * swiglu(rmsnorm(h, ffn2_norm[l]), ffn2_w_gate[l], ffn2_w_up[l], ffn2_w_down[l])
        if l == N_A - 1:
            shared = mla_shared_kv(rmsnorm(h, kv_in_norm), w_dkv, kv_latent_norm, w_uk, w_uv, cos, sin)
    out = rmsnorm(h, final_norm)
    return out[:, N_META:]


import jax as _jax
import jax.numpy as _jnp

TWIN_FORMAT = 'train_step'
FWD_PARAMS = ['x', 'meta_tokens', 'ffn1_norm', 'ffn1_w_gate', 'ffn1_w_up', 'ffn1_w_down', 'mix_norm', 'ffn2_norm', 'ffn2_w_gate', 'ffn2_w_up', 'ffn2_w_down', 'pool_w', 'pool_scale', 'kv_in_norm', 'w_dkv', 'kv_latent_norm', 'w_uk', 'w_uv', 'w_dq', 'q_latent_norm', 'w_uq', 'w_o', 'final_norm']
TWIN_WEIGHTS = ['meta_tokens', 'ffn1_norm', 'ffn1_w_gate', 'ffn1_w_up', 'ffn1_w_down', 'mix_norm', 'ffn2_norm', 'ffn2_w_gate', 'ffn2_w_up', 'ffn2_w_down', 'pool_w', 'pool_scale', 'kv_in_norm', 'w_dkv', 'kv_latent_norm', 'w_uk', 'w_uv', 'w_dq', 'q_latent_norm', 'w_uq', 'w_o', 'final_norm']
TWIN_DIFF_INPUT = 'x'
TWIN_INPUTS = ['x', 'meta_tokens', 'ffn1_norm', 'ffn1_w_gate', 'ffn1_w_up', 'ffn1_w_down', 'mix_norm', 'ffn2_norm', 'ffn2_w_gate', 'ffn2_w_up', 'ffn2_w_down', 'pool_w', 'pool_scale', 'kv_in_norm', 'w_dkv', 'kv_latent_norm', 'w_uk', 'w_uv', 'w_dq', 'q_latent_norm', 'w_uq', 'w_o', 'final_norm', 'loss_target', 'm_meta_tokens', 'm_ffn1_norm', 'm_ffn1_w_gate', 'm_ffn1_w_up', 'm_ffn1_w_down', 'm_mix_norm', 'm_ffn2_norm', 'm_ffn2_w_gate', 'm_ffn2_w_up', 'm_ffn2_w_down', 'm_pool_w', 'm_pool_scale', 'm_kv_in_norm', 'm_w_dkv', 'm_kv_latent_norm', 'm_w_uk', 'm_w_uv', 'm_w_dq', 'm_q_latent_norm', 'm_w_uq', 'm_w_o', 'm_final_norm', 'v_meta_tokens', 'v_ffn1_norm', 'v_ffn1_w_gate', 'v_ffn1_w_up', 'v_ffn1_w_down', 'v_mix_norm', 'v_ffn2_norm', 'v_ffn2_w_gate', 'v_ffn2_w_up', 'v_ffn2_w_down', 'v_pool_w', 'v_pool_scale', 'v_kv_in_norm', 'v_w_dkv', 'v_kv_latent_norm', 'v_w_uk', 'v_w_uv', 'v_w_dq', 'v_q_latent_norm', 'v_w_uq', 'v_w_o', 'v_final_norm']
TWIN_OUTPUTS = ['loss', 'grad_x', 'grad_meta_tokens', 'grad_ffn1_norm', 'grad_ffn1_w_gate', 'grad_ffn1_w_up', 'grad_ffn1_w_down', 'grad_mix_norm', 'grad_ffn2_norm', 'grad_ffn2_w_gate', 'grad_ffn2_w_up', 'grad_ffn2_w_down', 'grad_pool_w', 'grad_pool_scale', 'grad_kv_in_norm', 'grad_w_dkv', 'grad_kv_latent_norm', 'grad_w_uk', 'grad_w_uv', 'grad_w_dq', 'grad_q_latent_norm', 'grad_w_uq', 'grad_w_o', 'grad_final_norm', 'delta_meta_tokens', 'delta_ffn1_norm', 'delta_ffn1_w_gate', 'delta_ffn1_w_up', 'delta_ffn1_w_down', 'delta_mix_norm', 'delta_ffn2_norm', 'delta_ffn2_w_gate', 'delta_ffn2_w_up', 'delta_ffn2_w_down', 'delta_pool_w', 'delta_pool_scale', 'delta_kv_in_norm', 'delta_w_dkv', 'delta_kv_latent_norm', 'delta_w_uk', 'delta_w_uv', 'delta_w_dq', 'delta_q_latent_norm', 'delta_w_uq', 'delta_w_o', 'delta_final_norm', 'new_m_meta_tokens', 'new_m_ffn1_norm', 'new_m_ffn1_w_gate', 'new_m_ffn1_w_up', 'new_m_ffn1_w_down', 'new_m_mix_norm', 'new_m_ffn2_norm', 'new_m_ffn2_w_gate', 'new_m_ffn2_w_up', 'new_m_ffn2_w_down', 'new_m_pool_w', 'new_m_pool_scale', 'new_m_kv_in_norm', 'new_m_w_dkv', 'new_m_kv_latent_norm', 'new_m_w_uk', 'new_m_w_uv', 'new_m_w_dq', 'new_m_q_latent_norm', 'new_m_w_uq', 'new_m_w_o', 'new_m_final_norm', 'new_v_meta_tokens', 'new_v_ffn1_norm', 'new_v_ffn1_w_gate', 'new_v_ffn1_w_up', 'new_v_ffn1_w_down', 'new_v_mix_norm', 'new_v_ffn2_norm', 'new_v_ffn2_w_gate', 'new_v_ffn2_w_up', 'new_v_ffn2_w_down', 'new_v_pool_w', 'new_v_pool_scale', 'new_v_kv_in_norm', 'new_v_w_dkv', 'new_v_kv_latent_norm', 'new_v_w_uk', 'new_v_w_uv', 'new_v_w_dq', 'new_v_q_latent_norm', 'new_v_w_uq', 'new_v_w_o', 'new_v_final_norm']
TWIN_LEAF_KINDS = {'loss': 'loss', 'grad_x': 'grad_x', 'grad_meta_tokens': 'grad_w', 'grad_ffn1_norm': 'grad_w', 'grad_ffn1_w_gate': 'grad_w', 'grad_ffn1_w_up': 'grad_w', 'grad_ffn1_w_down': 'grad_w', 'grad_mix_norm': 'grad_w', 'grad_ffn2_norm': 'grad_w', 'grad_ffn2_w_gate': 'grad_w', 'grad_ffn2_w_up': 'grad_w', 'grad_ffn2_w_down': 'grad_w', 'grad_pool_w': 'grad_w', 'grad_pool_scale': 'grad_w', 'grad_kv_in_norm': 'grad_w', 'grad_w_dkv': 'grad_w', 'grad_kv_latent_norm': 'grad_w', 'grad_w_uk': 'grad_w', 'grad_w_uv': 'grad_w', 'grad_w_dq': 'grad_w', 'grad_q_latent_norm': 'grad_w', 'grad_w_uq': 'grad_w', 'grad_w_o': 'grad_w', 'grad_final_norm': 'grad_w', 'delta_meta_tokens': 'delta_w', 'delta_ffn1_norm': 'delta_w', 'delta_ffn1_w_gate': 'delta_w', 'delta_ffn1_w_up': 'delta_w', 'delta_ffn1_w_down': 'delta_w', 'delta_mix_norm': 'delta_w', 'delta_ffn2_norm': 'delta_w', 'delta_ffn2_w_gate': 'delta_w', 'delta_ffn2_w_up': 'delta_w', 'delta_ffn2_w_down': 'delta_w', 'delta_pool_w': 'delta_w', 'delta_pool_scale': 'delta_w', 'delta_kv_in_norm': 'delta_w', 'delta_w_dkv': 'delta_w', 'delta_kv_latent_norm': 'delta_w', 'delta_w_uk': 'delta_w', 'delta_w_uv': 'delta_w', 'delta_w_dq': 'delta_w', 'delta_q_latent_norm': 'delta_w', 'delta_w_uq': 'delta_w', 'delta_w_o': 'delta_w', 'delta_final_norm': 'delta_w', 'new_m_meta_tokens': 'new_m', 'new_m_ffn1_norm': 'new_m', 'new_m_ffn1_w_gate': 'new_m', 'new_m_ffn1_w_up': 'new_m', 'new_m_ffn1_w_down': 'new_m', 'new_m_mix_norm': 'new_m', 'new_m_ffn2_norm': 'new_m', 'new_m_ffn2_w_gate': 'new_m', 'new_m_ffn2_w_up': 'new_m', 'new_m_ffn2_w_down': 'new_m', 'new_m_pool_w': 'new_m', 'new_m_pool_scale': 'new_m', 'new_m_kv_in_norm': 'new_m', 'new_m_w_dkv': 'new_m', 'new_m_kv_latent_norm': 'new_m', 'new_m_w_uk': 'new_m', 'new_m_w_uv': 'new_m', 'new_m_w_dq': 'new_m', 'new_m_q_latent_norm': 'new_m', 'new_m_w_uq': 'new_m', 'new_m_w_o': 'new_m', 'new_m_final_norm': 'new_m', 'new_v_meta_tokens': 'new_v', 'new_v_ffn1_norm': 'new_v', 'new_v_ffn1_w_gate': 'new_v', 'new_v_ffn1_w_up': 'new_v', 'new_v_ffn1_w_down': 'new_v', 'new_v_mix_norm': 'new_v', 'new_v_ffn2_norm': 'new_v', 'new_v_ffn2_w_gate': 'new_v', 'new_v_ffn2_w_up': 'new_v', 'new_v_ffn2_w_down': 'new_v', 'new_v_pool_w': 'new_v', 'new_v_pool_scale': 'new_v', 'new_v_kv_in_norm': 'new_v', 'new_v_w_dkv': 'new_v', 'new_v_kv_latent_norm': 'new_v', 'new_v_w_uk': 'new_v', 'new_v_w_uv': 'new_v', 'new_v_w_dq': 'new_v', 'new_v_q_latent_norm': 'new_v', 'new_v_w_uq': 'new_v', 'new_v_w_o': 'new_v', 'new_v_final_norm': 'new_v'}


def _forward(args):
    return _fwd_reference(*[args[k] for k in FWD_PARAMS])


def _output_shape():
    def fwd():
        inp = _fwd_setup_inputs(0)
        return _fwd_reference(*[inp[k] for k in FWD_PARAMS])
    out = _jax.eval_shape(fwd)
    return out.shape, out.dtype

N_MICROBATCH = 1
ADAM_LR = 0.001
ADAM_B1 = 0.9
ADAM_B2 = 0.999
ADAM_EPS = 1e-08
ADAM_WD = 0.01
ADAM_STEP = 10
PER_EXAMPLE_BATCH_AXIS = {'x': 0, 'loss_target': 0}
SHARED_INPUTS = []
_WEIGHT_DTYPES = {'meta_tokens': _jnp.float32, 'ffn1_norm': _jnp.float32, 'ffn1_w_gate': _jnp.float32, 'ffn1_w_up': _jnp.float32, 'ffn1_w_down': _jnp.float32, 'mix_norm': _jnp.float32, 'ffn2_norm': _jnp.float32, 'ffn2_w_gate': _jnp.float32, 'ffn2_w_up': _jnp.float32, 'ffn2_w_down': _jnp.float32, 'pool_w': _jnp.float32, 'pool_scale': _jnp.float32, 'kv_in_norm': _jnp.float32, 'w_dkv': _jnp.float32, 'kv_latent_norm': _jnp.float32, 'w_uk': _jnp.float32, 'w_uv': _jnp.float32, 'w_dq': _jnp.float32, 'q_latent_norm': _jnp.float32, 'w_uq': _jnp.float32, 'w_o': _jnp.float32, 'final_norm': _jnp.float32}
MOMENT_SCALE = {'meta_tokens': 5.188471e-03, 'ffn1_norm': 8.977266e-02, 'ffn1_w_gate': 3.772945e-02, 'ffn1_w_up': 3.744343e-02, 'ffn1_w_down': 6.229284e-02, 'mix_norm': 2.829554e-01, 'ffn2_norm': 7.313034e-02, 'ffn2_w_gate': 3.089345e-02, 'ffn2_w_up': 3.117236e-02, 'ffn2_w_down': 5.187241e-02, 'pool_w': 3.143404e-01, 'pool_scale': 3.166964e+00, 'kv_in_norm': 4.021225e-02, 'w_dkv': 7.572070e-02, 'kv_latent_norm': 8.483258e-02, 'w_uk': 3.749887e-02, 'w_uv': 4.203383e-02, 'w_dq': 3.631007e-02, 'q_latent_norm': 3.472213e-02, 'w_uq': 2.556519e-02, 'w_o': 2.095645e-02, 'final_norm': 6.513117e+01}


def _to_microbatches(a, axis):
    t = _jnp.moveaxis(a, axis, 0)
    t = t.reshape((N_MICROBATCH, t.shape[0] // N_MICROBATCH) + t.shape[1:])
    return _jnp.moveaxis(t, 1, axis + 1)


def setup_inputs(seed: int = 0) -> dict:
    inp = _fwd_setup_inputs(seed)
    key = _jax.random.fold_in(_jax.random.key(seed), 7919)
    shape, _ = _output_shape()
    out = dict(inp)
    out["loss_target"] = _jax.random.normal(_jax.random.fold_in(key, 0), shape, _jnp.float32)
    for i, name in enumerate(TWIN_WEIGHTS):
        w = inp[name].astype(_jnp.float32)
        if MOMENT_SCALE is None:
            s = _jnp.sqrt(_jnp.mean(_jnp.square(w)) + 1e-30)
        else:
            s = MOMENT_SCALE[name]
        km, kv = _jax.random.split(_jax.random.fold_in(key, i + 1))
        out[name] = w
        out["m_" + name] = s * _jax.random.normal(km, w.shape, _jnp.float32)
        out["v_" + name] = (s * s) * _jax.random.uniform(kv, w.shape, _jnp.float32, 0.5, 1.5)
    if N_MICROBATCH > 1:
        for name, axis in PER_EXAMPLE_BATCH_AXIS.items():
            out[name] = _to_microbatches(out[name], axis)
    return {'x': out['x'], 'meta_tokens': out['meta_tokens'], 'ffn1_norm': out['ffn1_norm'], 'ffn1_w_gate': out['ffn1_w_gate'], 'ffn1_w_up': out['ffn1_w_up'], 'ffn1_w_down': out['ffn1_w_down'], 'mix_norm': out['mix_norm'], 'ffn2_norm': out['ffn2_norm'], 'ffn2_w_gate': out['ffn2_w_gate'], 'ffn2_w_up': out['ffn2_w_up'], 'ffn2_w_down': out['ffn2_w_down'], 'pool_w': out['pool_w'], 'pool_scale': out['pool_scale'], 'kv_in_norm': out['kv_in_norm'], 'w_dkv': out['w_dkv'], 'kv_latent_norm': out['kv_latent_norm'], 'w_uk': out['w_uk'], 'w_uv': out['w_uv'], 'w_dq': out['w_dq'], 'q_latent_norm': out['q_latent_norm'], 'w_uq': out['w_uq'], 'w_o': out['w_o'], 'final_norm': out['final_norm'], 'loss_target': out['loss_target'], 'm_meta_tokens': out['m_meta_tokens'], 'm_ffn1_norm': out['m_ffn1_norm'], 'm_ffn1_w_gate': out['m_ffn1_w_gate'], 'm_ffn1_w_up': out['m_ffn1_w_up'], 'm_ffn1_w_down': out['m_ffn1_w_down'], 'm_mix_norm': out['m_mix_norm'], 'm_ffn2_norm': out['m_ffn2_norm'], 'm_ffn2_w_gate': out['m_ffn2_w_gate'], 'm_ffn2_w_up': out['m_ffn2_w_up'], 'm_ffn2_w_down': out['m_ffn2_w_down'], 'm_pool_w': out['m_pool_w'], 'm_pool_scale': out['m_pool_scale'], 'm_kv_in_norm': out['m_kv_in_norm'], 'm_w_dkv': out['m_w_dkv'], 'm_kv_latent_norm': out['m_kv_latent_norm'], 'm_w_uk': out['m_w_uk'], 'm_w_uv': out['m_w_uv'], 'm_w_dq': out['m_w_dq'], 'm_q_latent_norm': out['m_q_latent_norm'], 'm_w_uq': out['m_w_uq'], 'm_w_o': out['m_w_o'], 'm_final_norm': out['m_final_norm'], 'v_meta_tokens': out['v_meta_tokens'], 'v_ffn1_norm': out['v_ffn1_norm'], 'v_ffn1_w_gate': out['v_ffn1_w_gate'], 'v_ffn1_w_up': out['v_ffn1_w_up'], 'v_ffn1_w_down': out['v_ffn1_w_down'], 'v_mix_norm': out['v_mix_norm'], 'v_ffn2_norm': out['v_ffn2_norm'], 'v_ffn2_w_gate': out['v_ffn2_w_gate'], 'v_ffn2_w_up': out['v_ffn2_w_up'], 'v_ffn2_w_down': out['v_ffn2_w_down'], 'v_pool_w': out['v_pool_w'], 'v_pool_scale': out['v_pool_scale'], 'v_kv_in_norm': out['v_kv_in_norm'], 'v_w_dkv': out['v_w_dkv'], 'v_kv_latent_norm': out['v_kv_latent_norm'], 'v_w_uk': out['v_w_uk'], 'v_w_uv': out['v_w_uv'], 'v_w_dq': out['v_w_dq'], 'v_q_latent_norm': out['v_q_latent_norm'], 'v_w_uq': out['v_w_uq'], 'v_w_o': out['v_w_o'], 'v_final_norm': out['v_final_norm']}


def _loss(weights, diff, rest, loss_target):
    with _jax.named_scope("forward"):
        args = {**rest, TWIN_DIFF_INPUT: diff, **{k: w.astype(_WEIGHT_DTYPES[k]) for k, w in weights.items()}}
        y = _forward(args)
    with _jax.named_scope("loss_head"):
        err = _jnp.square(y.astype(_jnp.float32) - loss_target)
        return 0.5 * _jnp.sum(_jnp.mean(err, axis=-1)) if err.ndim else 0.5 * err


def _adamw(w, g, m, v):
    m = ADAM_B1 * m + (1.0 - ADAM_B1) * g
    v = ADAM_B2 * v + (1.0 - ADAM_B2) * _jnp.square(g)
    m_hat = m / (1.0 - ADAM_B1 ** ADAM_STEP)
    v_hat = v / (1.0 - ADAM_B2 ** ADAM_STEP)
    delta = -ADAM_LR * (m_hat / (_jnp.sqrt(v_hat) + ADAM_EPS) + ADAM_WD * w)
    return delta, m, v


def reference(x, meta_tokens, ffn1_norm, ffn1_w_gate, ffn1_w_up, ffn1_w_down, mix_norm, ffn2_norm, ffn2_w_gate, ffn2_w_up, ffn2_w_down, pool_w, pool_scale, kv_in_norm, w_dkv, kv_latent_norm, w_uk, w_uv, w_dq, q_latent_norm, w_uq, w_o, final_norm, loss_target, m_meta_tokens, m_ffn1_norm, m_ffn1_w_gate, m_ffn1_w_up, m_ffn1_w_down, m_mix_norm, m_ffn2_norm, m_ffn2_w_gate, m_ffn2_w_up, m_ffn2_w_down, m_pool_w, m_pool_scale, m_kv_in_norm, m_w_dkv, m_kv_latent_norm, m_w_uk, m_w_uv, m_w_dq, m_q_latent_norm, m_w_uq, m_w_o, m_final_norm, v_meta_tokens, v_ffn1_norm, v_ffn1_w_gate, v_ffn1_w_up, v_ffn1_w_down, v_mix_norm, v_ffn2_norm, v_ffn2_w_gate, v_ffn2_w_up, v_ffn2_w_down, v_pool_w, v_pool_scale, v_kv_in_norm, v_w_dkv, v_kv_latent_norm, v_w_uk, v_w_uv, v_w_dq, v_q_latent_norm, v_w_uq, v_w_o, v_final_norm):
    given = dict(x=x, meta_tokens=meta_tokens, ffn1_norm=ffn1_norm, ffn1_w_gate=ffn1_w_gate, ffn1_w_up=ffn1_w_up, ffn1_w_down=ffn1_w_down, mix_norm=mix_norm, ffn2_norm=ffn2_norm, ffn2_w_gate=ffn2_w_gate, ffn2_w_up=ffn2_w_up, ffn2_w_down=ffn2_w_down, pool_w=pool_w, pool_scale=pool_scale, kv_in_norm=kv_in_norm, w_dkv=w_dkv, kv_latent_norm=kv_latent_norm, w_uk=w_uk, w_uv=w_uv, w_dq=w_dq, q_latent_norm=q_latent_norm, w_uq=w_uq, w_o=w_o, final_norm=final_norm, loss_target=loss_target, m_meta_tokens=m_meta_tokens, m_ffn1_norm=m_ffn1_norm, m_ffn1_w_gate=m_ffn1_w_gate, m_ffn1_w_up=m_ffn1_w_up, m_ffn1_w_down=m_ffn1_w_down, m_mix_norm=m_mix_norm, m_ffn2_norm=m_ffn2_norm, m_ffn2_w_gate=m_ffn2_w_gate, m_ffn2_w_up=m_ffn2_w_up, m_ffn2_w_down=m_ffn2_w_down, m_pool_w=m_pool_w, m_pool_scale=m_pool_scale, m_kv_in_norm=m_kv_in_norm, m_w_dkv=m_w_dkv, m_kv_latent_norm=m_kv_latent_norm, m_w_uk=m_w_uk, m_w_uv=m_w_uv, m_w_dq=m_w_dq, m_q_latent_norm=m_q_latent_norm, m_w_uq=m_w_uq, m_w_o=m_w_o, m_final_norm=m_final_norm, v_meta_tokens=v_meta_tokens, v_ffn1_norm=v_ffn1_norm, v_ffn1_w_gate=v_ffn1_w_gate, v_ffn1_w_up=v_ffn1_w_up, v_ffn1_w_down=v_ffn1_w_down, v_mix_norm=v_mix_norm, v_ffn2_norm=v_ffn2_norm, v_ffn2_w_gate=v_ffn2_w_gate, v_ffn2_w_up=v_ffn2_w_up, v_ffn2_w_down=v_ffn2_w_down, v_pool_w=v_pool_w, v_pool_scale=v_pool_scale, v_kv_in_norm=v_kv_in_norm, v_w_dkv=v_w_dkv, v_kv_latent_norm=v_kv_latent_norm, v_w_uk=v_w_uk, v_w_uv=v_w_uv, v_w_dq=v_w_dq, v_q_latent_norm=v_q_latent_norm, v_w_uq=v_w_uq, v_w_o=v_w_o, v_final_norm=v_final_norm)
    weights = {n: given[n] for n in TWIN_WEIGHTS}
    shared = {n: given[n] for n in SHARED_INPUTS}
    per_example = {n: given[n] for n in ['x']}
    grad_fn = _jax.value_and_grad(_loss, argnums=(0, 1))

    def one_microbatch(ex, loss_target):
        ex = dict(ex)
        diff = ex.pop(TWIN_DIFF_INPUT)
        return grad_fn(weights, diff, {**shared, **ex}, loss_target)

    if N_MICROBATCH == 1:
        loss, (grad_w, grad_x) = one_microbatch(per_example, given["loss_target"])
    else:
        def body(carry, xs):
            loss_sum, grad_sum = carry
            l_k, (gw_k, gx_k) = one_microbatch(xs[0], xs[1])
            with _jax.named_scope("update"):
                return (loss_sum + l_k, _jax.tree.map(_jnp.add, grad_sum, gw_k)), gx_k

        init = (_jnp.zeros((), _jnp.float32), _jax.tree.map(_jnp.zeros_like, weights))
        (loss, grad_w), grad_x = _jax.lax.scan(body, init, (per_example, given["loss_target"]))
    with _jax.named_scope("update"):
        delta_w, new_m, new_v = {}, {}, {}
        for n in TWIN_WEIGHTS:
            delta_w[n], new_m[n], new_v[n] = _adamw(weights[n], grad_w[n], given["m_" + n], given["v_" + n])
    return (loss, grad_x, *[grad_w[n] for n in TWIN_WEIGHTS], *[delta_w[n] for n in TWIN_WEIGHTS],
            *[new_m[n] for n in TWIN_WEIGHTS], *[new_v[n] for n in TWIN_WEIGHTS])
```

```python
import functools
import math

import jax
import jax.numpy as jnp
from jax import lax
from jax.experimental import pallas as pl
from jax.experimental.pallas import tpu as pltpu

F32 = jnp.float32
BF16 = jnp.bfloat16
MESH = pl.DeviceIdType.MESH

N_DEV = 8
N_CHIPS = 4
DEPTH = 4
N_POOL_LAYERS = 2
N_HEADS = 8
QK_NOPE = 64
QK_ROPE = 32
V_HEAD = 64
KV_RANK = 256
Q_RANK = 384
HEAD_LANES = 128
ROPE_LANE0 = QK_NOPE
N_META = 16
CHUNK_SHIFT = 6
FRONT_PAD = 112
SEQ_START = FRONT_PAD + N_META
HALO = 16
POOL_WINDOWS = (2, 4, 8, 16)
EPS = 1e-6
ROPE_THETA = 10000.0
NEG = -1e30
PACK_W = 1024
PACK_ROW_MULT = 256
VMEM_LIMIT = 56 * 1024 * 1024

ADAM_LR = 0.001
ADAM_B1 = 0.9
ADAM_B2 = 0.999
ADAM_EPS = 1e-08
ADAM_WD = 0.01
ADAM_STEP = 10

SHARDED = ["ffn1_w_gate", "ffn1_w_up", "ffn1_w_down", "ffn2_w_gate", "ffn2_w_up", "ffn2_w_down",
           "pool_w", "w_dkv", "w_uk", "w_uv", "w_dq", "w_uq", "w_o", "meta_tokens", "pool_scale"]
SHARDED_F32 = ("meta_tokens", "pool_scale")
REPLICATED = ["ffn1_norm", "mix_norm", "ffn2_norm", "kv_in_norm", "kv_latent_norm", "q_latent_norm",
              "final_norm"]
WEIGHTS = ['meta_tokens', 'ffn1_norm', 'ffn1_w_gate', 'ffn1_w_up', 'ffn1_w_down', 'mix_norm', 'ffn2_norm',
           'ffn2_w_gate', 'ffn2_w_up', 'ffn2_w_down', 'pool_w', 'pool_scale', 'kv_in_norm', 'w_dkv',
           'kv_latent_norm', 'w_uk', 'w_uv', 'w_dq', 'q_latent_norm', 'w_uq', 'w_o', 'final_norm']


def _dot(a, b):
    return jnp.dot(a.astype(BF16), b.astype(BF16), preferred_element_type=F32)


def _dot_nt(a, b):
    return lax.dot_general(a.astype(BF16), b.astype(BF16), (((1,), (1,)), ((), ())),
                           preferred_element_type=F32)


def _dot_tn(a, b):
    return lax.dot_general(a.astype(BF16), b.astype(BF16), (((0,), (0,)), ((), ())),
                           preferred_element_type=F32)


def _sigmoid(x):
    return 1.0 / (1.0 + jnp.exp(-x))


def _rms(x, g):
    r = lax.rsqrt(jnp.mean(x * x, axis=-1, keepdims=True) + EPS)
    xh = x * r
    return xh * g, xh, r


def _rms_bwd(dy, xh, r, g):
    dxh = dy * g
    dx = r * (dxh - xh * jnp.mean(dxh * xh, axis=-1, keepdims=True))
    return dx, jnp.sum(dy * xh, axis=0, keepdims=True)


def _rope(x, c, s1, s2):
    return x * c + pltpu.roll(x, HEAD_LANES - QK_ROPE // 2, 1) * s1 + pltpu.roll(x, QK_ROPE // 2, 1) * s2


def _rope_t(d, c, s1, s2):
    return d * c + pltpu.roll(d * s1, QK_ROPE // 2, 1) + pltpu.roll(d * s2, HEAD_LANES - QK_ROPE // 2, 1)


def _params(*sem):
    return pltpu.CompilerParams(dimension_semantics=sem, vmem_limit_bytes=VMEM_LIMIT)


def _pick(n, candidates):
    for c in candidates:
        if n % c == 0:
            return c
    return n


def _row_tile(L):
    return _pick(L, (640, 128))


def _row_call(name, body, L, row_ins, full_ins, row_outs, acc_outs):
    tm = _row_tile(L)
    n = L // tm
    hb = tm // HALO
    nb = L // HALO
    in_specs, args = [], []
    for arr, kind in row_ins:
        c = arr.shape[1]
        if kind == "tile":
            spec = pl.BlockSpec((tm, c), lambda i: (i, 0))
        elif kind == "prev":
            spec = pl.BlockSpec((HALO, c), lambda i: (jnp.maximum(i * hb - 1, 0), 0))
        else:
            spec = pl.BlockSpec((HALO, c), lambda i: (jnp.minimum((i + 1) * hb, nb - 1), 0))
        in_specs.append(spec)
        args.append(arr)
    for arr in full_ins:
        in_specs.append(pl.BlockSpec(arr.shape, lambda i, nd=arr.ndim: (0,) * nd))
        args.append(arr)
    out_shape = [jax.ShapeDtypeStruct((L, c), dt) for c, dt in row_outs]
    out_specs = [pl.BlockSpec((tm, c), lambda i: (i, 0)) for c, _ in row_outs]
    for shp in acc_outs:
        out_shape.append(jax.ShapeDtypeStruct(shp, F32))
        out_specs.append(pl.BlockSpec(shp, lambda i, nd=len(shp): (0,) * nd))
    n_in, n_ro = len(args), len(row_outs)

    def kern(*refs):
        i = pl.program_id(0)
        vals = [r[...] for r in refs[:n_in]]
        ro, ao = body(i, n, tm, *vals)
        for r, v in zip(refs[n_in:n_in + n_ro], ro):
            r[...] = v.astype(r.dtype)
        acc_refs = refs[n_in + n_ro:]

        @pl.when(i == 0)
        def _():
            for r in acc_refs:
                r[...] = jnp.zeros(r.shape, r.dtype)

        for r, v in zip(acc_refs, ao):
            r[...] += v

    return pl.pallas_call(kern, name=name, grid=(n,), in_specs=in_specs, out_specs=out_specs,
                          out_shape=out_shape, compiler_params=_params("arbitrary"))(*args)


def _ffn_fwd(h, gam, wg, wu, wd, l):
    L, D = h.shape
    F = wg.shape[2]
    tm = _row_tile(L)
    tf = _pick(F, (256, 128))
    nF = F // tf

    def kern(h_ref, gam_ref, wg_ref, wu_ref, wd_ref, ho_ref, xn_ref, gs_ref, us_ref, acc):
        f = pl.program_id(1)

        @pl.when(f == 0)
        def _():
            xn, _, _ = _rms(h_ref[...], gam_ref[...])
            xn_ref[...] = xn.astype(BF16)
            acc[...] = jnp.zeros(acc.shape, F32)

        xnb = xn_ref[...]
        g = _dot(xnb, wg_ref[...])
        u = _dot(xnb, wu_ref[...])
        gs_ref[...] = g.astype(BF16)
        us_ref[...] = u.astype(BF16)
        acc[...] += _dot(g * _sigmoid(g) * u, wd_ref[...])

        @pl.when(f == nF - 1)
        def _():
            ho_ref[...] = h_ref[...] + 0.5 * acc[...]

    return pl.pallas_call(
        kern, name="ffn_fwd", grid=(L // tm, nF),
        in_specs=[pl.BlockSpec((tm, D), lambda i, f: (i, 0)),
                  pl.BlockSpec((None, 1, D), lambda i, f: (l, 0, 0)),
                  pl.BlockSpec((None, D, tf), lambda i, f: (l, 0, f)),
                  pl.BlockSpec((None, D, tf), lambda i, f: (l, 0, f)),
                  pl.BlockSpec((None, tf, D), lambda i, f: (l, f, 0))],
        out_specs=[pl.BlockSpec((tm, D), lambda i, f: (i, 0)),
                   pl.BlockSpec((tm, D), lambda i, f: (i, 0)),
                   pl.BlockSpec((tm, tf), lambda i, f: (i, f)),
                   pl.BlockSpec((tm, tf), lambda i, f: (i, f))],
        out_shape=[jax.ShapeDtypeStruct((L, D), F32), jax.ShapeDtypeStruct((L, D), BF16),
                   jax.ShapeDtypeStruct((L, F), BF16), jax.ShapeDtypeStruct((L, F), BF16)],
        scratch_shapes=[pltpu.VMEM((tm, D), F32)],
        compiler_params=_params("arbitrary", "arbitrary"))(h, gam, wg, wu, wd)


def _ffn_bwd(dh, h, gs, us, gam, wg, wu, wd, l):
    L, D = h.shape
    F = wg.shape[2]
    tm = _row_tile(L)
    tf = _pick(F, (256, 128))
    nF = F // tf

    def kern(dh_ref, h_ref, gs_ref, us_ref, gam_ref, wg_ref, wu_ref, wd_ref,
             dhi_ref, dg_ref, du_ref, a_ref, dob_ref, dgam_ref, dxn):
        i = pl.program_id(0)
        f = pl.program_id(1)

        @pl.when(f == 0)
        def _():
            dxn[...] = jnp.zeros(dxn.shape, F32)
            dob_ref[...] = (0.5 * dh_ref[...]).astype(BF16)

        @pl.when((f == 0) & (i == 0))
        def _():
            dgam_ref[...] = jnp.zeros(dgam_ref.shape, F32)

        g = gs_ref[...].astype(F32)
        u = us_ref[...].astype(F32)
        sg = _sigmoid(g)
        silu = g * sg
        da = _dot_nt(dob_ref[...], wd_ref[...])
        a_ref[...] = (silu * u).astype(BF16)
        dgt = (da * u * (sg * (1.0 + g * (1.0 - sg)))).astype(BF16)
        dut = (da * silu).astype(BF16)
        dg_ref[...] = dgt
        du_ref[...] = dut
        dxn[...] += _dot_nt(dgt, wg_ref[...]) + _dot_nt(dut, wu_ref[...])

        @pl.when(f == nF - 1)
        def _():
            gamma = gam_ref[...]
            _, xh, r = _rms(h_ref[...], gamma)
            dx, dgam = _rms_bwd(dxn[...], xh, r, gamma)
            dhi_ref[...] = dh_ref[...] + dx
            dgam_ref[...] += dgam

    return pl.pallas_call(
        kern, name="ffn_bwd", grid=(L // tm, nF),
        in_specs=[pl.BlockSpec((tm, D), lambda i, f: (i, 0)),
                  pl.BlockSpec((tm, D), lambda i, f: (i, 0)),
                  pl.BlockSpec((tm, tf), lambda i, f: (i, f)),
                  pl.BlockSpec((tm, tf), lambda i, f: (i, f)),
                  pl.BlockSpec((None, 1, D), lambda i, f: (l, 0, 0)),
                  pl.BlockSpec((None, D, tf), lambda i, f: (l, 0, f)),
                  pl.BlockSpec((None, D, tf), lambda i, f: (l, 0, f)),
                  pl.BlockSpec((None, tf, D), lambda i, f: (l, f, 0))],
        out_specs=[pl.BlockSpec((tm, D), lambda i, f: (i, 0)),
                   pl.BlockSpec((tm, tf), lambda i, f: (i, f)),
                   pl.BlockSpec((tm, tf), lambda i, f: (i, f)),
                   pl.BlockSpec((tm, tf), lambda i, f: (i, f)),
                   pl.BlockSpec((tm, D), lambda i, f: (i, 0)),
                   pl.BlockSpec((1, D), lambda i, f: (0, 0))],
        out_shape=[jax.ShapeDtypeStruct((L, D), F32), jax.ShapeDtypeStruct((L, F), BF16),
                   jax.ShapeDtypeStruct((L, F), BF16), jax.ShapeDtypeStruct((L, F), BF16),
                   jax.ShapeDtypeStruct((L, D), BF16), jax.ShapeDtypeStruct((1, D), F32)],
        scratch_shapes=[pltpu.VMEM((tm, D), F32)],
        compiler_params=_params("arbitrary", "arbitrary"))(dh, h, gs, us, gam, wg, wu, wd)


def _mm_tn(a, b, name):
    L, M = a.shape
    N = b.shape[1]
    tm = _pick(M, (1408, 1024, 512))
    tn = _pick(N, (1408, 1024, 512))
    tk = _row_tile(L)

    def kern(a_ref, b_ref, o_ref):
        @pl.when(pl.program_id(2) == 0)
        def _():
            o_ref[...] = jnp.zeros(o_ref.shape, F32)

        o_ref[...] += _dot_tn(a_ref[...], b_ref[...])

    return pl.pallas_call(
        kern, name=name, grid=(M // tm, N // tn, L // tk),
        in_specs=[pl.BlockSpec((tk, tm), lambda i, j, k: (k, i)),
                  pl.BlockSpec((tk, tn), lambda i, j, k: (k, j))],
        out_specs=pl.BlockSpec((tm, tn), lambda i, j, k: (i, j)),
        out_shape=jax.ShapeDtypeStruct((M, N), F32),
        compiler_params=_params("arbitrary", "arbitrary", "arbitrary"))(a, b)


def _pool_counts(pos, w):
    return jnp.clip(pos - (FRONT_PAD - 1), 1, w).astype(F32)


def _pool_forward_values(i, tm, h, hprev, gamma, D):
    cg = D // len(POOL_WINDOWS)
    hext = jnp.concatenate([hprev, h], axis=0)
    uext, xh, r = _rms(hext, gamma)
    pos = i * tm + lax.broadcasted_iota(jnp.int32, (tm, 1), 0)
    pooled = []
    for gi, w in enumerate(POOL_WINDOWS):
        s = uext[:, gi * cg:(gi + 1) * cg]
        span = 1
        while span < w:
            s = s + pltpu.roll(s, span, 0)
            span *= 2
        s = s[HALO:]
        pooled.append(s / _pool_counts(pos, w) - uext[HALO:, gi * cg:(gi + 1) * cg])
    return uext, xh[HALO:], r[HALO:], pooled


def _pool_fwd(h, gam, w, scale, l):
    L, D = h.shape
    cg = D // len(POOL_WINDOWS)

    def body(i, n, tm, ht, hprev, gamma, wv, sc):
        _, _, _, pooled = _pool_forward_values(i, tm, ht, hprev, gamma, D)
        ys = [_dot(pooled[gi], wv[gi]) for gi in range(len(POOL_WINDOWS))]
        y = jnp.concatenate(ys, axis=1) * sc
        return [ht + y], []

    del cg
    return _row_call("pool_fwd", body, L, [(h, "tile"), (h, "prev")], [gam, w, scale], [(D, F32)], [])[0]


def _pool_bwd(dy, h, gam, w, scale):
    L, D = h.shape
    ng = len(POOL_WINDOWS)
    cg = D // ng

    def body(i, n, tm, ht, hprev, dyt, dynext, gamma, wv, sc):
        _, xh, r, pooled = _pool_forward_values(i, tm, ht, hprev, gamma, D)
        dynext = jnp.where(i == n - 1, jnp.zeros_like(dynext), dynext)
        dyext = jnp.concatenate([dyt, dynext], axis=0) * sc
        pos_ext = i * tm + lax.broadcasted_iota(jnp.int32, (tm + HALO, 1), 0)
        dws, dscs, dus = [], [], []
        for gi, wd in enumerate(POOL_WINDOWS):
            cols = slice(gi * cg, (gi + 1) * cg)
            pb = pooled[gi].astype(BF16)
            ypre = _dot(pb, wv[gi])
            dscs.append(jnp.sum(dyt[:, cols] * ypre, axis=0, keepdims=True))
            dws.append(_dot_tn(pb, dyext[:tm, cols])[None])
            dp = _dot_nt(dyext[:, cols], wv[gi])
            s = dp / _pool_counts(pos_ext, wd)
            span = 1
            while span < wd:
                s = s + pltpu.roll(s, tm + HALO - span, 0)
                span *= 2
            dus.append(s[:tm] - dp[:tm])
        du = jnp.concatenate(dus, axis=1)
        pos = pos_ext[:tm]
        du = jnp.where(pos >= FRONT_PAD, du, 0.0)
        dx, dgam = _rms_bwd(du, xh, r, gamma)
        return [dyt + dx], [jnp.concatenate(dws, axis=0), jnp.concatenate(dscs, axis=1), dgam]

    return _row_call("pool_bwd", body, L, [(h, "tile"), (h, "prev"), (dy, "tile"), (dy, "next")],
                     [gam, w, scale], [(D, F32)], [(ng, cg, cg), (1, D), (1, D)])


def _kv_fwd(h, tabs, g1, wdkv, g2, wuk, wuv):
    L, D = h.shape
    hw = N_HEADS * HEAD_LANES

    def body(i, n, tm, ht, ck, s1, s2, g1v, wdkv_v, g2v, wuk_v, wuv_v):
        xkv, _, _ = _rms(ht, g1v)
        ckr = _dot(xkv, wdkv_v)
        ckv, _, _ = _rms(ckr[:, :KV_RANK], g2v)
        krope = _rope(ckr[:, KV_RANK:], ck, s1, s2)
        return [_dot(ckv, wuk_v), krope, _dot(ckv, wuv_v), ckr], []

    ck, s1, s2 = tabs["ck"], tabs["s1"], tabs["s2"]
    return _row_call("kv_fwd", body, L, [(h, "tile"), (ck, "tile"), (s1, "tile"), (s2, "tile")],
                     [g1, wdkv, g2, wuk, wuv],
                     [(hw, BF16), (HEAD_LANES, BF16), (hw, BF16), (KV_RANK + HEAD_LANES, F32)], [])


def _kv_bwd(dh, h, ckr, dks, dvs, tabs, g1, wdkv, g2, wuk, wuv):
    L, D = h.shape
    hw = N_HEADS * HEAD_LANES
    nl = len(dks)

    def body(i, n, tm, *vals):
        dht, ht, ckr_t = vals[:3]
        dk = sum(vals[3:3 + nl][1:], vals[3])
        dv = sum(vals[3 + nl:3 + 2 * nl][1:], vals[3 + nl])
        ck, s1, s2, g1v, wdkv_v, g2v, wuk_v, wuv_v = vals[3 + 2 * nl:]
        xkv, xh1, r1 = _rms(ht, g1v)
        ckv, xh2, r2 = _rms(ckr_t[:, :KV_RANK], g2v)
        dckv = _dot_nt(dk, wuk_v) + _dot_nt(dv, wuv_v)
        dlat, dg2 = _rms_bwd(dckv, xh2, r2, g2v)
        dkr = dk[:, :HEAD_LANES]
        for hd in range(1, N_HEADS):
            dkr = dkr + dk[:, hd * HEAD_LANES:(hd + 1) * HEAD_LANES]
        dckr = jnp.concatenate([dlat, _rope_t(dkr, ck, s1, s2)], axis=1)
        dx, dg1 = _rms_bwd(_dot_nt(dckr, wdkv_v), xh1, r1, g1v)
        return [dht + dx], [_dot_tn(xkv, dckr), _dot_tn(ckv, dk), _dot_tn(ckv, dv), dg1, dg2]

    row_ins = [(dh, "tile"), (h, "tile"), (ckr, "tile")] + [(a, "tile") for a in dks + dvs]
    row_ins += [(tabs[k], "tile") for k in ("ck", "s1", "s2")]
    return _row_call("kv_bwd", body, L, row_ins, [g1, wdkv, g2, wuk, wuv], [(D, F32)],
                     [(D, KV_RANK + HEAD_LANES), (KV_RANK, hw), (KV_RANK, hw), (1, D), (1, KV_RANK)])


def _q_fwd(h, tabs, g, wdq, gq, wuq):
    L, D = h.shape
    hw = N_HEADS * HEAD_LANES

    def body(i, n, tm, ht, cq_t, s1, s2, gv, wdq_v, gqv, wuq_v):
        u, _, _ = _rms(ht, gv)
        cqp = _dot(u, wdq_v)
        cq, _, _ = _rms(cqp, gqv)
        qp = _dot(cq, wuq_v)
        q = [_rope(qp[:, hd * HEAD_LANES:(hd + 1) * HEAD_LANES], cq_t, s1, s2) for hd in range(N_HEADS)]
        return [jnp.concatenate(q, axis=1), cqp], []

    return _row_call("q_fwd", body, L, [(h, "tile")] + [(tabs[k], "tile") for k in ("cq", "s1", "s2")],
                     [g, wdq, gq, wuq], [(hw, BF16), (Q_RANK, F32)], [])


def _q_bwd(dh, h, cqp, dq, tabs, g, wdq, gq, wuq):
    L, D = h.shape
    hw = N_HEADS * HEAD_LANES

    def body(i, n, tm, dht, ht, cqp_t, dq_t, cq_t, s1, s2, gv, wdq_v, gqv, wuq_v):
        u, xh1, r1 = _rms(ht, gv)
        cq, xh2, r2 = _rms(cqp_t, gqv)
        dqp = jnp.concatenate([_rope_t(dq_t[:, hd * HEAD_LANES:(hd + 1) * HEAD_LANES], cq_t, s1, s2)
                               for hd in range(N_HEADS)], axis=1)
        dcqp, dgq = _rms_bwd(_dot_nt(dqp, wuq_v), xh2, r2, gqv)
        dx, dg = _rms_bwd(_dot_nt(dcqp, wdq_v), xh1, r1, gv)
        return [dht + dx], [_dot_tn(u, dcqp), _dot_tn(cq, dqp), dg, dgq]

    row_ins = [(dh, "tile"), (h, "tile"), (cqp, "tile"), (dq, "tile")]
    row_ins += [(tabs[k], "tile") for k in ("cq", "s1", "s2")]
    return _row_call("q_bwd", body, L, row_ins, [g, wdq, gq, wuq], [(D, F32)],
                     [(D, Q_RANK), (Q_RANK, hw), (1, D), (1, Q_RANK)])


def _oproj_fwd(h, o, wo):
    L, D = h.shape

    def body(i, n, tm, ht, ot, wov):
        return [ht + _dot(ot, wov)], []

    return _row_call("oproj_fwd", body, L, [(h, "tile"), (o, "tile")], [wo], [(D, F32)], [])[0]


def _oproj_bwd(dh, o, wo):
    L, D = dh.shape
    hw = N_HEADS * HEAD_LANES

    def body(i, n, tm, dht, ot, wov):
        return [_dot_nt(dht, wov)], [_dot_tn(ot, dht)]

    return _row_call("oproj_bwd", body, L, [(dh, "tile"), (o, "tile")], [wo], [(hw, BF16)], [(hw, D)])


def _visible(i, j, tq, tk):
    qpos = i * tq + lax.broadcasted_iota(jnp.int32, (tq, tk), 0)
    kpos = j * tk + lax.broadcasted_iota(jnp.int32, (tq, tk), 1)
    return ((kpos >> CHUNK_SHIFT) <= (qpos >> CHUNK_SHIFT)) & (kpos >= FRONT_PAD)


SM_SCALE = 1.0 / math.sqrt(QK_NOPE + QK_ROPE)


def _attn_fwd(q, kn, kr, v):
    L = q.shape[0]
    hw = N_HEADS * HEAD_LANES
    t = _row_tile(L)
    nq = L // t

    def kern(q_ref, kn_ref, kr_ref, v_ref, o_ref, lse_ref, m_s, l_s, acc_s):
        i, j = pl.program_id(1), pl.program_id(2)

        @pl.when(j == 0)
        def _():
            m_s[...] = jnp.full(m_s.shape, NEG, F32)
            l_s[...] = jnp.zeros(l_s.shape, F32)
            acc_s[...] = jnp.zeros(acc_s.shape, F32)

        @pl.when(j <= i)
        def _():
            k = kn_ref[...] + kr_ref[...]
            s = jnp.where(_visible(i, j, t, t), _dot_nt(q_ref[...], k) * SM_SCALE, NEG)
            m_prev = m_s[:, :1]
            m_new = jnp.maximum(m_prev, jnp.max(s, axis=-1, keepdims=True))
            alpha = jnp.exp(m_prev - m_new)
            p = jnp.exp(s - m_new)
            l_s[...] = jnp.broadcast_to(alpha * l_s[:, :1] + jnp.sum(p, axis=-1, keepdims=True), l_s.shape)
            acc_s[...] = alpha * acc_s[...] + _dot(p, v_ref[...])
            m_s[...] = jnp.broadcast_to(m_new, m_s.shape)

        @pl.when(j == nq - 1)
        def _():
            o_ref[...] = (acc_s[...] / l_s[...]).astype(BF16)
            lse_ref[...] = m_s[...] + jnp.log(l_s[...])

    qmap = lambda h, i, j: (i, h)
    kmap = lambda h, i, j: (jnp.minimum(j, i), h)
    return pl.pallas_call(
        kern, name="attn_fwd", grid=(N_HEADS, nq, nq),
        in_specs=[pl.BlockSpec((t, HEAD_LANES), qmap), pl.BlockSpec((t, HEAD_LANES), kmap),
                  pl.BlockSpec((t, HEAD_LANES), lambda h, i, j: (jnp.minimum(j, i), 0)),
                  pl.BlockSpec((t, HEAD_LANES), kmap)],
        out_specs=[pl.BlockSpec((t, HEAD_LANES), qmap), pl.BlockSpec((t, HEAD_LANES), qmap)],
        out_shape=[jax.ShapeDtypeStruct((L, hw), BF16), jax.ShapeDtypeStruct((L, hw), F32)],
        scratch_shapes=[pltpu.VMEM((t, HEAD_LANES), F32)] * 3,
        compiler_params=_params("arbitrary", "arbitrary", "arbitrary"))(q, kn, kr, v)


def _attn_probs(i, j, t, q, k, lse):
    s = jnp.where(_visible(i, j, t, t), _dot_nt(q, k) * SM_SCALE, NEG)
    return jnp.exp(s - lse[:, :1])


def _attn_dq(q, kn, kr, v, do, o, lse):
    L = q.shape[0]
    hw = N_HEADS * HEAD_LANES
    t = _row_tile(L)
    nq = L // t

    def kern(q_ref, kn_ref, kr_ref, v_ref, do_ref, o_ref, lse_ref, dq_ref, dl_s):
        i, j = pl.program_id(1), pl.program_id(2)

        @pl.when(j == 0)
        def _():
            dq_ref[...] = jnp.zeros(dq_ref.shape, F32)
            delta = jnp.sum(do_ref[...].astype(F32) * o_ref[...].astype(F32), axis=-1, keepdims=True)
            dl_s[...] = jnp.broadcast_to(delta, dl_s.shape)

        @pl.when(j <= i)
        def _():
            k = kn_ref[...] + kr_ref[...]
            p = _attn_probs(i, j, t, q_ref[...], k, lse_ref[...])
            dp = _dot_nt(do_ref[...], v_ref[...])
            ds = p * (dp - dl_s[:, :1]) * SM_SCALE
            dq_ref[...] += _dot(ds, k)

    qmap = lambda h, i, j: (i, h)
    kmap = lambda h, i, j: (jnp.minimum(j, i), h)
    return pl.pallas_call(
        kern, name="attn_dq", grid=(N_HEADS, nq, nq),
        in_specs=[pl.BlockSpec((t, HEAD_LANES), qmap), pl.BlockSpec((t, HEAD_LANES), kmap),
                  pl.BlockSpec((t, HEAD_LANES), lambda h, i, j: (jnp.minimum(j, i), 0)),
                  pl.BlockSpec((t, HEAD_LANES), kmap), pl.BlockSpec((t, HEAD_LANES), qmap),
                  pl.BlockSpec((t, HEAD_LANES), qmap), pl.BlockSpec((t, HEAD_LANES), qmap)],
        out_specs=pl.BlockSpec((t, HEAD_LANES), qmap),
        out_shape=jax.ShapeDtypeStruct((L, hw), F32),
        scratch_shapes=[pltpu.VMEM((t, HEAD_LANES), F32)],
        compiler_params=_params("arbitrary", "arbitrary", "arbitrary"))(q, kn, kr, v, do, o, lse)


def _attn_dkv(q, kn, kr, v, do, o, lse):
    L = q.shape[0]
    hw = N_HEADS * HEAD_LANES
    t = _row_tile(L)
    nq = L // t

    def kern(q_ref, kn_ref, kr_ref, v_ref, do_ref, o_ref, lse_ref, dk_ref, dv_ref):
        j, i = pl.program_id(1), pl.program_id(2)

        @pl.when(i == 0)
        def _():
            dk_ref[...] = jnp.zeros(dk_ref.shape, F32)
            dv_ref[...] = jnp.zeros(dv_ref.shape, F32)

        @pl.when(i >= j)
        def _():
            k = kn_ref[...] + kr_ref[...]
            dof = do_ref[...]
            p = _attn_probs(i, j, t, q_ref[...], k, lse_ref[...])
            delta = jnp.sum(dof.astype(F32) * o_ref[...].astype(F32), axis=-1, keepdims=True)
            dp = _dot_nt(dof, v_ref[...])
            ds = p * (dp - delta) * SM_SCALE
            dv_ref[...] += _dot_tn(p, dof)
            dk_ref[...] += _dot_tn(ds, q_ref[...])

    qmap = lambda h, j, i: (jnp.maximum(i, j), h)
    kmap = lambda h, j, i: (j, h)
    return pl.pallas_call(
        kern, name="attn_dkv", grid=(N_HEADS, nq, nq),
        in_specs=[pl.BlockSpec((t, HEAD_LANES), qmap), pl.BlockSpec((t, HEAD_LANES), kmap),
                  pl.BlockSpec((t, HEAD_LANES), lambda h, j, i: (j, 0)),
                  pl.BlockSpec((t, HEAD_LANES), kmap), pl.BlockSpec((t, HEAD_LANES), qmap),
                  pl.BlockSpec((t, HEAD_LANES), qmap), pl.BlockSpec((t, HEAD_LANES), qmap)],
        out_specs=[pl.BlockSpec((t, HEAD_LANES), kmap), pl.BlockSpec((t, HEAD_LANES), kmap)],
        out_shape=[jax.ShapeDtypeStruct((L, hw), F32), jax.ShapeDtypeStruct((L, hw), F32)],
        compiler_params=_params("arbitrary", "arbitrary", "arbitrary"))(q, kn, kr, v, do, o, lse)


def _head(h, target, g):
    L, D = h.shape

    def body(i, n, tm, ht, tt, gv):
        y, xh, r = _rms(ht, gv)
        pos = i * tm + lax.broadcasted_iota(jnp.int32, (tm, 1), 0)
        e = jnp.where(pos >= SEQ_START, y - tt, 0.0)
        loss = 0.5 * jnp.sum(jnp.mean(e * e, axis=-1, keepdims=True), axis=0, keepdims=True)
        dx, dg = _rms_bwd(e / D, xh, r, gv)
        return [dx], [jnp.broadcast_to(loss, (1, 128)), dg]

    return _row_call("loss_head", body, L, [(h, "tile"), (target, "tile")], [g], [(D, F32)], [(1, 128), (1, D)])


ANY = pl.BlockSpec(memory_space=pl.ANY)


def _coords():
    return lax.axis_index("x"), lax.axis_index("y"), lax.axis_index("c")


def _all_gather(x):
    R, W = x.shape

    def kern(x_ref, out_ref, send_sems, recv_sems, local_sem):
        mx, my, mc = _coords()
        me, sibling = (mx, my, mc), (mx, my, 1 - mc)
        chips = [(1 - mx, my), (mx, 1 - my), (1 - mx, 1 - my)]

        def slot(px, py, pc):
            return out_ref.at[4 * px + 2 * py + pc]

        def copy(k, block, to, src=None):
            return pltpu.make_async_remote_copy(
                src_ref=slot(*block) if src is None else src, dst_ref=slot(*block),
                send_sem=send_sems.at[k], recv_sem=recv_sems.at[k], device_id=to, device_id_type=MESH)

        mine = pltpu.make_async_copy(x_ref, slot(*me), local_sem)
        mine.start()
        first = [copy(0, me, sibling, src=x_ref)]
        first += [copy(1 + n, me, (*chip, mc), src=x_ref) for n, chip in enumerate(chips)]
        for cp in first:
            cp.start()
        passed = [copy(4 + n, (*chip, mc), sibling) for n, chip in enumerate(chips)]
        for n, chip in enumerate(chips):
            copy(1 + n, (*chip, mc), me).wait_recv()
            passed[n].start()
        copy(0, sibling, me).wait_recv()
        for n, chip in enumerate(chips):
            copy(4 + n, (*chip, 1 - mc), me).wait_recv()
        for cp in first + passed:
            cp.wait_send()
        mine.wait()

    return pl.pallas_call(
        kern, name="all_gather_%s" % jnp.dtype(x.dtype).name, in_specs=[ANY], out_specs=ANY,
        out_shape=jax.ShapeDtypeStruct((N_DEV, R, W), x.dtype),
        scratch_shapes=[pltpu.SemaphoreType.DMA((7,)), pltpu.SemaphoreType.DMA((7,)), pltpu.SemaphoreType.DMA],
    )(x)


def _sibling_exchange(g):
    _, _, R, W = g.shape

    def kern(g_ref, out_ref, send_sems, recv_sems, local_sems):
        mx, my, mc = _coords()
        sibling = (mx, my, 1 - mc)
        sends, locals_ = [], []
        for n in range(N_CHIPS):
            sends.append(pltpu.make_async_remote_copy(
                src_ref=g_ref.at[n, 1 - mc], dst_ref=out_ref.at[mc, n], send_sem=send_sems.at[n],
                recv_sem=recv_sems.at[n], device_id=sibling, device_id_type=MESH))
            locals_.append(pltpu.make_async_copy(g_ref.at[n, mc], out_ref.at[mc, n], local_sems.at[n]))
        for cp in sends + locals_:
            cp.start()
        for n in range(N_CHIPS):
            pltpu.make_async_remote_copy(
                src_ref=g_ref.at[n, 1 - mc], dst_ref=out_ref.at[1 - mc, n], send_sem=send_sems.at[n],
                recv_sem=recv_sems.at[n], device_id=sibling, device_id_type=MESH).wait_recv()
        for cp in sends:
            cp.wait_send()
        for cp in locals_:
            cp.wait()

    return pl.pallas_call(
        kern, name="sibling_exchange", in_specs=[ANY], out_specs=ANY,
        out_shape=jax.ShapeDtypeStruct((2, N_CHIPS, R, W), g.dtype),
        scratch_shapes=[pltpu.SemaphoreType.DMA((N_CHIPS,)), pltpu.SemaphoreType.DMA((N_CHIPS,)),
                        pltpu.SemaphoreType.DMA((N_CHIPS,))],
    )(g)


def _chip_exchange(p):
    _, R, W = p.shape

    def kern(p_ref, out_ref, send_sems, recv_sems, local_sem):
        mx, my, mc = _coords()
        mine = 2 * mx + my
        chips = [(1 - mx, my), (mx, 1 - my), (1 - mx, 1 - my)]
        keep = pltpu.make_async_copy(p_ref.at[mine], out_ref.at[mine], local_sem)
        keep.start()
        sends = [pltpu.make_async_remote_copy(
            src_ref=p_ref.at[2 * cx + cy], dst_ref=out_ref.at[mine], send_sem=send_sems.at[n],
            recv_sem=recv_sems.at[n], device_id=(cx, cy, mc), device_id_type=MESH)
            for n, (cx, cy) in enumerate(chips)]
        for cp in sends:
            cp.start()
        for n, (cx, cy) in enumerate(chips):
            pltpu.make_async_remote_copy(
                src_ref=p_ref.at[mine], dst_ref=out_ref.at[2 * cx + cy], send_sem=send_sems.at[n],
                recv_sem=recv_sems.at[n], device_id=(cx, cy, mc), device_id_type=MESH).wait_recv()
        for cp in sends:
            cp.wait_send()
        keep.wait()

    return pl.pallas_call(
        kern, name="chip_exchange", in_specs=[ANY], out_specs=ANY,
        out_shape=jax.ShapeDtypeStruct((N_CHIPS, R, W), p.dtype),
        scratch_shapes=[pltpu.SemaphoreType.DMA((3,)), pltpu.SemaphoreType.DMA((3,)), pltpu.SemaphoreType.DMA],
    )(p)


def _sum_lead(x, name):
    n, R, W = x.shape
    tr = _pick(R, (PACK_ROW_MULT, 8))

    def kern(x_ref, o_ref):
        acc = x_ref[0]
        for k in range(1, n):
            acc = acc + x_ref[k]
        o_ref[...] = acc

    return pl.pallas_call(
        kern, name=name, grid=(R // tr,),
        in_specs=[pl.BlockSpec((n, tr, W), lambda i: (0, i, 0))],
        out_specs=pl.BlockSpec((tr, W), lambda i: (i, 0)),
        out_shape=jax.ShapeDtypeStruct((R, W), F32), compiler_params=_params("arbitrary"))(x)


def _adamw(w, g, m, v):
    shape = w.shape
    cols = shape[-1]
    rows = w.size // cols
    tr = _pick(rows, (512, 352, 256, 128))
    if rows * cols * 4 <= (1 << 20):
        tr = rows

    def kern(w_ref, g_ref, m_ref, v_ref, d_ref, mo_ref, vo_ref):
        gv = g_ref[...]
        mn = ADAM_B1 * m_ref[...] + (1.0 - ADAM_B1) * gv
        vn = ADAM_B2 * v_ref[...] + (1.0 - ADAM_B2) * (gv * gv)
        m_hat = mn / (1.0 - ADAM_B1 ** ADAM_STEP)
        v_hat = vn / (1.0 - ADAM_B2 ** ADAM_STEP)
        d_ref[...] = -ADAM_LR * (m_hat / (jnp.sqrt(v_hat) + ADAM_EPS) + ADAM_WD * w_ref[...])
        mo_ref[...] = mn
        vo_ref[...] = vn

    spec = pl.BlockSpec((tr, cols), lambda i: (i, 0))
    outs = pl.pallas_call(
        kern, name="adamw", grid=(rows // tr,), in_specs=[spec] * 4, out_specs=[spec] * 3,
        out_shape=[jax.ShapeDtypeStruct((rows, cols), F32)] * 3, compiler_params=_params("arbitrary"),
    )(*[a.reshape(rows, cols) for a in (w, g, m, v)])
    return [o.reshape(shape) for o in outs]


def _pack(arrs, n_lead, row_mult):
    parts = []
    for a in arrs:
        lead = a.shape[:n_lead]
        flat = a.reshape(lead + (-1,))
        rows = -(-flat.shape[-1] // (PACK_W * row_mult)) * row_mult
        flat = jnp.pad(flat, [(0, 0)] * n_lead + [(0, rows * PACK_W - flat.shape[-1])])
        parts.append(flat.reshape(lead + (rows, PACK_W)))
    out = jnp.concatenate(parts, axis=n_lead)
    pad = -out.shape[n_lead] % PACK_ROW_MULT
    return jnp.pad(out, [(0, 0)] * n_lead + [(0, pad), (0, 0)])


def _unpack(pack, shapes, n_lead, row_mult):
    outs, row = [], 0
    lead = pack.shape[:n_lead]
    for shp in shapes:
        size = math.prod(shp)
        rows = -(-size // (PACK_W * row_mult)) * row_mult
        blk = lax.slice_in_dim(pack, row, row + rows, axis=n_lead)
        outs.append(blk.reshape(lead + (-1,))[..., :size].reshape(lead + tuple(shp)))
        row += rows
    return outs


def _to_words(a):
    return lax.bitcast_convert_type(a, BF16)


def _from_words(a):
    return lax.bitcast_convert_type(a, F32)


def _pad_axis(a, axis, size):
    pads = [(0, 0)] * a.ndim
    pads[axis] = (0, size - a.shape[axis])
    return jnp.pad(a, pads)


def _dense(name, s):
    if name.endswith("w_gate") or name.endswith("w_up"):
        _, nl, d, fs = s.shape
        return s.transpose(1, 2, 0, 3).reshape(nl, d, N_DEV * fs)
    if name.endswith("w_down"):
        _, nl, fs, d = s.shape
        return s.transpose(1, 0, 2, 3).reshape(nl, N_DEV * fs, d)
    if name == "pool_w":
        _, nl, ng, r, cg = s.shape
        return s.transpose(1, 2, 0, 3, 4).reshape(nl, ng, cg, cg)
    if name == "w_dkv":
        w = s.reshape(-1, s.shape[2])
        z = lambda n: jnp.zeros((w.shape[0], n), w.dtype)
        return jnp.concatenate([w[:, :KV_RANK], z(ROPE_LANE0), w[:, KV_RANK:],
                                z(HEAD_LANES - ROPE_LANE0 - QK_ROPE)], axis=1)
    if name in ("w_uk", "w_uv"):
        return _pad_axis(s.transpose(1, 0, 2), 2, HEAD_LANES).reshape(KV_RANK, N_HEADS * HEAD_LANES)
    if name == "w_dq":
        _, nl, ds, r = s.shape
        return s.transpose(1, 0, 2, 3).reshape(nl, N_DEV * ds, r)
    if name == "w_uq":
        nl = s.shape[1]
        return _pad_axis(s.transpose(1, 2, 0, 3), 3, HEAD_LANES).reshape(nl, Q_RANK, N_HEADS * HEAD_LANES)
    if name == "w_o":
        _, nl, k, dc = s.shape
        w = s.transpose(1, 2, 0, 3).reshape(nl, N_HEADS, V_HEAD, N_DEV * dc)
        return _pad_axis(w, 2, HEAD_LANES).reshape(nl, N_HEADS * HEAD_LANES, N_DEV * dc)
    if name in ("meta_tokens", "pool_scale"):
        r, dc = s.shape[1:]
        return s.transpose(1, 0, 2).reshape(r, N_DEV * dc)
    raise ValueError(name)


def _shards(name, g):
    if name.endswith("w_gate") or name.endswith("w_up"):
        nl, d, f = g.shape
        return g.reshape(nl, d, N_DEV, f // N_DEV).transpose(2, 0, 1, 3)
    if name.endswith("w_down"):
        nl, f, d = g.shape
        return g.reshape(nl, N_DEV, f // N_DEV, d).transpose(1, 0, 2, 3)
    if name == "pool_w":
        nl, ng, cg, _ = g.shape
        return g.reshape(nl, ng, N_DEV, cg // N_DEV, cg).transpose(2, 0, 1, 3, 4)
    if name == "w_dkv":
        w = jnp.concatenate([g[:, :KV_RANK], g[:, KV_RANK + ROPE_LANE0:KV_RANK + ROPE_LANE0 + QK_ROPE]], axis=1)
        return w.reshape(N_DEV, -1, KV_RANK + QK_ROPE)
    if name in ("w_uk", "w_uv"):
        return g.reshape(KV_RANK, N_HEADS, HEAD_LANES)[:, :, :V_HEAD].transpose(1, 0, 2)
    if name == "w_dq":
        nl, d, r = g.shape
        return g.reshape(nl, N_DEV, d // N_DEV, r).transpose(1, 0, 2, 3)
    if name == "w_uq":
        nl = g.shape[0]
        return g.reshape(nl, Q_RANK, N_HEADS, HEAD_LANES)[..., :QK_NOPE + QK_ROPE].transpose(2, 0, 1, 3)
    if name == "w_o":
        nl, _, d = g.shape
        w = g.reshape(nl, N_HEADS, HEAD_LANES, d)[:, :, :V_HEAD].reshape(nl, N_HEADS * V_HEAD, N_DEV, d // N_DEV)
        return w.transpose(2, 0, 1, 3)
    if name in ("meta_tokens", "pool_scale"):
        r, d = g.shape
        return g.reshape(r, N_DEV, d // N_DEV).transpose(1, 0, 2)
    raise ValueError(name)


def _rope_tables(L):
    pos = jnp.maximum(jnp.arange(L) - FRONT_PAD, 0).astype(F32)
    inv = 1.0 / (ROPE_THETA ** (jnp.arange(0, QK_ROPE, 2, dtype=F32) / QK_ROPE))
    ang = pos[:, None] * inv[None, :]
    cos, sin = jnp.cos(ang), jnp.sin(ang)
    half = QK_ROPE // 2
    z = lambda n: jnp.zeros((L, n), F32)
    tail = z(HEAD_LANES - ROPE_LANE0 - QK_ROPE)
    return {
        "cq": jnp.concatenate([jnp.ones((L, ROPE_LANE0), F32), cos, cos, tail], axis=1),
        "ck": jnp.concatenate([z(ROPE_LANE0), cos, cos, tail], axis=1),
        "s1": jnp.concatenate([z(ROPE_LANE0), -sin, z(half), tail], axis=1),
        "s2": jnp.concatenate([z(ROPE_LANE0), z(half), sin, tail], axis=1),
    }


def kernel(x, meta_tokens, ffn1_norm, ffn1_w_gate, ffn1_w_up, ffn1_w_down, mix_norm, ffn2_norm, ffn2_w_gate, ffn2_w_up, ffn2_w_down, pool_w, pool_scale, kv_in_norm, w_dkv, kv_latent_norm, w_uk, w_uv, w_dq, q_latent_norm, w_uq, w_o, final_norm, loss_target, m_meta_tokens, m_ffn1_norm, m_ffn1_w_gate, m_ffn1_w_up, m_ffn1_w_down, m_mix_norm, m_ffn2_norm, m_ffn2_w_gate, m_ffn2_w_up, m_ffn2_w_down, m_pool_w, m_pool_scale, m_kv_in_norm, m_w_dkv, m_kv_latent_norm, m_w_uk, m_w_uv, m_w_dq, m_q_latent_norm, m_w_uq, m_w_o, m_final_norm, v_meta_tokens, v_ffn1_norm, v_ffn1_w_gate, v_ffn1_w_up, v_ffn1_w_down, v_mix_norm, v_ffn2_norm, v_ffn2_w_gate, v_ffn2_w_up, v_ffn2_w_down, v_pool_w, v_pool_scale, v_kv_in_norm, v_w_dkv, v_kv_latent_norm, v_w_uk, v_w_uv, v_w_dq, v_q_latent_norm, v_w_uq, v_w_o, v_final_norm):
    args = dict(locals())
    W = {n: args[n] for n in WEIGHTS}
    M = {n: args["m_" + n] for n in WEIGHTS}
    V = {n: args["v_" + n] for n in WEIGHTS}
    seq, D = x.shape[1], x.shape[2]
    L = SEQ_START + seq

    shard_shapes = [W[n].shape + ((2,) if n in SHARDED_F32 else ()) for n in SHARDED]
    wpack = _pack([_to_words(W[n]) if n in SHARDED_F32 else W[n].astype(BF16) for n in SHARDED], 0, 16)
    gathered = _unpack(_all_gather(wpack), shard_shapes, 1, 16)
    P = {}
    for n, s in zip(SHARDED, gathered):
        P[n] = _dense(n, _from_words(s) if n in SHARDED_F32 else s)
    norm3 = lambda a: a.reshape(a.shape[0], 1, a.shape[-1])
    row = lambda a: a.reshape(1, -1)
    g_ffn1, g_mix, g_ffn2 = norm3(ffn1_norm), mix_norm, norm3(ffn2_norm)

    h = jnp.concatenate([jnp.zeros((FRONT_PAD, D), F32), P["meta_tokens"], x[0]], axis=0)
    target = jnp.concatenate([jnp.zeros((SEQ_START, D), F32), loss_target[0]], axis=0)
    tabs = _rope_tables(L)
    saved = []
    kv = None
    for l in range(DEPTH):
        s = {"h1": h}
        h, s["xn1"], s["g1"], s["u1"] = _ffn_fwd(h, g_ffn1, P["ffn1_w_gate"], P["ffn1_w_up"], P["ffn1_w_down"], l)
        s["hm"] = h
        if l < N_POOL_LAYERS:
            h = _pool_fwd(h, row(g_mix[l]), P["pool_w"][l], row(P["pool_scale"][l]), l)
        else:
            j = l - N_POOL_LAYERS
            s["q"], s["cqp"] = _q_fwd(h, tabs, row(g_mix[l]), P["w_dq"][j], row(q_latent_norm[j]), P["w_uq"][j])
            s["o"], s["lse"] = _attn_fwd(s["q"], kv["kn"], kv["kr"], kv["v"])
            h = _oproj_fwd(h, s["o"], P["w_o"][j])
        s["h2"] = h
        h, s["xn2"], s["g2"], s["u2"] = _ffn_fwd(h, g_ffn2, P["ffn2_w_gate"], P["ffn2_w_up"], P["ffn2_w_down"], l)
        if l == N_POOL_LAYERS - 1:
            kv = {"h": h}
            kv["kn"], kv["kr"], kv["v"], kv["ckr"] = _kv_fwd(
                h, tabs, row(kv_in_norm), P["w_dkv"], row(kv_latent_norm), P["w_uk"], P["w_uv"])
        saved.append(s)
    dh, loss_row, d_final = _head(h, target, row(final_norm))
    loss = lax.psum(loss_row[0, 0], ("x", "y", "c"))

    G = {}
    stack = {n: [None] * DEPTH for n in ("ffn1_norm", "ffn1_w_gate", "ffn1_w_up", "ffn1_w_down", "mix_norm",
                                         "ffn2_norm", "ffn2_w_gate", "ffn2_w_up", "ffn2_w_down")}
    pool_dw, pool_ds = [None] * N_POOL_LAYERS, [None] * N_POOL_LAYERS
    mla = {n: [None] * (DEPTH - N_POOL_LAYERS) for n in ("w_dq", "w_uq", "w_o", "q_latent_norm")}
    dks, dvs = [], []

    def ffn_backward(dh, s, which, gam, l):
        wg, wu, wd = P["ffn%d_w_gate" % which], P["ffn%d_w_up" % which], P["ffn%d_w_down" % which]
        dh, dg, du, act, dob, dgam = _ffn_bwd(dh, s["h%d" % which], s["g%d" % which], s["u%d" % which], gam, wg, wu, wd, l)
        xn = s["xn%d" % which]
        stack["ffn%d_w_gate" % which][l] = _mm_tn(xn, dg, "ffn_dw_in")
        stack["ffn%d_w_up" % which][l] = _mm_tn(xn, du, "ffn_dw_in")
        stack["ffn%d_w_down" % which][l] = _mm_tn(act, dob, "ffn_dw_down")
        stack["ffn%d_norm" % which][l] = dgam
        return dh

    for l in reversed(range(DEPTH)):
        s = saved[l]
        if l == N_POOL_LAYERS - 1:
            dh, d_dkv, d_uk, d_uv, d_kvin, d_kvlat = _kv_bwd(
                dh, kv["h"], kv["ckr"], dks, dvs, tabs, row(kv_in_norm), P["w_dkv"], row(kv_latent_norm),
                P["w_uk"], P["w_uv"])
            G.update(w_dkv=d_dkv, w_uk=d_uk, w_uv=d_uv, kv_in_norm=d_kvin, kv_latent_norm=d_kvlat)
        dh = ffn_backward(dh, s, 2, g_ffn2, l)
        if l < N_POOL_LAYERS:
            dh, pool_dw[l], pool_ds[l], stack["mix_norm"][l] = _pool_bwd(
                dh, s["hm"], row(g_mix[l]), P["pool_w"][l], row(P["pool_scale"][l]))
        else:
            j = l - N_POOL_LAYERS
            do, mla["w_o"][j] = _oproj_bwd(dh, s["o"], P["w_o"][j])
            att = (s["q"], kv["kn"], kv["kr"], kv["v"], do, s["o"], s["lse"])
            dq = _attn_dq(*att)
            dk, dv = _attn_dkv(*att)
            dks.append(dk)
            dvs.append(dv)
            dh, mla["w_dq"][j], mla["w_uq"][j], stack["mix_norm"][l], mla["q_latent_norm"][j] = _q_bwd(
                dh, s["hm"], s["cqp"], dq, tabs, row(g_mix[l]), P["w_dq"][j], row(q_latent_norm[j]), P["w_uq"][j])
        dh = ffn_backward(dh, s, 1, g_ffn1, l)
    grad_x = dh[SEQ_START:][None]
    for n, parts in stack.items():
        G[n] = jnp.stack(parts) if parts[0].ndim == 2 and parts[0].shape[0] > 1 else jnp.concatenate(parts, axis=0)
    G["pool_w"] = jnp.stack(pool_dw)
    G["pool_scale"] = jnp.concatenate(pool_ds, axis=0)
    G["w_dq"], G["w_uq"], G["w_o"] = (jnp.stack(mla[n]) for n in ("w_dq", "w_uq", "w_o"))
    G["q_latent_norm"] = jnp.concatenate(mla["q_latent_norm"], axis=0)
    G["meta_tokens"] = dh[FRONT_PAD:SEQ_START]
    G["final_norm"] = d_final

    gpack = _pack([_shards(n, G[n]) for n in SHARDED], 1, 8)
    R = gpack.shape[1]
    halves = _sibling_exchange(gpack.reshape(N_CHIPS, 2, R, PACK_W))
    partial = _sum_lead(halves.reshape(2, N_CHIPS * R, PACK_W), "sum_cores").reshape(N_CHIPS, R, PACK_W)
    mine = _sum_lead(_chip_exchange(partial), "sum_chips")
    local_shapes = [W[n].shape for n in SHARDED]
    grads = dict(zip(SHARDED, _unpack(mine, local_shapes, 0, 8)))
    rep_shapes = [W[n].shape for n in REPLICATED]
    rpack = _pack([G[n].reshape(W[n].shape) for n in REPLICATED], 0, 8)
    grads.update(zip(REPLICATED, _unpack(_sum_lead(_all_gather(rpack), "sum_devices"), rep_shapes, 0, 8)))

    delta, new_m, new_v = {}, {}, {}
    for n in WEIGHTS:
        delta[n], new_m[n], new_v[n] = _adamw(W[n], grads[n], M[n], V[n])
    return (loss, grad_x, *[grads[n] for n in WEIGHTS], *[delta[n] for n in WEIGHTS],
            *[new_m[n] for n in WEIGHTS], *[new_v[n] for n in WEIGHTS])
```

```python
import functools
import math

import jax
import jax.numpy as jnp
from jax import lax
from jax.experimental import pallas as pl
from jax.experimental.pallas import tpu as pltpu

F32 = jnp.float32
BF16 = jnp.bfloat16
MESH = pl.DeviceIdType.MESH

N_DEV = 8
N_CHIPS = 4
DEPTH = 4
N_POOL_LAYERS = 2
N_HEADS = 8
QK_NOPE = 64
QK_ROPE = 32
V_HEAD = 64
KV_RANK = 256
Q_RANK = 384
HEAD_LANES = 128
ROPE_LANE0 = QK_NOPE
BIAS_LANE = QK_NOPE + QK_ROPE
N_META = 16
CHUNK_SHIFT = 6
FRONT_PAD = 112
SEQ_START = FRONT_PAD + N_META
HALO = 16
POOL_WINDOWS = (2, 4, 8, 16)
EPS = 1e-6
ROPE_THETA = 10000.0
NEG = -1e30
PACK_W = 1024
PACK_ROW_MULT = 256
VMEM_LIMIT = 56 * 1024 * 1024

ADAM_LR = 0.001
ADAM_B1 = 0.9
ADAM_B2 = 0.999
ADAM_EPS = 1e-08
ADAM_WD = 0.01
ADAM_STEP = 10

SHARDED = ["ffn1_w_gate", "ffn1_w_up", "ffn1_w_down", "ffn2_w_gate", "ffn2_w_up", "ffn2_w_down",
           "pool_w", "w_dkv", "w_uk", "w_uv", "w_dq", "w_uq", "w_o", "meta_tokens", "pool_scale"]
SHARDED_F32 = ("meta_tokens", "pool_scale")
REPLICATED = ["ffn1_norm", "mix_norm", "ffn2_norm", "kv_in_norm", "kv_latent_norm", "q_latent_norm",
              "final_norm"]
WEIGHTS = ['meta_tokens', 'ffn1_norm', 'ffn1_w_gate', 'ffn1_w_up', 'ffn1_w_down', 'mix_norm', 'ffn2_norm',
           'ffn2_w_gate', 'ffn2_w_up', 'ffn2_w_down', 'pool_w', 'pool_scale', 'kv_in_norm', 'w_dkv',
           'kv_latent_norm', 'w_uk', 'w_uv', 'w_dq', 'q_latent_norm', 'w_uq', 'w_o', 'final_norm']


def _dot(a, b):
    return jnp.dot(a.astype(BF16), b.astype(BF16), preferred_element_type=F32)


def _dot_nt(a, b):
    return lax.dot_general(a.astype(BF16), b.astype(BF16), (((1,), (1,)), ((), ())),
                           preferred_element_type=F32)


def _dot_tn(a, b):
    return lax.dot_general(a.astype(BF16), b.astype(BF16), (((0,), (0,)), ((), ())),
                           preferred_element_type=F32)


def _sigmoid(x):
    return 1.0 / (1.0 + jnp.exp(-x))


def _rms(x, g):
    r = lax.rsqrt(jnp.mean(x * x, axis=-1, keepdims=True) + EPS)
    xh = x * r
    return xh * g, xh, r


def _rms_bwd(dy, xh, r, g):
    dxh = dy * g
    dx = r * (dxh - xh * jnp.mean(dxh * xh, axis=-1, keepdims=True))
    return dx, jnp.sum(dy * xh, axis=0, keepdims=True)


def _rope(x, c, s1, s2):
    return x * c + pltpu.roll(x, HEAD_LANES - QK_ROPE // 2, 1) * s1 + pltpu.roll(x, QK_ROPE // 2, 1) * s2


def _rope_t(d, c, s1, s2):
    return d * c + pltpu.roll(d * s1, QK_ROPE // 2, 1) + pltpu.roll(d * s2, HEAD_LANES - QK_ROPE // 2, 1)


def _params(*sem):
    return pltpu.CompilerParams(dimension_semantics=sem, vmem_limit_bytes=VMEM_LIMIT)


def _pick(n, candidates):
    for c in candidates:
        if n % c == 0:
            return c
    return n


def _row_tile(L):
    return _pick(L, (640, 128))


def _row_call(name, body, L, row_ins, full_ins, row_outs, acc_outs):
    tm = _row_tile(L)
    n = L // tm
    hb = tm // HALO
    nb = L // HALO
    in_specs, args = [], []
    for arr, kind in row_ins:
        c = arr.shape[1]
        if kind == "tile":
            spec = pl.BlockSpec((tm, c), lambda i: (i, 0))
        elif kind == "prev":
            spec = pl.BlockSpec((HALO, c), lambda i: (jnp.maximum(i * hb - 1, 0), 0))
        else:
            spec = pl.BlockSpec((HALO, c), lambda i: (jnp.minimum((i + 1) * hb, nb - 1), 0))
        in_specs.append(spec)
        args.append(arr)
    for arr in full_ins:
        in_specs.append(pl.BlockSpec(arr.shape, lambda i, nd=arr.ndim: (0,) * nd))
        args.append(arr)
    out_shape = [jax.ShapeDtypeStruct((L, c), dt) for c, dt in row_outs]
    out_specs = [pl.BlockSpec((tm, c), lambda i: (i, 0)) for c, _ in row_outs]
    for shp in acc_outs:
        out_shape.append(jax.ShapeDtypeStruct(shp, F32))
        out_specs.append(pl.BlockSpec(shp, lambda i, nd=len(shp): (0,) * nd))
    n_in, n_ro = len(args), len(row_outs)

    def kern(*refs):
        i = pl.program_id(0)
        vals = [r[...] for r in refs[:n_in]]
        ro, ao = body(i, n, tm, *vals)
        for r, v in zip(refs[n_in:n_in + n_ro], ro):
            r[...] = v.astype(r.dtype)
        acc_refs = refs[n_in + n_ro:]

        @pl.when(i == 0)
        def _():
            for r in acc_refs:
                r[...] = jnp.zeros(r.shape, r.dtype)

        for r, v in zip(acc_refs, ao):
            r[...] += v

    return pl.pallas_call(kern, name=name, grid=(n,), in_specs=in_specs, out_specs=out_specs,
                          out_shape=out_shape, compiler_params=_params("arbitrary"))(*args)


def _ffn_fwd(h, gam, wg, wu, wd, l):
    L, D = h.shape
    F = wg.shape[2]
    tm = _row_tile(L)
    tf = _pick(F, (256, 128))
    nF = F // tf

    def kern(h_ref, gam_ref, wg_ref, wu_ref, wd_ref, ho_ref, xn_ref, gs_ref, us_ref, acc):
        f = pl.program_id(1)

        @pl.when(f == 0)
        def _():
            xn, _, _ = _rms(h_ref[...], gam_ref[...])
            xn_ref[...] = xn.astype(BF16)
            acc[...] = jnp.zeros(acc.shape, F32)

        xnb = xn_ref[...]
        g = _dot(xnb, wg_ref[...])
        u = _dot(xnb, wu_ref[...])
        gs_ref[...] = g.astype(BF16)
        us_ref[...] = u.astype(BF16)
        acc[...] += _dot(g * _sigmoid(g) * u, wd_ref[...])

        @pl.when(f == nF - 1)
        def _():
            ho_ref[...] = h_ref[...] + 0.5 * acc[...]

    return pl.pallas_call(
        kern, name="ffn_fwd", grid=(L // tm, nF),
        in_specs=[pl.BlockSpec((tm, D), lambda i, f: (i, 0)),
                  pl.BlockSpec((None, 1, D), lambda i, f: (l, 0, 0)),
                  pl.BlockSpec((None, D, tf), lambda i, f: (l, 0, f)),
                  pl.BlockSpec((None, D, tf), lambda i, f: (l, 0, f)),
                  pl.BlockSpec((None, tf, D), lambda i, f: (l, f, 0))],
        out_specs=[pl.BlockSpec((tm, D), lambda i, f: (i, 0)),
                   pl.BlockSpec((tm, D), lambda i, f: (i, 0)),
                   pl.BlockSpec((tm, tf), lambda i, f: (i, f)),
                   pl.BlockSpec((tm, tf), lambda i, f: (i, f))],
        out_shape=[jax.ShapeDtypeStruct((L, D), F32), jax.ShapeDtypeStruct((L, D), BF16),
                   jax.ShapeDtypeStruct((L, F), BF16), jax.ShapeDtypeStruct((L, F), BF16)],
        scratch_shapes=[pltpu.VMEM((tm, D), F32)],
        compiler_params=_params("arbitrary", "arbitrary"))(h, gam, wg, wu, wd)


def _ffn_bwd(dh, h, gs, us, gam, wg, wu, wd, l):
    L, D = h.shape
    F = wg.shape[2]
    tm = _row_tile(L)
    tf = _pick(F, (256, 128))
    nF = F // tf

    def kern(dh_ref, h_ref, gs_ref, us_ref, gam_ref, wg_ref, wu_ref, wd_ref,
             dhi_ref, dg_ref, du_ref, a_ref, dob_ref, dgam_ref, dxn):
        i = pl.program_id(0)
        f = pl.program_id(1)

        @pl.when(f == 0)
        def _():
            dxn[...] = jnp.zeros(dxn.shape, F32)
            dob_ref[...] = (0.5 * dh_ref[...]).astype(BF16)

        @pl.when((f == 0) & (i == 0))
        def _():
            dgam_ref[...] = jnp.zeros(dgam_ref.shape, F32)

        g = gs_ref[...].astype(F32)
        u = us_ref[...].astype(F32)
        sg = _sigmoid(g)
        silu = g * sg
        da = _dot_nt(dob_ref[...], wd_ref[...])
        a_ref[...] = (silu * u).astype(BF16)
        dgt = (da * u * (sg * (1.0 + g * (1.0 - sg)))).astype(BF16)
        dut = (da * silu).astype(BF16)
        dg_ref[...] = dgt
        du_ref[...] = dut
        dxn[...] += _dot_nt(dgt, wg_ref[...]) + _dot_nt(dut, wu_ref[...])

        @pl.when(f == nF - 1)
        def _():
            gamma = gam_ref[...]
            _, xh, r = _rms(h_ref[...], gamma)
            dx, dgam = _rms_bwd(dxn[...], xh, r, gamma)
            dhi_ref[...] = dh_ref[...] + dx
            dgam_ref[...] += dgam

    return pl.pallas_call(
        kern, name="ffn_bwd", grid=(L // tm, nF),
        in_specs=[pl.BlockSpec((tm, D), lambda i, f: (i, 0)),
                  pl.BlockSpec((tm, D), lambda i, f: (i, 0)),
                  pl.BlockSpec((tm, tf), lambda i, f: (i, f)),
                  pl.BlockSpec((tm, tf), lambda i, f: (i, f)),
                  pl.BlockSpec((None, 1, D), lambda i, f: (l, 0, 0)),
                  pl.BlockSpec((None, D, tf), lambda i, f: (l, 0, f)),
                  pl.BlockSpec((None, D, tf), lambda i, f: (l, 0, f)),
                  pl.BlockSpec((None, tf, D), lambda i, f: (l, f, 0))],
        out_specs=[pl.BlockSpec((tm, D), lambda i, f: (i, 0)),
                   pl.BlockSpec((tm, tf), lambda i, f: (i, f)),
                   pl.BlockSpec((tm, tf), lambda i, f: (i, f)),
                   pl.BlockSpec((tm, tf), lambda i, f: (i, f)),
                   pl.BlockSpec((tm, D), lambda i, f: (i, 0)),
                   pl.BlockSpec((1, D), lambda i, f: (0, 0))],
        out_shape=[jax.ShapeDtypeStruct((L, D), F32), jax.ShapeDtypeStruct((L, F), BF16),
                   jax.ShapeDtypeStruct((L, F), BF16), jax.ShapeDtypeStruct((L, F), BF16),
                   jax.ShapeDtypeStruct((L, D), BF16), jax.ShapeDtypeStruct((1, D), F32)],
        scratch_shapes=[pltpu.VMEM((tm, D), F32)],
        compiler_params=_params("arbitrary", "arbitrary"))(dh, h, gs, us, gam, wg, wu, wd)


def _mm_tn(a, b, name):
    L, M = a.shape
    N = b.shape[1]
    tm = _pick(M, (1408, 1024, 512))
    tn = _pick(N, (1408, 1024, 512))
    tk = _row_tile(L)

    def kern(a_ref, b_ref, o_ref):
        @pl.when(pl.program_id(2) == 0)
        def _():
            o_ref[...] = jnp.zeros(o_ref.shape, F32)

        o_ref[...] += _dot_tn(a_ref[...], b_ref[...])

    return pl.pallas_call(
        kern, name=name, grid=(M // tm, N // tn, L // tk),
        in_specs=[pl.BlockSpec((tk, tm), lambda i, j, k: (k, i)),
                  pl.BlockSpec((tk, tn), lambda i, j, k: (k, j))],
        out_specs=pl.BlockSpec((tm, tn), lambda i, j, k: (i, j)),
        out_shape=jax.ShapeDtypeStruct((M, N), F32),
        compiler_params=_params("arbitrary", "arbitrary", "arbitrary"))(a, b)


def _pool_counts(pos, w):
    return jnp.clip(pos - (FRONT_PAD - 1), 1, w).astype(F32)


def _pool_forward_values(i, tm, h, hprev, gamma, D):
    cg = D // len(POOL_WINDOWS)
    hext = jnp.concatenate([hprev, h], axis=0)
    uext, xh, r = _rms(hext, gamma)
    pos = i * tm + lax.broadcasted_iota(jnp.int32, (tm, 1), 0)
    pooled = []
    for gi, w in enumerate(POOL_WINDOWS):
        s = uext[:, gi * cg:(gi + 1) * cg]
        span = 1
        while span < w:
            s = s + pltpu.roll(s, span, 0)
            span *= 2
        s = s[HALO:]
        pooled.append(s / _pool_counts(pos, w) - uext[HALO:, gi * cg:(gi + 1) * cg])
    return uext, xh[HALO:], r[HALO:], pooled


def _pool_fwd(h, gam, w, scale, l):
    L, D = h.shape
    cg = D // len(POOL_WINDOWS)

    def body(i, n, tm, ht, hprev, gamma, wv, sc):
        _, _, _, pooled = _pool_forward_values(i, tm, ht, hprev, gamma, D)
        ys = [_dot(pooled[gi], wv[gi]) for gi in range(len(POOL_WINDOWS))]
        y = jnp.concatenate(ys, axis=1) * sc
        return [ht + y], []

    del cg
    return _row_call("pool_fwd", body, L, [(h, "tile"), (h, "prev")], [gam, w, scale], [(D, F32)], [])[0]


def _pool_bwd(dy, h, gam, w, scale):
    L, D = h.shape
    ng = len(POOL_WINDOWS)
    cg = D // ng

    def body(i, n, tm, ht, hprev, dyt, dynext, gamma, wv, sc):
        _, xh, r, pooled = _pool_forward_values(i, tm, ht, hprev, gamma, D)
        dynext = jnp.where(i == n - 1, jnp.zeros_like(dynext), dynext)
        dyext = jnp.concatenate([dyt, dynext], axis=0) * sc
        pos_ext = i * tm + lax.broadcasted_iota(jnp.int32, (tm + HALO, 1), 0)
        dws, dscs, dus = [], [], []
        for gi, wd in enumerate(POOL_WINDOWS):
            cols = slice(gi * cg, (gi + 1) * cg)
            pb = pooled[gi].astype(BF16)
            ypre = _dot(pb, wv[gi])
            dscs.append(jnp.sum(dyt[:, cols] * ypre, axis=0, keepdims=True))
            dws.append(_dot_tn(pb, dyext[:tm, cols])[None])
            dp = _dot_nt(dyext[:, cols], wv[gi])
            s = dp / _pool_counts(pos_ext, wd)
            span = 1
            while span < wd:
                s = s + pltpu.roll(s, tm + HALO - span, 0)
                span *= 2
            dus.append(s[:tm] - dp[:tm])
        du = jnp.concatenate(dus, axis=1)
        pos = pos_ext[:tm]
        du = jnp.where(pos >= FRONT_PAD, du, 0.0)
        dx, dgam = _rms_bwd(du, xh, r, gamma)
        return [dyt + dx], [jnp.concatenate(dws, axis=0), jnp.concatenate(dscs, axis=1), dgam]

    return _row_call("pool_bwd", body, L, [(h, "tile"), (h, "prev"), (dy, "tile"), (dy, "next")],
                     [gam, w, scale], [(D, F32)], [(ng, cg, cg), (1, D), (1, D)])


def _kv_fwd(h, tabs, g1, wdkv, g2, wuk, wuv):
    L, D = h.shape
    hw = N_HEADS * HEAD_LANES

    def body(i, n, tm, ht, ck, s1, s2, g1v, wdkv_v, g2v, wuk_v, wuv_v):
        xkv, _, _ = _rms(ht, g1v)
        ckr = _dot(xkv, wdkv_v)
        ckv, _, _ = _rms(ckr[:, :KV_RANK], g2v)
        krope = _rope(ckr[:, KV_RANK:], ck, s1, s2)
        pos = i * tm + lax.broadcasted_iota(jnp.int32, (tm, HEAD_LANES), 0)
        lane = lax.broadcasted_iota(jnp.int32, (tm, HEAD_LANES), 1)
        krope = jnp.where((pos < FRONT_PAD) & (lane == BIAS_LANE), NEG, krope)
        return [_dot(ckv, wuk_v), krope, _dot(ckv, wuv_v), ckr], []

    ck, s1, s2 = tabs["ck"], tabs["s1"], tabs["s2"]
    return _row_call("kv_fwd", body, L, [(h, "tile"), (ck, "tile"), (s1, "tile"), (s2, "tile")],
                     [g1, wdkv, g2, wuk, wuv],
                     [(hw, BF16), (HEAD_LANES, BF16), (hw, BF16), (KV_RANK + HEAD_LANES, F32)], [])


def _kv_bwd(dh, h, ckr, dks, dvs, tabs, g1, wdkv, g2, wuk, wuv):
    L, D = h.shape
    hw = N_HEADS * HEAD_LANES
    nl = len(dks)

    def body(i, n, tm, *vals):
        dht, ht, ckr_t = vals[:3]
        dk = sum(vals[3:3 + nl][1:], vals[3])
        dv = sum(vals[3 + nl:3 + 2 * nl][1:], vals[3 + nl])
        ck, s1, s2, g1v, wdkv_v, g2v, wuk_v, wuv_v = vals[3 + 2 * nl:]
        xkv, xh1, r1 = _rms(ht, g1v)
        ckv, xh2, r2 = _rms(ckr_t[:, :KV_RANK], g2v)
        dckv = _dot_nt(dk, wuk_v) + _dot_nt(dv, wuv_v)
        dlat, dg2 = _rms_bwd(dckv, xh2, r2, g2v)
        dkr = dk[:, :HEAD_LANES]
        for hd in range(1, N_HEADS):
            dkr = dkr + dk[:, hd * HEAD_LANES:(hd + 1) * HEAD_LANES]
        dckr = jnp.concatenate([dlat, _rope_t(dkr, ck, s1, s2)], axis=1)
        dx, dg1 = _rms_bwd(_dot_nt(dckr, wdkv_v), xh1, r1, g1v)
        return [dht + dx], [_dot_tn(xkv, dckr), _dot_tn(ckv, dk), _dot_tn(ckv, dv), dg1, dg2]

    row_ins = [(dh, "tile"), (h, "tile"), (ckr, "tile")] + [(a, "tile") for a in dks + dvs]
    row_ins += [(tabs[k], "tile") for k in ("ck", "s1", "s2")]
    return _row_call("kv_bwd", body, L, row_ins, [g1, wdkv, g2, wuk, wuv], [(D, F32)],
                     [(D, KV_RANK + HEAD_LANES), (KV_RANK, hw), (KV_RANK, hw), (1, D), (1, KV_RANK)])


def _q_fwd(h, tabs, g, wdq, gq, wuq):
    L, D = h.shape
    hw = N_HEADS * HEAD_LANES

    def body(i, n, tm, ht, cq_t, s1, s2, gv, wdq_v, gqv, wuq_v):
        u, _, _ = _rms(ht, gv)
        cqp = _dot(u, wdq_v)
        cq, _, _ = _rms(cqp, gqv)
        qp = _dot(cq, wuq_v)
        bias = (lax.broadcasted_iota(jnp.int32, (1, HEAD_LANES), 1) == BIAS_LANE).astype(F32)
        q = [_rope(qp[:, hd * HEAD_LANES:(hd + 1) * HEAD_LANES], cq_t, s1, s2) + bias for hd in range(N_HEADS)]
        return [jnp.concatenate(q, axis=1), cqp], []

    return _row_call("q_fwd", body, L, [(h, "tile")] + [(tabs[k], "tile") for k in ("cq", "s1", "s2")],
                     [g, wdq, gq, wuq], [(hw, BF16), (Q_RANK, F32)], [])


def _q_bwd(dh, h, cqp, dq, tabs, g, wdq, gq, wuq):
    L, D = h.shape
    hw = N_HEADS * HEAD_LANES

    def body(i, n, tm, dht, ht, cqp_t, dq_t, cq_t, s1, s2, gv, wdq_v, gqv, wuq_v):
        u, xh1, r1 = _rms(ht, gv)
        cq, xh2, r2 = _rms(cqp_t, gqv)
        dqp = jnp.concatenate([_rope_t(dq_t[:, hd * HEAD_LANES:(hd + 1) * HEAD_LANES], cq_t, s1, s2)
                               for hd in range(N_HEADS)], axis=1)
        dcqp, dgq = _rms_bwd(_dot_nt(dqp, wuq_v), xh2, r2, gqv)
        dx, dg = _rms_bwd(_dot_nt(dcqp, wdq_v), xh1, r1, gv)
        return [dht + dx], [_dot_tn(u, dcqp), _dot_tn(cq, dqp), dg, dgq]

    row_ins = [(dh, "tile"), (h, "tile"), (cqp, "tile"), (dq, "tile")]
    row_ins += [(tabs[k], "tile") for k in ("cq", "s1", "s2")]
    return _row_call("q_bwd", body, L, row_ins, [g, wdq, gq, wuq], [(D, F32)],
                     [(D, Q_RANK), (Q_RANK, hw), (1, D), (1, Q_RANK)])


def _oproj_fwd(h, o, wo):
    L, D = h.shape

    def body(i, n, tm, ht, ot, wov):
        return [ht + _dot(ot, wov)], []

    return _row_call("oproj_fwd", body, L, [(h, "tile"), (o, "tile")], [wo], [(D, F32)], [])[0]


def _oproj_bwd(dh, o, wo):
    L, D = dh.shape
    hw = N_HEADS * HEAD_LANES

    def body(i, n, tm, dht, ot, wov):
        do = _dot_nt(dht, wov)
        prod = do * ot.astype(F32)
        delta = [jnp.broadcast_to(jnp.sum(prod[:, hd * HEAD_LANES:(hd + 1) * HEAD_LANES], axis=-1, keepdims=True),
                                  (tm, HEAD_LANES)) for hd in range(N_HEADS)]
        return [do, jnp.concatenate(delta, axis=1)], [_dot_tn(ot, dht)]

    return _row_call("oproj_bwd", body, L, [(dh, "tile"), (o, "tile")], [wo], [(hw, BF16), (hw, F32)], [(hw, D)])


def _causal(t):
    qpos = lax.broadcasted_iota(jnp.int32, (t, t), 0)
    kpos = lax.broadcasted_iota(jnp.int32, (t, t), 1)
    return (kpos >> CHUNK_SHIFT) <= (qpos >> CHUNK_SHIFT)


SM_SCALE = 1.0 / math.sqrt(QK_NOPE + QK_ROPE)


def _pairs(n, key_major):
    if key_major:
        order = [(i, j) for j in range(n) for i in range(j, n)]
    else:
        order = [(i, j) for i in range(n) for j in range(i + 1)]
    return (jnp.array([p[0] for p in order], jnp.int32), jnp.array([p[1] for p in order], jnp.int32))


def _attn_fwd(q, kn, kr, v):
    L = q.shape[0]
    hw = N_HEADS * HEAD_LANES
    t = _row_tile(L)
    it, jt = _pairs(L // t, key_major=False)

    def kern(it_ref, jt_ref, q_ref, kn_ref, kr_ref, v_ref, o_ref, lse_ref, m_s, l_s, acc_s):
        step = pl.program_id(1)
        i, j = it_ref[step], jt_ref[step]

        @pl.when(j == 0)
        def _():
            m_s[...] = jnp.full(m_s.shape, NEG, F32)
            l_s[...] = jnp.zeros(l_s.shape, F32)
            acc_s[...] = jnp.zeros(acc_s.shape, F32)

        def update(diagonal):
            k = kn_ref[...] + kr_ref[...]
            s = _dot_nt(q_ref[...], k) * SM_SCALE
            if diagonal:
                s = jnp.where(_causal(t), s, NEG)
            m_prev = m_s[:, :1]
            m_new = jnp.maximum(m_prev, jnp.max(s, axis=-1, keepdims=True))
            alpha = jnp.exp(m_prev - m_new)
            p = jnp.exp(s - m_new)
            l_s[...] = jnp.broadcast_to(alpha * l_s[:, :1] + jnp.sum(p, axis=-1, keepdims=True), l_s.shape)
            acc_s[...] = alpha * acc_s[...] + _dot(p, v_ref[...])
            m_s[...] = jnp.broadcast_to(m_new, m_s.shape)

        @pl.when(j < i)
        def _():
            update(False)

        @pl.when(j == i)
        def _():
            update(True)
            o_ref[...] = (acc_s[...] / l_s[...]).astype(BF16)
            lse_ref[...] = m_s[...] + jnp.log(l_s[...])

    qmap = lambda h, s, it, jt: (it[s], h)
    kmap = lambda h, s, it, jt: (jt[s], h)
    blk = (t, HEAD_LANES)
    return pl.pallas_call(
        kern, name="attn_fwd",
        grid_spec=pltpu.PrefetchScalarGridSpec(
            num_scalar_prefetch=2, grid=(N_HEADS, it.shape[0]),
            in_specs=[pl.BlockSpec(blk, qmap), pl.BlockSpec(blk, kmap),
                      pl.BlockSpec(blk, lambda h, s, it, jt: (jt[s], 0)), pl.BlockSpec(blk, kmap)],
            out_specs=[pl.BlockSpec(blk, qmap), pl.BlockSpec(blk, qmap)],
            scratch_shapes=[pltpu.VMEM(blk, F32)] * 3),
        out_shape=[jax.ShapeDtypeStruct((L, hw), BF16), jax.ShapeDtypeStruct((L, hw), F32)],
        compiler_params=_params("arbitrary", "arbitrary"))(it, jt, q, kn, kr, v)


def _attn_bwd(q, kn, kr, v, do, lse, delta):
    L = q.shape[0]
    hw = N_HEADS * HEAD_LANES
    t = _row_tile(L)
    it, jt = _pairs(L // t, key_major=True)

    def kern(it_ref, jt_ref, q_ref, kn_ref, kr_ref, v_ref, do_ref, lse_ref, dl_ref, dq_ref, dk_ref, dv_ref):
        step = pl.program_id(1)
        i, j = it_ref[step], jt_ref[step]

        @pl.when(step == 0)
        def _():
            dq_ref[...] = jnp.zeros(dq_ref.shape, F32)

        @pl.when(i == j)
        def _():
            dk_ref[...] = jnp.zeros(dk_ref.shape, F32)
            dv_ref[...] = jnp.zeros(dv_ref.shape, F32)

        def update(diagonal):
            k = kn_ref[...] + kr_ref[...]
            qv, dov = q_ref[...], do_ref[...]
            s = _dot_nt(qv, k) * SM_SCALE
            if diagonal:
                s = jnp.where(_causal(t), s, NEG)
            p = jnp.exp(s - lse_ref[:, :1])
            dp = _dot_nt(dov, v_ref[...])
            ds = (p * (dp - dl_ref[:, :1]) * SM_SCALE).astype(BF16)
            dv_ref[...] += _dot_tn(p, dov)
            dk_ref[...] += _dot_tn(ds, qv)
            rows = pl.ds(pl.multiple_of(i * t, t), t)
            dq_ref[rows, :] += _dot(ds, k)

        @pl.when(j < i)
        def _():
            update(False)

        @pl.when(j == i)
        def _():
            update(True)

    qmap = lambda h, s, it, jt: (it[s], h)
    kmap = lambda h, s, it, jt: (jt[s], h)
    blk = (t, HEAD_LANES)
    return pl.pallas_call(
        kern, name="attn_bwd",
        grid_spec=pltpu.PrefetchScalarGridSpec(
            num_scalar_prefetch=2, grid=(N_HEADS, it.shape[0]),
            in_specs=[pl.BlockSpec(blk, qmap), pl.BlockSpec(blk, kmap),
                      pl.BlockSpec(blk, lambda h, s, it, jt: (jt[s], 0)), pl.BlockSpec(blk, kmap),
                      pl.BlockSpec(blk, qmap), pl.BlockSpec(blk, qmap), pl.BlockSpec(blk, qmap)],
            out_specs=[pl.BlockSpec((L, HEAD_LANES), lambda h, s, it, jt: (0, h)),
                       pl.BlockSpec(blk, kmap), pl.BlockSpec(blk, kmap)]),
        out_shape=[jax.ShapeDtypeStruct((L, hw), F32)] * 3,
        compiler_params=_params("arbitrary", "arbitrary"))(it, jt, q, kn, kr, v, do, lse, delta)


def _head(h, target, g):
    L, D = h.shape

    def body(i, n, tm, ht, tt, gv):
        y, xh, r = _rms(ht, gv)
        pos = i * tm + lax.broadcasted_iota(jnp.int32, (tm, 1), 0)
        e = jnp.where(pos >= SEQ_START, y - tt, 0.0)
        loss = 0.5 * jnp.sum(jnp.mean(e * e, axis=-1, keepdims=True), axis=0, keepdims=True)
        dx, dg = _rms_bwd(e / D, xh, r, gv)
        return [dx], [jnp.broadcast_to(loss, (1, 128)), dg]

    return _row_call("loss_head", body, L, [(h, "tile"), (target, "tile")], [g], [(D, F32)], [(1, 128), (1, D)])


ANY = pl.BlockSpec(memory_space=pl.ANY)


def _coords():
    return lax.axis_index("x"), lax.axis_index("y"), lax.axis_index("c")


def _all_gather(x):
    R, W = x.shape

    def kern(x_ref, out_ref, send_sems, recv_sems, local_sem):
        mx, my, mc = _coords()
        me, sibling = (mx, my, mc), (mx, my, 1 - mc)
        chips = [(1 - mx, my), (mx, 1 - my), (1 - mx, 1 - my)]

        def slot(px, py, pc):
            return out_ref.at[4 * px + 2 * py + pc]

        def copy(k, block, to, src=None):
            return pltpu.make_async_remote_copy(
                src_ref=slot(*block) if src is None else src, dst_ref=slot(*block),
                send_sem=send_sems.at[k], recv_sem=recv_sems.at[k], device_id=to, device_id_type=MESH)

        mine = pltpu.make_async_copy(x_ref, slot(*me), local_sem)
        mine.start()
        first = [copy(0, me, sibling, src=x_ref)]
        first += [copy(1 + n, me, (*chip, mc), src=x_ref) for n, chip in enumerate(chips)]
        for cp in first:
            cp.start()
        passed = [copy(4 + n, (*chip, mc), sibling) for n, chip in enumerate(chips)]
        for n, chip in enumerate(chips):
            copy(1 + n, (*chip, mc), me).wait_recv()
            passed[n].start()
        copy(0, sibling, me).wait_recv()
        for n, chip in enumerate(chips):
            copy(4 + n, (*chip, 1 - mc), me).wait_recv()
        for cp in first + passed:
            cp.wait_send()
        mine.wait()

    return pl.pallas_call(
        kern, name="all_gather_%s" % jnp.dtype(x.dtype).name, in_specs=[ANY], out_specs=ANY,
        out_shape=jax.ShapeDtypeStruct((N_DEV, R, W), x.dtype),
        scratch_shapes=[pltpu.SemaphoreType.DMA((7,)), pltpu.SemaphoreType.DMA((7,)), pltpu.SemaphoreType.DMA],
    )(x)


def _sibling_exchange(g):
    _, _, R, W = g.shape

    def kern(g_ref, out_ref, send_sems, recv_sems):
        mx, my, mc = _coords()
        copies = [pltpu.make_async_remote_copy(
            src_ref=g_ref.at[n, 1 - mc], dst_ref=out_ref.at[n], send_sem=send_sems.at[n],
            recv_sem=recv_sems.at[n], device_id=(mx, my, 1 - mc), device_id_type=MESH) for n in range(N_CHIPS)]
        for cp in copies:
            cp.start()
        for cp in copies:
            cp.wait_recv()
        for cp in copies:
            cp.wait_send()

    return pl.pallas_call(
        kern, name="sibling_exchange", in_specs=[ANY], out_specs=ANY,
        out_shape=jax.ShapeDtypeStruct((N_CHIPS, R, W), g.dtype),
        scratch_shapes=[pltpu.SemaphoreType.DMA((N_CHIPS,)), pltpu.SemaphoreType.DMA((N_CHIPS,))],
    )(g)


def _chip_exchange(p):
    _, R, W = p.shape

    def kern(p_ref, out_ref, send_sems, recv_sems):
        mx, my, mc = _coords()
        chips = [(1 - mx, my), (mx, 1 - my), (1 - mx, 1 - my)]
        copies = [pltpu.make_async_remote_copy(
            src_ref=p_ref.at[2 * cx + cy], dst_ref=out_ref.at[n], send_sem=send_sems.at[n],
            recv_sem=recv_sems.at[n], device_id=(cx, cy, mc), device_id_type=MESH)
            for n, (cx, cy) in enumerate(chips)]
        for cp in copies:
            cp.start()
        for cp in copies:
            cp.wait_recv()
        for cp in copies:
            cp.wait_send()

    return pl.pallas_call(
        kern, name="chip_exchange", in_specs=[ANY], out_specs=ANY,
        out_shape=jax.ShapeDtypeStruct((3, R, W), p.dtype),
        scratch_shapes=[pltpu.SemaphoreType.DMA((3,)), pltpu.SemaphoreType.DMA((3,))],
    )(p)


def _add_own(own, sel, other, name):
    R, W = own.shape[-2:]
    tr = _pick(R, (PACK_ROW_MULT, 8))
    if own.ndim == 4:
        n = own.shape[0]
        grid = (n, R // tr)
        in_specs = [pl.BlockSpec((None, None, tr, W), lambda b, i, sel: (b, sel[0], i, 0)),
                    pl.BlockSpec((None, tr, W), lambda b, i, sel: (b, i, 0))]
        out_spec = pl.BlockSpec((None, tr, W), lambda b, i, sel: (b, i, 0))
        out_shape = jax.ShapeDtypeStruct((n, R, W), F32)

        def kern(sel_ref, own_ref, other_ref, o_ref):
            o_ref[...] = own_ref[...] + other_ref[...]
    else:
        k = other.shape[0]
        grid = (1, R // tr)
        in_specs = [pl.BlockSpec((None, tr, W), lambda b, i, sel: (sel[0], i, 0)),
                    pl.BlockSpec((k, tr, W), lambda b, i, sel: (0, i, 0))]
        out_spec = pl.BlockSpec((tr, W), lambda b, i, sel: (i, 0))
        out_shape = jax.ShapeDtypeStruct((R, W), F32)

        def kern(sel_ref, own_ref, other_ref, o_ref):
            acc = own_ref[...]
            for m in range(k):
                acc = acc + other_ref[m]
            o_ref[...] = acc

    return pl.pallas_call(
        kern, name=name,
        grid_spec=pltpu.PrefetchScalarGridSpec(num_scalar_prefetch=1, grid=grid, in_specs=in_specs,
                                               out_specs=out_spec),
        out_shape=out_shape, compiler_params=_params("arbitrary", "arbitrary"))(sel, own, other)


def _sum_lead(x, name):
    n, R, W = x.shape
    tr = _pick(R, (PACK_ROW_MULT, 8))

    def kern(x_ref, o_ref):
        acc = x_ref[0]
        for k in range(1, n):
            acc = acc + x_ref[k]
        o_ref[...] = acc

    return pl.pallas_call(
        kern, name=name, grid=(R // tr,),
        in_specs=[pl.BlockSpec((n, tr, W), lambda i: (0, i, 0))],
        out_specs=pl.BlockSpec((tr, W), lambda i: (i, 0)),
        out_shape=jax.ShapeDtypeStruct((R, W), F32), compiler_params=_params("arbitrary"))(x)


def _adamw(w, g, m, v):
    shape = w.shape
    cols = shape[-1]
    rows = w.size // cols
    tr = _pick(rows, (512, 352, 256, 128))
    if rows * cols * 4 <= (1 << 20):
        tr = rows

    def kern(w_ref, g_ref, m_ref, v_ref, d_ref, mo_ref, vo_ref):
        gv = g_ref[...]
        mn = ADAM_B1 * m_ref[...] + (1.0 - ADAM_B1) * gv
        vn = ADAM_B2 * v_ref[...] + (1.0 - ADAM_B2) * (gv * gv)
        m_hat = mn / (1.0 - ADAM_B1 ** ADAM_STEP)
        v_hat = vn / (1.0 - ADAM_B2 ** ADAM_STEP)
        d_ref[...] = -ADAM_LR * (m_hat / (jnp.sqrt(v_hat) + ADAM_EPS) + ADAM_WD * w_ref[...])
        mo_ref[...] = mn
        vo_ref[...] = vn

    spec = pl.BlockSpec((tr, cols), lambda i: (i, 0))
    outs = pl.pallas_call(
        kern, name="adamw", grid=(rows // tr,), in_specs=[spec] * 4, out_specs=[spec] * 3,
        out_shape=[jax.ShapeDtypeStruct((rows, cols), F32)] * 3, compiler_params=_params("arbitrary"),
    )(*[a.reshape(rows, cols) for a in (w, g, m, v)])
    return [o.reshape(shape) for o in outs]


def _pack(arrs, n_lead, row_mult):
    parts = []
    for a in arrs:
        lead = a.shape[:n_lead]
        flat = a.reshape(lead + (-1,))
        rows = -(-flat.shape[-1] // (PACK_W * row_mult)) * row_mult
        flat = jnp.pad(flat, [(0, 0)] * n_lead + [(0, rows * PACK_W - flat.shape[-1])])
        parts.append(flat.reshape(lead + (rows, PACK_W)))
    out = jnp.concatenate(parts, axis=n_lead)
    pad = -out.shape[n_lead] % PACK_ROW_MULT
    return jnp.pad(out, [(0, 0)] * n_lead + [(0, pad), (0, 0)])


def _unpack(pack, shapes, n_lead, row_mult):
    outs, row = [], 0
    lead = pack.shape[:n_lead]
    for shp in shapes:
        size = math.prod(shp)
        rows = -(-size // (PACK_W * row_mult)) * row_mult
        blk = lax.slice_in_dim(pack, row, row + rows, axis=n_lead)
        outs.append(blk.reshape(lead + (-1,))[..., :size].reshape(lead + tuple(shp)))
        row += rows
    return outs


def _to_words(a):
    return lax.bitcast_convert_type(a, BF16)


def _from_words(a):
    return lax.bitcast_convert_type(a, F32)


def _pad_axis(a, axis, size):
    pads = [(0, 0)] * a.ndim
    pads[axis] = (0, size - a.shape[axis])
    return jnp.pad(a, pads)


def _dense(name, s):
    if name.endswith("w_gate") or name.endswith("w_up"):
        _, nl, d, fs = s.shape
        return s.transpose(1, 2, 0, 3).reshape(nl, d, N_DEV * fs)
    if name.endswith("w_down"):
        _, nl, fs, d = s.shape
        return s.transpose(1, 0, 2, 3).reshape(nl, N_DEV * fs, d)
    if name == "pool_w":
        _, nl, ng, r, cg = s.shape
        return s.transpose(1, 2, 0, 3, 4).reshape(nl, ng, cg, cg)
    if name == "w_dkv":
        w = s.reshape(-1, s.shape[2])
        z = lambda n: jnp.zeros((w.shape[0], n), w.dtype)
        return jnp.concatenate([w[:, :KV_RANK], z(ROPE_LANE0), w[:, KV_RANK:],
                                z(HEAD_LANES - ROPE_LANE0 - QK_ROPE)], axis=1)
    if name in ("w_uk", "w_uv"):
        return _pad_axis(s.transpose(1, 0, 2), 2, HEAD_LANES).reshape(KV_RANK, N_HEADS * HEAD_LANES)
    if name == "w_dq":
        _, nl, ds, r = s.shape
        return s.transpose(1, 0, 2, 3).reshape(nl, N_DEV * ds, r)
    if name == "w_uq":
        nl = s.shape[1]
        return _pad_axis(s.transpose(1, 2, 0, 3), 3, HEAD_LANES).reshape(nl, Q_RANK, N_HEADS * HEAD_LANES)
    if name == "w_o":
        _, nl, k, dc = s.shape
        w = s.transpose(1, 2, 0, 3).reshape(nl, N_HEADS, V_HEAD, N_DEV * dc)
        return _pad_axis(w, 2, HEAD_LANES).reshape(nl, N_HEADS * HEAD_LANES, N_DEV * dc)
    if name in ("meta_tokens", "pool_scale"):
        r, dc = s.shape[1:]
        return s.transpose(1, 0, 2).reshape(r, N_DEV * dc)
    raise ValueError(name)


def _shards(name, g):
    if name.endswith("w_gate") or name.endswith("w_up"):
        nl, d, f = g.shape
        return g.reshape(nl, d, N_DEV, f // N_DEV).transpose(2, 0, 1, 3)
    if name.endswith("w_down"):
        nl, f, d = g.shape
        return g.reshape(nl, N_DEV, f // N_DEV, d).transpose(1, 0, 2, 3)
    if name == "pool_w":
        nl, ng, cg, _ = g.shape
        return g.reshape(nl, ng, N_DEV, cg // N_DEV, cg).transpose(2, 0, 1, 3, 4)
    if name == "w_dkv":
        w = jnp.concatenate([g[:, :KV_RANK], g[:, KV_RANK + ROPE_LANE0:KV_RANK + ROPE_LANE0 + QK_ROPE]], axis=1)
        return w.reshape(N_DEV, -1, KV_RANK + QK_ROPE)
    if name in ("w_uk", "w_uv"):
        return g.reshape(KV_RANK, N_HEADS, HEAD_LANES)[:, :, :V_HEAD].transpose(1, 0, 2)
    if name == "w_dq":
        nl, d, r = g.shape
        return g.reshape(nl, N_DEV, d // N_DEV, r).transpose(1, 0, 2, 3)
    if name == "w_uq":
        nl = g.shape[0]
        return g.reshape(nl, Q_RANK, N_HEADS, HEAD_LANES)[..., :QK_NOPE + QK_ROPE].transpose(2, 0, 1, 3)
    if name == "w_o":
        nl, _, d = g.shape
        w = g.reshape(nl, N_HEADS, HEAD_LANES, d)[:, :, :V_HEAD].reshape(nl, N_HEADS * V_HEAD, N_DEV, d // N_DEV)
        return w.transpose(2, 0, 1, 3)
    if name in ("meta_tokens", "pool_scale"):
        r, d = g.shape
        return g.reshape(r, N_DEV, d // N_DEV).transpose(1, 0, 2)
    raise ValueError(name)


def _rope_tables(L):
    pos = jnp.maximum(jnp.arange(L) - FRONT_PAD, 0).astype(F32)
    inv = 1.0 / (ROPE_THETA ** (jnp.arange(0, QK_ROPE, 2, dtype=F32) / QK_ROPE))
    ang = pos[:, None] * inv[None, :]
    cos, sin = jnp.cos(ang), jnp.sin(ang)
    half = QK_ROPE // 2
    z = lambda n: jnp.zeros((L, n), F32)
    tail = z(HEAD_LANES - ROPE_LANE0 - QK_ROPE)
    return {
        "cq": jnp.concatenate([jnp.ones((L, ROPE_LANE0), F32), cos, cos, tail], axis=1),
        "ck": jnp.concatenate([z(ROPE_LANE0), cos, cos, tail], axis=1),
        "s1": jnp.concatenate([z(ROPE_LANE0), -sin, z(half), tail], axis=1),
        "s2": jnp.concatenate([z(ROPE_LANE0), z(half), sin, tail], axis=1),
    }


def kernel(x, meta_tokens, ffn1_norm, ffn1_w_gate, ffn1_w_up, ffn1_w_down, mix_norm, ffn2_norm, ffn2_w_gate, ffn2_w_up, ffn2_w_down, pool_w, pool_scale, kv_in_norm, w_dkv, kv_latent_norm, w_uk, w_uv, w_dq, q_latent_norm, w_uq, w_o, final_norm, loss_target, m_meta_tokens, m_ffn1_norm, m_ffn1_w_gate, m_ffn1_w_up, m_ffn1_w_down, m_mix_norm, m_ffn2_norm, m_ffn2_w_gate, m_ffn2_w_up, m_ffn2_w_down, m_pool_w, m_pool_scale, m_kv_in_norm, m_w_dkv, m_kv_latent_norm, m_w_uk, m_w_uv, m_w_dq, m_q_latent_norm, m_w_uq, m_w_o, m_final_norm, v_meta_tokens, v_ffn1_norm, v_ffn1_w_gate, v_ffn1_w_up, v_ffn1_w_down, v_mix_norm, v_ffn2_norm, v_ffn2_w_gate, v_ffn2_w_up, v_ffn2_w_down, v_pool_w, v_pool_scale, v_kv_in_norm, v_w_dkv, v_kv_latent_norm, v_w_uk, v_w_uv, v_w_dq, v_q_latent_norm, v_w_uq, v_w_o, v_final_norm):
    args = dict(locals())
    W = {n: args[n] for n in WEIGHTS}
    M = {n: args["m_" + n] for n in WEIGHTS}
    V = {n: args["v_" + n] for n in WEIGHTS}
    seq, D = x.shape[1], x.shape[2]
    L = SEQ_START + seq

    shard_shapes = [W[n].shape + ((2,) if n in SHARDED_F32 else ()) for n in SHARDED]
    wpack = _pack([_to_words(W[n]) if n in SHARDED_F32 else W[n].astype(BF16) for n in SHARDED], 0, 16)
    gathered = _unpack(_all_gather(wpack), shard_shapes, 1, 16)
    P = {}
    for n, s in zip(SHARDED, gathered):
        P[n] = _dense(n, _from_words(s) if n in SHARDED_F32 else s)
    norm3 = lambda a: a.reshape(a.shape[0], 1, a.shape[-1])
    row = lambda a: a.reshape(1, -1)
    g_ffn1, g_mix, g_ffn2 = norm3(ffn1_norm), mix_norm, norm3(ffn2_norm)

    h = jnp.concatenate([jnp.zeros((FRONT_PAD, D), F32), P["meta_tokens"], x[0]], axis=0)
    target = jnp.concatenate([jnp.zeros((SEQ_START, D), F32), loss_target[0]], axis=0)
    tabs = _rope_tables(L)
    saved = []
    kv = None
    for l in range(DEPTH):
        s = {"h1": h}
        h, s["xn1"], s["g1"], s["u1"] = _ffn_fwd(h, g_ffn1, P["ffn1_w_gate"], P["ffn1_w_up"], P["ffn1_w_down"], l)
        s["hm"] = h
        if l < N_POOL_LAYERS:
            h = _pool_fwd(h, row(g_mix[l]), P["pool_w"][l], row(P["pool_scale"][l]), l)
        else:
            j = l - N_POOL_LAYERS
            s["q"], s["cqp"] = _q_fwd(h, tabs, row(g_mix[l]), P["w_dq"][j], row(q_latent_norm[j]), P["w_uq"][j])
            s["o"], s["lse"] = _attn_fwd(s["q"], kv["kn"], kv["kr"], kv["v"])
            h = _oproj_fwd(h, s["o"], P["w_o"][j])
        s["h2"] = h
        h, s["xn2"], s["g2"], s["u2"] = _ffn_fwd(h, g_ffn2, P["ffn2_w_gate"], P["ffn2_w_up"], P["ffn2_w_down"], l)
        if l == N_POOL_LAYERS - 1:
            kv = {"h": h}
            kv["kn"], kv["kr"], kv["v"], kv["ckr"] = _kv_fwd(
                h, tabs, row(kv_in_norm), P["w_dkv"], row(kv_latent_norm), P["w_uk"], P["w_uv"])
        saved.append(s)
    dh, loss_row, d_final = _head(h, target, row(final_norm))
    loss = lax.psum(loss_row[0, 0], ("x", "y", "c"))

    G = {}
    stack = {n: [None] * DEPTH for n in ("ffn1_norm", "ffn1_w_gate", "ffn1_w_up", "ffn1_w_down", "mix_norm",
                                         "ffn2_norm", "ffn2_w_gate", "ffn2_w_up", "ffn2_w_down")}
    pool_dw, pool_ds = [None] * N_POOL_LAYERS, [None] * N_POOL_LAYERS
    mla = {n: [None] * (DEPTH - N_POOL_LAYERS) for n in ("w_dq", "w_uq", "w_o", "q_latent_norm")}
    dks, dvs = [], []

    def ffn_backward(dh, s, which, gam, l):
        wg, wu, wd = P["ffn%d_w_gate" % which], P["ffn%d_w_up" % which], P["ffn%d_w_down" % which]
        dh, dg, du, act, dob, dgam = _ffn_bwd(dh, s["h%d" % which], s["g%d" % which], s["u%d" % which], gam, wg, wu, wd, l)
        xn = s["xn%d" % which]
        stack["ffn%d_w_gate" % which][l] = _mm_tn(xn, dg, "ffn_dw_in")
        stack["ffn%d_w_up" % which][l] = _mm_tn(xn, du, "ffn_dw_in")
        stack["ffn%d_w_down" % which][l] = _mm_tn(act, dob, "ffn_dw_down")
        stack["ffn%d_norm" % which][l] = dgam
        return dh

    for l in reversed(range(DEPTH)):
        s = saved[l]
        if l == N_POOL_LAYERS - 1:
            dh, d_dkv, d_uk, d_uv, d_kvin, d_kvlat = _kv_bwd(
                dh, kv["h"], kv["ckr"], dks, dvs, tabs, row(kv_in_norm), P["w_dkv"], row(kv_latent_norm),
                P["w_uk"], P["w_uv"])
            G.update(w_dkv=d_dkv, w_uk=d_uk, w_uv=d_uv, kv_in_norm=d_kvin, kv_latent_norm=d_kvlat)
        dh = ffn_backward(dh, s, 2, g_ffn2, l)
        if l < N_POOL_LAYERS:
            dh, pool_dw[l], pool_ds[l], stack["mix_norm"][l] = _pool_bwd(
                dh, s["hm"], row(g_mix[l]), P["pool_w"][l], row(P["pool_scale"][l]))
        else:
            j = l - N_POOL_LAYERS
            do, delta_o, mla["w_o"][j] = _oproj_bwd(dh, s["o"], P["w_o"][j])
            dq, dk, dv = _attn_bwd(s["q"], kv["kn"], kv["kr"], kv["v"], do, s["lse"], delta_o)
            dks.append(dk)
            dvs.append(dv)
            dh, mla["w_dq"][j], mla["w_uq"][j], stack["mix_norm"][l], mla["q_latent_norm"][j] = _q_bwd(
                dh, s["hm"], s["cqp"], dq, tabs, row(g_mix[l]), P["w_dq"][j], row(q_latent_norm[j]), P["w_uq"][j])
        dh = ffn_backward(dh, s, 1, g_ffn1, l)
    grad_x = dh[SEQ_START:][None]
    for n, parts in stack.items():
        G[n] = jnp.stack(parts) if parts[0].ndim == 2 and parts[0].shape[0] > 1 else jnp.concatenate(parts, axis=0)
    G["pool_w"] = jnp.stack(pool_dw)
    G["pool_scale"] = jnp.concatenate(pool_ds, axis=0)
    G["w_dq"], G["w_uq"], G["w_o"] = (jnp.stack(mla[n]) for n in ("w_dq", "w_uq", "w_o"))
    G["q_latent_norm"] = jnp.concatenate(mla["q_latent_norm"], axis=0)
    G["meta_tokens"] = dh[FRONT_PAD:SEQ_START]
    G["final_norm"] = d_final

    gpack = _pack([_shards(n, G[n]) for n in SHARDED], 1, 8)
    R = gpack.shape[1]
    gpack = gpack.reshape(N_CHIPS, 2, R, PACK_W)
    my_core = lax.axis_index("c").astype(jnp.int32).reshape(1)
    my_chip = (2 * lax.axis_index("x") + lax.axis_index("y")).astype(jnp.int32).reshape(1)
    partial = _add_own(gpack, my_core, _sibling_exchange(gpack), "sum_cores")
    mine = _add_own(partial, my_chip, _chip_exchange(partial), "sum_chips")
    local_shapes = [W[n].shape for n in SHARDED]
    grads = dict(zip(SHARDED, _unpack(mine, local_shapes, 0, 8)))
    rep_shapes = [W[n].shape for n in REPLICATED]
    rpack = _pack([G[n].reshape(W[n].shape) for n in REPLICATED], 0, 8)
    grads.update(zip(REPLICATED, _unpack(_sum_lead(_all_gather(rpack), "sum_devices"), rep_shapes, 0, 8)))

    delta, new_m, new_v = {}, {}, {}
    for n in WEIGHTS:
        delta[n], new_m[n], new_v[n] = _adamw(W[n], grads[n], M[n], V[n])
    return (loss, grad_x, *[grads[n] for n in WEIGHTS], *[delta[n] for n in WEIGHTS],
            *[new_m[n] for n in WEIGHTS], *[new_v[n] for n in WEIGHTS])
```

```python
import functools
import math

import jax
import jax.numpy as jnp
from jax import lax
from jax.experimental import pallas as pl
from jax.experimental.pallas import tpu as pltpu

F32 = jnp.float32
BF16 = jnp.bfloat16
MESH = pl.DeviceIdType.MESH

N_DEV = 8
N_CHIPS = 4
DEPTH = 4
N_POOL_LAYERS = 2
N_HEADS = 8
QK_NOPE = 64
QK_ROPE = 32
V_HEAD = 64
KV_RANK = 256
Q_RANK = 384
HEAD_LANES = 128
ROPE_LANE0 = QK_NOPE
BIAS_LANE = QK_NOPE + QK_ROPE
ONES_LANE = V_HEAD
LOG2E = math.log2(math.e)
N_META = 16
CHUNK_SHIFT = 6
FRONT_PAD = 112
SEQ_START = FRONT_PAD + N_META
HALO = 16
POOL_WINDOWS = (2, 4, 8, 16)
EPS = 1e-6
ROPE_THETA = 10000.0
NEG = -1e30
PACK_W = 1024
PACK_ROW_MULT = 256
VMEM_LIMIT = 56 * 1024 * 1024

ADAM_LR = 0.001
ADAM_B1 = 0.9
ADAM_B2 = 0.999
ADAM_EPS = 1e-08
ADAM_WD = 0.01
ADAM_STEP = 10

SHARDED = ["ffn1_w_gate", "ffn1_w_up", "ffn1_w_down", "ffn2_w_gate", "ffn2_w_up", "ffn2_w_down",
           "pool_w", "w_dkv", "w_uk", "w_uv", "w_dq", "w_uq", "w_o", "meta_tokens", "pool_scale"]
SHARDED_F32 = ("meta_tokens", "pool_scale")
REPLICATED = ["ffn1_norm", "mix_norm", "ffn2_norm", "kv_in_norm", "kv_latent_norm", "q_latent_norm",
              "final_norm"]
WEIGHTS = ['meta_tokens', 'ffn1_norm', 'ffn1_w_gate', 'ffn1_w_up', 'ffn1_w_down', 'mix_norm', 'ffn2_norm',
           'ffn2_w_gate', 'ffn2_w_up', 'ffn2_w_down', 'pool_w', 'pool_scale', 'kv_in_norm', 'w_dkv',
           'kv_latent_norm', 'w_uk', 'w_uv', 'w_dq', 'q_latent_norm', 'w_uq', 'w_o', 'final_norm']


def _dot(a, b):
    return jnp.dot(a.astype(BF16), b.astype(BF16), preferred_element_type=F32)


def _dot_nt(a, b):
    return lax.dot_general(a.astype(BF16), b.astype(BF16), (((1,), (1,)), ((), ())),
                           preferred_element_type=F32)


def _dot_tn(a, b):
    return lax.dot_general(a.astype(BF16), b.astype(BF16), (((0,), (0,)), ((), ())),
                           preferred_element_type=F32)


def _sigmoid(x):
    return 1.0 / (1.0 + jnp.exp(-x))


def _rms(x, g):
    r = lax.rsqrt(jnp.mean(x * x, axis=-1, keepdims=True) + EPS)
    xh = x * r
    return xh * g, xh, r


def _rms_bwd(dy, xh, r, g):
    dxh = dy * g
    dx = r * (dxh - xh * jnp.mean(dxh * xh, axis=-1, keepdims=True))
    return dx, jnp.sum(dy * xh, axis=0, keepdims=True)


def _rope(x, c, s1, s2):
    return x * c + pltpu.roll(x, HEAD_LANES - QK_ROPE // 2, 1) * s1 + pltpu.roll(x, QK_ROPE // 2, 1) * s2


def _rope_t(d, c, s1, s2):
    return d * c + pltpu.roll(d * s1, QK_ROPE // 2, 1) + pltpu.roll(d * s2, HEAD_LANES - QK_ROPE // 2, 1)


def _params(*sem):
    return pltpu.CompilerParams(dimension_semantics=sem, vmem_limit_bytes=VMEM_LIMIT)


def _pick(n, candidates):
    for c in candidates:
        if n % c == 0:
            return c
    return n


def _row_tile(L):
    return _pick(L, (640, 128))


def _ff_tile(F):
    return _pick(F, (1408, 512, 256, 128))


def _row_call(name, body, L, row_ins, full_ins, row_outs, acc_outs):
    tm = _row_tile(L)
    n = L // tm
    hb = tm // HALO
    nb = L // HALO
    in_specs, args = [], []
    for arr, kind in row_ins:
        c = arr.shape[1]
        if kind == "tile":
            spec = pl.BlockSpec((tm, c), lambda i: (i, 0))
        elif kind == "prev":
            spec = pl.BlockSpec((HALO, c), lambda i: (jnp.maximum(i * hb - 1, 0), 0))
        else:
            spec = pl.BlockSpec((HALO, c), lambda i: (jnp.minimum((i + 1) * hb, nb - 1), 0))
        in_specs.append(spec)
        args.append(arr)
    for arr in full_ins:
        in_specs.append(pl.BlockSpec(arr.shape, lambda i, nd=arr.ndim: (0,) * nd))
        args.append(arr)
    out_shape = [jax.ShapeDtypeStruct((L, c), dt) for c, dt in row_outs]
    out_specs = [pl.BlockSpec((tm, c), lambda i: (i, 0)) for c, _ in row_outs]
    for shp in acc_outs:
        out_shape.append(jax.ShapeDtypeStruct(shp, F32))
        out_specs.append(pl.BlockSpec(shp, lambda i, nd=len(shp): (0,) * nd))
    n_in, n_ro = len(args), len(row_outs)

    def kern(*refs):
        i = pl.program_id(0)
        vals = [r[...] for r in refs[:n_in]]
        ro, ao = body(i, n, tm, *vals)
        for r, v in zip(refs[n_in:n_in + n_ro], ro):
            r[...] = v.astype(r.dtype)
        acc_refs = refs[n_in + n_ro:]

        @pl.when(i == 0)
        def _():
            for r in acc_refs:
                r[...] = jnp.zeros(r.shape, r.dtype)

        for r, v in zip(acc_refs, ao):
            r[...] += v

    return pl.pallas_call(kern, name=name, grid=(n,), in_specs=in_specs, out_specs=out_specs,
                          out_shape=out_shape, compiler_params=_params("arbitrary"))(*args)


def _ffn_fwd(h, gam, wg, wu, wd, l):
    L, D = h.shape
    F = wg.shape[2]
    tm = _row_tile(L)
    tf = _ff_tile(F)
    nF = F // tf

    def kern(h_ref, gam_ref, wg_ref, wu_ref, wd_ref, ho_ref, xn_ref, gs_ref, us_ref, acc):
        f = pl.program_id(1)

        @pl.when(f == 0)
        def _():
            xn, _, _ = _rms(h_ref[...], gam_ref[...])
            xn_ref[...] = xn.astype(BF16)
            acc[...] = jnp.zeros(acc.shape, F32)

        xnb = xn_ref[...]
        g = _dot(xnb, wg_ref[...])
        u = _dot(xnb, wu_ref[...])
        gs_ref[...] = g.astype(BF16)
        us_ref[...] = u.astype(BF16)
        acc[...] += _dot(g * _sigmoid(g) * u, wd_ref[...])

        @pl.when(f == nF - 1)
        def _():
            ho_ref[...] = h_ref[...] + 0.5 * acc[...]

    return pl.pallas_call(
        kern, name="ffn_fwd", grid=(L // tm, nF),
        in_specs=[pl.BlockSpec((tm, D), lambda i, f: (i, 0)),
                  pl.BlockSpec((None, 1, D), lambda i, f: (l, 0, 0)),
                  pl.BlockSpec((None, D, tf), lambda i, f: (l, 0, f)),
                  pl.BlockSpec((None, D, tf), lambda i, f: (l, 0, f)),
                  pl.BlockSpec((None, tf, D), lambda i, f: (l, f, 0))],
        out_specs=[pl.BlockSpec((tm, D), lambda i, f: (i, 0)),
                   pl.BlockSpec((tm, D), lambda i, f: (i, 0)),
                   pl.BlockSpec((tm, tf), lambda i, f: (i, f)),
                   pl.BlockSpec((tm, tf), lambda i, f: (i, f))],
        out_shape=[jax.ShapeDtypeStruct((L, D), F32), jax.ShapeDtypeStruct((L, D), BF16),
                   jax.ShapeDtypeStruct((L, F), BF16), jax.ShapeDtypeStruct((L, F), BF16)],
        scratch_shapes=[pltpu.VMEM((tm, D), F32)],
        compiler_params=_params("arbitrary", "arbitrary"))(h, gam, wg, wu, wd)


def _ffn_bwd(dh, h, gs, us, gam, wg, wu, wd, l):
    L, D = h.shape
    F = wg.shape[2]
    tm = _pick(L, (416, 128))
    tf = _ff_tile(F)
    nF = F // tf

    def kern(dh_ref, h_ref, gs_ref, us_ref, gam_ref, wg_ref, wu_ref, wd_ref,
             dhi_ref, dg_ref, du_ref, a_ref, dob_ref, dgam_ref, dxn):
        i = pl.program_id(0)
        f = pl.program_id(1)

        @pl.when(f == 0)
        def _():
            dxn[...] = jnp.zeros(dxn.shape, F32)
            dob_ref[...] = (0.5 * dh_ref[...]).astype(BF16)

        @pl.when((f == 0) & (i == 0))
        def _():
            dgam_ref[...] = jnp.zeros(dgam_ref.shape, F32)

        g = gs_ref[...].astype(F32)
        u = us_ref[...].astype(F32)
        sg = _sigmoid(g)
        silu = g * sg
        da = _dot_nt(dob_ref[...], wd_ref[...])
        a_ref[...] = (silu * u).astype(BF16)
        dgt = (da * u * (sg * (1.0 + g * (1.0 - sg)))).astype(BF16)
        dut = (da * silu).astype(BF16)
        dg_ref[...] = dgt
        du_ref[...] = dut
        dxn[...] += _dot_nt(dgt, wg_ref[...]) + _dot_nt(dut, wu_ref[...])

        @pl.when(f == nF - 1)
        def _():
            gamma = gam_ref[...]
            _, xh, r = _rms(h_ref[...], gamma)
            dx, dgam = _rms_bwd(dxn[...], xh, r, gamma)
            dhi_ref[...] = dh_ref[...] + dx
            dgam_ref[...] += dgam

    return pl.pallas_call(
        kern, name="ffn_bwd", grid=(L // tm, nF),
        in_specs=[pl.BlockSpec((tm, D), lambda i, f: (i, 0)),
                  pl.BlockSpec((tm, D), lambda i, f: (i, 0)),
                  pl.BlockSpec((tm, tf), lambda i, f: (i, f)),
                  pl.BlockSpec((tm, tf), lambda i, f: (i, f)),
                  pl.BlockSpec((None, 1, D), lambda i, f: (l, 0, 0)),
                  pl.BlockSpec((None, D, tf), lambda i, f: (l, 0, f)),
                  pl.BlockSpec((None, D, tf), lambda i, f: (l, 0, f)),
                  pl.BlockSpec((None, tf, D), lambda i, f: (l, f, 0))],
        out_specs=[pl.BlockSpec((tm, D), lambda i, f: (i, 0)),
                   pl.BlockSpec((tm, tf), lambda i, f: (i, f)),
                   pl.BlockSpec((tm, tf), lambda i, f: (i, f)),
                   pl.BlockSpec((tm, tf), lambda i, f: (i, f)),
                   pl.BlockSpec((tm, D), lambda i, f: (i, 0)),
                   pl.BlockSpec((1, D), lambda i, f: (0, 0))],
        out_shape=[jax.ShapeDtypeStruct((L, D), F32), jax.ShapeDtypeStruct((L, F), BF16),
                   jax.ShapeDtypeStruct((L, F), BF16), jax.ShapeDtypeStruct((L, F), BF16),
                   jax.ShapeDtypeStruct((L, D), BF16), jax.ShapeDtypeStruct((1, D), F32)],
        scratch_shapes=[pltpu.VMEM((tm, D), F32)],
        compiler_params=_params("arbitrary", "arbitrary"))(dh, h, gs, us, gam, wg, wu, wd)


def _mm_tn(a, b, name):
    L, M = a.shape
    N = b.shape[1]
    tm = _pick(M, (1408, 1024, 512))
    tn = _pick(N, (1408, 1024, 512))
    tk = _row_tile(L)

    def kern(a_ref, b_ref, o_ref):
        @pl.when(pl.program_id(2) == 0)
        def _():
            o_ref[...] = jnp.zeros(o_ref.shape, F32)

        o_ref[...] += _dot_tn(a_ref[...], b_ref[...])

    return pl.pallas_call(
        kern, name=name, grid=(M // tm, N // tn, L // tk),
        in_specs=[pl.BlockSpec((tk, tm), lambda i, j, k: (k, i)),
                  pl.BlockSpec((tk, tn), lambda i, j, k: (k, j))],
        out_specs=pl.BlockSpec((tm, tn), lambda i, j, k: (i, j)),
        out_shape=jax.ShapeDtypeStruct((M, N), F32),
        compiler_params=_params("arbitrary", "arbitrary", "arbitrary"))(a, b)


def _pool_counts(pos, w):
    return jnp.clip(pos - (FRONT_PAD - 1), 1, w).astype(F32)


def _pool_forward_values(i, tm, h, hprev, gamma, D):
    cg = D // len(POOL_WINDOWS)
    hext = jnp.concatenate([hprev, h], axis=0)
    uext, xh, r = _rms(hext, gamma)
    pos = i * tm + lax.broadcasted_iota(jnp.int32, (tm, 1), 0)
    pooled = []
    for gi, w in enumerate(POOL_WINDOWS):
        s = uext[:, gi * cg:(gi + 1) * cg]
        span = 1
        while span < w:
            s = s + pltpu.roll(s, span, 0)
            span *= 2
        s = s[HALO:]
        pooled.append(s / _pool_counts(pos, w) - uext[HALO:, gi * cg:(gi + 1) * cg])
    return uext, xh[HALO:], r[HALO:], pooled


def _pool_fwd(h, gam, w, scale, l):
    L, D = h.shape
    cg = D // len(POOL_WINDOWS)

    def body(i, n, tm, ht, hprev, gamma, wv, sc):
        _, _, _, pooled = _pool_forward_values(i, tm, ht, hprev, gamma, D)
        ys = [_dot(pooled[gi], wv[gi]) for gi in range(len(POOL_WINDOWS))]
        y = jnp.concatenate(ys, axis=1) * sc
        return [ht + y], []

    del cg
    return _row_call("pool_fwd", body, L, [(h, "tile"), (h, "prev")], [gam, w, scale], [(D, F32)], [])[0]


def _pool_bwd(dy, h, gam, w, scale):
    L, D = h.shape
    ng = len(POOL_WINDOWS)
    cg = D // ng

    def body(i, n, tm, ht, hprev, dyt, dynext, gamma, wv, sc):
        _, xh, r, pooled = _pool_forward_values(i, tm, ht, hprev, gamma, D)
        dynext = jnp.where(i == n - 1, jnp.zeros_like(dynext), dynext)
        dyext = jnp.concatenate([dyt, dynext], axis=0) * sc
        pos_ext = i * tm + lax.broadcasted_iota(jnp.int32, (tm + HALO, 1), 0)
        dws, dscs, dus = [], [], []
        for gi, wd in enumerate(POOL_WINDOWS):
            cols = slice(gi * cg, (gi + 1) * cg)
            pb = pooled[gi].astype(BF16)
            ypre = _dot(pb, wv[gi])
            dscs.append(jnp.sum(dyt[:, cols] * ypre, axis=0, keepdims=True))
            dws.append(_dot_tn(pb, dyext[:tm, cols])[None])
            dp = _dot_nt(dyext[:, cols], wv[gi])
            s = dp / _pool_counts(pos_ext, wd)
            span = 1
            while span < wd:
                s = s + pltpu.roll(s, tm + HALO - span, 0)
                span *= 2
            dus.append(s[:tm] - dp[:tm])
        du = jnp.concatenate(dus, axis=1)
        pos = pos_ext[:tm]
        du = jnp.where(pos >= FRONT_PAD, du, 0.0)
        dx, dgam = _rms_bwd(du, xh, r, gamma)
        return [dyt + dx], [jnp.concatenate(dws, axis=0), jnp.concatenate(dscs, axis=1), dgam]

    return _row_call("pool_bwd", body, L, [(h, "tile"), (h, "prev"), (dy, "tile"), (dy, "next")],
                     [gam, w, scale], [(D, F32)], [(ng, cg, cg), (1, D), (1, D)])


def _kv_fwd(h, tabs, g1, wdkv, g2, wuk, wuv):
    L, D = h.shape
    hw = N_HEADS * HEAD_LANES

    def body(i, n, tm, ht, ck, s1, s2, g1v, wdkv_v, g2v, wuk_v, wuv_v):
        xkv, _, _ = _rms(ht, g1v)
        ckr = _dot(xkv, wdkv_v)
        ckv, _, _ = _rms(ckr[:, :KV_RANK], g2v)
        krope = _rope(ckr[:, KV_RANK:], ck, s1, s2)
        pos = i * tm + lax.broadcasted_iota(jnp.int32, (tm, HEAD_LANES), 0)
        lane = lax.broadcasted_iota(jnp.int32, (tm, HEAD_LANES), 1)
        krope = jnp.where((pos < FRONT_PAD) & (lane == BIAS_LANE), NEG, krope)
        ones = ((lax.broadcasted_iota(jnp.int32, (1, hw), 1) & (HEAD_LANES - 1)) == ONES_LANE).astype(F32)
        return [_dot(ckv, wuk_v), krope, _dot(ckv, wuv_v) + ones, ckr], []

    ck, s1, s2 = tabs["ck"], tabs["s1"], tabs["s2"]
    return _row_call("kv_fwd", body, L, [(h, "tile"), (ck, "tile"), (s1, "tile"), (s2, "tile")],
                     [g1, wdkv, g2, wuk, wuv],
                     [(hw, BF16), (HEAD_LANES, BF16), (hw, BF16), (KV_RANK + HEAD_LANES, F32)], [])


def _kv_bwd(dh, h, ckr, dks, dvs, tabs, g1, wdkv, g2, wuk, wuv):
    L, D = h.shape
    hw = N_HEADS * HEAD_LANES
    nl = len(dks)

    def body(i, n, tm, *vals):
        dht, ht, ckr_t = vals[:3]
        dk = sum(vals[3:3 + nl][1:], vals[3])
        dv = sum(vals[3 + nl:3 + 2 * nl][1:], vals[3 + nl])
        ck, s1, s2, g1v, wdkv_v, g2v, wuk_v, wuv_v = vals[3 + 2 * nl:]
        xkv, xh1, r1 = _rms(ht, g1v)
        ckv, xh2, r2 = _rms(ckr_t[:, :KV_RANK], g2v)
        dckv = _dot_nt(dk, wuk_v) + _dot_nt(dv, wuv_v)
        dlat, dg2 = _rms_bwd(dckv, xh2, r2, g2v)
        dkr = dk[:, :HEAD_LANES]
        for hd in range(1, N_HEADS):
            dkr = dkr + dk[:, hd * HEAD_LANES:(hd + 1) * HEAD_LANES]
        dckr = jnp.concatenate([dlat, _rope_t(dkr, ck, s1, s2)], axis=1)
        dx, dg1 = _rms_bwd(_dot_nt(dckr, wdkv_v), xh1, r1, g1v)
        return [dht + dx], [_dot_tn(xkv, dckr), _dot_tn(ckv, dk), _dot_tn(ckv, dv), dg1, dg2]

    row_ins = [(dh, "tile"), (h, "tile"), (ckr, "tile")] + [(a, "tile") for a in dks + dvs]
    row_ins += [(tabs[k], "tile") for k in ("ck", "s1", "s2")]
    return _row_call("kv_bwd", body, L, row_ins, [g1, wdkv, g2, wuk, wuv], [(D, F32)],
                     [(D, KV_RANK + HEAD_LANES), (KV_RANK, hw), (KV_RANK, hw), (1, D), (1, KV_RANK)])


def _q_fwd(h, tabs, g, wdq, gq, wuq):
    L, D = h.shape
    hw = N_HEADS * HEAD_LANES

    def body(i, n, tm, ht, cq_t, s1, s2, gv, wdq_v, gqv, wuq_v):
        u, _, _ = _rms(ht, gv)
        cqp = _dot(u, wdq_v)
        cq, _, _ = _rms(cqp, gqv)
        qp = _dot(cq, wuq_v)
        bias = (lax.broadcasted_iota(jnp.int32, (1, HEAD_LANES), 1) == BIAS_LANE).astype(F32)
        q = [_rope(qp[:, hd * HEAD_LANES:(hd + 1) * HEAD_LANES], cq_t, s1, s2) * (SM_SCALE * LOG2E) + bias
             for hd in range(N_HEADS)]
        return [jnp.concatenate(q, axis=1), cqp], []

    return _row_call("q_fwd", body, L, [(h, "tile")] + [(tabs[k], "tile") for k in ("cq", "s1", "s2")],
                     [g, wdq, gq, wuq], [(hw, BF16), (Q_RANK, F32)], [])


def _q_bwd(dh, h, cqp, dq, tabs, g, wdq, gq, wuq):
    L, D = h.shape
    hw = N_HEADS * HEAD_LANES

    def body(i, n, tm, dht, ht, cqp_t, dq_t, cq_t, s1, s2, gv, wdq_v, gqv, wuq_v):
        u, xh1, r1 = _rms(ht, gv)
        cq, xh2, r2 = _rms(cqp_t, gqv)
        dqp = jnp.concatenate([_rope_t(dq_t[:, hd * HEAD_LANES:(hd + 1) * HEAD_LANES], cq_t, s1, s2)
                               for hd in range(N_HEADS)], axis=1)
        dcqp, dgq = _rms_bwd(_dot_nt(dqp, wuq_v), xh2, r2, gqv)
        dx, dg = _rms_bwd(_dot_nt(dcqp, wdq_v), xh1, r1, gv)
        return [dht + dx], [_dot_tn(u, dcqp), _dot_tn(cq, dqp), dg, dgq]

    row_ins = [(dh, "tile"), (h, "tile"), (cqp, "tile"), (dq, "tile")]
    row_ins += [(tabs[k], "tile") for k in ("cq", "s1", "s2")]
    return _row_call("q_bwd", body, L, row_ins, [g, wdq, gq, wuq], [(D, F32)],
                     [(D, Q_RANK), (Q_RANK, hw), (1, D), (1, Q_RANK)])


def _oproj_fwd(h, o, wo):
    L, D = h.shape

    def body(i, n, tm, ht, ot, wov):
        return [ht + _dot(ot, wov)], []

    return _row_call("oproj_fwd", body, L, [(h, "tile"), (o, "tile")], [wo], [(D, F32)], [])[0]


def _oproj_bwd(dh, o, wo):
    L, D = dh.shape
    hw = N_HEADS * HEAD_LANES

    def body(i, n, tm, dht, ot, wov):
        do = _dot_nt(dht, wov)
        prod = do * ot.astype(F32)
        delta = [jnp.broadcast_to(jnp.sum(prod[:, hd * HEAD_LANES:(hd + 1) * HEAD_LANES], axis=-1, keepdims=True),
                                  (tm, HEAD_LANES)) for hd in range(N_HEADS)]
        return [do, jnp.concatenate(delta, axis=1)], [_dot_tn(ot, dht)]

    return _row_call("oproj_bwd", body, L, [(dh, "tile"), (o, "tile")], [wo], [(hw, BF16), (hw, F32)], [(hw, D)])


def _causal(t):
    qpos = lax.broadcasted_iota(jnp.int32, (t, t), 0)
    kpos = lax.broadcasted_iota(jnp.int32, (t, t), 1)
    return (kpos >> CHUNK_SHIFT) <= (qpos >> CHUNK_SHIFT)


SM_SCALE = 1.0 / math.sqrt(QK_NOPE + QK_ROPE)


def _pairs(n, key_major):
    if key_major:
        order = [(i, j) for j in range(n) for i in range(j, n)]
    else:
        order = [(i, j) for i in range(n) for j in range(i + 1)]
    return (jnp.array([p[0] for p in order], jnp.int32), jnp.array([p[1] for p in order], jnp.int32))


def _attn_fwd(q, kn, kr, v):
    L = q.shape[0]
    hw = N_HEADS * HEAD_LANES
    t = _row_tile(L)
    it, jt = _pairs(L // t, key_major=False)

    def kern(it_ref, jt_ref, q_ref, kn_ref, kr_ref, v_ref, o_ref, lse_ref, m_s, acc_s):
        step = pl.program_id(1)
        i, j = it_ref[step], jt_ref[step]

        @pl.when(j == 0)
        def _():
            m_s[...] = jnp.full(m_s.shape, NEG, F32)
            acc_s[...] = jnp.zeros(acc_s.shape, F32)

        def update(diagonal):
            k = kn_ref[...] + kr_ref[...]
            s = _dot_nt(q_ref[...], k)
            if diagonal:
                s = jnp.where(_causal(t), s, NEG)
            m_prev = m_s[:, :1]
            m_new = jnp.maximum(m_prev, jnp.max(s, axis=-1, keepdims=True))
            p = jnp.exp2(s - m_new)
            acc_s[...] = jnp.exp2(m_prev - m_new) * acc_s[...] + _dot(p, v_ref[...])
            m_s[...] = jnp.broadcast_to(m_new, m_s.shape)

        @pl.when(j < i)
        def _():
            update(False)

        @pl.when(j == i)
        def _():
            update(True)
            acc = acc_s[...]
            total = acc[:, ONES_LANE:ONES_LANE + 1]
            o_ref[...] = (acc / total).astype(BF16)
            lse_ref[...] = m_s[...] + jnp.log2(jnp.broadcast_to(total, m_s.shape))

    qmap = lambda h, s, it, jt: (it[s], h)
    kmap = lambda h, s, it, jt: (jt[s], h)
    blk = (t, HEAD_LANES)
    return pl.pallas_call(
        kern, name="attn_fwd",
        grid_spec=pltpu.PrefetchScalarGridSpec(
            num_scalar_prefetch=2, grid=(N_HEADS, it.shape[0]),
            in_specs=[pl.BlockSpec(blk, qmap), pl.BlockSpec(blk, kmap),
                      pl.BlockSpec(blk, lambda h, s, it, jt: (jt[s], 0)), pl.BlockSpec(blk, kmap)],
            out_specs=[pl.BlockSpec(blk, qmap), pl.BlockSpec(blk, qmap)],
            scratch_shapes=[pltpu.VMEM(blk, F32)] * 2),
        out_shape=[jax.ShapeDtypeStruct((L, hw), BF16), jax.ShapeDtypeStruct((L, hw), F32)],
        compiler_params=_params("arbitrary", "arbitrary"))(it, jt, q, kn, kr, v)


def _attn_bwd(q, kn, kr, v, do, lse, delta):
    L = q.shape[0]
    hw = N_HEADS * HEAD_LANES
    t = _row_tile(L)
    it, jt = _pairs(L // t, key_major=True)

    def kern(it_ref, jt_ref, q_ref, kn_ref, kr_ref, v_ref, do_ref, lse_ref, dl_ref, dq_ref, dk_ref, dv_ref):
        step = pl.program_id(1)
        i, j = it_ref[step], jt_ref[step]

        @pl.when(step == 0)
        def _():
            dq_ref[...] = jnp.zeros(dq_ref.shape, F32)

        @pl.when(i == j)
        def _():
            dk_ref[...] = jnp.zeros(dk_ref.shape, F32)
            dv_ref[...] = jnp.zeros(dv_ref.shape, F32)

        def update(diagonal):
            k = kn_ref[...] + kr_ref[...]
            qv, dov = q_ref[...], do_ref[...]
            s = _dot_nt(qv, k)
            if diagonal:
                s = jnp.where(_causal(t), s, NEG)
            p = jnp.exp2(s - lse_ref[:, :1])
            dp = _dot_nt(dov, v_ref[...])
            dz = (p * (dp - dl_ref[:, :1])).astype(BF16)
            dv_ref[...] += _dot_tn(p, dov)
            dk_ref[...] += _dot_tn(dz, qv) * (1.0 / LOG2E)
            rows = pl.ds(pl.multiple_of(i * t, t), t)
            dq_ref[rows, :] += _dot(dz, k) * SM_SCALE

        @pl.when(j < i)
        def _():
            update(False)

        @pl.when(j == i)
        def _():
            update(True)

    qmap = lambda h, s, it, jt: (it[s], h)
    kmap = lambda h, s, it, jt: (jt[s], h)
    blk = (t, HEAD_LANES)
    return pl.pallas_call(
        kern, name="attn_bwd",
        grid_spec=pltpu.PrefetchScalarGridSpec(
            num_scalar_prefetch=2, grid=(N_HEADS, it.shape[0]),
            in_specs=[pl.BlockSpec(blk, qmap), pl.BlockSpec(blk, kmap),
                      pl.BlockSpec(blk, lambda h, s, it, jt: (jt[s], 0)), pl.BlockSpec(blk, kmap),
                      pl.BlockSpec(blk, qmap), pl.BlockSpec(blk, qmap), pl.BlockSpec(blk, qmap)],
            out_specs=[pl.BlockSpec((L, HEAD_LANES), lambda h, s, it, jt: (0, h)),
                       pl.BlockSpec(blk, kmap), pl.BlockSpec(blk, kmap)]),
        out_shape=[jax.ShapeDtypeStruct((L, hw), F32)] * 3,
        compiler_params=_params("arbitrary", "arbitrary"))(it, jt, q, kn, kr, v, do, lse, delta)


def _head(h, target, g):
    L, D = h.shape

    def body(i, n, tm, ht, tt, gv):
        y, xh, r = _rms(ht, gv)
        pos = i * tm + lax.broadcasted_iota(jnp.int32, (tm, 1), 0)
        e = jnp.where(pos >= SEQ_START, y - tt, 0.0)
        loss = 0.5 * jnp.sum(jnp.mean(e * e, axis=-1, keepdims=True), axis=0, keepdims=True)
        dx, dg = _rms_bwd(e / D, xh, r, gv)
        return [dx], [jnp.broadcast_to(loss, (1, 128)), dg]

    return _row_call("loss_head", body, L, [(h, "tile"), (target, "tile")], [g], [(D, F32)], [(1, 128), (1, D)])


ANY = pl.BlockSpec(memory_space=pl.ANY)


def _coords():
    return lax.axis_index("x"), lax.axis_index("y"), lax.axis_index("c")


def _all_gather(x):
    R, W = x.shape

    def kern(x_ref, out_ref, send_sems, recv_sems, local_sem):
        mx, my, mc = _coords()
        me, sibling = (mx, my, mc), (mx, my, 1 - mc)
        chips = [(1 - mx, my), (mx, 1 - my), (1 - mx, 1 - my)]

        def slot(px, py, pc):
            return out_ref.at[4 * px + 2 * py + pc]

        def copy(k, block, to, src=None):
            return pltpu.make_async_remote_copy(
                src_ref=slot(*block) if src is None else src, dst_ref=slot(*block),
                send_sem=send_sems.at[k], recv_sem=recv_sems.at[k], device_id=to, device_id_type=MESH)

        mine = pltpu.make_async_copy(x_ref, slot(*me), local_sem)
        mine.start()
        first = [copy(0, me, sibling, src=x_ref)]
        first += [copy(1 + n, me, (*chip, mc), src=x_ref) for n, chip in enumerate(chips)]
        for cp in first:
            cp.start()
        passed = [copy(4 + n, (*chip, mc), sibling) for n, chip in enumerate(chips)]
        for n, chip in enumerate(chips):
            copy(1 + n, (*chip, mc), me).wait_recv()
            passed[n].start()
        copy(0, sibling, me).wait_recv()
        for n, chip in enumerate(chips):
            copy(4 + n, (*chip, 1 - mc), me).wait_recv()
        for cp in first + passed:
            cp.wait_send()
        mine.wait()

    return pl.pallas_call(
        kern, name="all_gather_%s" % jnp.dtype(x.dtype).name, in_specs=[ANY], out_specs=ANY,
        out_shape=jax.ShapeDtypeStruct((N_DEV, R, W), x.dtype),
        scratch_shapes=[pltpu.SemaphoreType.DMA((7,)), pltpu.SemaphoreType.DMA((7,)), pltpu.SemaphoreType.DMA],
    )(x)


def _sibling_exchange(g):
    _, _, R, W = g.shape

    def kern(g_ref, out_ref, send_sems, recv_sems):
        mx, my, mc = _coords()
        copies = [pltpu.make_async_remote_copy(
            src_ref=g_ref.at[n, 1 - mc], dst_ref=out_ref.at[n], send_sem=send_sems.at[n],
            recv_sem=recv_sems.at[n], device_id=(mx, my, 1 - mc), device_id_type=MESH) for n in range(N_CHIPS)]
        for cp in copies:
            cp.start()
        for cp in copies:
            cp.wait_recv()
        for cp in copies:
            cp.wait_send()

    return pl.pallas_call(
        kern, name="sibling_exchange", in_specs=[ANY], out_specs=ANY,
        out_shape=jax.ShapeDtypeStruct((N_CHIPS, R, W), g.dtype),
        scratch_shapes=[pltpu.SemaphoreType.DMA((N_CHIPS,)), pltpu.SemaphoreType.DMA((N_CHIPS,))],
    )(g)


def _chip_exchange(p):
    _, R, W = p.shape

    def kern(p_ref, out_ref, send_sems, recv_sems):
        mx, my, mc = _coords()
        chips = [(1 - mx, my), (mx, 1 - my), (1 - mx, 1 - my)]
        copies = [pltpu.make_async_remote_copy(
            src_ref=p_ref.at[2 * cx + cy], dst_ref=out_ref.at[n], send_sem=send_sems.at[n],
            recv_sem=recv_sems.at[n], device_id=(cx, cy, mc), device_id_type=MESH)
            for n, (cx, cy) in enumerate(chips)]
        for cp in copies:
            cp.start()
        for cp in copies:
            cp.wait_recv()
        for cp in copies:
            cp.wait_send()

    return pl.pallas_call(
        kern, name="chip_exchange", in_specs=[ANY], out_specs=ANY,
        out_shape=jax.ShapeDtypeStruct((3, R, W), p.dtype),
        scratch_shapes=[pltpu.SemaphoreType.DMA((3,)), pltpu.SemaphoreType.DMA((3,))],
    )(p)


def _add_own(own, sel, other, name):
    R, W = own.shape[-2:]
    tr = _pick(R, (PACK_ROW_MULT, 8))
    if own.ndim == 4:
        n = own.shape[0]
        grid = (n, R // tr)
        in_specs = [pl.BlockSpec((None, None, tr, W), lambda b, i, sel: (b, sel[0], i, 0)),
                    pl.BlockSpec((None, tr, W), lambda b, i, sel: (b, i, 0))]
        out_spec = pl.BlockSpec((None, tr, W), lambda b, i, sel: (b, i, 0))
        out_shape = jax.ShapeDtypeStruct((n, R, W), F32)

        def kern(sel_ref, own_ref, other_ref, o_ref):
            o_ref[...] = own_ref[...] + other_ref[...]
    else:
        k = other.shape[0]
        grid = (1, R // tr)
        in_specs = [pl.BlockSpec((None, tr, W), lambda b, i, sel: (sel[0], i, 0)),
                    pl.BlockSpec((k, tr, W), lambda b, i, sel: (0, i, 0))]
        out_spec = pl.BlockSpec((tr, W), lambda b, i, sel: (i, 0))
        out_shape = jax.ShapeDtypeStruct((R, W), F32)

        def kern(sel_ref, own_ref, other_ref, o_ref):
            acc = own_ref[...]
            for m in range(k):
                acc = acc + other_ref[m]
            o_ref[...] = acc

    return pl.pallas_call(
        kern, name=name,
        grid_spec=pltpu.PrefetchScalarGridSpec(num_scalar_prefetch=1, grid=grid, in_specs=in_specs,
                                               out_specs=out_spec),
        out_shape=out_shape, compiler_params=_params("arbitrary", "arbitrary"))(sel, own, other)


def _sum_lead(x, name):
    n, R, W = x.shape
    tr = _pick(R, (PACK_ROW_MULT, 8))

    def kern(x_ref, o_ref):
        acc = x_ref[0]
        for k in range(1, n):
            acc = acc + x_ref[k]
        o_ref[...] = acc

    return pl.pallas_call(
        kern, name=name, grid=(R // tr,),
        in_specs=[pl.BlockSpec((n, tr, W), lambda i: (0, i, 0))],
        out_specs=pl.BlockSpec((tr, W), lambda i: (i, 0)),
        out_shape=jax.ShapeDtypeStruct((R, W), F32), compiler_params=_params("arbitrary"))(x)


def _adamw(w, g, m, v):
    shape = w.shape
    cols = shape[-1]
    rows = w.size // cols
    tr = _pick(rows, (512, 352, 256, 128))
    if rows * cols * 4 <= (1 << 20):
        tr = rows

    def kern(w_ref, g_ref, m_ref, v_ref, d_ref, mo_ref, vo_ref):
        gv = g_ref[...]
        mn = ADAM_B1 * m_ref[...] + (1.0 - ADAM_B1) * gv
        vn = ADAM_B2 * v_ref[...] + (1.0 - ADAM_B2) * (gv * gv)
        m_hat = mn / (1.0 - ADAM_B1 ** ADAM_STEP)
        v_hat = vn / (1.0 - ADAM_B2 ** ADAM_STEP)
        d_ref[...] = -ADAM_LR * (m_hat / (jnp.sqrt(v_hat) + ADAM_EPS) + ADAM_WD * w_ref[...])
        mo_ref[...] = mn
        vo_ref[...] = vn

    spec = pl.BlockSpec((tr, cols), lambda i: (i, 0))
    outs = pl.pallas_call(
        kern, name="adamw", grid=(rows // tr,), in_specs=[spec] * 4, out_specs=[spec] * 3,
        out_shape=[jax.ShapeDtypeStruct((rows, cols), F32)] * 3, compiler_params=_params("arbitrary"),
    )(*[a.reshape(rows, cols) for a in (w, g, m, v)])
    return [o.reshape(shape) for o in outs]


def _pack(arrs, n_lead, row_mult):
    parts, total = [], 0
    for n, a in enumerate(arrs):
        pieces = a if isinstance(a, list) else [a]
        lead = pieces[0].shape[:n_lead]
        flat = [p.reshape(lead + (-1,)) for p in pieces]
        size = sum(f.shape[-1] for f in flat)
        rows = -(-size // (PACK_W * row_mult)) * row_mult
        if n == len(arrs) - 1:
            rows += -(total + rows) % PACK_ROW_MULT
        total += rows
        if rows * PACK_W > size:
            flat.append(jnp.zeros(lead + (rows * PACK_W - size,), flat[0].dtype))
        parts.append(jnp.concatenate(flat, axis=n_lead).reshape(lead + (rows, PACK_W)))
    return jnp.concatenate(parts, axis=n_lead)


def _unpack(pack, shapes, n_lead, row_mult):
    outs, row = [], 0
    lead = pack.shape[:n_lead]
    for shp in shapes:
        size = math.prod(shp)
        rows = -(-size // (PACK_W * row_mult)) * row_mult
        blk = lax.slice_in_dim(pack, row, row + rows, axis=n_lead)
        outs.append(blk.reshape(lead + (-1,))[..., :size].reshape(lead + tuple(shp)))
        row += rows
    return outs


def _to_words(a):
    return lax.bitcast_convert_type(a, BF16)


def _from_words(a):
    return lax.bitcast_convert_type(a, F32)


def _pad_axis(a, axis, size):
    pads = [(0, 0)] * a.ndim
    pads[axis] = (0, size - a.shape[axis])
    return jnp.pad(a, pads)


def _dense(name, s):
    if name.endswith("w_gate") or name.endswith("w_up"):
        _, nl, d, fs = s.shape
        return s.transpose(1, 2, 0, 3).reshape(nl, d, N_DEV * fs)
    if name.endswith("w_down"):
        _, nl, fs, d = s.shape
        return s.transpose(1, 0, 2, 3).reshape(nl, N_DEV * fs, d)
    if name == "pool_w":
        _, nl, ng, r, cg = s.shape
        return s.transpose(1, 2, 0, 3, 4).reshape(nl, ng, cg, cg)
    if name == "w_dkv":
        w = s.reshape(-1, s.shape[2])
        z = lambda n: jnp.zeros((w.shape[0], n), w.dtype)
        return jnp.concatenate([w[:, :KV_RANK], z(ROPE_LANE0), w[:, KV_RANK:],
                                z(HEAD_LANES - ROPE_LANE0 - QK_ROPE)], axis=1)
    if name in ("w_uk", "w_uv"):
        return _pad_axis(s.transpose(1, 0, 2), 2, HEAD_LANES).reshape(KV_RANK, N_HEADS * HEAD_LANES)
    if name == "w_dq":
        _, nl, ds, r = s.shape
        return s.transpose(1, 0, 2, 3).reshape(nl, N_DEV * ds, r)
    if name == "w_uq":
        nl = s.shape[1]
        return _pad_axis(s.transpose(1, 2, 0, 3), 3, HEAD_LANES).reshape(nl, Q_RANK, N_HEADS * HEAD_LANES)
    if name == "w_o":
        _, nl, k, dc = s.shape
        w = s.transpose(1, 2, 0, 3).reshape(nl, N_HEADS, V_HEAD, N_DEV * dc)
        return _pad_axis(w, 2, HEAD_LANES).reshape(nl, N_HEADS * HEAD_LANES, N_DEV * dc)
    if name in ("meta_tokens", "pool_scale"):
        r, dc = s.shape[1:]
        return s.transpose(1, 0, 2).reshape(r, N_DEV * dc)
    raise ValueError(name)


def _shards(name, g):
    if name.endswith("w_gate") or name.endswith("w_up"):
        nl, d, f = g.shape
        return g.reshape(nl, d, N_DEV, f // N_DEV).transpose(2, 0, 1, 3)
    if name.endswith("w_down"):
        nl, f, d = g.shape
        return g.reshape(nl, N_DEV, f // N_DEV, d).transpose(1, 0, 2, 3)
    if name == "pool_w":
        nl, ng, cg, _ = g.shape
        return g.reshape(nl, ng, N_DEV, cg // N_DEV, cg).transpose(2, 0, 1, 3, 4)
    if name == "w_dkv":
        w = jnp.concatenate([g[:, :KV_RANK], g[:, KV_RANK + ROPE_LANE0:KV_RANK + ROPE_LANE0 + QK_ROPE]], axis=1)
        return w.reshape(N_DEV, -1, KV_RANK + QK_ROPE)
    if name in ("w_uk", "w_uv"):
        return g.reshape(KV_RANK, N_HEADS, HEAD_LANES)[:, :, :V_HEAD].transpose(1, 0, 2)
    if name == "w_dq":
        nl, d, r = g.shape
        return g.reshape(nl, N_DEV, d // N_DEV, r).transpose(1, 0, 2, 3)
    if name == "w_uq":
        nl = g.shape[0]
        return g.reshape(nl, Q_RANK, N_HEADS, HEAD_LANES)[..., :QK_NOPE + QK_ROPE].transpose(2, 0, 1, 3)
    if name == "w_o":
        nl, _, d = g.shape
        w = g.reshape(nl, N_HEADS, HEAD_LANES, d)[:, :, :V_HEAD].reshape(nl, N_HEADS * V_HEAD, N_DEV, d // N_DEV)
        return w.transpose(2, 0, 1, 3)
    if name in ("meta_tokens", "pool_scale"):
        r, d = g.shape
        return g.reshape(r, N_DEV, d // N_DEV).transpose(1, 0, 2)
    raise ValueError(name)


def _rope_tables(L):
    pos = jnp.maximum(jnp.arange(L) - FRONT_PAD, 0).astype(F32)
    inv = 1.0 / (ROPE_THETA ** (jnp.arange(0, QK_ROPE, 2, dtype=F32) / QK_ROPE))
    ang = pos[:, None] * inv[None, :]
    cos, sin = jnp.cos(ang), jnp.sin(ang)
    half = QK_ROPE // 2
    z = lambda n: jnp.zeros((L, n), F32)
    tail = z(HEAD_LANES - ROPE_LANE0 - QK_ROPE)
    return {
        "cq": jnp.concatenate([jnp.ones((L, ROPE_LANE0), F32), cos, cos, tail], axis=1),
        "ck": jnp.concatenate([z(ROPE_LANE0), cos, cos, tail], axis=1),
        "s1": jnp.concatenate([z(ROPE_LANE0), -sin, z(half), tail], axis=1),
        "s2": jnp.concatenate([z(ROPE_LANE0), z(half), sin, tail], axis=1),
    }


def kernel(x, meta_tokens, ffn1_norm, ffn1_w_gate, ffn1_w_up, ffn1_w_down, mix_norm, ffn2_norm, ffn2_w_gate, ffn2_w_up, ffn2_w_down, pool_w, pool_scale, kv_in_norm, w_dkv, kv_latent_norm, w_uk, w_uv, w_dq, q_latent_norm, w_uq, w_o, final_norm, loss_target, m_meta_tokens, m_ffn1_norm, m_ffn1_w_gate, m_ffn1_w_up, m_ffn1_w_down, m_mix_norm, m_ffn2_norm, m_ffn2_w_gate, m_ffn2_w_up, m_ffn2_w_down, m_pool_w, m_pool_scale, m_kv_in_norm, m_w_dkv, m_kv_latent_norm, m_w_uk, m_w_uv, m_w_dq, m_q_latent_norm, m_w_uq, m_w_o, m_final_norm, v_meta_tokens, v_ffn1_norm, v_ffn1_w_gate, v_ffn1_w_up, v_ffn1_w_down, v_mix_norm, v_ffn2_norm, v_ffn2_w_gate, v_ffn2_w_up, v_ffn2_w_down, v_pool_w, v_pool_scale, v_kv_in_norm, v_w_dkv, v_kv_latent_norm, v_w_uk, v_w_uv, v_w_dq, v_q_latent_norm, v_w_uq, v_w_o, v_final_norm):
    args = dict(locals())
    W = {n: args[n] for n in WEIGHTS}
    M = {n: args["m_" + n] for n in WEIGHTS}
    V = {n: args["v_" + n] for n in WEIGHTS}
    seq, D = x.shape[1], x.shape[2]
    L = SEQ_START + seq

    shard_shapes = [W[n].shape + ((2,) if n in SHARDED_F32 else ()) for n in SHARDED]
    wpack = _pack([_to_words(W[n]) if n in SHARDED_F32 else W[n].astype(BF16) for n in SHARDED], 0, 16)
    gathered = _unpack(_all_gather(wpack), shard_shapes, 1, 16)
    P = {}
    for n, s in zip(SHARDED, gathered):
        P[n] = _dense(n, _from_words(s) if n in SHARDED_F32 else s)
    norm3 = lambda a: a.reshape(a.shape[0], 1, a.shape[-1])
    row = lambda a: a.reshape(1, -1)
    g_ffn1, g_mix, g_ffn2 = norm3(ffn1_norm), mix_norm, norm3(ffn2_norm)

    h = jnp.concatenate([jnp.zeros((FRONT_PAD, D), F32), P["meta_tokens"], x[0]], axis=0)
    target = jnp.concatenate([jnp.zeros((SEQ_START, D), F32), loss_target[0]], axis=0)
    tabs = _rope_tables(L)
    saved = []
    kv = None
    for l in range(DEPTH):
        s = {"h1": h}
        h, s["xn1"], s["g1"], s["u1"] = _ffn_fwd(h, g_ffn1, P["ffn1_w_gate"], P["ffn1_w_up"], P["ffn1_w_down"], l)
        s["hm"] = h
        if l < N_POOL_LAYERS:
            h = _pool_fwd(h, row(g_mix[l]), P["pool_w"][l], row(P["pool_scale"][l]), l)
        else:
            j = l - N_POOL_LAYERS
            s["q"], s["cqp"] = _q_fwd(h, tabs, row(g_mix[l]), P["w_dq"][j], row(q_latent_norm[j]), P["w_uq"][j])
            s["o"], s["lse"] = _attn_fwd(s["q"], kv["kn"], kv["kr"], kv["v"])
            h = _oproj_fwd(h, s["o"], P["w_o"][j])
        s["h2"] = h
        h, s["xn2"], s["g2"], s["u2"] = _ffn_fwd(h, g_ffn2, P["ffn2_w_gate"], P["ffn2_w_up"], P["ffn2_w_down"], l)
        if l == N_POOL_LAYERS - 1:
            kv = {"h": h}
            kv["kn"], kv["kr"], kv["v"], kv["ckr"] = _kv_fwd(
                h, tabs, row(kv_in_norm), P["w_dkv"], row(kv_latent_norm), P["w_uk"], P["w_uv"])
        saved.append(s)
    dh, loss_row, d_final = _head(h, target, row(final_norm))
    loss = lax.psum(loss_row[0, 0], ("x", "y", "c"))

    G = {}
    stack = {n: [None] * DEPTH for n in ("ffn1_norm", "ffn1_w_gate", "ffn1_w_up", "ffn1_w_down", "mix_norm",
                                         "ffn2_norm", "ffn2_w_gate", "ffn2_w_up", "ffn2_w_down")}
    pool_dw, pool_ds = [None] * N_POOL_LAYERS, [None] * N_POOL_LAYERS
    mla = {n: [None] * (DEPTH - N_POOL_LAYERS) for n in ("w_dq", "w_uq", "w_o", "q_latent_norm")}
    dks, dvs = [], []

    def ffn_backward(dh, s, which, gam, l):
        wg, wu, wd = P["ffn%d_w_gate" % which], P["ffn%d_w_up" % which], P["ffn%d_w_down" % which]
        dh, dg, du, act, dob, dgam = _ffn_bwd(dh, s["h%d" % which], s["g%d" % which], s["u%d" % which], gam, wg, wu, wd, l)
        xn = s["xn%d" % which]
        stack["ffn%d_w_gate" % which][l] = _mm_tn(xn, dg, "ffn_dw_in")
        stack["ffn%d_w_up" % which][l] = _mm_tn(xn, du, "ffn_dw_in")
        stack["ffn%d_w_down" % which][l] = _mm_tn(act, dob, "ffn_dw_down")
        stack["ffn%d_norm" % which][l] = dgam
        return dh

    for l in reversed(range(DEPTH)):
        s = saved[l]
        if l == N_POOL_LAYERS - 1:
            dh, d_dkv, d_uk, d_uv, d_kvin, d_kvlat = _kv_bwd(
                dh, kv["h"], kv["ckr"], dks, dvs, tabs, row(kv_in_norm), P["w_dkv"], row(kv_latent_norm),
                P["w_uk"], P["w_uv"])
            G.update(w_dkv=d_dkv, w_uk=d_uk, w_uv=d_uv, kv_in_norm=d_kvin, kv_latent_norm=d_kvlat)
        dh = ffn_backward(dh, s, 2, g_ffn2, l)
        if l < N_POOL_LAYERS:
            dh, pool_dw[l], pool_ds[l], stack["mix_norm"][l] = _pool_bwd(
                dh, s["hm"], row(g_mix[l]), P["pool_w"][l], row(P["pool_scale"][l]))
        else:
            j = l - N_POOL_LAYERS
            do, delta_o, mla["w_o"][j] = _oproj_bwd(dh, s["o"], P["w_o"][j])
            dq, dk, dv = _attn_bwd(s["q"], kv["kn"], kv["kr"], kv["v"], do, s["lse"], delta_o)
            dks.append(dk)
            dvs.append(dv)
            dh, mla["w_dq"][j], mla["w_uq"][j], stack["mix_norm"][l], mla["q_latent_norm"][j] = _q_bwd(
                dh, s["hm"], s["cqp"], dq, tabs, row(g_mix[l]), P["w_dq"][j], row(q_latent_norm[j]), P["w_uq"][j])
        dh = ffn_backward(dh, s, 1, g_ffn1, l)
    grad_x = dh[SEQ_START:][None]
    for n, parts in stack.items():
        G[n] = parts if parts[0].shape[0] > 1 else jnp.concatenate(parts, axis=0)
    G["pool_w"] = jnp.stack(pool_dw)
    G["pool_scale"] = jnp.concatenate(pool_ds, axis=0)
    G["w_dq"], G["w_uq"], G["w_o"] = (jnp.stack(mla[n]) for n in ("w_dq", "w_uq", "w_o"))
    G["q_latent_norm"] = jnp.concatenate(mla["q_latent_norm"], axis=0)
    G["meta_tokens"] = dh[FRONT_PAD:SEQ_START]
    G["final_norm"] = d_final

    gpack = _pack([[_shards(n, g[None]) for g in G[n]] if isinstance(G[n], list) else _shards(n, G[n])
                   for n in SHARDED], 1, 8)
    R = gpack.shape[1]
    gpack = gpack.reshape(N_CHIPS, 2, R, PACK_W)
    my_core = lax.axis_index("c").astype(jnp.int32).reshape(1)
    my_chip = (2 * lax.axis_index("x") + lax.axis_index("y")).astype(jnp.int32).reshape(1)
    partial = _add_own(gpack, my_core, _sibling_exchange(gpack), "sum_cores")
    mine = _add_own(partial, my_chip, _chip_exchange(partial), "sum_chips")
    local_shapes = [W[n].shape for n in SHARDED]
    grads = dict(zip(SHARDED, _unpack(mine, local_shapes, 0, 8)))
    rep_shapes = [W[n].shape for n in REPLICATED]
    rpack = _pack([G[n].reshape(W[n].shape) for n in REPLICATED], 0, 8)
    grads.update(zip(REPLICATED, _unpack(_sum_lead(_all_gather(rpack), "sum_devices"), rep_shapes, 0, 8)))

    delta, new_m, new_v = {}, {}, {}
    for n in WEIGHTS:
        delta[n], new_m[n], new_v[n] = _adamw(W[n], grads[n], M[n], V[n])
    return (loss, grad_x, *[grads[n] for n in WEIGHTS], *[delta[n] for n in WEIGHTS],
            *[new_m[n] for n in WEIGHTS], *[new_v[n] for n in WEIGHTS])
```

```python
import functools
import math

import jax
import jax.numpy as jnp
from jax import lax
from jax.experimental import pallas as pl
from jax.experimental.pallas import tpu as pltpu

F32 = jnp.float32
BF16 = jnp.bfloat16
MESH = pl.DeviceIdType.MESH

N_DEV = 8
N_CHIPS = 4
DEPTH = 4
N_POOL_LAYERS = 2
N_HEADS = 8
QK_NOPE = 64
QK_ROPE = 32
V_HEAD = 64
KV_RANK = 256
Q_RANK = 384
HEAD_LANES = 128
ROPE_LANE0 = QK_NOPE
BIAS_LANE = QK_NOPE + QK_ROPE
ONES_LANE = V_HEAD
LOG2E = math.log2(math.e)
N_META = 16
CHUNK_SHIFT = 6
FRONT_PAD = 112
SEQ_START = FRONT_PAD + N_META
HALO = 16
POOL_WINDOWS = (2, 4, 8, 16)
EPS = 1e-6
ROPE_THETA = 10000.0
NEG = -1e30
PACK_W = 1024
PACK_ROW_MULT = 256
VMEM_LIMIT = 56 * 1024 * 1024

ADAM_LR = 0.001
ADAM_B1 = 0.9
ADAM_B2 = 0.999
ADAM_EPS = 1e-08
ADAM_WD = 0.01
ADAM_STEP = 10

SHARDED = ["ffn1_w_gate", "ffn1_w_up", "ffn1_w_down", "ffn2_w_gate", "ffn2_w_up", "ffn2_w_down",
           "pool_w", "w_dkv", "w_uk", "w_uv", "w_dq", "w_uq", "w_o", "meta_tokens", "pool_scale"]
SHARDED_F32 = ("meta_tokens", "pool_scale")
REPLICATED = ["ffn1_norm", "mix_norm", "ffn2_norm", "kv_in_norm", "kv_latent_norm", "q_latent_norm",
              "final_norm"]
WEIGHTS = ['meta_tokens', 'ffn1_norm', 'ffn1_w_gate', 'ffn1_w_up', 'ffn1_w_down', 'mix_norm', 'ffn2_norm',
           'ffn2_w_gate', 'ffn2_w_up', 'ffn2_w_down', 'pool_w', 'pool_scale', 'kv_in_norm', 'w_dkv',
           'kv_latent_norm', 'w_uk', 'w_uv', 'w_dq', 'q_latent_norm', 'w_uq', 'w_o', 'final_norm']


def _dot(a, b):
    return jnp.dot(a.astype(BF16), b.astype(BF16), preferred_element_type=F32)


def _dot_nt(a, b):
    return lax.dot_general(a.astype(BF16), b.astype(BF16), (((1,), (1,)), ((), ())),
                           preferred_element_type=F32)


def _dot_tn(a, b):
    return lax.dot_general(a.astype(BF16), b.astype(BF16), (((0,), (0,)), ((), ())),
                           preferred_element_type=F32)


def _sigmoid(x):
    return 1.0 / (1.0 + jnp.exp(-x))


def _rms(x, g):
    r = lax.rsqrt(jnp.mean(x * x, axis=-1, keepdims=True) + EPS)
    xh = x * r
    return xh * g, xh, r


def _rms_bwd(dy, xh, r, g):
    dxh = dy * g
    dx = r * (dxh - xh * jnp.mean(dxh * xh, axis=-1, keepdims=True))
    return dx, jnp.sum(dy * xh, axis=0, keepdims=True)


def _rope(x, c, s1, s2):
    return x * c + pltpu.roll(x, HEAD_LANES - QK_ROPE // 2, 1) * s1 + pltpu.roll(x, QK_ROPE // 2, 1) * s2


def _rope_t(d, c, s1, s2):
    return d * c + pltpu.roll(d * s1, QK_ROPE // 2, 1) + pltpu.roll(d * s2, HEAD_LANES - QK_ROPE // 2, 1)


def _params(*sem):
    return pltpu.CompilerParams(dimension_semantics=sem, vmem_limit_bytes=VMEM_LIMIT)


def _pick(n, candidates):
    for c in candidates:
        if n % c == 0:
            return c
    return n


def _row_tile(L):
    return _pick(L, (640, 128))


def _ff_tile(F):
    return _pick(F, (1408, 512, 256, 128))


def _row_call(name, body, L, row_ins, full_ins, row_outs, acc_outs):
    tm = _row_tile(L)
    n = L // tm
    hb = tm // HALO
    nb = L // HALO
    in_specs, args = [], []
    for arr, kind in row_ins:
        c = arr.shape[1]
        if kind == "tile":
            spec = pl.BlockSpec((tm, c), lambda i: (i, 0))
        elif kind == "prev":
            spec = pl.BlockSpec((HALO, c), lambda i: (jnp.maximum(i * hb - 1, 0), 0))
        else:
            spec = pl.BlockSpec((HALO, c), lambda i: (jnp.minimum((i + 1) * hb, nb - 1), 0))
        in_specs.append(spec)
        args.append(arr)
    for arr in full_ins:
        in_specs.append(pl.BlockSpec(arr.shape, lambda i, nd=arr.ndim: (0,) * nd))
        args.append(arr)
    out_shape = [jax.ShapeDtypeStruct((L, c), dt) for c, dt in row_outs]
    out_specs = [pl.BlockSpec((tm, c), lambda i: (i, 0)) for c, _ in row_outs]
    for shp in acc_outs:
        out_shape.append(jax.ShapeDtypeStruct(shp, F32))
        out_specs.append(pl.BlockSpec(shp, lambda i, nd=len(shp): (0,) * nd))
    n_in, n_ro = len(args), len(row_outs)

    def kern(*refs):
        i = pl.program_id(0)
        vals = [r[...] for r in refs[:n_in]]
        ro, ao = body(i, n, tm, *vals)
        for r, v in zip(refs[n_in:n_in + n_ro], ro):
            r[...] = v.astype(r.dtype)
        acc_refs = refs[n_in + n_ro:]

        @pl.when(i == 0)
        def _():
            for r in acc_refs:
                r[...] = jnp.zeros(r.shape, r.dtype)

        for r, v in zip(acc_refs, ao):
            r[...] += v

    return pl.pallas_call(kern, name=name, grid=(n,), in_specs=in_specs, out_specs=out_specs,
                          out_shape=out_shape, compiler_params=_params("arbitrary"))(*args)


def _ffn_fwd(h, gam, wg, wu, wd, l):
    L, D = h.shape
    F = wg.shape[2]
    tm = _row_tile(L)
    tf = _ff_tile(F)
    nF = F // tf

    def kern(h_ref, gam_ref, wg_ref, wu_ref, wd_ref, ho_ref, xn_ref, gs_ref, us_ref, acc):
        f = pl.program_id(1)

        @pl.when(f == 0)
        def _():
            xn, _, _ = _rms(h_ref[...], gam_ref[...])
            xn_ref[...] = xn.astype(BF16)
            acc[...] = jnp.zeros(acc.shape, F32)

        xnb = xn_ref[...]
        g = _dot(xnb, wg_ref[...])
        u = _dot(xnb, wu_ref[...])
        gs_ref[...] = g.astype(BF16)
        us_ref[...] = u.astype(BF16)
        acc[...] += _dot(g * _sigmoid(g) * u, wd_ref[...])

        @pl.when(f == nF - 1)
        def _():
            ho_ref[...] = h_ref[...] + 0.5 * acc[...]

    return pl.pallas_call(
        kern, name="ffn_fwd", grid=(L // tm, nF),
        in_specs=[pl.BlockSpec((tm, D), lambda i, f: (i, 0)),
                  pl.BlockSpec((None, 1, D), lambda i, f: (l, 0, 0)),
                  pl.BlockSpec((None, D, tf), lambda i, f: (l, 0, f)),
                  pl.BlockSpec((None, D, tf), lambda i, f: (l, 0, f)),
                  pl.BlockSpec((None, tf, D), lambda i, f: (l, f, 0))],
        out_specs=[pl.BlockSpec((tm, D), lambda i, f: (i, 0)),
                   pl.BlockSpec((tm, D), lambda i, f: (i, 0)),
                   pl.BlockSpec((tm, tf), lambda i, f: (i, f)),
                   pl.BlockSpec((tm, tf), lambda i, f: (i, f))],
        out_shape=[jax.ShapeDtypeStruct((L, D), F32), jax.ShapeDtypeStruct((L, D), BF16),
                   jax.ShapeDtypeStruct((L, F), BF16), jax.ShapeDtypeStruct((L, F), BF16)],
        scratch_shapes=[pltpu.VMEM((tm, D), F32)],
        compiler_params=_params("arbitrary", "arbitrary"))(h, gam, wg, wu, wd)


def _ffn_bwd(dh, h, gs, us, gam, wg, wu, wd, l):
    L, D = h.shape
    F = wg.shape[2]
    tm = _pick(L, (416, 128))
    tf = _ff_tile(F)
    nF = F // tf

    def kern(dh_ref, h_ref, gs_ref, us_ref, gam_ref, wg_ref, wu_ref, wd_ref,
             dhi_ref, dg_ref, du_ref, a_ref, dob_ref, dgam_ref, dxn):
        i = pl.program_id(0)
        f = pl.program_id(1)

        @pl.when(f == 0)
        def _():
            dxn[...] = jnp.zeros(dxn.shape, F32)
            dob_ref[...] = (0.5 * dh_ref[...]).astype(BF16)

        @pl.when((f == 0) & (i == 0))
        def _():
            dgam_ref[...] = jnp.zeros(dgam_ref.shape, F32)

        g = gs_ref[...].astype(F32)
        u = us_ref[...].astype(F32)
        sg = _sigmoid(g)
        silu = g * sg
        da = _dot_nt(dob_ref[...], wd_ref[...])
        a_ref[...] = (silu * u).astype(BF16)
        dgt = (da * u * (sg * (1.0 + g * (1.0 - sg)))).astype(BF16)
        dut = (da * silu).astype(BF16)
        dg_ref[...] = dgt
        du_ref[...] = dut
        dxn[...] += _dot_nt(dgt, wg_ref[...]) + _dot_nt(dut, wu_ref[...])

        @pl.when(f == nF - 1)
        def _():
            gamma = gam_ref[...]
            _, xh, r = _rms(h_ref[...], gamma)
            dx, dgam = _rms_bwd(dxn[...], xh, r, gamma)
            dhi_ref[...] = dh_ref[...] + dx
            dgam_ref[...] += dgam

    return pl.pallas_call(
        kern, name="ffn_bwd", grid=(L // tm, nF),
        in_specs=[pl.BlockSpec((tm, D), lambda i, f: (i, 0)),
                  pl.BlockSpec((tm, D), lambda i, f: (i, 0)),
                  pl.BlockSpec((tm, tf), lambda i, f: (i, f)),
                  pl.BlockSpec((tm, tf), lambda i, f: (i, f)),
                  pl.BlockSpec((None, 1, D), lambda i, f: (l, 0, 0)),
                  pl.BlockSpec((None, D, tf), lambda i, f: (l, 0, f)),
                  pl.BlockSpec((None, D, tf), lambda i, f: (l, 0, f)),
                  pl.BlockSpec((None, tf, D), lambda i, f: (l, f, 0))],
        out_specs=[pl.BlockSpec((tm, D), lambda i, f: (i, 0)),
                   pl.BlockSpec((tm, tf), lambda i, f: (i, f)),
                   pl.BlockSpec((tm, tf), lambda i, f: (i, f)),
                   pl.BlockSpec((tm, tf), lambda i, f: (i, f)),
                   pl.BlockSpec((tm, D), lambda i, f: (i, 0)),
                   pl.BlockSpec((1, D), lambda i, f: (0, 0))],
        out_shape=[jax.ShapeDtypeStruct((L, D), F32), jax.ShapeDtypeStruct((L, F), BF16),
                   jax.ShapeDtypeStruct((L, F), BF16), jax.ShapeDtypeStruct((L, F), BF16),
                   jax.ShapeDtypeStruct((L, D), BF16), jax.ShapeDtypeStruct((1, D), F32)],
        scratch_shapes=[pltpu.VMEM((tm, D), F32)],
        compiler_params=_params("arbitrary", "arbitrary"))(dh, h, gs, us, gam, wg, wu, wd)


def _mm_tn(a, b, name):
    L, M = a.shape
    N = b.shape[1]
    tm = _pick(M, (1408, 1024, 512))
    tn = _pick(N, (1408, 1024, 512))
    tk = _pick(L, (2080, 640, 128))

    def kern(a_ref, b_ref, o_ref):
        @pl.when(pl.program_id(2) == 0)
        def _():
            o_ref[...] = jnp.zeros(o_ref.shape, F32)

        o_ref[...] += _dot_tn(a_ref[...], b_ref[...])

    return pl.pallas_call(
        kern, name=name, grid=(M // tm, N // tn, L // tk),
        in_specs=[pl.BlockSpec((tk, tm), lambda i, j, k: (k, i)),
                  pl.BlockSpec((tk, tn), lambda i, j, k: (k, j))],
        out_specs=pl.BlockSpec((tm, tn), lambda i, j, k: (i, j)),
        out_shape=jax.ShapeDtypeStruct((M, N), F32),
        compiler_params=_params("arbitrary", "arbitrary", "arbitrary"))(a, b)


def _pool_counts(pos, w):
    return jnp.clip(pos - (FRONT_PAD - 1), 1, w).astype(F32)


def _pool_forward_values(i, tm, h, hprev, gamma, D):
    cg = D // len(POOL_WINDOWS)
    hext = jnp.concatenate([hprev, h], axis=0)
    uext, xh, r = _rms(hext, gamma)
    pos = i * tm + lax.broadcasted_iota(jnp.int32, (tm, 1), 0)
    pooled = []
    for gi, w in enumerate(POOL_WINDOWS):
        s = uext[:, gi * cg:(gi + 1) * cg]
        span = 1
        while span < w:
            s = s + pltpu.roll(s, span, 0)
            span *= 2
        s = s[HALO:]
        pooled.append(s / _pool_counts(pos, w) - uext[HALO:, gi * cg:(gi + 1) * cg])
    return uext, xh[HALO:], r[HALO:], pooled


def _pool_fwd(h, gam, w, scale, l):
    L, D = h.shape
    cg = D // len(POOL_WINDOWS)

    def body(i, n, tm, ht, hprev, gamma, wv, sc):
        _, _, _, pooled = _pool_forward_values(i, tm, ht, hprev, gamma, D)
        ys = [_dot(pooled[gi], wv[gi]) for gi in range(len(POOL_WINDOWS))]
        y = jnp.concatenate(ys, axis=1) * sc
        return [ht + y], []

    del cg
    return _row_call("pool_fwd", body, L, [(h, "tile"), (h, "prev")], [gam, w, scale], [(D, F32)], [])[0]


def _pool_bwd(dy, h, gam, w, scale):
    L, D = h.shape
    ng = len(POOL_WINDOWS)
    cg = D // ng

    def body(i, n, tm, ht, hprev, dyt, dynext, gamma, wv, sc):
        _, xh, r, pooled = _pool_forward_values(i, tm, ht, hprev, gamma, D)
        dynext = jnp.where(i == n - 1, jnp.zeros_like(dynext), dynext)
        dyext = jnp.concatenate([dyt, dynext], axis=0) * sc
        pos_ext = i * tm + lax.broadcasted_iota(jnp.int32, (tm + HALO, 1), 0)
        dws, dscs, dus = [], [], []
        for gi, wd in enumerate(POOL_WINDOWS):
            cols = slice(gi * cg, (gi + 1) * cg)
            pb = pooled[gi].astype(BF16)
            ypre = _dot(pb, wv[gi])
            dscs.append(jnp.sum(dyt[:, cols] * ypre, axis=0, keepdims=True))
            dws.append(_dot_tn(pb, dyext[:tm, cols])[None])
            dp = _dot_nt(dyext[:, cols], wv[gi])
            s = dp / _pool_counts(pos_ext, wd)
            span = 1
            while span < wd:
                s = s + pltpu.roll(s, tm + HALO - span, 0)
                span *= 2
            dus.append(s[:tm] - dp[:tm])
        du = jnp.concatenate(dus, axis=1)
        pos = pos_ext[:tm]
        du = jnp.where(pos >= FRONT_PAD, du, 0.0)
        dx, dgam = _rms_bwd(du, xh, r, gamma)
        return [dyt + dx], [jnp.concatenate(dws, axis=0), jnp.concatenate(dscs, axis=1), dgam]

    return _row_call("pool_bwd", body, L, [(h, "tile"), (h, "prev"), (dy, "tile"), (dy, "next")],
                     [gam, w, scale], [(D, F32)], [(ng, cg, cg), (1, D), (1, D)])


def _kv_fwd(h, tabs, g1, wdkv, g2, wuk, wuv):
    L, D = h.shape
    hw = N_HEADS * HEAD_LANES

    def body(i, n, tm, ht, ck, s1, s2, g1v, wdkv_v, g2v, wuk_v, wuv_v):
        xkv, _, _ = _rms(ht, g1v)
        ckr = _dot(xkv, wdkv_v)
        ckv, _, _ = _rms(ckr[:, :KV_RANK], g2v)
        krope = _rope(ckr[:, KV_RANK:], ck, s1, s2)
        pos = i * tm + lax.broadcasted_iota(jnp.int32, (tm, HEAD_LANES), 0)
        lane = lax.broadcasted_iota(jnp.int32, (tm, HEAD_LANES), 1)
        krope = jnp.where((pos < FRONT_PAD) & (lane == BIAS_LANE), NEG, krope)
        ones = ((lax.broadcasted_iota(jnp.int32, (1, hw), 1) & (HEAD_LANES - 1)) == ONES_LANE).astype(F32)
        return [_dot(ckv, wuk_v), krope, _dot(ckv, wuv_v) + ones, ckr], []

    ck, s1, s2 = tabs["ck"], tabs["s1"], tabs["s2"]
    return _row_call("kv_fwd", body, L, [(h, "tile"), (ck, "tile"), (s1, "tile"), (s2, "tile")],
                     [g1, wdkv, g2, wuk, wuv],
                     [(hw, BF16), (HEAD_LANES, BF16), (hw, BF16), (KV_RANK + HEAD_LANES, F32)], [])


def _kv_bwd(dh, h, ckr, dks, dvs, tabs, g1, wdkv, g2, wuk, wuv):
    L, D = h.shape
    hw = N_HEADS * HEAD_LANES
    nl = len(dks)

    def body(i, n, tm, *vals):
        dht, ht, ckr_t = vals[:3]
        dk = sum(vals[3:3 + nl][1:], vals[3])
        dv = sum(vals[3 + nl:3 + 2 * nl][1:], vals[3 + nl])
        ck, s1, s2, g1v, wdkv_v, g2v, wuk_v, wuv_v = vals[3 + 2 * nl:]
        xkv, xh1, r1 = _rms(ht, g1v)
        ckv, xh2, r2 = _rms(ckr_t[:, :KV_RANK], g2v)
        dckv = _dot_nt(dk, wuk_v) + _dot_nt(dv, wuv_v)
        dlat, dg2 = _rms_bwd(dckv, xh2, r2, g2v)
        dkr = dk[:, :HEAD_LANES]
        for hd in range(1, N_HEADS):
            dkr = dkr + dk[:, hd * HEAD_LANES:(hd + 1) * HEAD_LANES]
        dckr = jnp.concatenate([dlat, _rope_t(dkr, ck, s1, s2)], axis=1)
        dx, dg1 = _rms_bwd(_dot_nt(dckr, wdkv_v), xh1, r1, g1v)
        return [dht + dx], [_dot_tn(xkv, dckr), _dot_tn(ckv, dk), _dot_tn(ckv, dv), dg1, dg2]

    row_ins = [(dh, "tile"), (h, "tile"), (ckr, "tile")] + [(a, "tile") for a in dks + dvs]
    row_ins += [(tabs[k], "tile") for k in ("ck", "s1", "s2")]
    return _row_call("kv_bwd", body, L, row_ins, [g1, wdkv, g2, wuk, wuv], [(D, F32)],
                     [(D, KV_RANK + HEAD_LANES), (KV_RANK, hw), (KV_RANK, hw), (1, D), (1, KV_RANK)])


def _q_fwd(h, tabs, g, wdq, gq, wuq):
    L, D = h.shape
    hw = N_HEADS * HEAD_LANES

    def body(i, n, tm, ht, cq_t, s1, s2, gv, wdq_v, gqv, wuq_v):
        u, _, _ = _rms(ht, gv)
        cqp = _dot(u, wdq_v)
        cq, _, _ = _rms(cqp, gqv)
        qp = _dot(cq, wuq_v)
        bias = (lax.broadcasted_iota(jnp.int32, (1, HEAD_LANES), 1) == BIAS_LANE).astype(F32)
        q = [_rope(qp[:, hd * HEAD_LANES:(hd + 1) * HEAD_LANES], cq_t, s1, s2) * (SM_SCALE * LOG2E) + bias
             for hd in range(N_HEADS)]
        return [jnp.concatenate(q, axis=1), cqp], []

    return _row_call("q_fwd", body, L, [(h, "tile")] + [(tabs[k], "tile") for k in ("cq", "s1", "s2")],
                     [g, wdq, gq, wuq], [(hw, BF16), (Q_RANK, F32)], [])


def _q_bwd(dh, h, cqp, dq, tabs, g, wdq, gq, wuq):
    L, D = h.shape
    hw = N_HEADS * HEAD_LANES

    def body(i, n, tm, dht, ht, cqp_t, dq_t, cq_t, s1, s2, gv, wdq_v, gqv, wuq_v):
        u, xh1, r1 = _rms(ht, gv)
        cq, xh2, r2 = _rms(cqp_t, gqv)
        dqp = jnp.concatenate([_rope_t(dq_t[:, hd * HEAD_LANES:(hd + 1) * HEAD_LANES], cq_t, s1, s2)
                               for hd in range(N_HEADS)], axis=1)
        dcqp, dgq = _rms_bwd(_dot_nt(dqp, wuq_v), xh2, r2, gqv)
        dx, dg = _rms_bwd(_dot_nt(dcqp, wdq_v), xh1, r1, gv)
        return [dht + dx], [_dot_tn(u, dcqp), _dot_tn(cq, dqp), dg, dgq]

    row_ins = [(dh, "tile"), (h, "tile"), (cqp, "tile"), (dq, "tile")]
    row_ins += [(tabs[k], "tile") for k in ("cq", "s1", "s2")]
    return _row_call("q_bwd", body, L, row_ins, [g, wdq, gq, wuq], [(D, F32)],
                     [(D, Q_RANK), (Q_RANK, hw), (1, D), (1, Q_RANK)])


def _oproj_fwd(h, o, wo):
    L, D = h.shape

    def body(i, n, tm, ht, ot, wov):
        return [ht + _dot(ot, wov)], []

    return _row_call("oproj_fwd", body, L, [(h, "tile"), (o, "tile")], [wo], [(D, F32)], [])[0]


def _oproj_bwd(dh, o, wo):
    L, D = dh.shape
    hw = N_HEADS * HEAD_LANES

    def body(i, n, tm, dht, ot, wov):
        do = _dot_nt(dht, wov)
        prod = do * ot.astype(F32)
        delta = [jnp.broadcast_to(jnp.sum(prod[:, hd * HEAD_LANES:(hd + 1) * HEAD_LANES], axis=-1, keepdims=True),
                                  (tm, HEAD_LANES)) for hd in range(N_HEADS)]
        return [do, jnp.concatenate(delta, axis=1)], [_dot_tn(ot, dht)]

    return _row_call("oproj_bwd", body, L, [(dh, "tile"), (o, "tile")], [wo], [(hw, BF16), (hw, F32)], [(hw, D)])


def _causal(t):
    qpos = lax.broadcasted_iota(jnp.int32, (t, t), 0)
    kpos = lax.broadcasted_iota(jnp.int32, (t, t), 1)
    return (kpos >> CHUNK_SHIFT) <= (qpos >> CHUNK_SHIFT)


SM_SCALE = 1.0 / math.sqrt(QK_NOPE + QK_ROPE)


def _pairs(n, key_major):
    if key_major:
        order = [(i, j) for j in range(n) for i in range(j, n)]
    else:
        order = [(i, j) for i in range(n) for j in range(i + 1)]
    return (jnp.array([p[0] for p in order], jnp.int32), jnp.array([p[1] for p in order], jnp.int32))


def _attn_fwd(q, kn, kr, v):
    L = q.shape[0]
    hw = N_HEADS * HEAD_LANES
    t = _row_tile(L)
    it, jt = _pairs(L // t, key_major=False)

    def kern(it_ref, jt_ref, q_ref, kn_ref, kr_ref, v_ref, o_ref, lse_ref, m_s, acc_s):
        step = pl.program_id(1)
        i, j = it_ref[step], jt_ref[step]

        @pl.when(j == 0)
        def _():
            m_s[...] = jnp.full(m_s.shape, NEG, F32)
            acc_s[...] = jnp.zeros(acc_s.shape, F32)

        def update(diagonal):
            k = kn_ref[...] + kr_ref[...]
            s = _dot_nt(q_ref[...], k)
            if diagonal:
                s = jnp.where(_causal(t), s, NEG)
            m_prev = m_s[:, :1]
            m_new = jnp.maximum(m_prev, jnp.max(s, axis=-1, keepdims=True))
            p = jnp.exp2(s - m_new)
            acc_s[...] = jnp.exp2(m_prev - m_new) * acc_s[...] + _dot(p, v_ref[...])
            m_s[...] = jnp.broadcast_to(m_new, m_s.shape)

        @pl.when(j < i)
        def _():
            update(False)

        @pl.when(j == i)
        def _():
            update(True)
            acc = acc_s[...]
            total = acc[:, ONES_LANE:ONES_LANE + 1]
            o_ref[...] = (acc / total).astype(BF16)
            lse_ref[...] = m_s[...] + jnp.log2(jnp.broadcast_to(total, m_s.shape))

    qmap = lambda h, s, it, jt: (it[s], h)
    kmap = lambda h, s, it, jt: (jt[s], h)
    blk = (t, HEAD_LANES)
    return pl.pallas_call(
        kern, name="attn_fwd",
        grid_spec=pltpu.PrefetchScalarGridSpec(
            num_scalar_prefetch=2, grid=(N_HEADS, it.shape[0]),
            in_specs=[pl.BlockSpec(blk, qmap), pl.BlockSpec(blk, kmap),
                      pl.BlockSpec(blk, lambda h, s, it, jt: (jt[s], 0)), pl.BlockSpec(blk, kmap)],
            out_specs=[pl.BlockSpec(blk, qmap), pl.BlockSpec(blk, qmap)],
            scratch_shapes=[pltpu.VMEM(blk, F32)] * 2),
        out_shape=[jax.ShapeDtypeStruct((L, hw), BF16), jax.ShapeDtypeStruct((L, hw), F32)],
        compiler_params=_params("arbitrary", "arbitrary"))(it, jt, q, kn, kr, v)


def _attn_bwd(q, kn, kr, v, do, lse, delta):
    L = q.shape[0]
    hw = N_HEADS * HEAD_LANES
    t = _row_tile(L)
    it, jt = _pairs(L // t, key_major=True)

    def kern(it_ref, jt_ref, q_ref, kn_ref, kr_ref, v_ref, do_ref, lse_ref, dl_ref, dq_ref, dk_ref, dv_ref):
        step = pl.program_id(1)
        i, j = it_ref[step], jt_ref[step]

        @pl.when(step == 0)
        def _():
            dq_ref[...] = jnp.zeros(dq_ref.shape, F32)

        @pl.when(i == j)
        def _():
            dk_ref[...] = jnp.zeros(dk_ref.shape, F32)
            dv_ref[...] = jnp.zeros(dv_ref.shape, F32)

        def update(diagonal):
            k = kn_ref[...] + kr_ref[...]
            qv, dov = q_ref[...], do_ref[...]
            s = _dot_nt(qv, k)
            if diagonal:
                s = jnp.where(_causal(t), s, NEG)
            p = jnp.exp2(s - lse_ref[:, :1])
            dp = _dot_nt(dov, v_ref[...])
            dz = (p * (dp - dl_ref[:, :1])).astype(BF16)
            dv_ref[...] += _dot_tn(p, dov)
            dk_ref[...] += _dot_tn(dz, qv) * (1.0 / LOG2E)
            rows = pl.ds(pl.multiple_of(i * t, t), t)
            dq_ref[rows, :] += _dot(dz, k) * SM_SCALE

        @pl.when(j < i)
        def _():
            update(False)

        @pl.when(j == i)
        def _():
            update(True)

    qmap = lambda h, s, it, jt: (it[s], h)
    kmap = lambda h, s, it, jt: (jt[s], h)
    blk = (t, HEAD_LANES)
    return pl.pallas_call(
        kern, name="attn_bwd",
        grid_spec=pltpu.PrefetchScalarGridSpec(
            num_scalar_prefetch=2, grid=(N_HEADS, it.shape[0]),
            in_specs=[pl.BlockSpec(blk, qmap), pl.BlockSpec(blk, kmap),
                      pl.BlockSpec(blk, lambda h, s, it, jt: (jt[s], 0)), pl.BlockSpec(blk, kmap),
                      pl.BlockSpec(blk, qmap), pl.BlockSpec(blk, qmap), pl.BlockSpec(blk, qmap)],
            out_specs=[pl.BlockSpec((L, HEAD_LANES), lambda h, s, it, jt: (0, h)),
                       pl.BlockSpec(blk, kmap), pl.BlockSpec(blk, kmap)]),
        out_shape=[jax.ShapeDtypeStruct((L, hw), F32)] * 3,
        compiler_params=_params("arbitrary", "arbitrary"))(it, jt, q, kn, kr, v, do, lse, delta)


def _head(h, target, g):
    L, D = h.shape

    def body(i, n, tm, ht, tt, gv):
        y, xh, r = _rms(ht, gv)
        pos = i * tm + lax.broadcasted_iota(jnp.int32, (tm, 1), 0)
        e = jnp.where(pos >= SEQ_START, y - tt, 0.0)
        loss = 0.5 * jnp.sum(jnp.mean(e * e, axis=-1, keepdims=True), axis=0, keepdims=True)
        dx, dg = _rms_bwd(e / D, xh, r, gv)
        return [dx], [jnp.broadcast_to(loss, (1, 128)), dg]

    return _row_call("loss_head", body, L, [(h, "tile"), (target, "tile")], [g], [(D, F32)], [(1, 128), (1, D)])


ANY = pl.BlockSpec(memory_space=pl.ANY)


def _coords():
    return lax.axis_index("x"), lax.axis_index("y"), lax.axis_index("c")


def _all_gather(x):
    R, W = x.shape

    def kern(x_ref, out_ref, send_sems, recv_sems, local_sem):
        mx, my, mc = _coords()
        me, sibling = (mx, my, mc), (mx, my, 1 - mc)
        chips = [(1 - mx, my), (mx, 1 - my), (1 - mx, 1 - my)]

        def slot(px, py, pc):
            return out_ref.at[4 * px + 2 * py + pc]

        def copy(k, block, to, src=None):
            return pltpu.make_async_remote_copy(
                src_ref=slot(*block) if src is None else src, dst_ref=slot(*block),
                send_sem=send_sems.at[k], recv_sem=recv_sems.at[k], device_id=to, device_id_type=MESH)

        mine = pltpu.make_async_copy(x_ref, slot(*me), local_sem)
        mine.start()
        first = [copy(0, me, sibling, src=x_ref)]
        first += [copy(1 + n, me, (*chip, mc), src=x_ref) for n, chip in enumerate(chips)]
        for cp in first:
            cp.start()
        passed = [copy(4 + n, (*chip, mc), sibling) for n, chip in enumerate(chips)]
        for n, chip in enumerate(chips):
            copy(1 + n, (*chip, mc), me).wait_recv()
            passed[n].start()
        copy(0, sibling, me).wait_recv()
        for n, chip in enumerate(chips):
            copy(4 + n, (*chip, 1 - mc), me).wait_recv()
        for cp in first + passed:
            cp.wait_send()
        mine.wait()

    return pl.pallas_call(
        kern, name="all_gather_%s" % jnp.dtype(x.dtype).name, in_specs=[ANY], out_specs=ANY,
        out_shape=jax.ShapeDtypeStruct((N_DEV, R, W), x.dtype),
        scratch_shapes=[pltpu.SemaphoreType.DMA((7,)), pltpu.SemaphoreType.DMA((7,)), pltpu.SemaphoreType.DMA],
    )(x)


def _sibling_exchange(g):
    _, _, R, W = g.shape

    def kern(g_ref, out_ref, send_sems, recv_sems):
        mx, my, mc = _coords()
        copies = [pltpu.make_async_remote_copy(
            src_ref=g_ref.at[n, 1 - mc], dst_ref=out_ref.at[n], send_sem=send_sems.at[n],
            recv_sem=recv_sems.at[n], device_id=(mx, my, 1 - mc), device_id_type=MESH) for n in range(N_CHIPS)]
        for cp in copies:
            cp.start()
        for cp in copies:
            cp.wait_recv()
        for cp in copies:
            cp.wait_send()

    return pl.pallas_call(
        kern, name="sibling_exchange", in_specs=[ANY], out_specs=ANY,
        out_shape=jax.ShapeDtypeStruct((N_CHIPS, R, W), g.dtype),
        scratch_shapes=[pltpu.SemaphoreType.DMA((N_CHIPS,)), pltpu.SemaphoreType.DMA((N_CHIPS,))],
    )(g)


def _chip_exchange(p):
    _, R, W = p.shape

    def kern(p_ref, out_ref, send_sems, recv_sems):
        mx, my, mc = _coords()
        chips = [(1 - mx, my), (mx, 1 - my), (1 - mx, 1 - my)]
        copies = [pltpu.make_async_remote_copy(
            src_ref=p_ref.at[2 * cx + cy], dst_ref=out_ref.at[n], send_sem=send_sems.at[n],
            recv_sem=recv_sems.at[n], device_id=(cx, cy, mc), device_id_type=MESH)
            for n, (cx, cy) in enumerate(chips)]
        for cp in copies:
            cp.start()
        for cp in copies:
            cp.wait_recv()
        for cp in copies:
            cp.wait_send()

    return pl.pallas_call(
        kern, name="chip_exchange", in_specs=[ANY], out_specs=ANY,
        out_shape=jax.ShapeDtypeStruct((3, R, W), p.dtype),
        scratch_shapes=[pltpu.SemaphoreType.DMA((3,)), pltpu.SemaphoreType.DMA((3,))],
    )(p)


def _add_own(own, sel, other, name):
    R, W = own.shape[-2:]
    tr = _pick(R, (PACK_ROW_MULT, 8))
    if own.ndim == 4:
        n = own.shape[0]
        grid = (n, R // tr)
        in_specs = [pl.BlockSpec((None, None, tr, W), lambda b, i, sel: (b, sel[0], i, 0)),
                    pl.BlockSpec((None, tr, W), lambda b, i, sel: (b, i, 0))]
        out_spec = [pl.BlockSpec((None, tr, W), lambda b, i, sel: (b, i, 0))] * 2
        out_shape = [jax.ShapeDtypeStruct((n, R, W), F32), jax.ShapeDtypeStruct((n, R, W), BF16)]

        def kern(sel_ref, own_ref, other_ref, o_ref, ob_ref):
            acc = own_ref[...] + other_ref[...]
            o_ref[...] = acc
            ob_ref[...] = acc.astype(BF16)
    else:
        k = other.shape[0]
        grid = (1, R // tr)
        in_specs = [pl.BlockSpec((None, tr, W), lambda b, i, sel: (sel[0], i, 0)),
                    pl.BlockSpec((k, tr, W), lambda b, i, sel: (0, i, 0))]
        out_spec = pl.BlockSpec((tr, W), lambda b, i, sel: (i, 0))
        out_shape = jax.ShapeDtypeStruct((R, W), F32)

        def kern(sel_ref, own_ref, other_ref, o_ref):
            acc = own_ref[...]
            for m in range(k):
                acc = acc + other_ref[m].astype(F32)
            o_ref[...] = acc

    return pl.pallas_call(
        kern, name=name,
        grid_spec=pltpu.PrefetchScalarGridSpec(num_scalar_prefetch=1, grid=grid, in_specs=in_specs,
                                               out_specs=out_spec),
        out_shape=out_shape, compiler_params=_params("arbitrary", "arbitrary"))(sel, own, other)


def _sum_lead(x, name):
    n, R, W = x.shape
    tr = _pick(R, (PACK_ROW_MULT, 8))

    def kern(x_ref, o_ref):
        acc = x_ref[0]
        for k in range(1, n):
            acc = acc + x_ref[k]
        o_ref[...] = acc

    return pl.pallas_call(
        kern, name=name, grid=(R // tr,),
        in_specs=[pl.BlockSpec((n, tr, W), lambda i: (0, i, 0))],
        out_specs=pl.BlockSpec((tr, W), lambda i: (i, 0)),
        out_shape=jax.ShapeDtypeStruct((R, W), F32), compiler_params=_params("arbitrary"))(x)


def _adamw(w, g, m, v):
    shape = w.shape
    cols = shape[-1]
    rows = w.size // cols
    tr = _pick(rows, (512, 352, 256, 128))
    if rows * cols * 4 <= (1 << 20):
        tr = rows

    def kern(w_ref, g_ref, m_ref, v_ref, d_ref, mo_ref, vo_ref):
        gv = g_ref[...]
        mn = ADAM_B1 * m_ref[...] + (1.0 - ADAM_B1) * gv
        vn = ADAM_B2 * v_ref[...] + (1.0 - ADAM_B2) * (gv * gv)
        m_hat = mn / (1.0 - ADAM_B1 ** ADAM_STEP)
        v_hat = vn / (1.0 - ADAM_B2 ** ADAM_STEP)
        d_ref[...] = -ADAM_LR * (m_hat / (jnp.sqrt(v_hat) + ADAM_EPS) + ADAM_WD * w_ref[...])
        mo_ref[...] = mn
        vo_ref[...] = vn

    spec = pl.BlockSpec((tr, cols), lambda i: (i, 0))
    outs = pl.pallas_call(
        kern, name="adamw", grid=(rows // tr,), in_specs=[spec] * 4, out_specs=[spec] * 3,
        out_shape=[jax.ShapeDtypeStruct((rows, cols), F32)] * 3, compiler_params=_params("arbitrary"),
    )(*[a.reshape(rows, cols) for a in (w, g, m, v)])
    return [o.reshape(shape) for o in outs]


def _pack(arrs, n_lead, row_mult):
    parts, total = [], 0
    for n, a in enumerate(arrs):
        pieces = a if isinstance(a, list) else [a]
        lead = pieces[0].shape[:n_lead]
        flat = [p.reshape(lead + (-1,)) for p in pieces]
        size = sum(f.shape[-1] for f in flat)
        rows = -(-size // (PACK_W * row_mult)) * row_mult
        if n == len(arrs) - 1:
            rows += -(total + rows) % PACK_ROW_MULT
        total += rows
        if rows * PACK_W > size:
            flat.append(jnp.zeros(lead + (rows * PACK_W - size,), flat[0].dtype))
        parts.append(jnp.concatenate(flat, axis=n_lead).reshape(lead + (rows, PACK_W)))
    return jnp.concatenate(parts, axis=n_lead)


def _unpack(pack, shapes, n_lead, row_mult):
    outs, row = [], 0
    lead = pack.shape[:n_lead]
    for shp in shapes:
        size = math.prod(shp)
        rows = -(-size // (PACK_W * row_mult)) * row_mult
        blk = lax.slice_in_dim(pack, row, row + rows, axis=n_lead)
        outs.append(blk.reshape(lead + (-1,))[..., :size].reshape(lead + tuple(shp)))
        row += rows
    return outs


def _to_words(a):
    return lax.bitcast_convert_type(a, BF16)


def _from_words(a):
    return lax.bitcast_convert_type(a, F32)


def _pad_axis(a, axis, size):
    pads = [(0, 0)] * a.ndim
    pads[axis] = (0, size - a.shape[axis])
    return jnp.pad(a, pads)


def _dense(name, s):
    if name.endswith("w_gate") or name.endswith("w_up"):
        _, nl, d, fs = s.shape
        return s.transpose(1, 2, 0, 3).reshape(nl, d, N_DEV * fs)
    if name.endswith("w_down"):
        _, nl, fs, d = s.shape
        return s.transpose(1, 0, 2, 3).reshape(nl, N_DEV * fs, d)
    if name == "pool_w":
        _, nl, ng, r, cg = s.shape
        return s.transpose(1, 2, 0, 3, 4).reshape(nl, ng, cg, cg)
    if name == "w_dkv":
        w = s.reshape(-1, s.shape[2])
        z = lambda n: jnp.zeros((w.shape[0], n), w.dtype)
        return jnp.concatenate([w[:, :KV_RANK], z(ROPE_LANE0), w[:, KV_RANK:],
                                z(HEAD_LANES - ROPE_LANE0 - QK_ROPE)], axis=1)
    if name in ("w_uk", "w_uv"):
        return _pad_axis(s.transpose(1, 0, 2), 2, HEAD_LANES).reshape(KV_RANK, N_HEADS * HEAD_LANES)
    if name == "w_dq":
        _, nl, ds, r = s.shape
        return s.transpose(1, 0, 2, 3).reshape(nl, N_DEV * ds, r)
    if name == "w_uq":
        nl = s.shape[1]
        return _pad_axis(s.transpose(1, 2, 0, 3), 3, HEAD_LANES).reshape(nl, Q_RANK, N_HEADS * HEAD_LANES)
    if name == "w_o":
        _, nl, k, dc = s.shape
        w = s.transpose(1, 2, 0, 3).reshape(nl, N_HEADS, V_HEAD, N_DEV * dc)
        return _pad_axis(w, 2, HEAD_LANES).reshape(nl, N_HEADS * HEAD_LANES, N_DEV * dc)
    if name in ("meta_tokens", "pool_scale"):
        r, dc = s.shape[1:]
        return s.transpose(1, 0, 2).reshape(r, N_DEV * dc)
    raise ValueError(name)


def _shards(name, g):
    if name.endswith("w_gate") or name.endswith("w_up"):
        nl, d, f = g.shape
        return g.reshape(nl, d, N_DEV, f // N_DEV).transpose(2, 0, 1, 3)
    if name.endswith("w_down"):
        nl, f, d = g.shape
        return g.reshape(nl, N_DEV, f // N_DEV, d).transpose(1, 0, 2, 3)
    if name == "pool_w":
        nl, ng, cg, _ = g.shape
        return g.reshape(nl, ng, N_DEV, cg // N_DEV, cg).transpose(2, 0, 1, 3, 4)
    if name == "w_dkv":
        w = jnp.concatenate([g[:, :KV_RANK], g[:, KV_RANK + ROPE_LANE0:KV_RANK + ROPE_LANE0 + QK_ROPE]], axis=1)
        return w.reshape(N_DEV, -1, KV_RANK + QK_ROPE)
    if name in ("w_uk", "w_uv"):
        return g.reshape(KV_RANK, N_HEADS, HEAD_LANES)[:, :, :V_HEAD].transpose(1, 0, 2)
    if name == "w_dq":
        nl, d, r = g.shape
        return g.reshape(nl, N_DEV, d // N_DEV, r).transpose(1, 0, 2, 3)
    if name == "w_uq":
        nl = g.shape[0]
        return g.reshape(nl, Q_RANK, N_HEADS, HEAD_LANES)[..., :QK_NOPE + QK_ROPE].transpose(2, 0, 1, 3)
    if name == "w_o":
        nl, _, d = g.shape
        w = g.reshape(nl, N_HEADS, HEAD_LANES, d)[:, :, :V_HEAD].reshape(nl, N_HEADS * V_HEAD, N_DEV, d // N_DEV)
        return w.transpose(2, 0, 1, 3)
    if name in ("meta_tokens", "pool_scale"):
        r, d = g.shape
        return g.reshape(r, N_DEV, d // N_DEV).transpose(1, 0, 2)
    raise ValueError(name)


def _rope_tables(L):
    pos = jnp.maximum(jnp.arange(L) - FRONT_PAD, 0).astype(F32)
    inv = 1.0 / (ROPE_THETA ** (jnp.arange(0, QK_ROPE, 2, dtype=F32) / QK_ROPE))
    ang = pos[:, None] * inv[None, :]
    cos, sin = jnp.cos(ang), jnp.sin(ang)
    half = QK_ROPE // 2
    z = lambda n: jnp.zeros((L, n), F32)
    tail = z(HEAD_LANES - ROPE_LANE0 - QK_ROPE)
    return {
        "cq": jnp.concatenate([jnp.ones((L, ROPE_LANE0), F32), cos, cos, tail], axis=1),
        "ck": jnp.concatenate([z(ROPE_LANE0), cos, cos, tail], axis=1),
        "s1": jnp.concatenate([z(ROPE_LANE0), -sin, z(half), tail], axis=1),
        "s2": jnp.concatenate([z(ROPE_LANE0), z(half), sin, tail], axis=1),
    }


def kernel(x, meta_tokens, ffn1_norm, ffn1_w_gate, ffn1_w_up, ffn1_w_down, mix_norm, ffn2_norm, ffn2_w_gate, ffn2_w_up, ffn2_w_down, pool_w, pool_scale, kv_in_norm, w_dkv, kv_latent_norm, w_uk, w_uv, w_dq, q_latent_norm, w_uq, w_o, final_norm, loss_target, m_meta_tokens, m_ffn1_norm, m_ffn1_w_gate, m_ffn1_w_up, m_ffn1_w_down, m_mix_norm, m_ffn2_norm, m_ffn2_w_gate, m_ffn2_w_up, m_ffn2_w_down, m_pool_w, m_pool_scale, m_kv_in_norm, m_w_dkv, m_kv_latent_norm, m_w_uk, m_w_uv, m_w_dq, m_q_latent_norm, m_w_uq, m_w_o, m_final_norm, v_meta_tokens, v_ffn1_norm, v_ffn1_w_gate, v_ffn1_w_up, v_ffn1_w_down, v_mix_norm, v_ffn2_norm, v_ffn2_w_gate, v_ffn2_w_up, v_ffn2_w_down, v_pool_w, v_pool_scale, v_kv_in_norm, v_w_dkv, v_kv_latent_norm, v_w_uk, v_w_uv, v_w_dq, v_q_latent_norm, v_w_uq, v_w_o, v_final_norm):
    args = dict(locals())
    W = {n: args[n] for n in WEIGHTS}
    M = {n: args["m_" + n] for n in WEIGHTS}
    V = {n: args["v_" + n] for n in WEIGHTS}
    seq, D = x.shape[1], x.shape[2]
    L = SEQ_START + seq

    shard_shapes = [W[n].shape + ((2,) if n in SHARDED_F32 else ()) for n in SHARDED]
    wpack = _pack([_to_words(W[n]) if n in SHARDED_F32 else W[n].astype(BF16) for n in SHARDED], 0, 16)
    gathered = _unpack(_all_gather(wpack), shard_shapes, 1, 16)
    P = {}
    for n, s in zip(SHARDED, gathered):
        P[n] = _dense(n, _from_words(s) if n in SHARDED_F32 else s)
    norm3 = lambda a: a.reshape(a.shape[0], 1, a.shape[-1])
    row = lambda a: a.reshape(1, -1)
    g_ffn1, g_mix, g_ffn2 = norm3(ffn1_norm), mix_norm, norm3(ffn2_norm)

    h = jnp.concatenate([jnp.zeros((FRONT_PAD, D), F32), P["meta_tokens"], x[0]], axis=0)
    target = jnp.concatenate([jnp.zeros((SEQ_START, D), F32), loss_target[0]], axis=0)
    tabs = _rope_tables(L)
    saved = []
    kv = None
    for l in range(DEPTH):
        s = {"h1": h}
        h, s["xn1"], s["g1"], s["u1"] = _ffn_fwd(h, g_ffn1, P["ffn1_w_gate"], P["ffn1_w_up"], P["ffn1_w_down"], l)
        s["hm"] = h
        if l < N_POOL_LAYERS:
            h = _pool_fwd(h, row(g_mix[l]), P["pool_w"][l], row(P["pool_scale"][l]), l)
        else:
            j = l - N_POOL_LAYERS
            s["q"], s["cqp"] = _q_fwd(h, tabs, row(g_mix[l]), P["w_dq"][j], row(q_latent_norm[j]), P["w_uq"][j])
            s["o"], s["lse"] = _attn_fwd(s["q"], kv["kn"], kv["kr"], kv["v"])
            h = _oproj_fwd(h, s["o"], P["w_o"][j])
        s["h2"] = h
        h, s["xn2"], s["g2"], s["u2"] = _ffn_fwd(h, g_ffn2, P["ffn2_w_gate"], P["ffn2_w_up"], P["ffn2_w_down"], l)
        if l == N_POOL_LAYERS - 1:
            kv = {"h": h}
            kv["kn"], kv["kr"], kv["v"], kv["ckr"] = _kv_fwd(
                h, tabs, row(kv_in_norm), P["w_dkv"], row(kv_latent_norm), P["w_uk"], P["w_uv"])
        saved.append(s)
    dh, loss_row, d_final = _head(h, target, row(final_norm))
    loss = lax.psum(loss_row[0, 0], ("x", "y", "c"))

    G = {}
    stack = {n: [None] * DEPTH for n in ("ffn1_norm", "ffn1_w_gate", "ffn1_w_up", "ffn1_w_down", "mix_norm",
                                         "ffn2_norm", "ffn2_w_gate", "ffn2_w_up", "ffn2_w_down")}
    pool_dw, pool_ds = [None] * N_POOL_LAYERS, [None] * N_POOL_LAYERS
    mla = {n: [None] * (DEPTH - N_POOL_LAYERS) for n in ("w_dq", "w_uq", "w_o", "q_latent_norm")}
    dks, dvs = [], []

    def ffn_backward(dh, s, which, gam, l):
        wg, wu, wd = P["ffn%d_w_gate" % which], P["ffn%d_w_up" % which], P["ffn%d_w_down" % which]
        dh, dg, du, act, dob, dgam = _ffn_bwd(dh, s["h%d" % which], s["g%d" % which], s["u%d" % which], gam, wg, wu, wd, l)
        xn = s["xn%d" % which]
        stack["ffn%d_w_gate" % which][l] = _mm_tn(xn, dg, "ffn_dw_in")
        stack["ffn%d_w_up" % which][l] = _mm_tn(xn, du, "ffn_dw_in")
        stack["ffn%d_w_down" % which][l] = _mm_tn(act, dob, "ffn_dw_down")
        stack["ffn%d_norm" % which][l] = dgam
        return dh

    for l in reversed(range(DEPTH)):
        s = saved[l]
        if l == N_POOL_LAYERS - 1:
            dh, d_dkv, d_uk, d_uv, d_kvin, d_kvlat = _kv_bwd(
                dh, kv["h"], kv["ckr"], dks, dvs, tabs, row(kv_in_norm), P["w_dkv"], row(kv_latent_norm),
                P["w_uk"], P["w_uv"])
            G.update(w_dkv=d_dkv, w_uk=d_uk, w_uv=d_uv, kv_in_norm=d_kvin, kv_latent_norm=d_kvlat)
        dh = ffn_backward(dh, s, 2, g_ffn2, l)
        if l < N_POOL_LAYERS:
            dh, pool_dw[l], pool_ds[l], stack["mix_norm"][l] = _pool_bwd(
                dh, s["hm"], row(g_mix[l]), P["pool_w"][l], row(P["pool_scale"][l]))
        else:
            j = l - N_POOL_LAYERS
            do, delta_o, mla["w_o"][j] = _oproj_bwd(dh, s["o"], P["w_o"][j])
            dq, dk, dv = _attn_bwd(s["q"], kv["kn"], kv["kr"], kv["v"], do, s["lse"], delta_o)
            dks.append(dk)
            dvs.append(dv)
            dh, mla["w_dq"][j], mla["w_uq"][j], stack["mix_norm"][l], mla["q_latent_norm"][j] = _q_bwd(
                dh, s["hm"], s["cqp"], dq, tabs, row(g_mix[l]), P["w_dq"][j], row(q_latent_norm[j]), P["w_uq"][j])
        dh = ffn_backward(dh, s, 1, g_ffn1, l)
    grad_x = dh[SEQ_START:][None]
    for n, parts in stack.items():
        G[n] = parts if parts[0].shape[0] > 1 else jnp.concatenate(parts, axis=0)
    G["pool_w"] = jnp.stack(pool_dw)
    G["pool_scale"] = jnp.concatenate(pool_ds, axis=0)
    G["w_dq"], G["w_uq"], G["w_o"] = (jnp.stack(mla[n]) for n in ("w_dq", "w_uq", "w_o"))
    G["q_latent_norm"] = jnp.concatenate(mla["q_latent_norm"], axis=0)
    G["meta_tokens"] = dh[FRONT_PAD:SEQ_START]
    G["final_norm"] = d_final

    gpack = _pack([[_shards(n, g[None]) for g in G[n]] if isinstance(G[n], list) else _shards(n, G[n])
                   for n in SHARDED], 1, 8)
    R = gpack.shape[1]
    gpack = gpack.reshape(N_CHIPS, 2, R, PACK_W)
    my_core = lax.axis_index("c").astype(jnp.int32).reshape(1)
    my_chip = (2 * lax.axis_index("x") + lax.axis_index("y")).astype(jnp.int32).reshape(1)
    partial, partial_bf16 = _add_own(gpack, my_core, _sibling_exchange(gpack), "sum_cores")
    mine = _add_own(partial, my_chip, _chip_exchange(partial_bf16), "sum_chips")
    local_shapes = [W[n].shape for n in SHARDED]
    grads = dict(zip(SHARDED, _unpack(mine, local_shapes, 0, 8)))
    rep_shapes = [W[n].shape for n in REPLICATED]
    rpack = _pack([G[n].reshape(W[n].shape) for n in REPLICATED], 0, 8)
    grads.update(zip(REPLICATED, _unpack(_sum_lead(_all_gather(rpack), "sum_devices"), rep_shapes, 0, 8)))

    delta, new_m, new_v = {}, {}, {}
    for n in WEIGHTS:
        delta[n], new_m[n], new_v[n] = _adamw(W[n], grads[n], M[n], V[n])
    return (loss, grad_x, *[grads[n] for n in WEIGHTS], *[delta[n] for n in WEIGHTS],
            *[new_m[n] for n in WEIGHTS], *[new_v[n] for n in WEIGHTS])
```

```python
import functools
import math

import jax
import jax.numpy as jnp
from jax import lax
from jax.experimental import pallas as pl
from jax.experimental.pallas import tpu as pltpu

F32 = jnp.float32
BF16 = jnp.bfloat16
MESH = pl.DeviceIdType.MESH

N_DEV = 8
N_CHIPS = 4
DEPTH = 4
N_POOL_LAYERS = 2
N_HEADS = 8
QK_NOPE = 64
QK_ROPE = 32
V_HEAD = 64
KV_RANK = 256
Q_RANK = 384
HEAD_LANES = 128
ROPE_LANE0 = QK_NOPE
BIAS_LANE = QK_NOPE + QK_ROPE
ONES_LANE = V_HEAD
LOG2E = math.log2(math.e)
N_META = 16
CHUNK_SHIFT = 6
FRONT_PAD = 112
SEQ_START = FRONT_PAD + N_META
HALO = 16
POOL_WINDOWS = (2, 4, 8, 16)
EPS = 1e-6
ROPE_THETA = 10000.0
NEG = -1e30
PACK_W = 1024
PACK_ROW_MULT = 256
VMEM_LIMIT = 56 * 1024 * 1024

ADAM_LR = 0.001
ADAM_B1 = 0.9
ADAM_B2 = 0.999
ADAM_EPS = 1e-08
ADAM_WD = 0.01
ADAM_STEP = 10

SHARDED = ["ffn1_w_gate", "ffn1_w_up", "ffn1_w_down", "ffn2_w_gate", "ffn2_w_up", "ffn2_w_down",
           "pool_w", "w_dkv", "w_uk", "w_uv", "w_dq", "w_uq", "w_o", "meta_tokens", "pool_scale"]
FFN_WEIGHTS = SHARDED[:6]
SMALL_SHARDED = SHARDED[6:]
SHARDED_F32 = ("meta_tokens", "pool_scale")
REPLICATED = ["ffn1_norm", "mix_norm", "ffn2_norm", "kv_in_norm", "kv_latent_norm", "q_latent_norm",
              "final_norm"]
WEIGHTS = ['meta_tokens', 'ffn1_norm', 'ffn1_w_gate', 'ffn1_w_up', 'ffn1_w_down', 'mix_norm', 'ffn2_norm',
           'ffn2_w_gate', 'ffn2_w_up', 'ffn2_w_down', 'pool_w', 'pool_scale', 'kv_in_norm', 'w_dkv',
           'kv_latent_norm', 'w_uk', 'w_uv', 'w_dq', 'q_latent_norm', 'w_uq', 'w_o', 'final_norm']


def _dot(a, b):
    return jnp.dot(a.astype(BF16), b.astype(BF16), preferred_element_type=F32)


def _dot_nt(a, b):
    return lax.dot_general(a.astype(BF16), b.astype(BF16), (((1,), (1,)), ((), ())),
                           preferred_element_type=F32)


def _dot_tn(a, b):
    return lax.dot_general(a.astype(BF16), b.astype(BF16), (((0,), (0,)), ((), ())),
                           preferred_element_type=F32)


def _sigmoid(x):
    return 1.0 / (1.0 + jnp.exp(-x))


def _rms(x, g):
    r = lax.rsqrt(jnp.mean(x * x, axis=-1, keepdims=True) + EPS)
    xh = x * r
    return xh * g, xh, r


def _rms_bwd(dy, xh, r, g):
    dxh = dy * g
    dx = r * (dxh - xh * jnp.mean(dxh * xh, axis=-1, keepdims=True))
    return dx, jnp.sum(dy * xh, axis=0, keepdims=True)


def _rope(x, c, s1, s2):
    return x * c + pltpu.roll(x, HEAD_LANES - QK_ROPE // 2, 1) * s1 + pltpu.roll(x, QK_ROPE // 2, 1) * s2


def _rope_t(d, c, s1, s2):
    return d * c + pltpu.roll(d * s1, QK_ROPE // 2, 1) + pltpu.roll(d * s2, HEAD_LANES - QK_ROPE // 2, 1)


def _params(*sem):
    return pltpu.CompilerParams(dimension_semantics=sem, vmem_limit_bytes=VMEM_LIMIT)


def _pick(n, candidates):
    for c in candidates:
        if n % c == 0:
            return c
    return n


def _row_tile(L):
    return _pick(L, (640, 128))


def _ff_tile(F):
    return _pick(F, (1408, 512, 256, 128))


def _row_call(name, body, L, row_ins, full_ins, row_outs, acc_outs):
    tm = _row_tile(L)
    n = L // tm
    hb = tm // HALO
    nb = L // HALO
    in_specs, args = [], []
    for arr, kind in row_ins:
        c = arr.shape[1]
        if kind == "tile":
            spec = pl.BlockSpec((tm, c), lambda i: (i, 0))
        elif kind == "prev":
            spec = pl.BlockSpec((HALO, c), lambda i: (jnp.maximum(i * hb - 1, 0), 0))
        else:
            spec = pl.BlockSpec((HALO, c), lambda i: (jnp.minimum((i + 1) * hb, nb - 1), 0))
        in_specs.append(spec)
        args.append(arr)
    for arr in full_ins:
        in_specs.append(pl.BlockSpec(arr.shape, lambda i, nd=arr.ndim: (0,) * nd))
        args.append(arr)
    out_shape = [jax.ShapeDtypeStruct((L, c), dt) for c, dt in row_outs]
    out_specs = [pl.BlockSpec((tm, c), lambda i: (i, 0)) for c, _ in row_outs]
    for shp in acc_outs:
        out_shape.append(jax.ShapeDtypeStruct(shp, F32))
        out_specs.append(pl.BlockSpec(shp, lambda i, nd=len(shp): (0,) * nd))
    n_in, n_ro = len(args), len(row_outs)

    def kern(*refs):
        i = pl.program_id(0)
        vals = [r[...] for r in refs[:n_in]]
        ro, ao = body(i, n, tm, *vals)
        for r, v in zip(refs[n_in:n_in + n_ro], ro):
            r[...] = v.astype(r.dtype)
        acc_refs = refs[n_in + n_ro:]

        @pl.when(i == 0)
        def _():
            for r in acc_refs:
                r[...] = jnp.zeros(r.shape, r.dtype)

        for r, v in zip(acc_refs, ao):
            r[...] += v

    return pl.pallas_call(kern, name=name, grid=(n,), in_specs=in_specs, out_specs=out_specs,
                          out_shape=out_shape, compiler_params=_params("arbitrary"))(*args)


class _Side:
    def __init__(self, ins, out_shapes, sems, start, finish):
        self.ins, self.out_shapes, self.sems, self.start, self.finish = ins, out_shapes, sems, start, finish


ANY = pl.BlockSpec(memory_space=pl.ANY)


def _hosted(kern, n_in, n_out, n_scratch, side, is_first, is_last):
    if side is None:
        return kern
    ns_in, ns_out = len(side.ins), len(side.out_shapes)

    def wrapped(*refs):
        ins, refs = refs[:n_in], refs[n_in:]
        side_ins, refs = refs[:ns_in], refs[ns_in:]
        outs, refs = refs[:n_out], refs[n_out:]
        side_outs, refs = refs[:ns_out], refs[ns_out:]
        scratch, side_sems = refs[:n_scratch], refs[n_scratch:]

        @pl.when(is_first())
        def _():
            side.start(side_ins, side_outs, side_sems)

        kern(*ins, *outs, *scratch)

        @pl.when(is_last())
        def _():
            side.finish(side_ins, side_outs, side_sems)

    return wrapped


def _side_args(side):
    if side is None:
        return [], [], [], [], []
    return ([ANY] * len(side.ins), [ANY] * len(side.out_shapes), list(side.out_shapes), list(side.sems),
            list(side.ins))


def _ffn_fwd(h, gam, wg, wu, wd, l, side=None):
    L, D = h.shape
    F = wg.shape[2]
    tm = _row_tile(L)
    tf = _ff_tile(F)
    nL, nF = L // tm, F // tf
    s_in, s_out, s_shape, s_sems, s_args = _side_args(side)

    def kern(h_ref, gam_ref, wg_ref, wu_ref, wd_ref, ho_ref, xn_ref, gs_ref, us_ref, acc):
        f = pl.program_id(1)

        @pl.when(f == 0)
        def _():
            xn, _, _ = _rms(h_ref[...], gam_ref[...])
            xn_ref[...] = xn.astype(BF16)
            acc[...] = jnp.zeros(acc.shape, F32)

        xnb = xn_ref[...]
        g = _dot(xnb, wg_ref[...])
        u = _dot(xnb, wu_ref[...])
        gs_ref[...] = g.astype(BF16)
        us_ref[...] = u.astype(BF16)
        acc[...] += _dot(g * _sigmoid(g) * u, wd_ref[...])

        @pl.when(f == nF - 1)
        def _():
            ho_ref[...] = h_ref[...] + 0.5 * acc[...]

    first = lambda: (pl.program_id(0) == 0) & (pl.program_id(1) == 0)
    last = lambda: (pl.program_id(0) == nL - 1) & (pl.program_id(1) == nF - 1)
    return pl.pallas_call(
        _hosted(kern, 5, 4, 1, side, first, last),
        name="ffn_fwd" if side is None else "ffn_fwd_hosting", grid=(nL, nF),
        in_specs=[pl.BlockSpec((tm, D), lambda i, f: (i, 0)),
                  pl.BlockSpec((None, 1, D), lambda i, f: (l, 0, 0)),
                  pl.BlockSpec((None, D, tf), lambda i, f: (l, 0, f)),
                  pl.BlockSpec((None, D, tf), lambda i, f: (l, 0, f)),
                  pl.BlockSpec((None, tf, D), lambda i, f: (l, f, 0))] + s_in,
        out_specs=[pl.BlockSpec((tm, D), lambda i, f: (i, 0)),
                   pl.BlockSpec((tm, D), lambda i, f: (i, 0)),
                   pl.BlockSpec((tm, tf), lambda i, f: (i, f)),
                   pl.BlockSpec((tm, tf), lambda i, f: (i, f))] + s_out,
        out_shape=[jax.ShapeDtypeStruct((L, D), F32), jax.ShapeDtypeStruct((L, D), BF16),
                   jax.ShapeDtypeStruct((L, F), BF16), jax.ShapeDtypeStruct((L, F), BF16)] + s_shape,
        scratch_shapes=[pltpu.VMEM((tm, D), F32)] + s_sems,
        compiler_params=_params("arbitrary", "arbitrary"))(h, gam, wg, wu, wd, *s_args)


def _ffn_bwd(dh, h, gs, us, gam, wg, wu, wd, l, side=None):
    L, D = h.shape
    F = wg.shape[2]
    tm = _pick(L, (416, 128))
    tf = _ff_tile(F)
    nL, nF = L // tm, F // tf
    s_in, s_out, s_shape, s_sems, s_args = _side_args(side)

    def kern(dh_ref, h_ref, gs_ref, us_ref, gam_ref, wg_ref, wu_ref, wd_ref,
             dhi_ref, dg_ref, du_ref, a_ref, dob_ref, dgam_ref, dxn):
        i = pl.program_id(0)
        f = pl.program_id(1)

        @pl.when(f == 0)
        def _():
            dxn[...] = jnp.zeros(dxn.shape, F32)
            dob_ref[...] = (0.5 * dh_ref[...]).astype(BF16)

        @pl.when((f == 0) & (i == 0))
        def _():
            dgam_ref[...] = jnp.zeros(dgam_ref.shape, F32)

        g = gs_ref[...].astype(F32)
        u = us_ref[...].astype(F32)
        sg = _sigmoid(g)
        silu = g * sg
        da = _dot_nt(dob_ref[...], wd_ref[...])
        a_ref[...] = (silu * u).astype(BF16)
        dgt = (da * u * (sg * (1.0 + g * (1.0 - sg)))).astype(BF16)
        dut = (da * silu).astype(BF16)
        dg_ref[...] = dgt
        du_ref[...] = dut
        dxn[...] += _dot_nt(dgt, wg_ref[...]) + _dot_nt(dut, wu_ref[...])

        @pl.when(f == nF - 1)
        def _():
            gamma = gam_ref[...]
            _, xh, r = _rms(h_ref[...], gamma)
            dx, dgam = _rms_bwd(dxn[...], xh, r, gamma)
            dhi_ref[...] = dh_ref[...] + dx
            dgam_ref[...] += dgam

    first = lambda: (pl.program_id(0) == 0) & (pl.program_id(1) == 0)
    last = lambda: (pl.program_id(0) == nL - 1) & (pl.program_id(1) == nF - 1)
    return pl.pallas_call(
        _hosted(kern, 8, 6, 1, side, first, last),
        name="ffn_bwd" if side is None else "ffn_bwd_hosting", grid=(nL, nF),
        in_specs=[pl.BlockSpec((tm, D), lambda i, f: (i, 0)),
                  pl.BlockSpec((tm, D), lambda i, f: (i, 0)),
                  pl.BlockSpec((tm, tf), lambda i, f: (i, f)),
                  pl.BlockSpec((tm, tf), lambda i, f: (i, f)),
                  pl.BlockSpec((None, 1, D), lambda i, f: (l, 0, 0)),
                  pl.BlockSpec((None, D, tf), lambda i, f: (l, 0, f)),
                  pl.BlockSpec((None, D, tf), lambda i, f: (l, 0, f)),
                  pl.BlockSpec((None, tf, D), lambda i, f: (l, f, 0))] + s_in,
        out_specs=[pl.BlockSpec((tm, D), lambda i, f: (i, 0)),
                   pl.BlockSpec((tm, tf), lambda i, f: (i, f)),
                   pl.BlockSpec((tm, tf), lambda i, f: (i, f)),
                   pl.BlockSpec((tm, tf), lambda i, f: (i, f)),
                   pl.BlockSpec((tm, D), lambda i, f: (i, 0)),
                   pl.BlockSpec((1, D), lambda i, f: (0, 0))] + s_out,
        out_shape=[jax.ShapeDtypeStruct((L, D), F32), jax.ShapeDtypeStruct((L, F), BF16),
                   jax.ShapeDtypeStruct((L, F), BF16), jax.ShapeDtypeStruct((L, F), BF16),
                   jax.ShapeDtypeStruct((L, D), BF16), jax.ShapeDtypeStruct((1, D), F32)] + s_shape,
        scratch_shapes=[pltpu.VMEM((tm, D), F32)] + s_sems,
        compiler_params=_params("arbitrary", "arbitrary"))(dh, h, gs, us, gam, wg, wu, wd, *s_args)


def _mm_tn(a, b, name):
    L, M = a.shape
    N = b.shape[1]
    tm = _pick(M, (1408, 1024, 512))
    tn = _pick(N, (1408, 1024, 512))
    tk = _pick(L, (2080, 640, 128))

    def kern(a_ref, b_ref, o_ref):
        @pl.when(pl.program_id(2) == 0)
        def _():
            o_ref[...] = jnp.zeros(o_ref.shape, F32)

        o_ref[...] += _dot_tn(a_ref[...], b_ref[...])

    return pl.pallas_call(
        kern, name=name, grid=(M // tm, N // tn, L // tk),
        in_specs=[pl.BlockSpec((tk, tm), lambda i, j, k: (k, i)),
                  pl.BlockSpec((tk, tn), lambda i, j, k: (k, j))],
        out_specs=pl.BlockSpec((tm, tn), lambda i, j, k: (i, j)),
        out_shape=jax.ShapeDtypeStruct((M, N), F32),
        compiler_params=_params("arbitrary", "arbitrary", "arbitrary"))(a, b)


def _pool_counts(pos, w):
    return jnp.clip(pos - (FRONT_PAD - 1), 1, w).astype(F32)


def _pool_forward_values(i, tm, h, hprev, gamma, D):
    cg = D // len(POOL_WINDOWS)
    hext = jnp.concatenate([hprev, h], axis=0)
    uext, xh, r = _rms(hext, gamma)
    pos = i * tm + lax.broadcasted_iota(jnp.int32, (tm, 1), 0)
    pooled = []
    for gi, w in enumerate(POOL_WINDOWS):
        s = uext[:, gi * cg:(gi + 1) * cg]
        span = 1
        while span < w:
            s = s + pltpu.roll(s, span, 0)
            span *= 2
        s = s[HALO:]
        pooled.append(s / _pool_counts(pos, w) - uext[HALO:, gi * cg:(gi + 1) * cg])
    return uext, xh[HALO:], r[HALO:], pooled


def _pool_fwd(h, gam, w, scale, l):
    L, D = h.shape
    cg = D // len(POOL_WINDOWS)

    def body(i, n, tm, ht, hprev, gamma, wv, sc):
        _, _, _, pooled = _pool_forward_values(i, tm, ht, hprev, gamma, D)
        ys = [_dot(pooled[gi], wv[gi]) for gi in range(len(POOL_WINDOWS))]
        y = jnp.concatenate(ys, axis=1) * sc
        return [ht + y], []

    del cg
    return _row_call("pool_fwd", body, L, [(h, "tile"), (h, "prev")], [gam, w, scale], [(D, F32)], [])[0]


def _pool_bwd(dy, h, gam, w, scale):
    L, D = h.shape
    ng = len(POOL_WINDOWS)
    cg = D // ng

    def body(i, n, tm, ht, hprev, dyt, dynext, gamma, wv, sc):
        _, xh, r, pooled = _pool_forward_values(i, tm, ht, hprev, gamma, D)
        dynext = jnp.where(i == n - 1, jnp.zeros_like(dynext), dynext)
        dyext = jnp.concatenate([dyt, dynext], axis=0) * sc
        pos_ext = i * tm + lax.broadcasted_iota(jnp.int32, (tm + HALO, 1), 0)
        dws, dscs, dus = [], [], []
        for gi, wd in enumerate(POOL_WINDOWS):
            cols = slice(gi * cg, (gi + 1) * cg)
            pb = pooled[gi].astype(BF16)
            ypre = _dot(pb, wv[gi])
            dscs.append(jnp.sum(dyt[:, cols] * ypre, axis=0, keepdims=True))
            dws.append(_dot_tn(pb, dyext[:tm, cols])[None])
            dp = _dot_nt(dyext[:, cols], wv[gi])
            s = dp / _pool_counts(pos_ext, wd)
            span = 1
            while span < wd:
                s = s + pltpu.roll(s, tm + HALO - span, 0)
                span *= 2
            dus.append(s[:tm] - dp[:tm])
        du = jnp.concatenate(dus, axis=1)
        pos = pos_ext[:tm]
        du = jnp.where(pos >= FRONT_PAD, du, 0.0)
        dx, dgam = _rms_bwd(du, xh, r, gamma)
        return [dyt + dx], [jnp.concatenate(dws, axis=0), jnp.concatenate(dscs, axis=1), dgam]

    return _row_call("pool_bwd", body, L, [(h, "tile"), (h, "prev"), (dy, "tile"), (dy, "next")],
                     [gam, w, scale], [(D, F32)], [(ng, cg, cg), (1, D), (1, D)])


def _kv_fwd(h, tabs, g1, wdkv, g2, wuk, wuv):
    L, D = h.shape
    hw = N_HEADS * HEAD_LANES

    def body(i, n, tm, ht, ck, s1, s2, g1v, wdkv_v, g2v, wuk_v, wuv_v):
        xkv, _, _ = _rms(ht, g1v)
        ckr = _dot(xkv, wdkv_v)
        ckv, _, _ = _rms(ckr[:, :KV_RANK], g2v)
        krope = _rope(ckr[:, KV_RANK:], ck, s1, s2)
        pos = i * tm + lax.broadcasted_iota(jnp.int32, (tm, HEAD_LANES), 0)
        lane = lax.broadcasted_iota(jnp.int32, (tm, HEAD_LANES), 1)
        krope = jnp.where((pos < FRONT_PAD) & (lane == BIAS_LANE), NEG, krope)
        ones = ((lax.broadcasted_iota(jnp.int32, (1, hw), 1) & (HEAD_LANES - 1)) == ONES_LANE).astype(F32)
        return [_dot(ckv, wuk_v), krope, _dot(ckv, wuv_v) + ones, ckr], []

    ck, s1, s2 = tabs["ck"], tabs["s1"], tabs["s2"]
    return _row_call("kv_fwd", body, L, [(h, "tile"), (ck, "tile"), (s1, "tile"), (s2, "tile")],
                     [g1, wdkv, g2, wuk, wuv],
                     [(hw, BF16), (HEAD_LANES, BF16), (hw, BF16), (KV_RANK + HEAD_LANES, F32)], [])


def _kv_bwd(dh, h, ckr, dks, dvs, tabs, g1, wdkv, g2, wuk, wuv):
    L, D = h.shape
    hw = N_HEADS * HEAD_LANES
    nl = len(dks)

    def body(i, n, tm, *vals):
        dht, ht, ckr_t = vals[:3]
        dk = sum(vals[3:3 + nl][1:], vals[3])
        dv = sum(vals[3 + nl:3 + 2 * nl][1:], vals[3 + nl])
        ck, s1, s2, g1v, wdkv_v, g2v, wuk_v, wuv_v = vals[3 + 2 * nl:]
        xkv, xh1, r1 = _rms(ht, g1v)
        ckv, xh2, r2 = _rms(ckr_t[:, :KV_RANK], g2v)
        dckv = _dot_nt(dk, wuk_v) + _dot_nt(dv, wuv_v)
        dlat, dg2 = _rms_bwd(dckv, xh2, r2, g2v)
        dkr = dk[:, :HEAD_LANES]
        for hd in range(1, N_HEADS):
            dkr = dkr + dk[:, hd * HEAD_LANES:(hd + 1) * HEAD_LANES]
        dckr = jnp.concatenate([dlat, _rope_t(dkr, ck, s1, s2)], axis=1)
        dx, dg1 = _rms_bwd(_dot_nt(dckr, wdkv_v), xh1, r1, g1v)
        return [dht + dx], [_dot_tn(xkv, dckr), _dot_tn(ckv, dk), _dot_tn(ckv, dv), dg1, dg2]

    row_ins = [(dh, "tile"), (h, "tile"), (ckr, "tile")] + [(a, "tile") for a in dks + dvs]
    row_ins += [(tabs[k], "tile") for k in ("ck", "s1", "s2")]
    return _row_call("kv_bwd", body, L, row_ins, [g1, wdkv, g2, wuk, wuv], [(D, F32)],
                     [(D, KV_RANK + HEAD_LANES), (KV_RANK, hw), (KV_RANK, hw), (1, D), (1, KV_RANK)])


def _q_fwd(h, tabs, g, wdq, gq, wuq):
    L, D = h.shape
    hw = N_HEADS * HEAD_LANES

    def body(i, n, tm, ht, cq_t, s1, s2, gv, wdq_v, gqv, wuq_v):
        u, _, _ = _rms(ht, gv)
        cqp = _dot(u, wdq_v)
        cq, _, _ = _rms(cqp, gqv)
        qp = _dot(cq, wuq_v)
        bias = (lax.broadcasted_iota(jnp.int32, (1, HEAD_LANES), 1) == BIAS_LANE).astype(F32)
        q = [_rope(qp[:, hd * HEAD_LANES:(hd + 1) * HEAD_LANES], cq_t, s1, s2) * (SM_SCALE * LOG2E) + bias
             for hd in range(N_HEADS)]
        return [jnp.concatenate(q, axis=1), cqp], []

    return _row_call("q_fwd", body, L, [(h, "tile")] + [(tabs[k], "tile") for k in ("cq", "s1", "s2")],
                     [g, wdq, gq, wuq], [(hw, BF16), (Q_RANK, F32)], [])


def _q_bwd(dh, h, cqp, dq, tabs, g, wdq, gq, wuq):
    L, D = h.shape
    hw = N_HEADS * HEAD_LANES

    def body(i, n, tm, dht, ht, cqp_t, dq_t, cq_t, s1, s2, gv, wdq_v, gqv, wuq_v):
        u, xh1, r1 = _rms(ht, gv)
        cq, xh2, r2 = _rms(cqp_t, gqv)
        dqp = jnp.concatenate([_rope_t(dq_t[:, hd * HEAD_LANES:(hd + 1) * HEAD_LANES], cq_t, s1, s2)
                               for hd in range(N_HEADS)], axis=1)
        dcqp, dgq = _rms_bwd(_dot_nt(dqp, wuq_v), xh2, r2, gqv)
        dx, dg = _rms_bwd(_dot_nt(dcqp, wdq_v), xh1, r1, gv)
        return [dht + dx], [_dot_tn(u, dcqp), _dot_tn(cq, dqp), dg, dgq]

    row_ins = [(dh, "tile"), (h, "tile"), (cqp, "tile"), (dq, "tile")]
    row_ins += [(tabs[k], "tile") for k in ("cq", "s1", "s2")]
    return _row_call("q_bwd", body, L, row_ins, [g, wdq, gq, wuq], [(D, F32)],
                     [(D, Q_RANK), (Q_RANK, hw), (1, D), (1, Q_RANK)])


def _oproj_fwd(h, o, wo):
    L, D = h.shape

    def body(i, n, tm, ht, ot, wov):
        return [ht + _dot(ot, wov)], []

    return _row_call("oproj_fwd", body, L, [(h, "tile"), (o, "tile")], [wo], [(D, F32)], [])[0]


def _oproj_bwd(dh, o, wo):
    L, D = dh.shape
    hw = N_HEADS * HEAD_LANES

    def body(i, n, tm, dht, ot, wov):
        do = _dot_nt(dht, wov)
        prod = do * ot.astype(F32)
        delta = [jnp.broadcast_to(jnp.sum(prod[:, hd * HEAD_LANES:(hd + 1) * HEAD_LANES], axis=-1, keepdims=True),
                                  (tm, HEAD_LANES)) for hd in range(N_HEADS)]
        return [do, jnp.concatenate(delta, axis=1)], [_dot_tn(ot, dht)]

    return _row_call("oproj_bwd", body, L, [(dh, "tile"), (o, "tile")], [wo], [(hw, BF16), (hw, F32)], [(hw, D)])


def _causal(t):
    qpos = lax.broadcasted_iota(jnp.int32, (t, t), 0)
    kpos = lax.broadcasted_iota(jnp.int32, (t, t), 1)
    return (kpos >> CHUNK_SHIFT) <= (qpos >> CHUNK_SHIFT)


SM_SCALE = 1.0 / math.sqrt(QK_NOPE + QK_ROPE)


def _pairs(n, key_major):
    if key_major:
        order = [(i, j) for j in range(n) for i in range(j, n)]
    else:
        order = [(i, j) for i in range(n) for j in range(i + 1)]
    return (jnp.array([p[0] for p in order], jnp.int32), jnp.array([p[1] for p in order], jnp.int32))


def _attn_fwd(q, kn, kr, v):
    L = q.shape[0]
    hw = N_HEADS * HEAD_LANES
    t = _row_tile(L)
    it, jt = _pairs(L // t, key_major=False)

    def kern(it_ref, jt_ref, q_ref, kn_ref, kr_ref, v_ref, o_ref, lse_ref, m_s, acc_s):
        step = pl.program_id(1)
        i, j = it_ref[step], jt_ref[step]

        @pl.when(j == 0)
        def _():
            m_s[...] = jnp.full(m_s.shape, NEG, F32)
            acc_s[...] = jnp.zeros(acc_s.shape, F32)

        def update(diagonal):
            k = kn_ref[...] + kr_ref[...]
            s = _dot_nt(q_ref[...], k)
            if diagonal:
                s = jnp.where(_causal(t), s, NEG)
            m_prev = m_s[:, :1]
            m_new = jnp.maximum(m_prev, jnp.max(s, axis=-1, keepdims=True))
            p = jnp.exp2(s - m_new)
            acc_s[...] = jnp.exp2(m_prev - m_new) * acc_s[...] + _dot(p, v_ref[...])
            m_s[...] = jnp.broadcast_to(m_new, m_s.shape)

        @pl.when(j < i)
        def _():
            update(False)

        @pl.when(j == i)
        def _():
            update(True)
            acc = acc_s[...]
            total = acc[:, ONES_LANE:ONES_LANE + 1]
            o_ref[...] = (acc / total).astype(BF16)
            lse_ref[...] = m_s[...] + jnp.log2(jnp.broadcast_to(total, m_s.shape))

    qmap = lambda h, s, it, jt: (it[s], h)
    kmap = lambda h, s, it, jt: (jt[s], h)
    blk = (t, HEAD_LANES)
    return pl.pallas_call(
        kern, name="attn_fwd",
        grid_spec=pltpu.PrefetchScalarGridSpec(
            num_scalar_prefetch=2, grid=(N_HEADS, it.shape[0]),
            in_specs=[pl.BlockSpec(blk, qmap), pl.BlockSpec(blk, kmap),
                      pl.BlockSpec(blk, lambda h, s, it, jt: (jt[s], 0)), pl.BlockSpec(blk, kmap)],
            out_specs=[pl.BlockSpec(blk, qmap), pl.BlockSpec(blk, qmap)],
            scratch_shapes=[pltpu.VMEM(blk, F32)] * 2),
        out_shape=[jax.ShapeDtypeStruct((L, hw), BF16), jax.ShapeDtypeStruct((L, hw), F32)],
        compiler_params=_params("arbitrary", "arbitrary"))(it, jt, q, kn, kr, v)


def _attn_bwd(q, kn, kr, v, do, lse, delta):
    L = q.shape[0]
    hw = N_HEADS * HEAD_LANES
    t = _row_tile(L)
    it, jt = _pairs(L // t, key_major=True)

    def kern(it_ref, jt_ref, q_ref, kn_ref, kr_ref, v_ref, do_ref, lse_ref, dl_ref, dq_ref, dk_ref, dv_ref):
        step = pl.program_id(1)
        i, j = it_ref[step], jt_ref[step]

        @pl.when(step == 0)
        def _():
            dq_ref[...] = jnp.zeros(dq_ref.shape, F32)

        @pl.when(i == j)
        def _():
            dk_ref[...] = jnp.zeros(dk_ref.shape, F32)
            dv_ref[...] = jnp.zeros(dv_ref.shape, F32)

        def update(diagonal):
            k = kn_ref[...] + kr_ref[...]
            qv, dov = q_ref[...], do_ref[...]
            s = _dot_nt(qv, k)
            if diagonal:
                s = jnp.where(_causal(t), s, NEG)
            p = jnp.exp2(s - lse_ref[:, :1])
            dp = _dot_nt(dov, v_ref[...])
            dz = (p * (dp - dl_ref[:, :1])).astype(BF16)
            dv_ref[...] += _dot_tn(p, dov)
            dk_ref[...] += _dot_tn(dz, qv) * (1.0 / LOG2E)
            rows = pl.ds(pl.multiple_of(i * t, t), t)
            dq_ref[rows, :] += _dot(dz, k) * SM_SCALE

        @pl.when(j < i)
        def _():
            update(False)

        @pl.when(j == i)
        def _():
            update(True)

    qmap = lambda h, s, it, jt: (it[s], h)
    kmap = lambda h, s, it, jt: (jt[s], h)
    blk = (t, HEAD_LANES)
    return pl.pallas_call(
        kern, name="attn_bwd",
        grid_spec=pltpu.PrefetchScalarGridSpec(
            num_scalar_prefetch=2, grid=(N_HEADS, it.shape[0]),
            in_specs=[pl.BlockSpec(blk, qmap), pl.BlockSpec(blk, kmap),
                      pl.BlockSpec(blk, lambda h, s, it, jt: (jt[s], 0)), pl.BlockSpec(blk, kmap),
                      pl.BlockSpec(blk, qmap), pl.BlockSpec(blk, qmap), pl.BlockSpec(blk, qmap)],
            out_specs=[pl.BlockSpec((L, HEAD_LANES), lambda h, s, it, jt: (0, h)),
                       pl.BlockSpec(blk, kmap), pl.BlockSpec(blk, kmap)]),
        out_shape=[jax.ShapeDtypeStruct((L, hw), F32)] * 3,
        compiler_params=_params("arbitrary", "arbitrary"))(it, jt, q, kn, kr, v, do, lse, delta)


def _head(h, target, g):
    L, D = h.shape

    def body(i, n, tm, ht, tt, gv):
        y, xh, r = _rms(ht, gv)
        pos = i * tm + lax.broadcasted_iota(jnp.int32, (tm, 1), 0)
        e = jnp.where(pos >= SEQ_START, y - tt, 0.0)
        loss = 0.5 * jnp.sum(jnp.mean(e * e, axis=-1, keepdims=True), axis=0, keepdims=True)
        dx, dg = _rms_bwd(e / D, xh, r, gv)
        return [dx], [jnp.broadcast_to(loss, (1, 128)), dg]

    return _row_call("loss_head", body, L, [(h, "tile"), (target, "tile")], [g], [(D, F32)], [(1, 128), (1, D)])


def _coords():
    return lax.axis_index("x"), lax.axis_index("y"), lax.axis_index("c")


def _my_index():
    mx, my, mc = _coords()
    return 4 * mx + 2 * my + mc


def _gather_side(x):
    R, W = x.shape

    def copies(x_ref, out_ref, send_sems, recv_sems):
        mx, my, mc = _coords()
        me, sibling = (mx, my, mc), (mx, my, 1 - mc)
        chips = [(1 - mx, my), (mx, 1 - my), (1 - mx, 1 - my)]

        def slot(px, py, pc):
            return out_ref.at[4 * px + 2 * py + pc]

        def copy(k, block, to, src=None):
            return pltpu.make_async_remote_copy(
                src_ref=slot(*block) if src is None else src, dst_ref=slot(*block),
                send_sem=send_sems.at[k], recv_sem=recv_sems.at[k], device_id=to, device_id_type=MESH)

        first = [copy(0, me, sibling, src=x_ref)]
        first += [copy(1 + n, me, (*chip, mc), src=x_ref) for n, chip in enumerate(chips)]
        passed = [copy(4 + n, (*chip, mc), sibling) for n, chip in enumerate(chips)]
        landed = [copy(1 + n, (*chip, mc), me) for n, chip in enumerate(chips)]
        from_sibling = [copy(0, sibling, me)] + [copy(4 + n, (*chip, 1 - mc), me) for n, chip in enumerate(chips)]
        return first, passed, landed, from_sibling

    def start(ins, outs, sems):
        for cp in copies(ins[0], outs[0], *sems)[0]:
            cp.start()

    def finish(ins, outs, sems):
        first, passed, landed, from_sibling = copies(ins[0], outs[0], *sems)
        for arrived, onward in zip(landed, passed):
            arrived.wait_recv()
            onward.start()
        for cp in from_sibling:
            cp.wait_recv()
        for cp in first + passed:
            cp.wait_send()

    return _Side([x], [jax.ShapeDtypeStruct((N_DEV, R, W), x.dtype)],
                 [pltpu.SemaphoreType.DMA((7,)), pltpu.SemaphoreType.DMA((7,))], start, finish)


def _with_own(gathered, x):
    return lax.dynamic_update_slice(gathered, x[None], (_my_index(), 0, 0))


def _chip_side(p):
    _, R, W = p.shape

    def copies(p_ref, out_ref, send_sems, recv_sems):
        mx, my, mc = _coords()
        chips = [(1 - mx, my), (mx, 1 - my), (1 - mx, 1 - my)]
        return [pltpu.make_async_remote_copy(
            src_ref=p_ref.at[2 * cx + cy], dst_ref=out_ref.at[n], send_sem=send_sems.at[n],
            recv_sem=recv_sems.at[n], device_id=(cx, cy, mc), device_id_type=MESH)
            for n, (cx, cy) in enumerate(chips)]

    def start(ins, outs, sems):
        for cp in copies(ins[0], outs[0], *sems):
            cp.start()

    def finish(ins, outs, sems):
        cps = copies(ins[0], outs[0], *sems)
        for cp in cps:
            cp.wait_recv()
        for cp in cps:
            cp.wait_send()

    return _Side([p], [jax.ShapeDtypeStruct((3, R, W), p.dtype)],
                 [pltpu.SemaphoreType.DMA((3,)), pltpu.SemaphoreType.DMA((3,))], start, finish)


def _run_side(side, name):
    def kern(*refs):
        n_in, n_out = len(side.ins), len(side.out_shapes)
        ins, outs, sems = refs[:n_in], refs[n_in:n_in + n_out], refs[n_in + n_out:]
        side.start(ins, outs, sems)
        side.finish(ins, outs, sems)

    return pl.pallas_call(kern, name=name, in_specs=[ANY] * len(side.ins), out_specs=[ANY] * len(side.out_shapes),
                          out_shape=list(side.out_shapes), scratch_shapes=list(side.sems))(*side.ins)


def _sibling_exchange(g):
    _, _, R, W = g.shape

    def kern(g_ref, out_ref, send_sems, recv_sems):
        mx, my, mc = _coords()
        copies = [pltpu.make_async_remote_copy(
            src_ref=g_ref.at[n, 1 - mc], dst_ref=out_ref.at[n], send_sem=send_sems.at[n],
            recv_sem=recv_sems.at[n], device_id=(mx, my, 1 - mc), device_id_type=MESH) for n in range(N_CHIPS)]
        for cp in copies:
            cp.start()
        for cp in copies:
            cp.wait_recv()
        for cp in copies:
            cp.wait_send()

    return pl.pallas_call(
        kern, name="sibling_exchange", in_specs=[ANY], out_specs=ANY,
        out_shape=jax.ShapeDtypeStruct((N_CHIPS, R, W), g.dtype),
        scratch_shapes=[pltpu.SemaphoreType.DMA((N_CHIPS,)), pltpu.SemaphoreType.DMA((N_CHIPS,))],
    )(g)


def _add_own(own, sel, other, name):
    R, W = own.shape[-2:]
    tr = _pick(R, (PACK_ROW_MULT, 8))
    if own.ndim == 4:
        n = own.shape[0]
        grid = (n, R // tr)
        in_specs = [pl.BlockSpec((None, None, tr, W), lambda b, i, sel: (b, sel[0], i, 0)),
                    pl.BlockSpec((None, tr, W), lambda b, i, sel: (b, i, 0))]
        out_spec = [pl.BlockSpec((None, tr, W), lambda b, i, sel: (b, i, 0))] * 2
        out_shape = [jax.ShapeDtypeStruct((n, R, W), F32), jax.ShapeDtypeStruct((n, R, W), BF16)]

        def kern(sel_ref, own_ref, other_ref, o_ref, ob_ref):
            acc = own_ref[...] + other_ref[...]
            o_ref[...] = acc
            ob_ref[...] = acc.astype(BF16)
    else:
        k = other.shape[0]
        grid = (1, R // tr)
        in_specs = [pl.BlockSpec((None, tr, W), lambda b, i, sel: (sel[0], i, 0)),
                    pl.BlockSpec((k, tr, W), lambda b, i, sel: (0, i, 0))]
        out_spec = pl.BlockSpec((tr, W), lambda b, i, sel: (i, 0))
        out_shape = jax.ShapeDtypeStruct((R, W), F32)

        def kern(sel_ref, own_ref, other_ref, o_ref):
            acc = own_ref[...]
            for m in range(k):
                acc = acc + other_ref[m].astype(F32)
            o_ref[...] = acc

    return pl.pallas_call(
        kern, name=name,
        grid_spec=pltpu.PrefetchScalarGridSpec(num_scalar_prefetch=1, grid=grid, in_specs=in_specs,
                                               out_specs=out_spec),
        out_shape=out_shape, compiler_params=_params("arbitrary", "arbitrary"))(sel, own, other)


def _sum_lead(x, name):
    n, R, W = x.shape
    tr = _pick(R, (PACK_ROW_MULT, 8))

    def kern(x_ref, o_ref):
        acc = x_ref[0]
        for k in range(1, n):
            acc = acc + x_ref[k]
        o_ref[...] = acc

    return pl.pallas_call(
        kern, name=name, grid=(R // tr,),
        in_specs=[pl.BlockSpec((n, tr, W), lambda i: (0, i, 0))],
        out_specs=pl.BlockSpec((tr, W), lambda i: (i, 0)),
        out_shape=jax.ShapeDtypeStruct((R, W), F32), compiler_params=_params("arbitrary"))(x)


def _adamw(w, g, m, v):
    shape = w.shape
    cols = shape[-1]
    rows = w.size // cols
    tr = _pick(rows, (512, 352, 256, 128))
    if rows * cols * 4 <= (1 << 20):
        tr = rows

    def kern(w_ref, g_ref, m_ref, v_ref, d_ref, mo_ref, vo_ref):
        gv = g_ref[...]
        mn = ADAM_B1 * m_ref[...] + (1.0 - ADAM_B1) * gv
        vn = ADAM_B2 * v_ref[...] + (1.0 - ADAM_B2) * (gv * gv)
        m_hat = mn / (1.0 - ADAM_B1 ** ADAM_STEP)
        v_hat = vn / (1.0 - ADAM_B2 ** ADAM_STEP)
        d_ref[...] = -ADAM_LR * (m_hat / (jnp.sqrt(v_hat) + ADAM_EPS) + ADAM_WD * w_ref[...])
        mo_ref[...] = mn
        vo_ref[...] = vn

    spec = pl.BlockSpec((tr, cols), lambda i: (i, 0))
    outs = pl.pallas_call(
        kern, name="adamw", grid=(rows // tr,), in_specs=[spec] * 4, out_specs=[spec] * 3,
        out_shape=[jax.ShapeDtypeStruct((rows, cols), F32)] * 3, compiler_params=_params("arbitrary"),
    )(*[a.reshape(rows, cols) for a in (w, g, m, v)])
    return [o.reshape(shape) for o in outs]


def _pack(arrs, n_lead, row_mult):
    parts, total = [], 0
    for n, a in enumerate(arrs):
        pieces = a if isinstance(a, list) else [a]
        lead = pieces[0].shape[:n_lead]
        flat = [p.reshape(lead + (-1,)) for p in pieces]
        size = sum(f.shape[-1] for f in flat)
        rows = -(-size // (PACK_W * row_mult)) * row_mult
        if n == len(arrs) - 1:
            rows += -(total + rows) % PACK_ROW_MULT
        total += rows
        if rows * PACK_W > size:
            flat.append(jnp.zeros(lead + (rows * PACK_W - size,), flat[0].dtype))
        parts.append(jnp.concatenate(flat, axis=n_lead).reshape(lead + (rows, PACK_W)))
    return jnp.concatenate(parts, axis=n_lead)


def _unpack(pack, shapes, n_lead, row_mult):
    outs, row = [], 0
    lead = pack.shape[:n_lead]
    for shp in shapes:
        size = math.prod(shp)
        rows = -(-size // (PACK_W * row_mult)) * row_mult
        blk = lax.slice_in_dim(pack, row, row + rows, axis=n_lead)
        outs.append(blk.reshape(lead + (-1,))[..., :size].reshape(lead + tuple(shp)))
        row += rows
    return outs


def _to_words(a):
    return lax.bitcast_convert_type(a, BF16)


def _from_words(a):
    return lax.bitcast_convert_type(a, F32)


def _pad_axis(a, axis, size):
    pads = [(0, 0)] * a.ndim
    pads[axis] = (0, size - a.shape[axis])
    return jnp.pad(a, pads)


def _dense(name, s):
    if name.endswith("w_gate") or name.endswith("w_up"):
        _, nl, d, fs = s.shape
        return s.transpose(1, 2, 0, 3).reshape(nl, d, N_DEV * fs)
    if name.endswith("w_down"):
        _, nl, fs, d = s.shape
        return s.transpose(1, 0, 2, 3).reshape(nl, N_DEV * fs, d)
    if name == "pool_w":
        _, nl, ng, r, cg = s.shape
        return s.transpose(1, 2, 0, 3, 4).reshape(nl, ng, cg, cg)
    if name == "w_dkv":
        w = s.reshape(-1, s.shape[2])
        z = lambda n: jnp.zeros((w.shape[0], n), w.dtype)
        return jnp.concatenate([w[:, :KV_RANK], z(ROPE_LANE0), w[:, KV_RANK:],
                                z(HEAD_LANES - ROPE_LANE0 - QK_ROPE)], axis=1)
    if name in ("w_uk", "w_uv"):
        return _pad_axis(s.transpose(1, 0, 2), 2, HEAD_LANES).reshape(KV_RANK, N_HEADS * HEAD_LANES)
    if name == "w_dq":
        _, nl, ds, r = s.shape
        return s.transpose(1, 0, 2, 3).reshape(nl, N_DEV * ds, r)
    if name == "w_uq":
        nl = s.shape[1]
        return _pad_axis(s.transpose(1, 2, 0, 3), 3, HEAD_LANES).reshape(nl, Q_RANK, N_HEADS * HEAD_LANES)
    if name == "w_o":
        _, nl, k, dc = s.shape
        w = s.transpose(1, 2, 0, 3).reshape(nl, N_HEADS, V_HEAD, N_DEV * dc)
        return _pad_axis(w, 2, HEAD_LANES).reshape(nl, N_HEADS * HEAD_LANES, N_DEV * dc)
    if name in ("meta_tokens", "pool_scale"):
        r, dc = s.shape[1:]
        return s.transpose(1, 0, 2).reshape(r, N_DEV * dc)
    raise ValueError(name)


def _shards(name, g):
    if name.endswith("w_gate") or name.endswith("w_up"):
        nl, d, f = g.shape
        return g.reshape(nl, d, N_DEV, f // N_DEV).transpose(2, 0, 1, 3)
    if name.endswith("w_down"):
        nl, f, d = g.shape
        return g.reshape(nl, N_DEV, f // N_DEV, d).transpose(1, 0, 2, 3)
    if name == "pool_w":
        nl, ng, cg, _ = g.shape
        return g.reshape(nl, ng, N_DEV, cg // N_DEV, cg).transpose(2, 0, 1, 3, 4)
    if name == "w_dkv":
        w = jnp.concatenate([g[:, :KV_RANK], g[:, KV_RANK + ROPE_LANE0:KV_RANK + ROPE_LANE0 + QK_ROPE]], axis=1)
        return w.reshape(N_DEV, -1, KV_RANK + QK_ROPE)
    if name in ("w_uk", "w_uv"):
        return g.reshape(KV_RANK, N_HEADS, HEAD_LANES)[:, :, :V_HEAD].transpose(1, 0, 2)
    if name == "w_dq":
        nl, d, r = g.shape
        return g.reshape(nl, N_DEV, d // N_DEV, r).transpose(1, 0, 2, 3)
    if name == "w_uq":
        nl = g.shape[0]
        return g.reshape(nl, Q_RANK, N_HEADS, HEAD_LANES)[..., :QK_NOPE + QK_ROPE].transpose(2, 0, 1, 3)
    if name == "w_o":
        nl, _, d = g.shape
        w = g.reshape(nl, N_HEADS, HEAD_LANES, d)[:, :, :V_HEAD].reshape(nl, N_HEADS * V_HEAD, N_DEV, d // N_DEV)
        return w.transpose(2, 0, 1, 3)
    if name in ("meta_tokens", "pool_scale"):
        r, d = g.shape
        return g.reshape(r, N_DEV, d // N_DEV).transpose(1, 0, 2)
    raise ValueError(name)


def _rope_tables(L):
    pos = jnp.maximum(jnp.arange(L) - FRONT_PAD, 0).astype(F32)
    inv = 1.0 / (ROPE_THETA ** (jnp.arange(0, QK_ROPE, 2, dtype=F32) / QK_ROPE))
    ang = pos[:, None] * inv[None, :]
    cos, sin = jnp.cos(ang), jnp.sin(ang)
    half = QK_ROPE // 2
    z = lambda n: jnp.zeros((L, n), F32)
    tail = z(HEAD_LANES - ROPE_LANE0 - QK_ROPE)
    return {
        "cq": jnp.concatenate([jnp.ones((L, ROPE_LANE0), F32), cos, cos, tail], axis=1),
        "ck": jnp.concatenate([z(ROPE_LANE0), cos, cos, tail], axis=1),
        "s1": jnp.concatenate([z(ROPE_LANE0), -sin, z(half), tail], axis=1),
        "s2": jnp.concatenate([z(ROPE_LANE0), z(half), sin, tail], axis=1),
    }


def kernel(x, meta_tokens, ffn1_norm, ffn1_w_gate, ffn1_w_up, ffn1_w_down, mix_norm, ffn2_norm, ffn2_w_gate, ffn2_w_up, ffn2_w_down, pool_w, pool_scale, kv_in_norm, w_dkv, kv_latent_norm, w_uk, w_uv, w_dq, q_latent_norm, w_uq, w_o, final_norm, loss_target, m_meta_tokens, m_ffn1_norm, m_ffn1_w_gate, m_ffn1_w_up, m_ffn1_w_down, m_mix_norm, m_ffn2_norm, m_ffn2_w_gate, m_ffn2_w_up, m_ffn2_w_down, m_pool_w, m_pool_scale, m_kv_in_norm, m_w_dkv, m_kv_latent_norm, m_w_uk, m_w_uv, m_w_dq, m_q_latent_norm, m_w_uq, m_w_o, m_final_norm, v_meta_tokens, v_ffn1_norm, v_ffn1_w_gate, v_ffn1_w_up, v_ffn1_w_down, v_mix_norm, v_ffn2_norm, v_ffn2_w_gate, v_ffn2_w_up, v_ffn2_w_down, v_pool_w, v_pool_scale, v_kv_in_norm, v_w_dkv, v_kv_latent_norm, v_w_uk, v_w_uv, v_w_dq, v_q_latent_norm, v_w_uq, v_w_o, v_final_norm):
    args = dict(locals())
    W = {n: args[n] for n in WEIGHTS}
    M = {n: args["m_" + n] for n in WEIGHTS}
    V = {n: args["v_" + n] for n in WEIGHTS}
    seq, D = x.shape[1], x.shape[2]
    L = SEQ_START + seq

    def weight_piece(l):
        parts = [W[n][l].astype(BF16) for n in FFN_WEIGHTS]
        if l == 0:
            parts += [_to_words(W[n]) if n in SHARDED_F32 else W[n].astype(BF16) for n in SMALL_SHARDED]
        return _pack(parts, 0, 16)

    def piece_shapes(l):
        shapes = [W[n].shape[1:] for n in FFN_WEIGHTS]
        if l == 0:
            shapes += [W[n].shape + ((2,) if n in SHARDED_F32 else ()) for n in SMALL_SHARDED]
        return shapes

    pieces = [weight_piece(l) for l in range(DEPTH)]
    P = {n: [None] * DEPTH for n in FFN_WEIGHTS}

    def take_piece(l, gathered):
        parts = _unpack(_with_own(gathered, pieces[l]), piece_shapes(l), 1, 16)
        for n, s in zip(FFN_WEIGHTS, parts):
            P[n][l] = _dense(n, s[:, None])
        for n, s in zip(SMALL_SHARDED, parts[len(FFN_WEIGHTS):]):
            P[n] = _dense(n, _from_words(s) if n in SHARDED_F32 else s)

    take_piece(0, _run_side(_gather_side(pieces[0]), "all_gather_first")[0])
    norm3 = lambda a: a.reshape(a.shape[0], 1, a.shape[-1])
    row = lambda a: a.reshape(1, -1)
    g_ffn, g_mix = {1: norm3(ffn1_norm), 2: norm3(ffn2_norm)}, mix_norm
    hosted_gather = {(0, 1): 1, (0, 2): 2, (1, 1): 3}

    def ffn_forward(h, l, which):
        nxt = hosted_gather.get((l, which))
        side = None if nxt is None else _gather_side(pieces[nxt])
        outs = _ffn_fwd(h, g_ffn[which][l:l + 1], P["ffn%d_w_gate" % which][l], P["ffn%d_w_up" % which][l],
                        P["ffn%d_w_down" % which][l], 0, side)
        if nxt is not None:
            take_piece(nxt, outs[4])
        return outs[:4]

    h = jnp.concatenate([jnp.zeros((FRONT_PAD, D), F32), P["meta_tokens"], x[0]], axis=0)
    target = jnp.concatenate([jnp.zeros((SEQ_START, D), F32), loss_target[0]], axis=0)
    tabs = _rope_tables(L)
    saved = []
    kv = None
    for l in range(DEPTH):
        s = {"h1": h}
        h, s["xn1"], s["g1"], s["u1"] = ffn_forward(h, l, 1)
        s["hm"] = h
        if l < N_POOL_LAYERS:
            h = _pool_fwd(h, row(g_mix[l]), P["pool_w"][l], row(P["pool_scale"][l]), l)
        else:
            j = l - N_POOL_LAYERS
            s["q"], s["cqp"] = _q_fwd(h, tabs, row(g_mix[l]), P["w_dq"][j], row(q_latent_norm[j]), P["w_uq"][j])
            s["o"], s["lse"] = _attn_fwd(s["q"], kv["kn"], kv["kr"], kv["v"])
            h = _oproj_fwd(h, s["o"], P["w_o"][j])
        s["h2"] = h
        h, s["xn2"], s["g2"], s["u2"] = ffn_forward(h, l, 2)
        if l == N_POOL_LAYERS - 1:
            kv = {"h": h}
            kv["kn"], kv["kr"], kv["v"], kv["ckr"] = _kv_fwd(
                h, tabs, row(kv_in_norm), P["w_dkv"], row(kv_latent_norm), P["w_uk"], P["w_uv"])
        saved.append(s)
    dh, loss_row, d_final = _head(h, target, row(final_norm))
    loss = lax.psum(loss_row[0, 0], ("x", "y", "c"))

    G = {}
    stack = {n: [None] * DEPTH for n in ("ffn1_norm", "ffn1_w_gate", "ffn1_w_up", "ffn1_w_down", "mix_norm",
                                         "ffn2_norm", "ffn2_w_gate", "ffn2_w_up", "ffn2_w_down")}
    pool_dw, pool_ds = [None] * N_POOL_LAYERS, [None] * N_POOL_LAYERS
    mla = {n: [None] * (DEPTH - N_POOL_LAYERS) for n in ("w_dq", "w_uq", "w_o", "q_latent_norm")}
    dks, dvs = [], []
    my_core = lax.axis_index("c").astype(jnp.int32).reshape(1)
    my_chip = (2 * lax.axis_index("x") + lax.axis_index("y")).astype(jnp.int32).reshape(1)
    layer_grads = {n: [None] * DEPTH for n in FFN_WEIGHTS}
    grads = {}

    def reduce_cores(entries):
        gpack = _pack(entries, 1, 8)
        gpack = gpack.reshape(N_CHIPS, 2, gpack.shape[1], PACK_W)
        return _add_own(gpack, my_core, _sibling_exchange(gpack), "sum_cores")

    def finish_layer(l, partial, from_chips, small_shapes=()):
        mine = _add_own(partial, my_chip, from_chips, "sum_chips")
        parts = _unpack(mine, [W[n].shape[1:] for n in FFN_WEIGHTS] + list(small_shapes), 0, 8)
        for n, g in zip(FFN_WEIGHTS, parts):
            layer_grads[n][l] = g
        return parts[len(FFN_WEIGHTS):]

    def ffn_backward(dh, s, which, l, side=None):
        wg, wu, wd = P["ffn%d_w_gate" % which][l], P["ffn%d_w_up" % which][l], P["ffn%d_w_down" % which][l]
        outs = _ffn_bwd(dh, s["h%d" % which], s["g%d" % which], s["u%d" % which], g_ffn[which][l:l + 1],
                        wg, wu, wd, 0, side)
        dh, dg, du, act, dob, dgam = outs[:6]
        xn = s["xn%d" % which]
        stack["ffn%d_w_gate" % which][l] = _mm_tn(xn, dg, "ffn_dw_in")
        stack["ffn%d_w_up" % which][l] = _mm_tn(xn, du, "ffn_dw_in")
        stack["ffn%d_w_down" % which][l] = _mm_tn(act, dob, "ffn_dw_down")
        stack["ffn%d_norm" % which][l] = dgam
        return dh, outs[6:]

    pending = None
    for l in reversed(range(DEPTH)):
        s = saved[l]
        if l == N_POOL_LAYERS - 1:
            dh, d_dkv, d_uk, d_uv, d_kvin, d_kvlat = _kv_bwd(
                dh, kv["h"], kv["ckr"], dks, dvs, tabs, row(kv_in_norm), P["w_dkv"], row(kv_latent_norm),
                P["w_uk"], P["w_uv"])
            G.update(w_dkv=d_dkv, w_uk=d_uk, w_uv=d_uv, kv_in_norm=d_kvin, kv_latent_norm=d_kvlat)
        if pending is None:
            dh, _ = ffn_backward(dh, s, 2, l)
        else:
            dh, from_chips = ffn_backward(dh, s, 2, l, _chip_side(pending[2]))
            finish_layer(pending[0], pending[1], from_chips[0])
        if l < N_POOL_LAYERS:
            dh, pool_dw[l], pool_ds[l], stack["mix_norm"][l] = _pool_bwd(
                dh, s["hm"], row(g_mix[l]), P["pool_w"][l], row(P["pool_scale"][l]))
        else:
            j = l - N_POOL_LAYERS
            do, delta_o, mla["w_o"][j] = _oproj_bwd(dh, s["o"], P["w_o"][j])
            dq, dk, dv = _attn_bwd(s["q"], kv["kn"], kv["kr"], kv["v"], do, s["lse"], delta_o)
            dks.append(dk)
            dvs.append(dv)
            dh, mla["w_dq"][j], mla["w_uq"][j], stack["mix_norm"][l], mla["q_latent_norm"][j] = _q_bwd(
                dh, s["hm"], s["cqp"], dq, tabs, row(g_mix[l]), P["w_dq"][j], row(q_latent_norm[j]), P["w_uq"][j])
        dh, _ = ffn_backward(dh, s, 1, l)
        entries = [_shards(n, stack[n][l][None]) for n in FFN_WEIGHTS]
        if l > 0:
            pending = (l,) + tuple(reduce_cores(entries))
    grad_x = dh[SEQ_START:][None]
    for n in ("ffn1_norm", "mix_norm", "ffn2_norm"):
        G[n] = jnp.concatenate(stack[n], axis=0)
    G["pool_w"] = jnp.stack(pool_dw)
    G["pool_scale"] = jnp.concatenate(pool_ds, axis=0)
    G["w_dq"], G["w_uq"], G["w_o"] = (jnp.stack(mla[n]) for n in ("w_dq", "w_uq", "w_o"))
    G["q_latent_norm"] = jnp.concatenate(mla["q_latent_norm"], axis=0)
    G["meta_tokens"] = dh[FRONT_PAD:SEQ_START]
    G["final_norm"] = d_final

    partial, partial_bf16 = reduce_cores(entries + [_shards(n, G[n]) for n in SMALL_SHARDED])
    from_chips = _run_side(_chip_side(partial_bf16), "chip_exchange_last")[0]
    small = finish_layer(0, partial, from_chips, [W[n].shape for n in SMALL_SHARDED])
    grads.update(zip(SMALL_SHARDED, small))
    for n in FFN_WEIGHTS:
        grads[n] = jnp.stack(layer_grads[n])
    rep_shapes = [W[n].shape for n in REPLICATED]
    rpack = _pack([G[n].reshape(W[n].shape) for n in REPLICATED], 0, 8)
    everyones = _with_own(_run_side(_gather_side(rpack), "all_gather_norm_grads")[0], rpack)
    grads.update(zip(REPLICATED, _unpack(_sum_lead(everyones, "sum_devices"), rep_shapes, 0, 8)))

    delta, new_m, new_v = {}, {}, {}
    for n in WEIGHTS:
        delta[n], new_m[n], new_v[n] = _adamw(W[n], grads[n], M[n], V[n])
    return (loss, grad_x, *[grads[n] for n in WEIGHTS], *[delta[n] for n in WEIGHTS],
            *[new_m[n] for n in WEIGHTS], *[new_v[n] for n in WEIGHTS])
```

```python
import functools
import math

import jax
import jax.numpy as jnp
from jax import lax
from jax.experimental import pallas as pl
from jax.experimental.pallas import tpu as pltpu

F32 = jnp.float32
BF16 = jnp.bfloat16
MESH = pl.DeviceIdType.MESH

N_DEV = 8
N_CHIPS = 4
DEPTH = 4
N_POOL_LAYERS = 2
N_HEADS = 8
QK_NOPE = 64
QK_ROPE = 32
V_HEAD = 64
KV_RANK = 256
Q_RANK = 384
HEAD_LANES = 128
ROPE_LANE0 = QK_NOPE
BIAS_LANE = QK_NOPE + QK_ROPE
ONES_LANE = V_HEAD
LOG2E = math.log2(math.e)
N_META = 16
CHUNK_SHIFT = 6
FRONT_PAD = 112
SEQ_START = FRONT_PAD + N_META
HALO = 16
POOL_WINDOWS = (2, 4, 8, 16)
EPS = 1e-6
ROPE_THETA = 10000.0
NEG = -1e30
PACK_ROW_MULT = 256
VMEM_LIMIT = 56 * 1024 * 1024

ADAM_LR = 0.001
ADAM_B1 = 0.9
ADAM_B2 = 0.999
ADAM_EPS = 1e-08
ADAM_WD = 0.01
ADAM_STEP = 10

SHARDED = ["ffn1_w_gate", "ffn1_w_up", "ffn1_w_down", "ffn2_w_gate", "ffn2_w_up", "ffn2_w_down",
           "pool_w", "w_dkv", "w_uk", "w_uv", "w_dq", "w_uq", "w_o", "meta_tokens", "pool_scale"]
FFN_WEIGHTS = SHARDED[:6]
SMALL_SHARDED = SHARDED[6:]
SHARDED_F32 = ("meta_tokens", "pool_scale")
REPLICATED = ["ffn1_norm", "mix_norm", "ffn2_norm", "kv_in_norm", "kv_latent_norm", "q_latent_norm",
              "final_norm"]
WEIGHTS = ['meta_tokens', 'ffn1_norm', 'ffn1_w_gate', 'ffn1_w_up', 'ffn1_w_down', 'mix_norm', 'ffn2_norm',
           'ffn2_w_gate', 'ffn2_w_up', 'ffn2_w_down', 'pool_w', 'pool_scale', 'kv_in_norm', 'w_dkv',
           'kv_latent_norm', 'w_uk', 'w_uv', 'w_dq', 'q_latent_norm', 'w_uq', 'w_o', 'final_norm']


def _dot(a, b):
    return jnp.dot(a.astype(BF16), b.astype(BF16), preferred_element_type=F32)


def _dot_nt(a, b):
    return lax.dot_general(a.astype(BF16), b.astype(BF16), (((1,), (1,)), ((), ())),
                           preferred_element_type=F32)


def _dot_tn(a, b):
    return lax.dot_general(a.astype(BF16), b.astype(BF16), (((0,), (0,)), ((), ())),
                           preferred_element_type=F32)


def _sigmoid(x):
    return 1.0 / (1.0 + jnp.exp(-x))


def _rms(x, g):
    r = lax.rsqrt(jnp.mean(x * x, axis=-1, keepdims=True) + EPS)
    xh = x * r
    return xh * g, xh, r


def _rms_bwd(dy, xh, r, g):
    dxh = dy * g
    dx = r * (dxh - xh * jnp.mean(dxh * xh, axis=-1, keepdims=True))
    return dx, jnp.sum(dy * xh, axis=0, keepdims=True)


def _rope(x, c, s1, s2):
    return x * c + pltpu.roll(x, HEAD_LANES - QK_ROPE // 2, 1) * s1 + pltpu.roll(x, QK_ROPE // 2, 1) * s2


def _rope_t(d, c, s1, s2):
    return d * c + pltpu.roll(d * s1, QK_ROPE // 2, 1) + pltpu.roll(d * s2, HEAD_LANES - QK_ROPE // 2, 1)


def _params(*sem):
    return pltpu.CompilerParams(dimension_semantics=sem, vmem_limit_bytes=VMEM_LIMIT)


def _pick(n, candidates):
    for c in candidates:
        if n % c == 0:
            return c
    return n


def _row_tile(L):
    return _pick(L, (640, 128))


FF_OWNERS = 4


def _row_call(name, body, L, row_ins, full_ins, row_outs, acc_outs):
    tm = _row_tile(L)
    n = L // tm
    hb = tm // HALO
    nb = L // HALO
    in_specs, args = [], []
    for arr, kind in row_ins:
        c = arr.shape[1]
        if kind == "tile":
            spec = pl.BlockSpec((tm, c), lambda i: (i, 0))
        elif kind == "prev":
            spec = pl.BlockSpec((HALO, c), lambda i: (jnp.maximum(i * hb - 1, 0), 0))
        else:
            spec = pl.BlockSpec((HALO, c), lambda i: (jnp.minimum((i + 1) * hb, nb - 1), 0))
        in_specs.append(spec)
        args.append(arr)
    for arr in full_ins:
        in_specs.append(pl.BlockSpec(arr.shape, lambda i, nd=arr.ndim: (0,) * nd))
        args.append(arr)
    out_shape = [jax.ShapeDtypeStruct((L, c), dt) for c, dt in row_outs]
    out_specs = [pl.BlockSpec((tm, c), lambda i: (i, 0)) for c, _ in row_outs]
    for shp in acc_outs:
        out_shape.append(jax.ShapeDtypeStruct(shp, F32))
        out_specs.append(pl.BlockSpec(shp, lambda i, nd=len(shp): (0,) * nd))
    n_in, n_ro = len(args), len(row_outs)

    def kern(*refs):
        i = pl.program_id(0)
        vals = [r[...] for r in refs[:n_in]]
        ro, ao = body(i, n, tm, *vals)
        for r, v in zip(refs[n_in:n_in + n_ro], ro):
            r[...] = v.astype(r.dtype)
        acc_refs = refs[n_in + n_ro:]

        @pl.when(i == 0)
        def _():
            for r in acc_refs:
                r[...] = jnp.zeros(r.shape, r.dtype)

        for r, v in zip(acc_refs, ao):
            r[...] += v

    return pl.pallas_call(kern, name=name, grid=(n,), in_specs=in_specs, out_specs=out_specs,
                          out_shape=out_shape, compiler_params=_params("arbitrary"))(*args)


class _Side:
    def __init__(self, ins, out_shapes, sems, start, finish):
        self.ins, self.out_shapes, self.sems, self.start, self.finish = ins, out_shapes, sems, start, finish


ANY = pl.BlockSpec(memory_space=pl.ANY)


def _hosted(kern, n_in, n_out, n_scratch, side, is_first, is_last):
    if side is None:
        return kern
    ns_in, ns_out = len(side.ins), len(side.out_shapes)

    def wrapped(*refs):
        ins, refs = refs[:n_in], refs[n_in:]
        side_ins, refs = refs[:ns_in], refs[ns_in:]
        outs, refs = refs[:n_out], refs[n_out:]
        side_outs, refs = refs[:ns_out], refs[ns_out:]
        scratch, side_sems = refs[:n_scratch], refs[n_scratch:]

        @pl.when(is_first())
        def _():
            side.start(side_ins, side_outs, side_sems)

        kern(*ins, *outs, *scratch)

        @pl.when(is_last())
        def _():
            side.finish(side_ins, side_outs, side_sems)

    return wrapped


def _side_args(side):
    if side is None:
        return [], [], [], [], []
    return ([ANY] * len(side.ins), [ANY] * len(side.out_shapes), list(side.out_shapes), list(side.sems),
            list(side.ins))


def _ffn_weight_specs(fs, D, ents):
    return [pl.BlockSpec((FF_OWNERS, fs, D), lambda i, f, e=e: (f, e, 0)) for e in ents]


def _ffn_fwd(h, gam, wpiece, ents, fs, side=None):
    L, D = h.shape
    F = N_DEV * fs
    tm = _row_tile(L)
    tf = FF_OWNERS * fs
    nL, nF = L // tm, F // tf
    s_in, s_out, s_shape, s_sems, s_args = _side_args(side)

    def kern(h_ref, gam_ref, wg_ref, wu_ref, wd_ref, ho_ref, xn_ref, gs_ref, us_ref, acc):
        f = pl.program_id(1)

        @pl.when(f == 0)
        def _():
            xn, _, _ = _rms(h_ref[...], gam_ref[...])
            xn_ref[...] = xn.astype(BF16)
            acc[...] = jnp.zeros(acc.shape, F32)

        xnb = xn_ref[...]
        g = _dot_nt(xnb, wg_ref[...].reshape(tf, D))
        u = _dot_nt(xnb, wu_ref[...].reshape(tf, D))
        gs_ref[...] = g.astype(BF16)
        us_ref[...] = u.astype(BF16)
        acc[...] += _dot(g * _sigmoid(g) * u, wd_ref[...].reshape(tf, D))

        @pl.when(f == nF - 1)
        def _():
            ho_ref[...] = h_ref[...] + 0.5 * acc[...]

    first = lambda: (pl.program_id(0) == 0) & (pl.program_id(1) == 0)
    last = lambda: (pl.program_id(0) == nL - 1) & (pl.program_id(1) == nF - 1)
    return pl.pallas_call(
        _hosted(kern, 5, 4, 1, side, first, last),
        name="ffn_fwd" if side is None else "ffn_fwd_hosting", grid=(nL, nF),
        in_specs=[pl.BlockSpec((tm, D), lambda i, f: (i, 0)),
                  pl.BlockSpec((None, 1, D), lambda i, f: (0, 0, 0))] + _ffn_weight_specs(fs, D, ents) + s_in,
        out_specs=[pl.BlockSpec((tm, D), lambda i, f: (i, 0)),
                   pl.BlockSpec((tm, D), lambda i, f: (i, 0)),
                   pl.BlockSpec((tm, tf), lambda i, f: (i, f)),
                   pl.BlockSpec((tm, tf), lambda i, f: (i, f))] + s_out,
        out_shape=[jax.ShapeDtypeStruct((L, D), F32), jax.ShapeDtypeStruct((L, D), BF16),
                   jax.ShapeDtypeStruct((L, F), BF16), jax.ShapeDtypeStruct((L, F), BF16)] + s_shape,
        scratch_shapes=[pltpu.VMEM((tm, D), F32)] + s_sems,
        compiler_params=_params("arbitrary", "arbitrary"))(h, gam, wpiece, wpiece, wpiece, *s_args)


def _ffn_bwd(dh, h, gs, us, gam, wpiece, ents, fs, side=None):
    L, D = h.shape
    F = N_DEV * fs
    tm = _pick(L, (416, 128))
    tf = FF_OWNERS * fs
    nL, nF = L // tm, F // tf
    s_in, s_out, s_shape, s_sems, s_args = _side_args(side)

    def kern(dh_ref, h_ref, gs_ref, us_ref, gam_ref, wg_ref, wu_ref, wd_ref,
             dhi_ref, dg_ref, du_ref, a_ref, dob_ref, dgam_ref, dxn):
        i = pl.program_id(0)
        f = pl.program_id(1)

        @pl.when(f == 0)
        def _():
            dxn[...] = jnp.zeros(dxn.shape, F32)
            dob_ref[...] = (0.5 * dh_ref[...]).astype(BF16)

        @pl.when((f == 0) & (i == 0))
        def _():
            dgam_ref[...] = jnp.zeros(dgam_ref.shape, F32)

        g = gs_ref[...].astype(F32)
        u = us_ref[...].astype(F32)
        sg = _sigmoid(g)
        silu = g * sg
        da = _dot_nt(dob_ref[...], wd_ref[...].reshape(tf, D))
        a_ref[...] = (silu * u).astype(BF16)
        dgt = (da * u * (sg * (1.0 + g * (1.0 - sg)))).astype(BF16)
        dut = (da * silu).astype(BF16)
        dg_ref[...] = dgt
        du_ref[...] = dut
        dxn[...] += _dot(dgt, wg_ref[...].reshape(tf, D)) + _dot(dut, wu_ref[...].reshape(tf, D))

        @pl.when(f == nF - 1)
        def _():
            gamma = gam_ref[...]
            _, xh, r = _rms(h_ref[...], gamma)
            dx, dgam = _rms_bwd(dxn[...], xh, r, gamma)
            dhi_ref[...] = dh_ref[...] + dx
            dgam_ref[...] += dgam

    first = lambda: (pl.program_id(0) == 0) & (pl.program_id(1) == 0)
    last = lambda: (pl.program_id(0) == nL - 1) & (pl.program_id(1) == nF - 1)
    return pl.pallas_call(
        _hosted(kern, 8, 6, 1, side, first, last),
        name="ffn_bwd" if side is None else "ffn_bwd_hosting", grid=(nL, nF),
        in_specs=[pl.BlockSpec((tm, D), lambda i, f: (i, 0)),
                  pl.BlockSpec((tm, D), lambda i, f: (i, 0)),
                  pl.BlockSpec((tm, tf), lambda i, f: (i, f)),
                  pl.BlockSpec((tm, tf), lambda i, f: (i, f)),
                  pl.BlockSpec((None, 1, D), lambda i, f: (0, 0, 0))] + _ffn_weight_specs(fs, D, ents) + s_in,
        out_specs=[pl.BlockSpec((tm, D), lambda i, f: (i, 0)),
                   pl.BlockSpec((tm, tf), lambda i, f: (i, f)),
                   pl.BlockSpec((tm, tf), lambda i, f: (i, f)),
                   pl.BlockSpec((tm, tf), lambda i, f: (i, f)),
                   pl.BlockSpec((tm, D), lambda i, f: (i, 0)),
                   pl.BlockSpec((1, D), lambda i, f: (0, 0))] + s_out,
        out_shape=[jax.ShapeDtypeStruct((L, D), F32), jax.ShapeDtypeStruct((L, F), BF16),
                   jax.ShapeDtypeStruct((L, F), BF16), jax.ShapeDtypeStruct((L, F), BF16),
                   jax.ShapeDtypeStruct((L, D), BF16), jax.ShapeDtypeStruct((1, D), F32)] + s_shape,
        scratch_shapes=[pltpu.VMEM((tm, D), F32)] + s_sems,
        compiler_params=_params("arbitrary", "arbitrary"))(dh, h, gs, us, gam, wpiece, wpiece, wpiece, *s_args)


def _mm_tn(a, b, name):
    L, M = a.shape
    N = b.shape[1]
    tm = _pick(M, (1408, 1024, 512))
    tn = _pick(N, (1408, 1024, 512))
    tk = _pick(L, (2080, 640, 128))

    def kern(a_ref, b_ref, o_ref):
        @pl.when(pl.program_id(2) == 0)
        def _():
            o_ref[...] = jnp.zeros(o_ref.shape, F32)

        o_ref[...] += _dot_tn(a_ref[...], b_ref[...])

    return pl.pallas_call(
        kern, name=name, grid=(M // tm, N // tn, L // tk),
        in_specs=[pl.BlockSpec((tk, tm), lambda i, j, k: (k, i)),
                  pl.BlockSpec((tk, tn), lambda i, j, k: (k, j))],
        out_specs=pl.BlockSpec((tm, tn), lambda i, j, k: (i, j)),
        out_shape=jax.ShapeDtypeStruct((M, N), F32),
        compiler_params=_params("arbitrary", "arbitrary", "arbitrary"))(a, b)


def _pool_counts(pos, w):
    return jnp.clip(pos - (FRONT_PAD - 1), 1, w).astype(F32)


def _pool_forward_values(i, tm, h, hprev, gamma, D):
    cg = D // len(POOL_WINDOWS)
    hext = jnp.concatenate([hprev, h], axis=0)
    uext, xh, r = _rms(hext, gamma)
    pos = i * tm + lax.broadcasted_iota(jnp.int32, (tm, 1), 0)
    pooled = []
    for gi, w in enumerate(POOL_WINDOWS):
        s = uext[:, gi * cg:(gi + 1) * cg]
        span = 1
        while span < w:
            s = s + pltpu.roll(s, span, 0)
            span *= 2
        s = s[HALO:]
        pooled.append(s / _pool_counts(pos, w) - uext[HALO:, gi * cg:(gi + 1) * cg])
    return uext, xh[HALO:], r[HALO:], pooled


def _pool_fwd(h, gam, w, scale, l):
    L, D = h.shape
    cg = D // len(POOL_WINDOWS)

    def body(i, n, tm, ht, hprev, gamma, wv, sc):
        _, _, _, pooled = _pool_forward_values(i, tm, ht, hprev, gamma, D)
        ys = [_dot(pooled[gi], wv[gi]) for gi in range(len(POOL_WINDOWS))]
        y = jnp.concatenate(ys, axis=1) * sc
        return [ht + y], []

    del cg
    return _row_call("pool_fwd", body, L, [(h, "tile"), (h, "prev")], [gam, w, scale], [(D, F32)], [])[0]


def _pool_bwd(dy, h, gam, w, scale):
    L, D = h.shape
    ng = len(POOL_WINDOWS)
    cg = D // ng

    def body(i, n, tm, ht, hprev, dyt, dynext, gamma, wv, sc):
        _, xh, r, pooled = _pool_forward_values(i, tm, ht, hprev, gamma, D)
        dynext = jnp.where(i == n - 1, jnp.zeros_like(dynext), dynext)
        dyext = jnp.concatenate([dyt, dynext], axis=0) * sc
        pos_ext = i * tm + lax.broadcasted_iota(jnp.int32, (tm + HALO, 1), 0)
        dws, dscs, dus = [], [], []
        for gi, wd in enumerate(POOL_WINDOWS):
            cols = slice(gi * cg, (gi + 1) * cg)
            pb = pooled[gi].astype(BF16)
            ypre = _dot(pb, wv[gi])
            dscs.append(jnp.sum(dyt[:, cols] * ypre, axis=0, keepdims=True))
            dws.append(_dot_tn(pb, dyext[:tm, cols])[None])
            dp = _dot_nt(dyext[:, cols], wv[gi])
            s = dp / _pool_counts(pos_ext, wd)
            span = 1
            while span < wd:
                s = s + pltpu.roll(s, tm + HALO - span, 0)
                span *= 2
            dus.append(s[:tm] - dp[:tm])
        du = jnp.concatenate(dus, axis=1)
        pos = pos_ext[:tm]
        du = jnp.where(pos >= FRONT_PAD, du, 0.0)
        dx, dgam = _rms_bwd(du, xh, r, gamma)
        return [dyt + dx], [jnp.concatenate(dws, axis=0), jnp.concatenate(dscs, axis=1), dgam]

    return _row_call("pool_bwd", body, L, [(h, "tile"), (h, "prev"), (dy, "tile"), (dy, "next")],
                     [gam, w, scale], [(D, F32)], [(ng, cg, cg), (1, D), (1, D)])


def _kv_fwd(h, tabs, g1, wdkv, g2, wuk, wuv):
    L, D = h.shape
    hw = N_HEADS * HEAD_LANES

    def body(i, n, tm, ht, ck, s1, s2, g1v, wdkv_v, g2v, wuk_v, wuv_v):
        xkv, _, _ = _rms(ht, g1v)
        ckr = _dot(xkv, wdkv_v)
        ckv, _, _ = _rms(ckr[:, :KV_RANK], g2v)
        krope = _rope(ckr[:, KV_RANK:], ck, s1, s2)
        pos = i * tm + lax.broadcasted_iota(jnp.int32, (tm, HEAD_LANES), 0)
        lane = lax.broadcasted_iota(jnp.int32, (tm, HEAD_LANES), 1)
        krope = jnp.where((pos < FRONT_PAD) & (lane == BIAS_LANE), NEG, krope)
        ones = ((lax.broadcasted_iota(jnp.int32, (1, hw), 1) & (HEAD_LANES - 1)) == ONES_LANE).astype(F32)
        return [_dot(ckv, wuk_v), krope, _dot(ckv, wuv_v) + ones, ckr], []

    ck, s1, s2 = tabs["ck"], tabs["s1"], tabs["s2"]
    return _row_call("kv_fwd", body, L, [(h, "tile"), (ck, "tile"), (s1, "tile"), (s2, "tile")],
                     [g1, wdkv, g2, wuk, wuv],
                     [(hw, BF16), (HEAD_LANES, BF16), (hw, BF16), (KV_RANK + HEAD_LANES, F32)], [])


def _kv_bwd(dh, h, ckr, dks, dvs, tabs, g1, wdkv, g2, wuk, wuv):
    L, D = h.shape
    hw = N_HEADS * HEAD_LANES
    nl = len(dks)

    def body(i, n, tm, *vals):
        dht, ht, ckr_t = vals[:3]
        dk = sum(vals[3:3 + nl][1:], vals[3])
        dv = sum(vals[3 + nl:3 + 2 * nl][1:], vals[3 + nl])
        ck, s1, s2, g1v, wdkv_v, g2v, wuk_v, wuv_v = vals[3 + 2 * nl:]
        xkv, xh1, r1 = _rms(ht, g1v)
        ckv, xh2, r2 = _rms(ckr_t[:, :KV_RANK], g2v)
        dckv = _dot_nt(dk, wuk_v) + _dot_nt(dv, wuv_v)
        dlat, dg2 = _rms_bwd(dckv, xh2, r2, g2v)
        dkr = dk[:, :HEAD_LANES]
        for hd in range(1, N_HEADS):
            dkr = dkr + dk[:, hd * HEAD_LANES:(hd + 1) * HEAD_LANES]
        dckr = jnp.concatenate([dlat, _rope_t(dkr, ck, s1, s2)], axis=1)
        dx, dg1 = _rms_bwd(_dot_nt(dckr, wdkv_v), xh1, r1, g1v)
        return [dht + dx], [_dot_tn(xkv, dckr), _dot_tn(ckv, dk), _dot_tn(ckv, dv), dg1, dg2]

    row_ins = [(dh, "tile"), (h, "tile"), (ckr, "tile")] + [(a, "tile") for a in dks + dvs]
    row_ins += [(tabs[k], "tile") for k in ("ck", "s1", "s2")]
    return _row_call("kv_bwd", body, L, row_ins, [g1, wdkv, g2, wuk, wuv], [(D, F32)],
                     [(D, KV_RANK + HEAD_LANES), (KV_RANK, hw), (KV_RANK, hw), (1, D), (1, KV_RANK)])


def _q_fwd(h, tabs, g, wdq, gq, wuq):
    L, D = h.shape
    hw = N_HEADS * HEAD_LANES

    def body(i, n, tm, ht, cq_t, s1, s2, gv, wdq_v, gqv, wuq_v):
        u, _, _ = _rms(ht, gv)
        cqp = _dot(u, wdq_v)
        cq, _, _ = _rms(cqp, gqv)
        qp = _dot(cq, wuq_v)
        bias = (lax.broadcasted_iota(jnp.int32, (1, HEAD_LANES), 1) == BIAS_LANE).astype(F32)
        q = [_rope(qp[:, hd * HEAD_LANES:(hd + 1) * HEAD_LANES], cq_t, s1, s2) * (SM_SCALE * LOG2E) + bias
             for hd in range(N_HEADS)]
        return [jnp.concatenate(q, axis=1), cqp], []

    return _row_call("q_fwd", body, L, [(h, "tile")] + [(tabs[k], "tile") for k in ("cq", "s1", "s2")],
                     [g, wdq, gq, wuq], [(hw, BF16), (Q_RANK, F32)], [])


def _q_bwd(dh, h, cqp, dq, tabs, g, wdq, gq, wuq):
    L, D = h.shape
    hw = N_HEADS * HEAD_LANES

    def body(i, n, tm, dht, ht, cqp_t, dq_t, cq_t, s1, s2, gv, wdq_v, gqv, wuq_v):
        u, xh1, r1 = _rms(ht, gv)
        cq, xh2, r2 = _rms(cqp_t, gqv)
        dqp = jnp.concatenate([_rope_t(dq_t[:, hd * HEAD_LANES:(hd + 1) * HEAD_LANES], cq_t, s1, s2)
                               for hd in range(N_HEADS)], axis=1)
        dcqp, dgq = _rms_bwd(_dot_nt(dqp, wuq_v), xh2, r2, gqv)
        dx, dg = _rms_bwd(_dot_nt(dcqp, wdq_v), xh1, r1, gv)
        return [dht + dx], [_dot_tn(u, dcqp), _dot_tn(cq, dqp), dg, dgq]

    row_ins = [(dh, "tile"), (h, "tile"), (cqp, "tile"), (dq, "tile")]
    row_ins += [(tabs[k], "tile") for k in ("cq", "s1", "s2")]
    return _row_call("q_bwd", body, L, row_ins, [g, wdq, gq, wuq], [(D, F32)],
                     [(D, Q_RANK), (Q_RANK, hw), (1, D), (1, Q_RANK)])


def _oproj_fwd(h, o, wo):
    L, D = h.shape

    def body(i, n, tm, ht, ot, wov):
        return [ht + _dot(ot, wov)], []

    return _row_call("oproj_fwd", body, L, [(h, "tile"), (o, "tile")], [wo], [(D, F32)], [])[0]


def _oproj_bwd(dh, o, wo):
    L, D = dh.shape
    hw = N_HEADS * HEAD_LANES

    def body(i, n, tm, dht, ot, wov):
        do = _dot_nt(dht, wov)
        prod = do * ot.astype(F32)
        delta = [jnp.broadcast_to(jnp.sum(prod[:, hd * HEAD_LANES:(hd + 1) * HEAD_LANES], axis=-1, keepdims=True),
                                  (tm, HEAD_LANES)) for hd in range(N_HEADS)]
        return [do, jnp.concatenate(delta, axis=1)], [_dot_tn(ot, dht)]

    return _row_call("oproj_bwd", body, L, [(dh, "tile"), (o, "tile")], [wo], [(hw, BF16), (hw, F32)], [(hw, D)])


def _causal(t):
    qpos = lax.broadcasted_iota(jnp.int32, (t, t), 0)
    kpos = lax.broadcasted_iota(jnp.int32, (t, t), 1)
    return (kpos >> CHUNK_SHIFT) <= (qpos >> CHUNK_SHIFT)


SM_SCALE = 1.0 / math.sqrt(QK_NOPE + QK_ROPE)


def _pairs(n, key_major):
    if key_major:
        order = [(i, j) for j in range(n) for i in range(j, n)]
    else:
        order = [(i, j) for i in range(n) for j in range(i + 1)]
    return (jnp.array([p[0] for p in order], jnp.int32), jnp.array([p[1] for p in order], jnp.int32))


def _attn_fwd(q, kn, kr, v):
    L = q.shape[0]
    hw = N_HEADS * HEAD_LANES
    t = _row_tile(L)
    it, jt = _pairs(L // t, key_major=False)

    def kern(it_ref, jt_ref, q_ref, kn_ref, kr_ref, v_ref, o_ref, lse_ref, m_s, acc_s):
        step = pl.program_id(1)
        i, j = it_ref[step], jt_ref[step]

        @pl.when(j == 0)
        def _():
            m_s[...] = jnp.full(m_s.shape, NEG, F32)
            acc_s[...] = jnp.zeros(acc_s.shape, F32)

        def update(diagonal):
            k = kn_ref[...] + kr_ref[...]
            s = _dot_nt(q_ref[...], k)
            if diagonal:
                s = jnp.where(_causal(t), s, NEG)
            m_prev = m_s[:, :1]
            m_new = jnp.maximum(m_prev, jnp.max(s, axis=-1, keepdims=True))
            p = jnp.exp2(s - m_new)
            acc_s[...] = jnp.exp2(m_prev - m_new) * acc_s[...] + _dot(p, v_ref[...])
            m_s[...] = jnp.broadcast_to(m_new, m_s.shape)

        @pl.when(j < i)
        def _():
            update(False)

        @pl.when(j == i)
        def _():
            update(True)
            acc = acc_s[...]
            total = acc[:, ONES_LANE:ONES_LANE + 1]
            o_ref[...] = (acc / total).astype(BF16)
            lse_ref[...] = m_s[...] + jnp.log2(jnp.broadcast_to(total, m_s.shape))

    qmap = lambda h, s, it, jt: (it[s], h)
    kmap = lambda h, s, it, jt: (jt[s], h)
    blk = (t, HEAD_LANES)
    return pl.pallas_call(
        kern, name="attn_fwd",
        grid_spec=pltpu.PrefetchScalarGridSpec(
            num_scalar_prefetch=2, grid=(N_HEADS, it.shape[0]),
            in_specs=[pl.BlockSpec(blk, qmap), pl.BlockSpec(blk, kmap),
                      pl.BlockSpec(blk, lambda h, s, it, jt: (jt[s], 0)), pl.BlockSpec(blk, kmap)],
            out_specs=[pl.BlockSpec(blk, qmap), pl.BlockSpec(blk, qmap)],
            scratch_shapes=[pltpu.VMEM(blk, F32)] * 2),
        out_shape=[jax.ShapeDtypeStruct((L, hw), BF16), jax.ShapeDtypeStruct((L, hw), F32)],
        compiler_params=_params("arbitrary", "arbitrary"))(it, jt, q, kn, kr, v)


def _attn_bwd(q, kn, kr, v, do, lse, delta):
    L = q.shape[0]
    hw = N_HEADS * HEAD_LANES
    t = _row_tile(L)
    it, jt = _pairs(L // t, key_major=True)

    def kern(it_ref, jt_ref, q_ref, kn_ref, kr_ref, v_ref, do_ref, lse_ref, dl_ref, dq_ref, dk_ref, dv_ref):
        step = pl.program_id(1)
        i, j = it_ref[step], jt_ref[step]

        @pl.when(step == 0)
        def _():
            dq_ref[...] = jnp.zeros(dq_ref.shape, F32)

        @pl.when(i == j)
        def _():
            dk_ref[...] = jnp.zeros(dk_ref.shape, F32)
            dv_ref[...] = jnp.zeros(dv_ref.shape, F32)

        def update(diagonal):
            k = kn_ref[...] + kr_ref[...]
            qv, dov = q_ref[...], do_ref[...]
            s = _dot_nt(qv, k)
            if diagonal:
                s = jnp.where(_causal(t), s, NEG)
            p = jnp.exp2(s - lse_ref[:, :1])
            dp = _dot_nt(dov, v_ref[...])
            dz = (p * (dp - dl_ref[:, :1])).astype(BF16)
            dv_ref[...] += _dot_tn(p, dov)
            dk_ref[...] += _dot_tn(dz, qv) * (1.0 / LOG2E)
            rows = pl.ds(pl.multiple_of(i * t, t), t)
            dq_ref[rows, :] += _dot(dz, k) * SM_SCALE

        @pl.when(j < i)
        def _():
            update(False)

        @pl.when(j == i)
        def _():
            update(True)

    qmap = lambda h, s, it, jt: (it[s], h)
    kmap = lambda h, s, it, jt: (jt[s], h)
    blk = (t, HEAD_LANES)
    return pl.pallas_call(
        kern, name="attn_bwd",
        grid_spec=pltpu.PrefetchScalarGridSpec(
            num_scalar_prefetch=2, grid=(N_HEADS, it.shape[0]),
            in_specs=[pl.BlockSpec(blk, qmap), pl.BlockSpec(blk, kmap),
                      pl.BlockSpec(blk, lambda h, s, it, jt: (jt[s], 0)), pl.BlockSpec(blk, kmap),
                      pl.BlockSpec(blk, qmap), pl.BlockSpec(blk, qmap), pl.BlockSpec(blk, qmap)],
            out_specs=[pl.BlockSpec((L, HEAD_LANES), lambda h, s, it, jt: (0, h)),
                       pl.BlockSpec(blk, kmap), pl.BlockSpec(blk, kmap)]),
        out_shape=[jax.ShapeDtypeStruct((L, hw), F32)] * 3,
        compiler_params=_params("arbitrary", "arbitrary"))(it, jt, q, kn, kr, v, do, lse, delta)


def _head(h, target, g):
    L, D = h.shape

    def body(i, n, tm, ht, tt, gv):
        y, xh, r = _rms(ht, gv)
        pos = i * tm + lax.broadcasted_iota(jnp.int32, (tm, 1), 0)
        e = jnp.where(pos >= SEQ_START, y - tt, 0.0)
        loss = 0.5 * jnp.sum(jnp.mean(e * e, axis=-1, keepdims=True), axis=0, keepdims=True)
        dx, dg = _rms_bwd(e / D, xh, r, gv)
        return [dx], [jnp.broadcast_to(loss, (1, 128)), dg]

    return _row_call("loss_head", body, L, [(h, "tile"), (target, "tile")], [g], [(D, F32)], [(1, 128), (1, D)])


def _coords():
    return lax.axis_index("x"), lax.axis_index("y"), lax.axis_index("c")


def _my_index():
    mx, my, mc = _coords()
    return 4 * mx + 2 * my + mc


def _gather_side(x):
    R, W = x.shape

    def copies(x_ref, out_ref, send_sems, recv_sems):
        mx, my, mc = _coords()
        me, sibling = (mx, my, mc), (mx, my, 1 - mc)
        chips = [(1 - mx, my), (mx, 1 - my), (1 - mx, 1 - my)]

        def slot(px, py, pc):
            return out_ref.at[4 * px + 2 * py + pc]

        def copy(k, block, to, src=None):
            return pltpu.make_async_remote_copy(
                src_ref=slot(*block) if src is None else src, dst_ref=slot(*block),
                send_sem=send_sems.at[k], recv_sem=recv_sems.at[k], device_id=to, device_id_type=MESH)

        first = [copy(0, me, sibling, src=x_ref)]
        first += [copy(1 + n, me, (*chip, mc), src=x_ref) for n, chip in enumerate(chips)]
        passed = [copy(4 + n, (*chip, mc), sibling) for n, chip in enumerate(chips)]
        landed = [copy(1 + n, (*chip, mc), me) for n, chip in enumerate(chips)]
        from_sibling = [copy(0, sibling, me)] + [copy(4 + n, (*chip, 1 - mc), me) for n, chip in enumerate(chips)]
        return first, passed, landed, from_sibling

    def start(ins, outs, sems):
        for cp in copies(ins[0], outs[0], *sems)[0]:
            cp.start()

    def finish(ins, outs, sems):
        first, passed, landed, from_sibling = copies(ins[0], outs[0], *sems)
        for arrived, onward in zip(landed, passed):
            arrived.wait_recv()
            onward.start()
        for cp in from_sibling:
            cp.wait_recv()
        for cp in first + passed:
            cp.wait_send()

    return _Side([x], [jax.ShapeDtypeStruct((N_DEV, R, W), x.dtype)],
                 [pltpu.SemaphoreType.DMA((7,)), pltpu.SemaphoreType.DMA((7,))], start, finish)


def _with_own(gathered, x):
    return lax.dynamic_update_slice(gathered, x[None], (_my_index(), 0, 0))


def _chip_side(parts):
    n_arr = len(parts)

    def copies(p_refs, out_refs, send_sems, recv_sems):
        mx, my, mc = _coords()
        chips = [(1 - mx, my), (mx, 1 - my), (1 - mx, 1 - my)]
        return [pltpu.make_async_remote_copy(
            src_ref=p_ref.at[2 * cx + cy], dst_ref=out_ref.at[n], send_sem=send_sems.at[3 * a + n],
            recv_sem=recv_sems.at[3 * a + n], device_id=(cx, cy, mc), device_id_type=MESH)
            for a, (p_ref, out_ref) in enumerate(zip(p_refs, out_refs)) for n, (cx, cy) in enumerate(chips)]

    def start(ins, outs, sems):
        for cp in copies(ins, outs, *sems):
            cp.start()

    def finish(ins, outs, sems):
        cps = copies(ins, outs, *sems)
        for cp in cps:
            cp.wait_recv()
        for cp in cps:
            cp.wait_send()

    return _Side(list(parts), [jax.ShapeDtypeStruct((3,) + p.shape[1:], p.dtype) for p in parts],
                 [pltpu.SemaphoreType.DMA((3 * n_arr,)), pltpu.SemaphoreType.DMA((3 * n_arr,))], start, finish)


def _run_side(side, name):
    def kern(*refs):
        n_in, n_out = len(side.ins), len(side.out_shapes)
        ins, outs, sems = refs[:n_in], refs[n_in:n_in + n_out], refs[n_in + n_out:]
        side.start(ins, outs, sems)
        side.finish(ins, outs, sems)

    return pl.pallas_call(kern, name=name, in_specs=[ANY] * len(side.ins), out_specs=[ANY] * len(side.out_shapes),
                          out_shape=list(side.out_shapes), scratch_shapes=list(side.sems))(*side.ins)


def _sibling_side(arrs):
    n_arr = len(arrs)

    def copies(g_refs, out_refs, send_sems, recv_sems):
        mx, my, mc = _coords()
        return [pltpu.make_async_remote_copy(
            src_ref=g_ref.at[n, 1 - mc], dst_ref=out_ref.at[n], send_sem=send_sems.at[N_CHIPS * a + n],
            recv_sem=recv_sems.at[N_CHIPS * a + n], device_id=(mx, my, 1 - mc), device_id_type=MESH)
            for a, (g_ref, out_ref) in enumerate(zip(g_refs, out_refs)) for n in range(N_CHIPS)]

    def start(ins, outs, sems):
        for cp in copies(ins, outs, *sems):
            cp.start()

    def finish(ins, outs, sems):
        cps = copies(ins, outs, *sems)
        for cp in cps:
            cp.wait_recv()
        for cp in cps:
            cp.wait_send()

    return _Side(list(arrs), [jax.ShapeDtypeStruct((N_CHIPS,) + g.shape[2:], g.dtype) for g in arrs],
                 [pltpu.SemaphoreType.DMA((N_CHIPS * n_arr,)), pltpu.SemaphoreType.DMA((N_CHIPS * n_arr,))],
                 start, finish)


def _add_own(owns, sel, others, name):
    n_arr = len(owns)
    R, W = owns[0].shape[-2:]
    tr = _pick(R, (PACK_ROW_MULT, 176, 64, 8))
    first_phase = owns[0].ndim == 4
    if first_phase:
        n = owns[0].shape[0]
        grid = (n, R // tr)
        in_specs = ([pl.BlockSpec((None, None, tr, W), lambda b, i, sel: (b, sel[0], i, 0))] * n_arr
                    + [pl.BlockSpec((None, tr, W), lambda b, i, sel: (b, i, 0))] * n_arr)
        out_specs = [pl.BlockSpec((None, tr, W), lambda b, i, sel: (b, i, 0))] * (2 * n_arr)
        out_shape = [jax.ShapeDtypeStruct((n, R, W), F32)] * n_arr + [jax.ShapeDtypeStruct((n, R, W), BF16)] * n_arr

        def kern(sel_ref, *refs):
            for a in range(n_arr):
                acc = refs[a][...] + refs[n_arr + a][...]
                refs[2 * n_arr + a][...] = acc
                refs[3 * n_arr + a][...] = acc.astype(BF16)
    else:
        k = others[0].shape[0]
        grid = (1, R // tr)
        in_specs = ([pl.BlockSpec((None, tr, W), lambda b, i, sel: (sel[0], i, 0))] * n_arr
                    + [pl.BlockSpec((k, tr, W), lambda b, i, sel: (0, i, 0))] * n_arr)
        out_specs = [pl.BlockSpec((tr, W), lambda b, i, sel: (i, 0))] * n_arr
        out_shape = [jax.ShapeDtypeStruct((R, W), F32)] * n_arr

        def kern(sel_ref, *refs):
            for a in range(n_arr):
                acc = refs[a][...]
                for m in range(k):
                    acc = acc + refs[n_arr + a][m].astype(F32)
                refs[2 * n_arr + a][...] = acc

    outs = pl.pallas_call(
        kern, name=name,
        grid_spec=pltpu.PrefetchScalarGridSpec(num_scalar_prefetch=1, grid=grid, in_specs=in_specs,
                                               out_specs=out_specs),
        out_shape=out_shape, compiler_params=_params("arbitrary", "arbitrary"))(sel, *owns, *others)
    return (outs[:n_arr], outs[n_arr:]) if first_phase else outs


def _sum_lead(x, name):
    n, R, W = x.shape
    tr = _pick(R, (PACK_ROW_MULT, 8))

    def kern(x_ref, o_ref):
        acc = x_ref[0]
        for k in range(1, n):
            acc = acc + x_ref[k]
        o_ref[...] = acc

    return pl.pallas_call(
        kern, name=name, grid=(R // tr,),
        in_specs=[pl.BlockSpec((n, tr, W), lambda i: (0, i, 0))],
        out_specs=pl.BlockSpec((tr, W), lambda i: (i, 0)),
        out_shape=jax.ShapeDtypeStruct((R, W), F32), compiler_params=_params("arbitrary"))(x)


def _adamw(w, g, m, v):
    shape = w.shape
    cols = shape[-1]
    rows = w.size // cols
    tr = _pick(rows, (512, 352, 256, 128))
    if rows * cols * 4 <= (1 << 20):
        tr = rows

    def kern(w_ref, g_ref, m_ref, v_ref, d_ref, mo_ref, vo_ref):
        gv = g_ref[...]
        mn = ADAM_B1 * m_ref[...] + (1.0 - ADAM_B1) * gv
        vn = ADAM_B2 * v_ref[...] + (1.0 - ADAM_B2) * (gv * gv)
        m_hat = mn / (1.0 - ADAM_B1 ** ADAM_STEP)
        v_hat = vn / (1.0 - ADAM_B2 ** ADAM_STEP)
        d_ref[...] = -ADAM_LR * (m_hat / (jnp.sqrt(v_hat) + ADAM_EPS) + ADAM_WD * w_ref[...])
        mo_ref[...] = mn
        vo_ref[...] = vn

    spec = pl.BlockSpec((tr, cols), lambda i: (i, 0))
    outs = pl.pallas_call(
        kern, name="adamw", grid=(rows // tr,), in_specs=[spec] * 4, out_specs=[spec] * 3,
        out_shape=[jax.ShapeDtypeStruct((rows, cols), F32)] * 3, compiler_params=_params("arbitrary"),
    )(*[a.reshape(rows, cols) for a in (w, g, m, v)])
    return [o.reshape(shape) for o in outs]


def _pack(arrs, n_lead, row_mult, width):
    parts, total = [], 0
    for n, a in enumerate(arrs):
        lead = a.shape[:n_lead]
        flat = a.reshape(lead + (-1,))
        size = flat.shape[-1]
        rows = -(-size // (width * row_mult)) * row_mult
        if n == len(arrs) - 1:
            rows += -(total + rows) % PACK_ROW_MULT
        total += rows
        if rows * width > size:
            flat = jnp.concatenate([flat, jnp.zeros(lead + (rows * width - size,), flat.dtype)], axis=n_lead)
        parts.append(flat.reshape(lead + (rows, width)))
    return jnp.concatenate(parts, axis=n_lead)


def _unpack(pack, shapes, n_lead, row_mult):
    outs, row = [], 0
    lead = pack.shape[:n_lead]
    width = pack.shape[-1]
    for shp in shapes:
        size = math.prod(shp)
        rows = -(-size // (width * row_mult)) * row_mult
        blk = lax.slice_in_dim(pack, row, row + rows, axis=n_lead)
        outs.append(blk.reshape(lead + (-1,))[..., :size].reshape(lead + tuple(shp)))
        row += rows
    return outs


def _to_words(a):
    return lax.bitcast_convert_type(a, BF16)


def _from_words(a):
    return lax.bitcast_convert_type(a, F32)


def _pad_axis(a, axis, size):
    pads = [(0, 0)] * a.ndim
    pads[axis] = (0, size - a.shape[axis])
    return jnp.pad(a, pads)


def _dense(name, s):
    if name.endswith("w_gate") or name.endswith("w_up"):
        _, nl, d, fs = s.shape
        return s.transpose(1, 2, 0, 3).reshape(nl, d, N_DEV * fs)
    if name.endswith("w_down"):
        _, nl, fs, d = s.shape
        return s.transpose(1, 0, 2, 3).reshape(nl, N_DEV * fs, d)
    if name == "pool_w":
        _, nl, ng, r, cg = s.shape
        return s.transpose(1, 2, 0, 3, 4).reshape(nl, ng, cg, cg)
    if name == "w_dkv":
        w = s.reshape(-1, s.shape[2])
        z = lambda n: jnp.zeros((w.shape[0], n), w.dtype)
        return jnp.concatenate([w[:, :KV_RANK], z(ROPE_LANE0), w[:, KV_RANK:],
                                z(HEAD_LANES - ROPE_LANE0 - QK_ROPE)], axis=1)
    if name in ("w_uk", "w_uv"):
        return _pad_axis(s.transpose(1, 0, 2), 2, HEAD_LANES).reshape(KV_RANK, N_HEADS * HEAD_LANES)
    if name == "w_dq":
        _, nl, ds, r = s.shape
        return s.transpose(1, 0, 2, 3).reshape(nl, N_DEV * ds, r)
    if name == "w_uq":
        nl = s.shape[1]
        return _pad_axis(s.transpose(1, 2, 0, 3), 3, HEAD_LANES).reshape(nl, Q_RANK, N_HEADS * HEAD_LANES)
    if name == "w_o":
        _, nl, k, dc = s.shape
        w = s.transpose(1, 2, 0, 3).reshape(nl, N_HEADS, V_HEAD, N_DEV * dc)
        return _pad_axis(w, 2, HEAD_LANES).reshape(nl, N_HEADS * HEAD_LANES, N_DEV * dc)
    if name in ("meta_tokens", "pool_scale"):
        r, dc = s.shape[1:]
        return s.transpose(1, 0, 2).reshape(r, N_DEV * dc)
    raise ValueError(name)


def _shards(name, g):
    if name.endswith("w_gate") or name.endswith("w_up"):
        nl, d, f = g.shape
        return g.reshape(nl, d, N_DEV, f // N_DEV).transpose(2, 0, 1, 3)
    if name.endswith("w_down"):
        nl, f, d = g.shape
        return g.reshape(nl, N_DEV, f // N_DEV, d).transpose(1, 0, 2, 3)
    if name == "pool_w":
        nl, ng, cg, _ = g.shape
        return g.reshape(nl, ng, N_DEV, cg // N_DEV, cg).transpose(2, 0, 1, 3, 4)
    if name == "w_dkv":
        w = jnp.concatenate([g[:, :KV_RANK], g[:, KV_RANK + ROPE_LANE0:KV_RANK + ROPE_LANE0 + QK_ROPE]], axis=1)
        return w.reshape(N_DEV, -1, KV_RANK + QK_ROPE)
    if name in ("w_uk", "w_uv"):
        return g.reshape(KV_RANK, N_HEADS, HEAD_LANES)[:, :, :V_HEAD].transpose(1, 0, 2)
    if name == "w_dq":
        nl, d, r = g.shape
        return g.reshape(nl, N_DEV, d // N_DEV, r).transpose(1, 0, 2, 3)
    if name == "w_uq":
        nl = g.shape[0]
        return g.reshape(nl, Q_RANK, N_HEADS, HEAD_LANES)[..., :QK_NOPE + QK_ROPE].transpose(2, 0, 1, 3)
    if name == "w_o":
        nl, _, d = g.shape
        w = g.reshape(nl, N_HEADS, HEAD_LANES, d)[:, :, :V_HEAD].reshape(nl, N_HEADS * V_HEAD, N_DEV, d // N_DEV)
        return w.transpose(2, 0, 1, 3)
    if name in ("meta_tokens", "pool_scale"):
        r, d = g.shape
        return g.reshape(r, N_DEV, d // N_DEV).transpose(1, 0, 2)
    raise ValueError(name)


def _rope_tables(L):
    pos = jnp.maximum(jnp.arange(L) - FRONT_PAD, 0).astype(F32)
    inv = 1.0 / (ROPE_THETA ** (jnp.arange(0, QK_ROPE, 2, dtype=F32) / QK_ROPE))
    ang = pos[:, None] * inv[None, :]
    cos, sin = jnp.cos(ang), jnp.sin(ang)
    half = QK_ROPE // 2
    z = lambda n: jnp.zeros((L, n), F32)
    tail = z(HEAD_LANES - ROPE_LANE0 - QK_ROPE)
    return {
        "cq": jnp.concatenate([jnp.ones((L, ROPE_LANE0), F32), cos, cos, tail], axis=1),
        "ck": jnp.concatenate([z(ROPE_LANE0), cos, cos, tail], axis=1),
        "s1": jnp.concatenate([z(ROPE_LANE0), -sin, z(half), tail], axis=1),
        "s2": jnp.concatenate([z(ROPE_LANE0), z(half), sin, tail], axis=1),
    }


def kernel(x, meta_tokens, ffn1_norm, ffn1_w_gate, ffn1_w_up, ffn1_w_down, mix_norm, ffn2_norm, ffn2_w_gate, ffn2_w_up, ffn2_w_down, pool_w, pool_scale, kv_in_norm, w_dkv, kv_latent_norm, w_uk, w_uv, w_dq, q_latent_norm, w_uq, w_o, final_norm, loss_target, m_meta_tokens, m_ffn1_norm, m_ffn1_w_gate, m_ffn1_w_up, m_ffn1_w_down, m_mix_norm, m_ffn2_norm, m_ffn2_w_gate, m_ffn2_w_up, m_ffn2_w_down, m_pool_w, m_pool_scale, m_kv_in_norm, m_w_dkv, m_kv_latent_norm, m_w_uk, m_w_uv, m_w_dq, m_q_latent_norm, m_w_uq, m_w_o, m_final_norm, v_meta_tokens, v_ffn1_norm, v_ffn1_w_gate, v_ffn1_w_up, v_ffn1_w_down, v_mix_norm, v_ffn2_norm, v_ffn2_w_gate, v_ffn2_w_up, v_ffn2_w_down, v_pool_w, v_pool_scale, v_kv_in_norm, v_w_dkv, v_kv_latent_norm, v_w_uk, v_w_uv, v_w_dq, v_q_latent_norm, v_w_uq, v_w_o, v_final_norm):
    args = dict(locals())
    W = {n: args[n] for n in WEIGHTS}
    M = {n: args["m_" + n] for n in WEIGHTS}
    V = {n: args["v_" + n] for n in WEIGHTS}
    seq, D = x.shape[1], x.shape[2]
    L = SEQ_START + seq

    fs = ffn1_w_down.shape[1]

    def weight_piece(l):
        parts = [(W[n][l] if n.endswith("w_down") else W[n][l].T).astype(BF16) for n in FFN_WEIGHTS]
        if l == 0:
            parts += [_to_words(W[n]) if n in SHARDED_F32 else W[n].astype(BF16) for n in SMALL_SHARDED]
        return _pack(parts, 0, 16, D)

    pieces = [weight_piece(l) for l in range(DEPTH)]
    full = [None] * DEPTH
    P = {}

    def take_piece(l, gathered):
        full[l] = _with_own(gathered, pieces[l])
        if l == 0:
            shapes = [(fs, D)] * len(FFN_WEIGHTS)
            shapes += [W[n].shape + ((2,) if n in SHARDED_F32 else ()) for n in SMALL_SHARDED]
            for n, s in zip(SMALL_SHARDED, _unpack(full[0], shapes, 1, 16)[len(FFN_WEIGHTS):]):
                P[n] = _dense(n, _from_words(s) if n in SHARDED_F32 else s)

    take_piece(0, _run_side(_gather_side(pieces[0]), "all_gather_first")[0])
    norm3 = lambda a: a.reshape(a.shape[0], 1, a.shape[-1])
    row = lambda a: a.reshape(1, -1)
    g_ffn, g_mix = {1: norm3(ffn1_norm), 2: norm3(ffn2_norm)}, mix_norm
    hosted_gather = {(0, 1): 1, (0, 2): 2, (1, 1): 3}
    ffn_entries = lambda which: tuple(range(3 * (which - 1), 3 * which))

    def ffn_forward(h, l, which):
        nxt = hosted_gather.get((l, which))
        side = None if nxt is None else _gather_side(pieces[nxt])
        outs = _ffn_fwd(h, g_ffn[which][l:l + 1], full[l], ffn_entries(which), fs, side)
        if nxt is not None:
            take_piece(nxt, outs[4])
        return outs[:4]

    h = jnp.concatenate([jnp.zeros((FRONT_PAD, D), F32), P["meta_tokens"], x[0]], axis=0)
    target = jnp.concatenate([jnp.zeros((SEQ_START, D), F32), loss_target[0]], axis=0)
    tabs = _rope_tables(L)
    saved = []
    kv = None
    for l in range(DEPTH):
        s = {"h1": h}
        h, s["xn1"], s["g1"], s["u1"] = ffn_forward(h, l, 1)
        s["hm"] = h
        if l < N_POOL_LAYERS:
            h = _pool_fwd(h, row(g_mix[l]), P["pool_w"][l], row(P["pool_scale"][l]), l)
        else:
            j = l - N_POOL_LAYERS
            s["q"], s["cqp"] = _q_fwd(h, tabs, row(g_mix[l]), P["w_dq"][j], row(q_latent_norm[j]), P["w_uq"][j])
            s["o"], s["lse"] = _attn_fwd(s["q"], kv["kn"], kv["kr"], kv["v"])
            h = _oproj_fwd(h, s["o"], P["w_o"][j])
        s["h2"] = h
        h, s["xn2"], s["g2"], s["u2"] = ffn_forward(h, l, 2)
        if l == N_POOL_LAYERS - 1:
            kv = {"h": h}
            kv["kn"], kv["kr"], kv["v"], kv["ckr"] = _kv_fwd(
                h, tabs, row(kv_in_norm), P["w_dkv"], row(kv_latent_norm), P["w_uk"], P["w_uv"])
        saved.append(s)
    dh, loss_row, d_final = _head(h, target, row(final_norm))
    loss = lax.psum(loss_row[0, 0], ("x", "y", "c"))

    G = {}
    stack = {n: [None] * DEPTH for n in ("ffn1_norm", "ffn1_w_gate", "ffn1_w_up", "ffn1_w_down", "mix_norm",
                                         "ffn2_norm", "ffn2_w_gate", "ffn2_w_up", "ffn2_w_down")}
    pool_dw, pool_ds = [None] * N_POOL_LAYERS, [None] * N_POOL_LAYERS
    mla = {n: [None] * (DEPTH - N_POOL_LAYERS) for n in ("w_dq", "w_uq", "w_o", "q_latent_norm")}
    dks, dvs = [], []
    my_core = lax.axis_index("c").astype(jnp.int32).reshape(1)
    my_chip = (2 * lax.axis_index("x") + lax.axis_index("y")).astype(jnp.int32).reshape(1)
    layer_grads = {n: [None] * DEPTH for n in FFN_WEIGHTS}
    grads = {}

    by_owner = lambda g: g.reshape((N_CHIPS, 2, g.shape[0] // N_DEV) + g.shape[1:])

    def reduce_cores(arrs, others):
        return _add_own(arrs, my_core, others, "sum_cores")

    def finish_layer(l, partials, from_chips):
        for n, g in zip(FFN_WEIGHTS, _add_own(partials, my_chip, from_chips, "sum_chips")):
            layer_grads[n][l] = g if n.endswith("w_down") else g.T

    def ffn_backward(dh, s, which, l, side=None):
        outs = _ffn_bwd(dh, s["h%d" % which], s["g%d" % which], s["u%d" % which], g_ffn[which][l:l + 1],
                        full[l], ffn_entries(which), fs, side)
        dh, dg, du, act, dob, dgam = outs[:6]
        xn = s["xn%d" % which]
        stack["ffn%d_w_gate" % which][l] = _mm_tn(dg, xn, "ffn_dw")
        stack["ffn%d_w_up" % which][l] = _mm_tn(du, xn, "ffn_dw")
        stack["ffn%d_w_down" % which][l] = _mm_tn(act, dob, "ffn_dw")
        stack["ffn%d_norm" % which][l] = dgam
        return dh, outs[6:]

    pending = None
    for l in reversed(range(DEPTH)):
        s = saved[l]
        if l == N_POOL_LAYERS - 1:
            dh, d_dkv, d_uk, d_uv, d_kvin, d_kvlat = _kv_bwd(
                dh, kv["h"], kv["ckr"], dks, dvs, tabs, row(kv_in_norm), P["w_dkv"], row(kv_latent_norm),
                P["w_uk"], P["w_uv"])
            G.update(w_dkv=d_dkv, w_uk=d_uk, w_uv=d_uv, kv_in_norm=d_kvin, kv_latent_norm=d_kvlat)
        if pending is None:
            dh, _ = ffn_backward(dh, s, 2, l)
        else:
            dh, from_chips = ffn_backward(dh, s, 2, l, _chip_side(pending[2]))
            finish_layer(pending[0], pending[1], from_chips)
        if l < N_POOL_LAYERS:
            dh, pool_dw[l], pool_ds[l], stack["mix_norm"][l] = _pool_bwd(
                dh, s["hm"], row(g_mix[l]), P["pool_w"][l], row(P["pool_scale"][l]))
        else:
            j = l - N_POOL_LAYERS
            do, delta_o, mla["w_o"][j] = _oproj_bwd(dh, s["o"], P["w_o"][j])
            dq, dk, dv = _attn_bwd(s["q"], kv["kn"], kv["kr"], kv["v"], do, s["lse"], delta_o)
            dks.append(dk)
            dvs.append(dv)
            dh, mla["w_dq"][j], mla["w_uq"][j], stack["mix_norm"][l], mla["q_latent_norm"][j] = _q_bwd(
                dh, s["hm"], s["cqp"], dq, tabs, row(g_mix[l]), P["w_dq"][j], row(q_latent_norm[j]), P["w_uq"][j])
        dh, _ = ffn_backward(dh, s, 1, l)
        arrs = [by_owner(stack[n][l]) for n in FFN_WEIGHTS]
        if l > 0:
            pending = (l,) + tuple(reduce_cores(arrs, _run_side(_sibling_side(arrs), "sibling_exchange")))
    grad_x = dh[SEQ_START:][None]
    for n in ("ffn1_norm", "mix_norm", "ffn2_norm"):
        G[n] = jnp.concatenate(stack[n], axis=0)
    G["pool_w"] = jnp.stack(pool_dw)
    G["pool_scale"] = jnp.concatenate(pool_ds, axis=0)
    G["w_dq"], G["w_uq"], G["w_o"] = (jnp.stack(mla[n]) for n in ("w_dq", "w_uq", "w_o"))
    G["q_latent_norm"] = jnp.concatenate(mla["q_latent_norm"], axis=0)
    G["meta_tokens"] = dh[FRONT_PAD:SEQ_START]
    G["final_norm"] = d_final

    spack = _pack([_shards(n, G[n]) for n in SMALL_SHARDED], 1, 8, D)
    spack = spack.reshape((N_CHIPS, 2) + spack.shape[1:])
    others = _run_side(_sibling_side(arrs + [spack]), "sibling_exchange_last")
    partials, partials_bf16 = reduce_cores(arrs, others[:-1])
    small, small_bf16 = reduce_cores([spack], others[-1:])
    from_chips = _run_side(_chip_side(list(partials_bf16) + list(small_bf16)), "chip_exchange_last")
    finish_layer(0, partials, from_chips[:-1])
    small_mine = _add_own(small, my_chip, from_chips[-1:], "sum_chips")[0]
    grads.update(zip(SMALL_SHARDED, _unpack(small_mine, [W[n].shape for n in SMALL_SHARDED], 0, 8)))
    for n in FFN_WEIGHTS:
        grads[n] = jnp.stack(layer_grads[n])
    rep_shapes = [W[n].shape for n in REPLICATED]
    rpack = _pack([G[n].reshape(W[n].shape) for n in REPLICATED], 0, 8, D)
    everyones = _with_own(_run_side(_gather_side(rpack), "all_gather_norm_grads")[0], rpack)
    grads.update(zip(REPLICATED, _unpack(_sum_lead(everyones, "sum_devices"), rep_shapes, 0, 8)))

    delta, new_m, new_v = {}, {}, {}
    for n in WEIGHTS:
        delta[n], new_m[n], new_v[n] = _adamw(W[n], grads[n], M[n], V[n])
    return (loss, grad_x, *[grads[n] for n in WEIGHTS], *[delta[n] for n in WEIGHTS],
            *[new_m[n] for n in WEIGHTS], *[new_v[n] for n in WEIGHTS])
```

```python
import functools
import math

import jax
import jax.numpy as jnp
from jax import lax
from jax.experimental import pallas as pl
from jax.experimental.pallas import tpu as pltpu

F32 = jnp.float32
BF16 = jnp.bfloat16
MESH = pl.DeviceIdType.MESH

N_DEV = 8
N_CHIPS = 4
DEPTH = 4
N_POOL_LAYERS = 2
N_HEADS = 8
QK_NOPE = 64
QK_ROPE = 32
V_HEAD = 64
KV_RANK = 256
Q_RANK = 384
HEAD_LANES = 128
ROPE_LANE0 = QK_NOPE
BIAS_LANE = QK_NOPE + QK_ROPE
ONES_LANE = V_HEAD
LOG2E = math.log2(math.e)
N_META = 16
CHUNK_SHIFT = 6
FRONT_PAD = 112
SEQ_START = FRONT_PAD + N_META
HALO = 16
POOL_WINDOWS = (2, 4, 8, 16)
EPS = 1e-6
ROPE_THETA = 10000.0
NEG = -1e30
PACK_ROW_MULT = 256
VMEM_LIMIT = 56 * 1024 * 1024

ADAM_LR = 0.001
ADAM_B1 = 0.9
ADAM_B2 = 0.999
ADAM_EPS = 1e-08
ADAM_WD = 0.01
ADAM_STEP = 10

SHARDED = ["ffn1_w_gate", "ffn1_w_up", "ffn1_w_down", "ffn2_w_gate", "ffn2_w_up", "ffn2_w_down",
           "pool_w", "w_dkv", "w_uk", "w_uv", "w_dq", "w_uq", "w_o", "meta_tokens", "pool_scale"]
FFN_WEIGHTS = SHARDED[:6]
SMALL_SHARDED = SHARDED[6:]
SHARDED_F32 = ("meta_tokens", "pool_scale")
REPLICATED = ["ffn1_norm", "mix_norm", "ffn2_norm", "kv_in_norm", "kv_latent_norm", "q_latent_norm",
              "final_norm"]
WEIGHTS = ['meta_tokens', 'ffn1_norm', 'ffn1_w_gate', 'ffn1_w_up', 'ffn1_w_down', 'mix_norm', 'ffn2_norm',
           'ffn2_w_gate', 'ffn2_w_up', 'ffn2_w_down', 'pool_w', 'pool_scale', 'kv_in_norm', 'w_dkv',
           'kv_latent_norm', 'w_uk', 'w_uv', 'w_dq', 'q_latent_norm', 'w_uq', 'w_o', 'final_norm']


def _dot(a, b):
    return jnp.dot(a.astype(BF16), b.astype(BF16), preferred_element_type=F32)


def _dot_nt(a, b):
    return lax.dot_general(a.astype(BF16), b.astype(BF16), (((1,), (1,)), ((), ())),
                           preferred_element_type=F32)


def _dot_tn(a, b):
    return lax.dot_general(a.astype(BF16), b.astype(BF16), (((0,), (0,)), ((), ())),
                           preferred_element_type=F32)


def _sigmoid(x):
    return 1.0 / (1.0 + jnp.exp(-x))


def _rms(x, g):
    r = lax.rsqrt(jnp.mean(x * x, axis=-1, keepdims=True) + EPS)
    xh = x * r
    return xh * g, xh, r


def _rms_bwd(dy, xh, r, g):
    dxh = dy * g
    dx = r * (dxh - xh * jnp.mean(dxh * xh, axis=-1, keepdims=True))
    return dx, jnp.sum(dy * xh, axis=0, keepdims=True)


def _rope(x, c, s1, s2):
    return x * c + pltpu.roll(x, HEAD_LANES - QK_ROPE // 2, 1) * s1 + pltpu.roll(x, QK_ROPE // 2, 1) * s2


def _rope_t(d, c, s1, s2):
    return d * c + pltpu.roll(d * s1, QK_ROPE // 2, 1) + pltpu.roll(d * s2, HEAD_LANES - QK_ROPE // 2, 1)


def _params(*sem):
    return pltpu.CompilerParams(dimension_semantics=sem, vmem_limit_bytes=VMEM_LIMIT)


def _pick(n, candidates):
    for c in candidates:
        if n % c == 0:
            return c
    return n


def _row_tile(L):
    return _pick(L, (640, 128))


FF_OWNERS = 4


def _row_call(name, body, L, row_ins, full_ins, row_outs, acc_outs):
    tm = _row_tile(L)
    n = L // tm
    hb = tm // HALO
    nb = L // HALO
    in_specs, args = [], []
    for arr, kind in row_ins:
        c = arr.shape[1]
        if kind == "tile":
            spec = pl.BlockSpec((tm, c), lambda i: (i, 0))
        elif kind == "prev":
            spec = pl.BlockSpec((HALO, c), lambda i: (jnp.maximum(i * hb - 1, 0), 0))
        else:
            spec = pl.BlockSpec((HALO, c), lambda i: (jnp.minimum((i + 1) * hb, nb - 1), 0))
        in_specs.append(spec)
        args.append(arr)
    for arr in full_ins:
        in_specs.append(pl.BlockSpec(arr.shape, lambda i, nd=arr.ndim: (0,) * nd))
        args.append(arr)
    out_shape = [jax.ShapeDtypeStruct((L, c), dt) for c, dt in row_outs]
    out_specs = [pl.BlockSpec((tm, c), lambda i: (i, 0)) for c, _ in row_outs]
    for shp in acc_outs:
        out_shape.append(jax.ShapeDtypeStruct(shp, F32))
        out_specs.append(pl.BlockSpec(shp, lambda i, nd=len(shp): (0,) * nd))
    n_in, n_ro = len(args), len(row_outs)

    def kern(*refs):
        i = pl.program_id(0)
        vals = [r[...] for r in refs[:n_in]]
        ro, ao = body(i, n, tm, *vals)
        for r, v in zip(refs[n_in:n_in + n_ro], ro):
            r[...] = v.astype(r.dtype)
        acc_refs = refs[n_in + n_ro:]

        @pl.when(i == 0)
        def _():
            for r in acc_refs:
                r[...] = jnp.zeros(r.shape, r.dtype)

        for r, v in zip(acc_refs, ao):
            r[...] += v

    return pl.pallas_call(kern, name=name, grid=(n,), in_specs=in_specs, out_specs=out_specs,
                          out_shape=out_shape, compiler_params=_params("arbitrary"))(*args)


class _Side:
    def __init__(self, ins, out_shapes, sems, start, finish):
        self.ins, self.out_shapes, self.sems, self.start, self.finish = ins, out_shapes, sems, start, finish


ANY = pl.BlockSpec(memory_space=pl.ANY)


def _hosted(kern, n_in, n_out, n_scratch, side, is_first, is_last):
    if side is None:
        return kern
    ns_in, ns_out = len(side.ins), len(side.out_shapes)

    def wrapped(*refs):
        ins, refs = refs[:n_in], refs[n_in:]
        side_ins, refs = refs[:ns_in], refs[ns_in:]
        outs, refs = refs[:n_out], refs[n_out:]
        side_outs, refs = refs[:ns_out], refs[ns_out:]
        scratch, side_sems = refs[:n_scratch], refs[n_scratch:]

        @pl.when(is_first())
        def _():
            side.start(side_ins, side_outs, side_sems)

        kern(*ins, *outs, *scratch)

        @pl.when(is_last())
        def _():
            side.finish(side_ins, side_outs, side_sems)

    return wrapped


def _side_args(side):
    if side is None:
        return [], [], [], [], []
    return ([ANY] * len(side.ins), [ANY] * len(side.out_shapes), list(side.out_shapes), list(side.sems),
            list(side.ins))


def _ffn_weight_specs(fs, D, ents):
    return [pl.BlockSpec((FF_OWNERS, fs, D), lambda i, f, e=e: (f, e, 0)) for e in ents]


def _ffn_fwd(h, gam, wpiece, ents, fs, side=None):
    L, D = h.shape
    F = N_DEV * fs
    tm = _row_tile(L)
    tf = FF_OWNERS * fs
    nL, nF = L // tm, F // tf
    s_in, s_out, s_shape, s_sems, s_args = _side_args(side)

    def kern(h_ref, gam_ref, wg_ref, wu_ref, wd_ref, ho_ref, xn_ref, gs_ref, us_ref, acc):
        f = pl.program_id(1)

        @pl.when(f == 0)
        def _():
            xn, _, _ = _rms(h_ref[...], gam_ref[...])
            xn_ref[...] = xn.astype(BF16)
            acc[...] = jnp.zeros(acc.shape, F32)

        xnb = xn_ref[...]
        g = _dot_nt(xnb, wg_ref[...].reshape(tf, D))
        u = _dot_nt(xnb, wu_ref[...].reshape(tf, D))
        gs_ref[...] = g.astype(BF16)
        us_ref[...] = u.astype(BF16)
        acc[...] += _dot(g * _sigmoid(g) * u, wd_ref[...].reshape(tf, D))

        @pl.when(f == nF - 1)
        def _():
            ho_ref[...] = h_ref[...] + 0.5 * acc[...]

    first = lambda: (pl.program_id(0) == 0) & (pl.program_id(1) == 0)
    last = lambda: (pl.program_id(0) == nL - 1) & (pl.program_id(1) == nF - 1)
    return pl.pallas_call(
        _hosted(kern, 5, 4, 1, side, first, last),
        name="ffn_fwd" if side is None else "ffn_fwd_hosting", grid=(nL, nF),
        in_specs=[pl.BlockSpec((tm, D), lambda i, f: (i, 0)),
                  pl.BlockSpec((None, 1, D), lambda i, f: (0, 0, 0))] + _ffn_weight_specs(fs, D, ents) + s_in,
        out_specs=[pl.BlockSpec((tm, D), lambda i, f: (i, 0)),
                   pl.BlockSpec((tm, D), lambda i, f: (i, 0)),
                   pl.BlockSpec((tm, tf), lambda i, f: (i, f)),
                   pl.BlockSpec((tm, tf), lambda i, f: (i, f))] + s_out,
        out_shape=[jax.ShapeDtypeStruct((L, D), F32), jax.ShapeDtypeStruct((L, D), BF16),
                   jax.ShapeDtypeStruct((L, F), BF16), jax.ShapeDtypeStruct((L, F), BF16)] + s_shape,
        scratch_shapes=[pltpu.VMEM((tm, D), F32)] + s_sems,
        compiler_params=_params("arbitrary", "arbitrary"))(h, gam, wpiece, wpiece, wpiece, *s_args)


def _ffn_bwd(dh, h, gs, us, gam, wpiece, ents, fs, side=None):
    L, D = h.shape
    F = N_DEV * fs
    tm = _pick(L, (416, 128))
    tf = FF_OWNERS * fs
    nL, nF = L // tm, F // tf
    s_in, s_out, s_shape, s_sems, s_args = _side_args(side)

    def kern(dh_ref, h_ref, gs_ref, us_ref, gam_ref, wg_ref, wu_ref, wd_ref,
             dhi_ref, dg_ref, du_ref, a_ref, dob_ref, dgam_ref, dxn):
        i = pl.program_id(0)
        f = pl.program_id(1)

        @pl.when(f == 0)
        def _():
            dxn[...] = jnp.zeros(dxn.shape, F32)
            dob_ref[...] = (0.5 * dh_ref[...]).astype(BF16)

        @pl.when((f == 0) & (i == 0))
        def _():
            dgam_ref[...] = jnp.zeros(dgam_ref.shape, F32)

        g = gs_ref[...].astype(F32)
        u = us_ref[...].astype(F32)
        sg = _sigmoid(g)
        silu = g * sg
        da = _dot_nt(dob_ref[...], wd_ref[...].reshape(tf, D))
        a_ref[...] = (silu * u).astype(BF16)
        dgt = (da * u * (sg * (1.0 + g * (1.0 - sg)))).astype(BF16)
        dut = (da * silu).astype(BF16)
        dg_ref[...] = dgt
        du_ref[...] = dut
        dxn[...] += _dot(dgt, wg_ref[...].reshape(tf, D)) + _dot(dut, wu_ref[...].reshape(tf, D))

        @pl.when(f == nF - 1)
        def _():
            gamma = gam_ref[...]
            _, xh, r = _rms(h_ref[...], gamma)
            dx, dgam = _rms_bwd(dxn[...], xh, r, gamma)
            dhi_ref[...] = dh_ref[...] + dx
            dgam_ref[...] += dgam

    first = lambda: (pl.program_id(0) == 0) & (pl.program_id(1) == 0)
    last = lambda: (pl.program_id(0) == nL - 1) & (pl.program_id(1) == nF - 1)
    return pl.pallas_call(
        _hosted(kern, 8, 6, 1, side, first, last),
        name="ffn_bwd" if side is None else "ffn_bwd_hosting", grid=(nL, nF),
        in_specs=[pl.BlockSpec((tm, D), lambda i, f: (i, 0)),
                  pl.BlockSpec((tm, D), lambda i, f: (i, 0)),
                  pl.BlockSpec((tm, tf), lambda i, f: (i, f)),
                  pl.BlockSpec((tm, tf), lambda i, f: (i, f)),
                  pl.BlockSpec((None, 1, D), lambda i, f: (0, 0, 0))] + _ffn_weight_specs(fs, D, ents) + s_in,
        out_specs=[pl.BlockSpec((tm, D), lambda i, f: (i, 0)),
                   pl.BlockSpec((tm, tf), lambda i, f: (i, f)),
                   pl.BlockSpec((tm, tf), lambda i, f: (i, f)),
                   pl.BlockSpec((tm, tf), lambda i, f: (i, f)),
                   pl.BlockSpec((tm, D), lambda i, f: (i, 0)),
                   pl.BlockSpec((1, D), lambda i, f: (0, 0))] + s_out,
        out_shape=[jax.ShapeDtypeStruct((L, D), F32), jax.ShapeDtypeStruct((L, F), BF16),
                   jax.ShapeDtypeStruct((L, F), BF16), jax.ShapeDtypeStruct((L, F), BF16),
                   jax.ShapeDtypeStruct((L, D), BF16), jax.ShapeDtypeStruct((1, D), F32)] + s_shape,
        scratch_shapes=[pltpu.VMEM((tm, D), F32)] + s_sems,
        compiler_params=_params("arbitrary", "arbitrary"))(dh, h, gs, us, gam, wpiece, wpiece, wpiece, *s_args)


def _mm_tn(a, b, name):
    L, M = a.shape
    N = b.shape[1]
    tm = _pick(M, (1408, 1024, 512))
    tn = _pick(N, (1408, 1024, 512))
    tk = _pick(L, (2080, 640, 128))

    def kern(a_ref, b_ref, o_ref):
        @pl.when(pl.program_id(2) == 0)
        def _():
            o_ref[...] = jnp.zeros(o_ref.shape, F32)

        o_ref[...] += _dot_tn(a_ref[...], b_ref[...])

    return pl.pallas_call(
        kern, name=name, grid=(M // tm, N // tn, L // tk),
        in_specs=[pl.BlockSpec((tk, tm), lambda i, j, k: (k, i)),
                  pl.BlockSpec((tk, tn), lambda i, j, k: (k, j))],
        out_specs=pl.BlockSpec((tm, tn), lambda i, j, k: (i, j)),
        out_shape=jax.ShapeDtypeStruct((M, N), F32),
        compiler_params=_params("arbitrary", "arbitrary", "arbitrary"))(a, b)


def _pool_counts(pos, w):
    return jnp.clip(pos - (FRONT_PAD - 1), 1, w).astype(F32)


def _pool_forward_values(i, tm, h, hprev, gamma, D):
    cg = D // len(POOL_WINDOWS)
    hext = jnp.concatenate([hprev, h], axis=0)
    uext, xh, r = _rms(hext, gamma)
    pos = i * tm + lax.broadcasted_iota(jnp.int32, (tm, 1), 0)
    pooled = []
    for gi, w in enumerate(POOL_WINDOWS):
        s = uext[:, gi * cg:(gi + 1) * cg]
        span = 1
        while span < w:
            s = s + pltpu.roll(s, span, 0)
            span *= 2
        s = s[HALO:]
        pooled.append(s / _pool_counts(pos, w) - uext[HALO:, gi * cg:(gi + 1) * cg])
    return uext, xh[HALO:], r[HALO:], pooled


def _pool_fwd(h, gam, w, scale, l):
    L, D = h.shape
    cg = D // len(POOL_WINDOWS)

    def body(i, n, tm, ht, hprev, gamma, wv, sc):
        _, _, _, pooled = _pool_forward_values(i, tm, ht, hprev, gamma, D)
        ys = [_dot(pooled[gi], wv[gi]) for gi in range(len(POOL_WINDOWS))]
        y = jnp.concatenate(ys, axis=1) * sc
        return [ht + y], []

    del cg
    return _row_call("pool_fwd", body, L, [(h, "tile"), (h, "prev")], [gam, w, scale], [(D, F32)], [])[0]


def _pool_bwd(dy, h, gam, w, scale):
    L, D = h.shape
    ng = len(POOL_WINDOWS)
    cg = D // ng

    def body(i, n, tm, ht, hprev, dyt, dynext, gamma, wv, sc):
        _, xh, r, pooled = _pool_forward_values(i, tm, ht, hprev, gamma, D)
        dynext = jnp.where(i == n - 1, jnp.zeros_like(dynext), dynext)
        dyext = jnp.concatenate([dyt, dynext], axis=0) * sc
        pos_ext = i * tm + lax.broadcasted_iota(jnp.int32, (tm + HALO, 1), 0)
        dws, dscs, dus = [], [], []
        for gi, wd in enumerate(POOL_WINDOWS):
            cols = slice(gi * cg, (gi + 1) * cg)
            pb = pooled[gi].astype(BF16)
            ypre = _dot(pb, wv[gi])
            dscs.append(jnp.sum(dyt[:, cols] * ypre, axis=0, keepdims=True))
            dws.append(_dot_tn(pb, dyext[:tm, cols])[None])
            dp = _dot_nt(dyext[:, cols], wv[gi])
            s = dp / _pool_counts(pos_ext, wd)
            span = 1
            while span < wd:
                s = s + pltpu.roll(s, tm + HALO - span, 0)
                span *= 2
            dus.append(s[:tm] - dp[:tm])
        du = jnp.concatenate(dus, axis=1)
        pos = pos_ext[:tm]
        du = jnp.where(pos >= FRONT_PAD, du, 0.0)
        dx, dgam = _rms_bwd(du, xh, r, gamma)
        return [dyt + dx], [jnp.concatenate(dws, axis=0), jnp.concatenate(dscs, axis=1), dgam]

    return _row_call("pool_bwd", body, L, [(h, "tile"), (h, "prev"), (dy, "tile"), (dy, "next")],
                     [gam, w, scale], [(D, F32)], [(ng, cg, cg), (1, D), (1, D)])


def _kv_fwd(h, tabs, g1, wdkv, g2, wuk, wuv):
    L, D = h.shape
    hw = N_HEADS * HEAD_LANES

    def body(i, n, tm, ht, ck, s1, s2, g1v, wdkv_v, g2v, wuk_v, wuv_v):
        xkv, _, _ = _rms(ht, g1v)
        ckr = _dot(xkv, wdkv_v)
        ckv, _, _ = _rms(ckr[:, :KV_RANK], g2v)
        krope = _rope(ckr[:, KV_RANK:], ck, s1, s2)
        pos = i * tm + lax.broadcasted_iota(jnp.int32, (tm, HEAD_LANES), 0)
        lane = lax.broadcasted_iota(jnp.int32, (tm, HEAD_LANES), 1)
        krope = jnp.where((pos < FRONT_PAD) & (lane == BIAS_LANE), NEG, krope)
        ones = ((lax.broadcasted_iota(jnp.int32, (1, hw), 1) & (HEAD_LANES - 1)) == ONES_LANE).astype(F32)
        return [_dot(ckv, wuk_v), krope, _dot(ckv, wuv_v) + ones, ckr], []

    ck, s1, s2 = tabs["ck"], tabs["s1"], tabs["s2"]
    return _row_call("kv_fwd", body, L, [(h, "tile"), (ck, "tile"), (s1, "tile"), (s2, "tile")],
                     [g1, wdkv, g2, wuk, wuv],
                     [(hw, BF16), (HEAD_LANES, BF16), (hw, BF16), (KV_RANK + HEAD_LANES, F32)], [])


def _kv_bwd(dh, h, ckr, dks, dvs, tabs, g1, wdkv, g2, wuk, wuv):
    L, D = h.shape
    hw = N_HEADS * HEAD_LANES
    nl = len(dks)

    def body(i, n, tm, *vals):
        dht, ht, ckr_t = vals[:3]
        dk = sum(vals[3:3 + nl][1:], vals[3])
        dv = sum(vals[3 + nl:3 + 2 * nl][1:], vals[3 + nl])
        ck, s1, s2, g1v, wdkv_v, g2v, wuk_v, wuv_v = vals[3 + 2 * nl:]
        xkv, xh1, r1 = _rms(ht, g1v)
        ckv, xh2, r2 = _rms(ckr_t[:, :KV_RANK], g2v)
        dckv = _dot_nt(dk, wuk_v) + _dot_nt(dv, wuv_v)
        dlat, dg2 = _rms_bwd(dckv, xh2, r2, g2v)
        dkr = dk[:, :HEAD_LANES]
        for hd in range(1, N_HEADS):
            dkr = dkr + dk[:, hd * HEAD_LANES:(hd + 1) * HEAD_LANES]
        dckr = jnp.concatenate([dlat, _rope_t(dkr, ck, s1, s2)], axis=1)
        dx, dg1 = _rms_bwd(_dot_nt(dckr, wdkv_v), xh1, r1, g1v)
        return [dht + dx], [_dot_tn(xkv, dckr), _dot_tn(ckv, dk), _dot_tn(ckv, dv), dg1, dg2]

    row_ins = [(dh, "tile"), (h, "tile"), (ckr, "tile")] + [(a, "tile") for a in dks + dvs]
    row_ins += [(tabs[k], "tile") for k in ("ck", "s1", "s2")]
    return _row_call("kv_bwd", body, L, row_ins, [g1, wdkv, g2, wuk, wuv], [(D, F32)],
                     [(D, KV_RANK + HEAD_LANES), (KV_RANK, hw), (KV_RANK, hw), (1, D), (1, KV_RANK)])


def _q_fwd(h, tabs, g, wdq, gq, wuq):
    L, D = h.shape
    hw = N_HEADS * HEAD_LANES

    def body(i, n, tm, ht, cq_t, s1, s2, gv, wdq_v, gqv, wuq_v):
        u, _, _ = _rms(ht, gv)
        cqp = _dot(u, wdq_v)
        cq, _, _ = _rms(cqp, gqv)
        qp = _dot(cq, wuq_v)
        bias = (lax.broadcasted_iota(jnp.int32, (1, HEAD_LANES), 1) == BIAS_LANE).astype(F32)
        q = [_rope(qp[:, hd * HEAD_LANES:(hd + 1) * HEAD_LANES], cq_t, s1, s2) * (SM_SCALE * LOG2E) + bias
             for hd in range(N_HEADS)]
        return [jnp.concatenate(q, axis=1), cqp], []

    return _row_call("q_fwd", body, L, [(h, "tile")] + [(tabs[k], "tile") for k in ("cq", "s1", "s2")],
                     [g, wdq, gq, wuq], [(hw, BF16), (Q_RANK, F32)], [])


def _q_bwd(dh, h, cqp, dq, tabs, g, wdq, gq, wuq):
    L, D = h.shape
    hw = N_HEADS * HEAD_LANES

    def body(i, n, tm, dht, ht, cqp_t, dq_t, cq_t, s1, s2, gv, wdq_v, gqv, wuq_v):
        u, xh1, r1 = _rms(ht, gv)
        cq, xh2, r2 = _rms(cqp_t, gqv)
        dqp = jnp.concatenate([_rope_t(dq_t[:, hd * HEAD_LANES:(hd + 1) * HEAD_LANES], cq_t, s1, s2)
                               for hd in range(N_HEADS)], axis=1)
        dcqp, dgq = _rms_bwd(_dot_nt(dqp, wuq_v), xh2, r2, gqv)
        dx, dg = _rms_bwd(_dot_nt(dcqp, wdq_v), xh1, r1, gv)
        return [dht + dx], [_dot_tn(u, dcqp), _dot_tn(cq, dqp), dg, dgq]

    row_ins = [(dh, "tile"), (h, "tile"), (cqp, "tile"), (dq, "tile")]
    row_ins += [(tabs[k], "tile") for k in ("cq", "s1", "s2")]
    return _row_call("q_bwd", body, L, row_ins, [g, wdq, gq, wuq], [(D, F32)],
                     [(D, Q_RANK), (Q_RANK, hw), (1, D), (1, Q_RANK)])


def _oproj_fwd(h, o, wo):
    L, D = h.shape

    def body(i, n, tm, ht, ot, wov):
        return [ht + _dot(ot, wov)], []

    return _row_call("oproj_fwd", body, L, [(h, "tile"), (o, "tile")], [wo], [(D, F32)], [])[0]


def _oproj_bwd(dh, o, wo):
    L, D = dh.shape
    hw = N_HEADS * HEAD_LANES

    def body(i, n, tm, dht, ot, wov):
        do = _dot_nt(dht, wov)
        prod = do * ot.astype(F32)
        delta = [jnp.broadcast_to(jnp.sum(prod[:, hd * HEAD_LANES:(hd + 1) * HEAD_LANES], axis=-1, keepdims=True),
                                  (tm, HEAD_LANES)) for hd in range(N_HEADS)]
        return [do, jnp.concatenate(delta, axis=1)], [_dot_tn(ot, dht)]

    return _row_call("oproj_bwd", body, L, [(dh, "tile"), (o, "tile")], [wo], [(hw, BF16), (hw, F32)], [(hw, D)])


def _causal(t):
    qpos = lax.broadcasted_iota(jnp.int32, (t, t), 0)
    kpos = lax.broadcasted_iota(jnp.int32, (t, t), 1)
    return (kpos >> CHUNK_SHIFT) <= (qpos >> CHUNK_SHIFT)


SM_SCALE = 1.0 / math.sqrt(QK_NOPE + QK_ROPE)


def _pairs(n, key_major):
    if key_major:
        order = [(i, j) for j in range(n) for i in range(j, n)]
    else:
        order = [(i, j) for i in range(n) for j in range(i + 1)]
    return (jnp.array([p[0] for p in order], jnp.int32), jnp.array([p[1] for p in order], jnp.int32))


def _attn_fwd(q, kn, kr, v):
    L = q.shape[0]
    hw = N_HEADS * HEAD_LANES
    t = _row_tile(L)
    it, jt = _pairs(L // t, key_major=False)

    def kern(it_ref, jt_ref, q_ref, kn_ref, kr_ref, v_ref, o_ref, lse_ref, m_s, acc_s):
        step = pl.program_id(1)
        i, j = it_ref[step], jt_ref[step]

        @pl.when(j == 0)
        def _():
            m_s[...] = jnp.full(m_s.shape, NEG, F32)
            acc_s[...] = jnp.zeros(acc_s.shape, F32)

        def update(diagonal):
            k = kn_ref[...] + kr_ref[...]
            s = _dot_nt(q_ref[...], k)
            if diagonal:
                s = jnp.where(_causal(t), s, NEG)
            m_prev = m_s[:, :1]
            m_new = jnp.maximum(m_prev, jnp.max(s, axis=-1, keepdims=True))
            p = jnp.exp2(s - m_new)
            acc_s[...] = jnp.exp2(m_prev - m_new) * acc_s[...] + _dot(p, v_ref[...])
            m_s[...] = jnp.broadcast_to(m_new, m_s.shape)

        @pl.when(j < i)
        def _():
            update(False)

        @pl.when(j == i)
        def _():
            update(True)
            acc = acc_s[...]
            total = acc[:, ONES_LANE:ONES_LANE + 1]
            o_ref[...] = (acc / total).astype(BF16)
            lse_ref[...] = m_s[...] + jnp.log2(jnp.broadcast_to(total, m_s.shape))

    qmap = lambda h, s, it, jt: (it[s], h)
    kmap = lambda h, s, it, jt: (jt[s], h)
    blk = (t, HEAD_LANES)
    return pl.pallas_call(
        kern, name="attn_fwd",
        grid_spec=pltpu.PrefetchScalarGridSpec(
            num_scalar_prefetch=2, grid=(N_HEADS, it.shape[0]),
            in_specs=[pl.BlockSpec(blk, qmap), pl.BlockSpec(blk, kmap),
                      pl.BlockSpec(blk, lambda h, s, it, jt: (jt[s], 0)), pl.BlockSpec(blk, kmap)],
            out_specs=[pl.BlockSpec(blk, qmap), pl.BlockSpec(blk, qmap)],
            scratch_shapes=[pltpu.VMEM(blk, F32)] * 2),
        out_shape=[jax.ShapeDtypeStruct((L, hw), BF16), jax.ShapeDtypeStruct((L, hw), F32)],
        compiler_params=_params("arbitrary", "arbitrary"))(it, jt, q, kn, kr, v)


def _attn_bwd(q, kn, kr, v, do, lse, delta):
    L = q.shape[0]
    hw = N_HEADS * HEAD_LANES
    t = _row_tile(L)
    it, jt = _pairs(L // t, key_major=True)

    def kern(it_ref, jt_ref, q_ref, kn_ref, kr_ref, v_ref, do_ref, lse_ref, dl_ref, dq_ref, dk_ref, dv_ref):
        step = pl.program_id(1)
        i, j = it_ref[step], jt_ref[step]

        @pl.when(step == 0)
        def _():
            dq_ref[...] = jnp.zeros(dq_ref.shape, F32)

        @pl.when(i == j)
        def _():
            dk_ref[...] = jnp.zeros(dk_ref.shape, F32)
            dv_ref[...] = jnp.zeros(dv_ref.shape, F32)

        def update(diagonal):
            k = kn_ref[...] + kr_ref[...]
            qv, dov = q_ref[...], do_ref[...]
            s = _dot_nt(qv, k)
            if diagonal:
                s = jnp.where(_causal(t), s, NEG)
            p = jnp.exp2(s - lse_ref[:, :1])
            dp = _dot_nt(dov, v_ref[...])
            dz = (p * (dp - dl_ref[:, :1])).astype(BF16)
            dv_ref[...] += _dot_tn(p, dov)
            dk_ref[...] += _dot_tn(dz, qv) * (1.0 / LOG2E)
            rows = pl.ds(pl.multiple_of(i * t, t), t)
            dq_ref[rows, :] += _dot(dz, k) * SM_SCALE

        @pl.when(j < i)
        def _():
            update(False)

        @pl.when(j == i)
        def _():
            update(True)

    qmap = lambda h, s, it, jt: (it[s], h)
    kmap = lambda h, s, it, jt: (jt[s], h)
    blk = (t, HEAD_LANES)
    return pl.pallas_call(
        kern, name="attn_bwd",
        grid_spec=pltpu.PrefetchScalarGridSpec(
            num_scalar_prefetch=2, grid=(N_HEADS, it.shape[0]),
            in_specs=[pl.BlockSpec(blk, qmap), pl.BlockSpec(blk, kmap),
                      pl.BlockSpec(blk, lambda h, s, it, jt: (jt[s], 0)), pl.BlockSpec(blk, kmap),
                      pl.BlockSpec(blk, qmap), pl.BlockSpec(blk, qmap), pl.BlockSpec(blk, qmap)],
            out_specs=[pl.BlockSpec((L, HEAD_LANES), lambda h, s, it, jt: (0, h)),
                       pl.BlockSpec(blk, kmap), pl.BlockSpec(blk, kmap)]),
        out_shape=[jax.ShapeDtypeStruct((L, hw), F32)] * 3,
        compiler_params=_params("arbitrary", "arbitrary"))(it, jt, q, kn, kr, v, do, lse, delta)


def _head(h, target, g):
    L, D = h.shape

    def body(i, n, tm, ht, tt, gv):
        y, xh, r = _rms(ht, gv)
        pos = i * tm + lax.broadcasted_iota(jnp.int32, (tm, 1), 0)
        e = jnp.where(pos >= SEQ_START, y - tt, 0.0)
        loss = 0.5 * jnp.sum(jnp.mean(e * e, axis=-1, keepdims=True), axis=0, keepdims=True)
        dx, dg = _rms_bwd(e / D, xh, r, gv)
        return [dx], [jnp.broadcast_to(loss, (1, 128)), dg]

    return _row_call("loss_head", body, L, [(h, "tile"), (target, "tile")], [g], [(D, F32)], [(1, 128), (1, D)])


def _coords():
    return lax.axis_index("x"), lax.axis_index("y"), lax.axis_index("c")


def _my_index():
    mx, my, mc = _coords()
    return 4 * mx + 2 * my + mc


def _gather_side(x):
    R, W = x.shape

    def copies(x_ref, out_ref, send_sems, recv_sems):
        mx, my, mc = _coords()
        me, sibling = (mx, my, mc), (mx, my, 1 - mc)
        chips = [(1 - mx, my), (mx, 1 - my), (1 - mx, 1 - my)]

        def slot(px, py, pc):
            return out_ref.at[4 * px + 2 * py + pc]

        def copy(k, block, to, src=None):
            return pltpu.make_async_remote_copy(
                src_ref=slot(*block) if src is None else src, dst_ref=slot(*block),
                send_sem=send_sems.at[k], recv_sem=recv_sems.at[k], device_id=to, device_id_type=MESH)

        first = [copy(0, me, sibling, src=x_ref)]
        first += [copy(1 + n, me, (*chip, mc), src=x_ref) for n, chip in enumerate(chips)]
        passed = [copy(4 + n, (*chip, mc), sibling) for n, chip in enumerate(chips)]
        landed = [copy(1 + n, (*chip, mc), me) for n, chip in enumerate(chips)]
        from_sibling = [copy(0, sibling, me)] + [copy(4 + n, (*chip, 1 - mc), me) for n, chip in enumerate(chips)]
        return first, passed, landed, from_sibling

    def start(ins, outs, sems):
        for cp in copies(ins[0], outs[0], *sems)[0]:
            cp.start()

    def finish(ins, outs, sems):
        first, passed, landed, from_sibling = copies(ins[0], outs[0], *sems)
        for arrived, onward in zip(landed, passed):
            arrived.wait_recv()
            onward.start()
        for cp in from_sibling:
            cp.wait_recv()
        for cp in first + passed:
            cp.wait_send()

    return _Side([x], [jax.ShapeDtypeStruct((N_DEV, R, W), x.dtype)],
                 [pltpu.SemaphoreType.DMA((7,)), pltpu.SemaphoreType.DMA((7,))], start, finish)


def _with_own(gathered, x):
    return lax.dynamic_update_slice(gathered, x[None], (_my_index(), 0, 0))


def _chip_side(parts):
    n_arr = len(parts)

    def copies(p_refs, out_refs, send_sems, recv_sems):
        mx, my, mc = _coords()
        chips = [(1 - mx, my), (mx, 1 - my), (1 - mx, 1 - my)]
        return [pltpu.make_async_remote_copy(
            src_ref=p_ref.at[2 * cx + cy], dst_ref=out_ref.at[n], send_sem=send_sems.at[3 * a + n],
            recv_sem=recv_sems.at[3 * a + n], device_id=(cx, cy, mc), device_id_type=MESH)
            for a, (p_ref, out_ref) in enumerate(zip(p_refs, out_refs)) for n, (cx, cy) in enumerate(chips)]

    def start(ins, outs, sems):
        for cp in copies(ins, outs, *sems):
            cp.start()

    def finish(ins, outs, sems):
        cps = copies(ins, outs, *sems)
        for cp in cps:
            cp.wait_recv()
        for cp in cps:
            cp.wait_send()

    return _Side(list(parts), [jax.ShapeDtypeStruct((3,) + p.shape[1:], p.dtype) for p in parts],
                 [pltpu.SemaphoreType.DMA((3 * n_arr,)), pltpu.SemaphoreType.DMA((3 * n_arr,))], start, finish)


def _run_side(side, name):
    def kern(*refs):
        n_in, n_out = len(side.ins), len(side.out_shapes)
        ins, outs, sems = refs[:n_in], refs[n_in:n_in + n_out], refs[n_in + n_out:]
        side.start(ins, outs, sems)
        side.finish(ins, outs, sems)

    return pl.pallas_call(kern, name=name, in_specs=[ANY] * len(side.ins), out_specs=[ANY] * len(side.out_shapes),
                          out_shape=list(side.out_shapes), scratch_shapes=list(side.sems))(*side.ins)


def _sibling_side(arrs):
    n_arr = len(arrs)

    def copies(g_refs, out_refs, send_sems, recv_sems):
        mx, my, mc = _coords()
        return [pltpu.make_async_remote_copy(
            src_ref=g_ref.at[n, 1 - mc], dst_ref=out_ref.at[n], send_sem=send_sems.at[N_CHIPS * a + n],
            recv_sem=recv_sems.at[N_CHIPS * a + n], device_id=(mx, my, 1 - mc), device_id_type=MESH)
            for a, (g_ref, out_ref) in enumerate(zip(g_refs, out_refs)) for n in range(N_CHIPS)]

    def start(ins, outs, sems):
        for cp in copies(ins, outs, *sems):
            cp.start()

    def finish(ins, outs, sems):
        cps = copies(ins, outs, *sems)
        for cp in cps:
            cp.wait_recv()
        for cp in cps:
            cp.wait_send()

    return _Side(list(arrs), [jax.ShapeDtypeStruct((N_CHIPS,) + g.shape[2:], g.dtype) for g in arrs],
                 [pltpu.SemaphoreType.DMA((N_CHIPS * n_arr,)), pltpu.SemaphoreType.DMA((N_CHIPS * n_arr,))],
                 start, finish)


def _add_own(owns, sel, others, name):
    n_arr = len(owns)
    R, W = owns[0].shape[-2:]
    tr = _pick(R, (PACK_ROW_MULT, 176, 64, 8))
    first_phase = owns[0].ndim == 4
    if first_phase:
        n = owns[0].shape[0]
        grid = (n, R // tr)
        in_specs = ([pl.BlockSpec((None, None, tr, W), lambda b, i, sel: (b, sel[0], i, 0))] * n_arr
                    + [pl.BlockSpec((None, tr, W), lambda b, i, sel: (b, i, 0))] * n_arr)
        out_specs = [pl.BlockSpec((None, tr, W), lambda b, i, sel: (b, i, 0))] * (2 * n_arr)
        out_shape = [jax.ShapeDtypeStruct((n, R, W), F32)] * n_arr + [jax.ShapeDtypeStruct((n, R, W), BF16)] * n_arr

        def kern(sel_ref, *refs):
            for a in range(n_arr):
                acc = refs[a][...] + refs[n_arr + a][...]
                refs[2 * n_arr + a][...] = acc
                refs[3 * n_arr + a][...] = acc.astype(BF16)
    else:
        k = others[0].shape[0]
        grid = (1, R // tr)
        in_specs = ([pl.BlockSpec((None, tr, W), lambda b, i, sel: (sel[0], i, 0))] * n_arr
                    + [pl.BlockSpec((k, tr, W), lambda b, i, sel: (0, i, 0))] * n_arr)
        out_specs = [pl.BlockSpec((tr, W), lambda b, i, sel: (i, 0))] * n_arr
        out_shape = [jax.ShapeDtypeStruct((R, W), F32)] * n_arr

        def kern(sel_ref, *refs):
            for a in range(n_arr):
                acc = refs[a][...]
                for m in range(k):
                    acc = acc + refs[n_arr + a][m].astype(F32)
                refs[2 * n_arr + a][...] = acc

    outs = pl.pallas_call(
        kern, name=name,
        grid_spec=pltpu.PrefetchScalarGridSpec(num_scalar_prefetch=1, grid=grid, in_specs=in_specs,
                                               out_specs=out_specs),
        out_shape=out_shape, compiler_params=_params("arbitrary", "arbitrary"))(sel, *owns, *others)
    return (outs[:n_arr], outs[n_arr:]) if first_phase else outs


def _sum_lead(x, name):
    n, R, W = x.shape
    tr = _pick(R, (PACK_ROW_MULT, 8))

    def kern(x_ref, o_ref):
        acc = x_ref[0]
        for k in range(1, n):
            acc = acc + x_ref[k]
        o_ref[...] = acc

    return pl.pallas_call(
        kern, name=name, grid=(R // tr,),
        in_specs=[pl.BlockSpec((n, tr, W), lambda i: (0, i, 0))],
        out_specs=pl.BlockSpec((tr, W), lambda i: (i, 0)),
        out_shape=jax.ShapeDtypeStruct((R, W), F32), compiler_params=_params("arbitrary"))(x)


def _adamw(w, g, m, v):
    shape = w.shape
    cols = shape[-1]
    rows = w.size // cols
    tr = _pick(rows, (512, 352, 256, 128))
    if rows * cols * 4 <= (1 << 20):
        tr = rows

    def kern(w_ref, g_ref, m_ref, v_ref, d_ref, mo_ref, vo_ref):
        gv = g_ref[...]
        mn = ADAM_B1 * m_ref[...] + (1.0 - ADAM_B1) * gv
        vn = ADAM_B2 * v_ref[...] + (1.0 - ADAM_B2) * (gv * gv)
        m_hat = mn / (1.0 - ADAM_B1 ** ADAM_STEP)
        v_hat = vn / (1.0 - ADAM_B2 ** ADAM_STEP)
        d_ref[...] = -ADAM_LR * (m_hat / (jnp.sqrt(v_hat) + ADAM_EPS) + ADAM_WD * w_ref[...])
        mo_ref[...] = mn
        vo_ref[...] = vn

    spec = pl.BlockSpec((tr, cols), lambda i: (i, 0))
    outs = pl.pallas_call(
        kern, name="adamw", grid=(rows // tr,), in_specs=[spec] * 4, out_specs=[spec] * 3,
        out_shape=[jax.ShapeDtypeStruct((rows, cols), F32)] * 3, compiler_params=_params("arbitrary"),
    )(*[a.reshape(rows, cols) for a in (w, g, m, v)])
    return [o.reshape(shape) for o in outs]


def _pack(arrs, n_lead, row_mult, width, total_mult=PACK_ROW_MULT):
    parts, total = [], 0
    for n, a in enumerate(arrs):
        lead = a.shape[:n_lead]
        flat = a.reshape(lead + (-1,))
        size = flat.shape[-1]
        rows = -(-size // (width * row_mult)) * row_mult
        if n == len(arrs) - 1:
            rows += -(total + rows) % total_mult
        total += rows
        if rows * width > size:
            flat = jnp.concatenate([flat, jnp.zeros(lead + (rows * width - size,), flat.dtype)], axis=n_lead)
        parts.append(flat.reshape(lead + (rows, width)))
    return jnp.concatenate(parts, axis=n_lead)


def _unpack(pack, shapes, n_lead, row_mult):
    outs, row = [], 0
    lead = pack.shape[:n_lead]
    width = pack.shape[-1]
    for shp in shapes:
        size = math.prod(shp)
        rows = -(-size // (width * row_mult)) * row_mult
        blk = lax.slice_in_dim(pack, row, row + rows, axis=n_lead)
        outs.append(blk.reshape(lead + (-1,))[..., :size].reshape(lead + tuple(shp)))
        row += rows
    return outs


def _to_words(a):
    return lax.bitcast_convert_type(a, BF16)


def _from_words(a):
    return lax.bitcast_convert_type(a, F32)


def _pad_axis(a, axis, size):
    pads = [(0, 0)] * a.ndim
    pads[axis] = (0, size - a.shape[axis])
    return jnp.pad(a, pads)


def _dense(name, s):
    if name.endswith("w_gate") or name.endswith("w_up"):
        _, nl, d, fs = s.shape
        return s.transpose(1, 2, 0, 3).reshape(nl, d, N_DEV * fs)
    if name.endswith("w_down"):
        _, nl, fs, d = s.shape
        return s.transpose(1, 0, 2, 3).reshape(nl, N_DEV * fs, d)
    if name == "pool_w":
        _, nl, ng, r, cg = s.shape
        return s.transpose(1, 2, 0, 3, 4).reshape(nl, ng, cg, cg)
    if name == "w_dkv":
        w = s.reshape(-1, s.shape[2])
        z = lambda n: jnp.zeros((w.shape[0], n), w.dtype)
        return jnp.concatenate([w[:, :KV_RANK], z(ROPE_LANE0), w[:, KV_RANK:],
                                z(HEAD_LANES - ROPE_LANE0 - QK_ROPE)], axis=1)
    if name in ("w_uk", "w_uv"):
        return _pad_axis(s.transpose(1, 0, 2), 2, HEAD_LANES).reshape(KV_RANK, N_HEADS * HEAD_LANES)
    if name == "w_dq":
        _, nl, ds, r = s.shape
        return s.transpose(1, 0, 2, 3).reshape(nl, N_DEV * ds, r)
    if name == "w_uq":
        nl = s.shape[1]
        return _pad_axis(s.transpose(1, 2, 0, 3), 3, HEAD_LANES).reshape(nl, Q_RANK, N_HEADS * HEAD_LANES)
    if name == "w_o":
        _, nl, k, dc = s.shape
        w = s.transpose(1, 2, 0, 3).reshape(nl, N_HEADS, V_HEAD, N_DEV * dc)
        return _pad_axis(w, 2, HEAD_LANES).reshape(nl, N_HEADS * HEAD_LANES, N_DEV * dc)
    if name in ("meta_tokens", "pool_scale"):
        r, dc = s.shape[1:]
        return s.transpose(1, 0, 2).reshape(r, N_DEV * dc)
    raise ValueError(name)


def _shards(name, g):
    if name.endswith("w_gate") or name.endswith("w_up"):
        nl, d, f = g.shape
        return g.reshape(nl, d, N_DEV, f // N_DEV).transpose(2, 0, 1, 3)
    if name.endswith("w_down"):
        nl, f, d = g.shape
        return g.reshape(nl, N_DEV, f // N_DEV, d).transpose(1, 0, 2, 3)
    if name == "pool_w":
        nl, ng, cg, _ = g.shape
        return g.reshape(nl, ng, N_DEV, cg // N_DEV, cg).transpose(2, 0, 1, 3, 4)
    if name == "w_dkv":
        w = jnp.concatenate([g[:, :KV_RANK], g[:, KV_RANK + ROPE_LANE0:KV_RANK + ROPE_LANE0 + QK_ROPE]], axis=1)
        return w.reshape(N_DEV, -1, KV_RANK + QK_ROPE)
    if name in ("w_uk", "w_uv"):
        return g.reshape(KV_RANK, N_HEADS, HEAD_LANES)[:, :, :V_HEAD].transpose(1, 0, 2)
    if name == "w_dq":
        nl, d, r = g.shape
        return g.reshape(nl, N_DEV, d // N_DEV, r).transpose(1, 0, 2, 3)
    if name == "w_uq":
        nl = g.shape[0]
        return g.reshape(nl, Q_RANK, N_HEADS, HEAD_LANES)[..., :QK_NOPE + QK_ROPE].transpose(2, 0, 1, 3)
    if name == "w_o":
        nl, _, d = g.shape
        w = g.reshape(nl, N_HEADS, HEAD_LANES, d)[:, :, :V_HEAD].reshape(nl, N_HEADS * V_HEAD, N_DEV, d // N_DEV)
        return w.transpose(2, 0, 1, 3)
    if name in ("meta_tokens", "pool_scale"):
        r, d = g.shape
        return g.reshape(r, N_DEV, d // N_DEV).transpose(1, 0, 2)
    raise ValueError(name)


def _rope_tables(L):
    pos = jnp.maximum(jnp.arange(L) - FRONT_PAD, 0).astype(F32)
    inv = 1.0 / (ROPE_THETA ** (jnp.arange(0, QK_ROPE, 2, dtype=F32) / QK_ROPE))
    ang = pos[:, None] * inv[None, :]
    cos, sin = jnp.cos(ang), jnp.sin(ang)
    half = QK_ROPE // 2
    z = lambda n: jnp.zeros((L, n), F32)
    tail = z(HEAD_LANES - ROPE_LANE0 - QK_ROPE)
    return {
        "cq": jnp.concatenate([jnp.ones((L, ROPE_LANE0), F32), cos, cos, tail], axis=1),
        "ck": jnp.concatenate([z(ROPE_LANE0), cos, cos, tail], axis=1),
        "s1": jnp.concatenate([z(ROPE_LANE0), -sin, z(half), tail], axis=1),
        "s2": jnp.concatenate([z(ROPE_LANE0), z(half), sin, tail], axis=1),
    }


def kernel(x, meta_tokens, ffn1_norm, ffn1_w_gate, ffn1_w_up, ffn1_w_down, mix_norm, ffn2_norm, ffn2_w_gate, ffn2_w_up, ffn2_w_down, pool_w, pool_scale, kv_in_norm, w_dkv, kv_latent_norm, w_uk, w_uv, w_dq, q_latent_norm, w_uq, w_o, final_norm, loss_target, m_meta_tokens, m_ffn1_norm, m_ffn1_w_gate, m_ffn1_w_up, m_ffn1_w_down, m_mix_norm, m_ffn2_norm, m_ffn2_w_gate, m_ffn2_w_up, m_ffn2_w_down, m_pool_w, m_pool_scale, m_kv_in_norm, m_w_dkv, m_kv_latent_norm, m_w_uk, m_w_uv, m_w_dq, m_q_latent_norm, m_w_uq, m_w_o, m_final_norm, v_meta_tokens, v_ffn1_norm, v_ffn1_w_gate, v_ffn1_w_up, v_ffn1_w_down, v_mix_norm, v_ffn2_norm, v_ffn2_w_gate, v_ffn2_w_up, v_ffn2_w_down, v_pool_w, v_pool_scale, v_kv_in_norm, v_w_dkv, v_kv_latent_norm, v_w_uk, v_w_uv, v_w_dq, v_q_latent_norm, v_w_uq, v_w_o, v_final_norm):
    args = dict(locals())
    W = {n: args[n] for n in WEIGHTS}
    M = {n: args["m_" + n] for n in WEIGHTS}
    V = {n: args["v_" + n] for n in WEIGHTS}
    seq, D = x.shape[1], x.shape[2]
    L = SEQ_START + seq

    fs = ffn1_w_down.shape[1]

    ffns = [(l, which) for l in range(DEPTH) for which in (1, 2)]
    ffn_names = lambda which: FFN_WEIGHTS[3 * (which - 1):3 * which]

    def weight_piece(l, which):
        parts = [(W[n][l] if n.endswith("w_down") else W[n][l].T).astype(BF16) for n in ffn_names(which)]
        if (l, which) == ffns[0]:
            parts += [_to_words(W[n]) if n in SHARDED_F32 else W[n].astype(BF16) for n in SMALL_SHARDED]
        return _pack(parts, 0, 16, D, 16)

    pieces = {k: weight_piece(*k) for k in ffns}
    full = {}
    P = {}

    def take_piece(k, gathered):
        full[k] = _with_own(gathered, pieces[k])
        if k == ffns[0]:
            shapes = [(fs, D)] * 3 + [W[n].shape + ((2,) if n in SHARDED_F32 else ()) for n in SMALL_SHARDED]
            for n, s in zip(SMALL_SHARDED, _unpack(full[k], shapes, 1, 16)[3:]):
                P[n] = _dense(n, _from_words(s) if n in SHARDED_F32 else s)

    take_piece(ffns[0], _run_side(_gather_side(pieces[ffns[0]]), "all_gather_first")[0])
    norm3 = lambda a: a.reshape(a.shape[0], 1, a.shape[-1])
    row = lambda a: a.reshape(1, -1)
    g_ffn, g_mix = {1: norm3(ffn1_norm), 2: norm3(ffn2_norm)}, mix_norm
    ffn_entries = (0, 1, 2)

    def ffn_forward(h, l, which):
        at = ffns.index((l, which))
        nxt = ffns[at + 1] if at + 1 < len(ffns) else None
        side = None if nxt is None else _gather_side(pieces[nxt])
        outs = _ffn_fwd(h, g_ffn[which][l:l + 1], full[(l, which)], ffn_entries, fs, side)
        if nxt is not None:
            take_piece(nxt, outs[4])
        return outs[:4]

    h = jnp.concatenate([jnp.zeros((FRONT_PAD, D), F32), P["meta_tokens"], x[0]], axis=0)
    target = jnp.concatenate([jnp.zeros((SEQ_START, D), F32), loss_target[0]], axis=0)
    tabs = _rope_tables(L)
    saved = []
    kv = None
    for l in range(DEPTH):
        s = {"h1": h}
        h, s["xn1"], s["g1"], s["u1"] = ffn_forward(h, l, 1)
        s["hm"] = h
        if l < N_POOL_LAYERS:
            h = _pool_fwd(h, row(g_mix[l]), P["pool_w"][l], row(P["pool_scale"][l]), l)
        else:
            j = l - N_POOL_LAYERS
            s["q"], s["cqp"] = _q_fwd(h, tabs, row(g_mix[l]), P["w_dq"][j], row(q_latent_norm[j]), P["w_uq"][j])
            s["o"], s["lse"] = _attn_fwd(s["q"], kv["kn"], kv["kr"], kv["v"])
            h = _oproj_fwd(h, s["o"], P["w_o"][j])
        s["h2"] = h
        h, s["xn2"], s["g2"], s["u2"] = ffn_forward(h, l, 2)
        if l == N_POOL_LAYERS - 1:
            kv = {"h": h}
            kv["kn"], kv["kr"], kv["v"], kv["ckr"] = _kv_fwd(
                h, tabs, row(kv_in_norm), P["w_dkv"], row(kv_latent_norm), P["w_uk"], P["w_uv"])
        saved.append(s)
    dh, loss_row, d_final = _head(h, target, row(final_norm))
    loss = lax.psum(loss_row[0, 0], ("x", "y", "c"))

    G = {}
    stack = {n: [None] * DEPTH for n in ("ffn1_norm", "ffn1_w_gate", "ffn1_w_up", "ffn1_w_down", "mix_norm",
                                         "ffn2_norm", "ffn2_w_gate", "ffn2_w_up", "ffn2_w_down")}
    pool_dw, pool_ds = [None] * N_POOL_LAYERS, [None] * N_POOL_LAYERS
    mla = {n: [None] * (DEPTH - N_POOL_LAYERS) for n in ("w_dq", "w_uq", "w_o", "q_latent_norm")}
    dks, dvs = [], []
    my_core = lax.axis_index("c").astype(jnp.int32).reshape(1)
    my_chip = (2 * lax.axis_index("x") + lax.axis_index("y")).astype(jnp.int32).reshape(1)
    layer_grads = {n: [None] * DEPTH for n in FFN_WEIGHTS}
    grads = {}

    by_owner = lambda g: g.reshape((N_CHIPS, 2, g.shape[0] // N_DEV) + g.shape[1:])

    def reduce_cores(arrs, others):
        return _add_own(arrs, my_core, others, "sum_cores")

    def finish_layer(l, partials, from_chips):
        for n, g in zip(FFN_WEIGHTS, _add_own(partials, my_chip, from_chips, "sum_chips")):
            layer_grads[n][l] = g if n.endswith("w_down") else g.T

    def ffn_backward(dh, s, which, l, side=None):
        outs = _ffn_bwd(dh, s["h%d" % which], s["g%d" % which], s["u%d" % which], g_ffn[which][l:l + 1],
                        full[(l, which)], ffn_entries, fs, side)
        dh, dg, du, act, dob, dgam = outs[:6]
        xn = s["xn%d" % which]
        stack["ffn%d_w_gate" % which][l] = _mm_tn(dg, xn, "ffn_dw")
        stack["ffn%d_w_up" % which][l] = _mm_tn(du, xn, "ffn_dw")
        stack["ffn%d_w_down" % which][l] = _mm_tn(act, dob, "ffn_dw")
        stack["ffn%d_norm" % which][l] = dgam
        return dh, outs[6:]

    arrs = None
    for l in reversed(range(DEPTH)):
        s = saved[l]
        if l == N_POOL_LAYERS - 1:
            dh, d_dkv, d_uk, d_uv, d_kvin, d_kvlat = _kv_bwd(
                dh, kv["h"], kv["ckr"], dks, dvs, tabs, row(kv_in_norm), P["w_dkv"], row(kv_latent_norm),
                P["w_uk"], P["w_uv"])
            G.update(w_dkv=d_dkv, w_uk=d_uk, w_uv=d_uv, kv_in_norm=d_kvin, kv_latent_norm=d_kvlat)
        if arrs is None:
            dh, _ = ffn_backward(dh, s, 2, l)
        else:
            dh, from_sibling = ffn_backward(dh, s, 2, l, _sibling_side(arrs))
            partials, partials_bf16 = reduce_cores(arrs, from_sibling)
        if l < N_POOL_LAYERS:
            dh, pool_dw[l], pool_ds[l], stack["mix_norm"][l] = _pool_bwd(
                dh, s["hm"], row(g_mix[l]), P["pool_w"][l], row(P["pool_scale"][l]))
        else:
            j = l - N_POOL_LAYERS
            do, delta_o, mla["w_o"][j] = _oproj_bwd(dh, s["o"], P["w_o"][j])
            dq, dk, dv = _attn_bwd(s["q"], kv["kn"], kv["kr"], kv["v"], do, s["lse"], delta_o)
            dks.append(dk)
            dvs.append(dv)
            dh, mla["w_dq"][j], mla["w_uq"][j], stack["mix_norm"][l], mla["q_latent_norm"][j] = _q_bwd(
                dh, s["hm"], s["cqp"], dq, tabs, row(g_mix[l]), P["w_dq"][j], row(q_latent_norm[j]), P["w_uq"][j])
        if arrs is None:
            dh, _ = ffn_backward(dh, s, 1, l)
        else:
            dh, from_chips = ffn_backward(dh, s, 1, l, _chip_side(partials_bf16))
            finish_layer(l + 1, partials, from_chips)
        arrs = [by_owner(stack[n][l]) for n in FFN_WEIGHTS]
    grad_x = dh[SEQ_START:][None]
    for n in ("ffn1_norm", "mix_norm", "ffn2_norm"):
        G[n] = jnp.concatenate(stack[n], axis=0)
    G["pool_w"] = jnp.stack(pool_dw)
    G["pool_scale"] = jnp.concatenate(pool_ds, axis=0)
    G["w_dq"], G["w_uq"], G["w_o"] = (jnp.stack(mla[n]) for n in ("w_dq", "w_uq", "w_o"))
    G["q_latent_norm"] = jnp.concatenate(mla["q_latent_norm"], axis=0)
    G["meta_tokens"] = dh[FRONT_PAD:SEQ_START]
    G["final_norm"] = d_final

    spack = _pack([_shards(n, G[n]) for n in SMALL_SHARDED], 1, 8, D)
    spack = spack.reshape((N_CHIPS, 2) + spack.shape[1:])
    others = _run_side(_sibling_side(arrs + [spack]), "sibling_exchange_last")
    partials, partials_bf16 = reduce_cores(arrs, others[:-1])
    small, small_bf16 = reduce_cores([spack], others[-1:])
    from_chips = _run_side(_chip_side(list(partials_bf16) + list(small_bf16)), "chip_exchange_last")
    finish_layer(0, partials, from_chips[:-1])
    small_mine = _add_own(small, my_chip, from_chips[-1:], "sum_chips")[0]
    grads.update(zip(SMALL_SHARDED, _unpack(small_mine, [W[n].shape for n in SMALL_SHARDED], 0, 8)))
    for n in FFN_WEIGHTS:
        grads[n] = jnp.stack(layer_grads[n])
    rep_shapes = [W[n].shape for n in REPLICATED]
    rpack = _pack([G[n].reshape(W[n].shape) for n in REPLICATED], 0, 8, D)
    everyones = _with_own(_run_side(_gather_side(rpack), "all_gather_norm_grads")[0], rpack)
    grads.update(zip(REPLICATED, _unpack(_sum_lead(everyones, "sum_devices"), rep_shapes, 0, 8)))

    delta, new_m, new_v = {}, {}, {}
    for n in WEIGHTS:
        delta[n], new_m[n], new_v[n] = _adamw(W[n], grads[n], M[n], V[n])
    return (loss, grad_x, *[grads[n] for n in WEIGHTS], *[delta[n] for n in WEIGHTS],
            *[new_m[n] for n in WEIGHTS], *[new_v[n] for n in WEIGHTS])
```

```python
import functools
import math

import jax
import jax.numpy as jnp
from jax import lax
from jax.experimental import pallas as pl
from jax.experimental.pallas import tpu as pltpu

F32 = jnp.float32
BF16 = jnp.bfloat16
MESH = pl.DeviceIdType.MESH

N_DEV = 8
N_CHIPS = 4
DEPTH = 4
N_POOL_LAYERS = 2
N_HEADS = 8
QK_NOPE = 64
QK_ROPE = 32
V_HEAD = 64
KV_RANK = 256
Q_RANK = 384
HEAD_LANES = 128
ROPE_LANE0 = QK_NOPE
BIAS_LANE = QK_NOPE + QK_ROPE
ONES_LANE = V_HEAD
LOG2E = math.log2(math.e)
N_META = 16
CHUNK_SHIFT = 6
FRONT_PAD = 112
SEQ_START = FRONT_PAD + N_META
HALO = 16
POOL_WINDOWS = (2, 4, 8, 16)
EPS = 1e-6
ROPE_THETA = 10000.0
NEG = -1e30
PACK_ROW_MULT = 256
VMEM_LIMIT = 56 * 1024 * 1024

ADAM_LR = 0.001
ADAM_B1 = 0.9
ADAM_B2 = 0.999
ADAM_EPS = 1e-08
ADAM_WD = 0.01
ADAM_STEP = 10

SHARDED = ["ffn1_w_gate", "ffn1_w_up", "ffn1_w_down", "ffn2_w_gate", "ffn2_w_up", "ffn2_w_down",
           "pool_w", "w_dkv", "w_uk", "w_uv", "w_dq", "w_uq", "w_o", "meta_tokens", "pool_scale"]
FFN_WEIGHTS = SHARDED[:6]
SMALL_SHARDED = SHARDED[6:]
SHARDED_F32 = ("meta_tokens", "pool_scale")
REPLICATED = ["ffn1_norm", "mix_norm", "ffn2_norm", "kv_in_norm", "kv_latent_norm", "q_latent_norm",
              "final_norm"]
WEIGHTS = ['meta_tokens', 'ffn1_norm', 'ffn1_w_gate', 'ffn1_w_up', 'ffn1_w_down', 'mix_norm', 'ffn2_norm',
           'ffn2_w_gate', 'ffn2_w_up', 'ffn2_w_down', 'pool_w', 'pool_scale', 'kv_in_norm', 'w_dkv',
           'kv_latent_norm', 'w_uk', 'w_uv', 'w_dq', 'q_latent_norm', 'w_uq', 'w_o', 'final_norm']


def _dot(a, b):
    return jnp.dot(a.astype(BF16), b.astype(BF16), preferred_element_type=F32)


def _dot_nt(a, b):
    return lax.dot_general(a.astype(BF16), b.astype(BF16), (((1,), (1,)), ((), ())),
                           preferred_element_type=F32)


def _dot_tn(a, b):
    return lax.dot_general(a.astype(BF16), b.astype(BF16), (((0,), (0,)), ((), ())),
                           preferred_element_type=F32)


def _sigmoid(x):
    return 1.0 / (1.0 + jnp.exp(-x))


def _rms(x, g):
    r = lax.rsqrt(jnp.mean(x * x, axis=-1, keepdims=True) + EPS)
    xh = x * r
    return xh * g, xh, r


def _rms_bwd(dy, xh, r, g):
    dxh = dy * g
    dx = r * (dxh - xh * jnp.mean(dxh * xh, axis=-1, keepdims=True))
    return dx, jnp.sum(dy * xh, axis=0, keepdims=True)


def _rope(x, c, s1, s2):
    return x * c + pltpu.roll(x, HEAD_LANES - QK_ROPE // 2, 1) * s1 + pltpu.roll(x, QK_ROPE // 2, 1) * s2


def _rope_t(d, c, s1, s2):
    return d * c + pltpu.roll(d * s1, QK_ROPE // 2, 1) + pltpu.roll(d * s2, HEAD_LANES - QK_ROPE // 2, 1)


def _params(*sem):
    return pltpu.CompilerParams(dimension_semantics=sem, vmem_limit_bytes=VMEM_LIMIT)


def _pick(n, candidates):
    for c in candidates:
        if n % c == 0:
            return c
    return n


def _row_tile(L):
    return _pick(L, (640, 128))


FF_OWNERS = 4


def _row_call(name, body, L, row_ins, full_ins, row_outs, acc_outs):
    tm = _row_tile(L)
    n = L // tm
    hb = tm // HALO
    nb = L // HALO
    in_specs, args = [], []
    for arr, kind in row_ins:
        c = arr.shape[1]
        if kind == "tile":
            spec = pl.BlockSpec((tm, c), lambda i: (i, 0))
        elif kind == "prev":
            spec = pl.BlockSpec((HALO, c), lambda i: (jnp.maximum(i * hb - 1, 0), 0))
        else:
            spec = pl.BlockSpec((HALO, c), lambda i: (jnp.minimum((i + 1) * hb, nb - 1), 0))
        in_specs.append(spec)
        args.append(arr)
    for arr in full_ins:
        in_specs.append(pl.BlockSpec(arr.shape, lambda i, nd=arr.ndim: (0,) * nd))
        args.append(arr)
    out_shape = [jax.ShapeDtypeStruct((L, c), dt) for c, dt in row_outs]
    out_specs = [pl.BlockSpec((tm, c), lambda i: (i, 0)) for c, _ in row_outs]
    for shp in acc_outs:
        out_shape.append(jax.ShapeDtypeStruct(shp, F32))
        out_specs.append(pl.BlockSpec(shp, lambda i, nd=len(shp): (0,) * nd))
    n_in, n_ro = len(args), len(row_outs)

    def kern(*refs):
        i = pl.program_id(0)
        vals = [r[...] for r in refs[:n_in]]
        ro, ao = body(i, n, tm, *vals)
        for r, v in zip(refs[n_in:n_in + n_ro], ro):
            r[...] = v.astype(r.dtype)
        acc_refs = refs[n_in + n_ro:]

        @pl.when(i == 0)
        def _():
            for r in acc_refs:
                r[...] = jnp.zeros(r.shape, r.dtype)

        for r, v in zip(acc_refs, ao):
            r[...] += v

    return pl.pallas_call(kern, name=name, grid=(n,), in_specs=in_specs, out_specs=out_specs,
                          out_shape=out_shape, compiler_params=_params("arbitrary"))(*args)


class _Side:
    def __init__(self, ins, out_shapes, sems, start, finish):
        self.ins, self.out_shapes, self.sems, self.start, self.finish = ins, out_shapes, sems, start, finish


ANY = pl.BlockSpec(memory_space=pl.ANY)


def _hosted(kern, n_in, n_out, n_scratch, side, is_first, is_last):
    if side is None:
        return kern
    ns_in, ns_out = len(side.ins), len(side.out_shapes)

    def wrapped(*refs):
        ins, refs = refs[:n_in], refs[n_in:]
        side_ins, refs = refs[:ns_in], refs[ns_in:]
        outs, refs = refs[:n_out], refs[n_out:]
        side_outs, refs = refs[:ns_out], refs[ns_out:]
        scratch, side_sems = refs[:n_scratch], refs[n_scratch:]

        @pl.when(is_first())
        def _():
            side.start(side_ins, side_outs, side_sems)

        kern(*ins, *outs, *scratch)

        @pl.when(is_last())
        def _():
            side.finish(side_ins, side_outs, side_sems)

    return wrapped


def _side_args(side):
    if side is None:
        return [], [], [], [], []
    return ([ANY] * len(side.ins), [ANY] * len(side.out_shapes), list(side.out_shapes), list(side.sems),
            list(side.ins))


def _ffn_weight_specs(fs, D, ents):
    return [pl.BlockSpec((FF_OWNERS, fs, D), lambda i, f, e=e: (f, e, 0)) for e in ents]


def _ffn_fwd(h, gam, wpiece, ents, fs, side=None):
    L, D = h.shape
    F = N_DEV * fs
    tm = _row_tile(L)
    tf = FF_OWNERS * fs
    nL, nF = L // tm, F // tf
    s_in, s_out, s_shape, s_sems, s_args = _side_args(side)

    def kern(h_ref, gam_ref, wg_ref, wu_ref, wd_ref, ho_ref, xn_ref, gs_ref, us_ref, acc):
        f = pl.program_id(1)

        @pl.when(f == 0)
        def _():
            xn, _, _ = _rms(h_ref[...], gam_ref[...])
            xn_ref[...] = xn.astype(BF16)
            acc[...] = jnp.zeros(acc.shape, F32)

        xnb = xn_ref[...]
        g = _dot_nt(xnb, wg_ref[...].reshape(tf, D))
        u = _dot_nt(xnb, wu_ref[...].reshape(tf, D))
        gs_ref[...] = g.astype(BF16)
        us_ref[...] = u.astype(BF16)
        acc[...] += _dot(g * _sigmoid(g) * u, wd_ref[...].reshape(tf, D))

        @pl.when(f == nF - 1)
        def _():
            ho_ref[...] = h_ref[...] + 0.5 * acc[...]

    first = lambda: (pl.program_id(0) == 0) & (pl.program_id(1) == 0)
    last = lambda: (pl.program_id(0) == nL - 1) & (pl.program_id(1) == nF - 1)
    return pl.pallas_call(
        _hosted(kern, 5, 4, 1, side, first, last),
        name="ffn_fwd" if side is None else "ffn_fwd_hosting", grid=(nL, nF),
        in_specs=[pl.BlockSpec((tm, D), lambda i, f: (i, 0)),
                  pl.BlockSpec((None, 1, D), lambda i, f: (0, 0, 0))] + _ffn_weight_specs(fs, D, ents) + s_in,
        out_specs=[pl.BlockSpec((tm, D), lambda i, f: (i, 0)),
                   pl.BlockSpec((tm, D), lambda i, f: (i, 0)),
                   pl.BlockSpec((tm, tf), lambda i, f: (i, f)),
                   pl.BlockSpec((tm, tf), lambda i, f: (i, f))] + s_out,
        out_shape=[jax.ShapeDtypeStruct((L, D), F32), jax.ShapeDtypeStruct((L, D), BF16),
                   jax.ShapeDtypeStruct((L, F), BF16), jax.ShapeDtypeStruct((L, F), BF16)] + s_shape,
        scratch_shapes=[pltpu.VMEM((tm, D), F32)] + s_sems,
        compiler_params=_params("arbitrary", "arbitrary"))(h, gam, wpiece, wpiece, wpiece, *s_args)


def _ffn_bwd(dh, h, gs, us, gam, wpiece, ents, fs, side=None):
    L, D = h.shape
    F = N_DEV * fs
    tm = _pick(L, (416, 128))
    tf = FF_OWNERS * fs
    nL, nF = L // tm, F // tf
    s_in, s_out, s_shape, s_sems, s_args = _side_args(side)

    def kern(dh_ref, h_ref, gs_ref, us_ref, gam_ref, wg_ref, wu_ref, wd_ref,
             dhi_ref, dg_ref, du_ref, a_ref, dob_ref, dgam_ref, dxn):
        i = pl.program_id(0)
        f = pl.program_id(1)

        @pl.when(f == 0)
        def _():
            dxn[...] = jnp.zeros(dxn.shape, F32)
            dob_ref[...] = (0.5 * dh_ref[...]).astype(BF16)

        @pl.when((f == 0) & (i == 0))
        def _():
            dgam_ref[...] = jnp.zeros(dgam_ref.shape, F32)

        g = gs_ref[...].astype(F32)
        u = us_ref[...].astype(F32)
        sg = _sigmoid(g)
        silu = g * sg
        da = _dot_nt(dob_ref[...], wd_ref[...].reshape(tf, D))
        a_ref[...] = (silu * u).astype(BF16)
        dgt = (da * u * (sg * (1.0 + g * (1.0 - sg)))).astype(BF16)
        dut = (da * silu).astype(BF16)
        dg_ref[...] = dgt
        du_ref[...] = dut
        dxn[...] += _dot(dgt, wg_ref[...].reshape(tf, D)) + _dot(dut, wu_ref[...].reshape(tf, D))

        @pl.when(f == nF - 1)
        def _():
            gamma = gam_ref[...]
            _, xh, r = _rms(h_ref[...], gamma)
            dx, dgam = _rms_bwd(dxn[...], xh, r, gamma)
            dhi_ref[...] = dh_ref[...] + dx
            dgam_ref[...] += dgam

    first = lambda: (pl.program_id(0) == 0) & (pl.program_id(1) == 0)
    last = lambda: (pl.program_id(0) == nL - 1) & (pl.program_id(1) == nF - 1)
    return pl.pallas_call(
        _hosted(kern, 8, 6, 1, side, first, last),
        name="ffn_bwd" if side is None else "ffn_bwd_hosting", grid=(nL, nF),
        in_specs=[pl.BlockSpec((tm, D), lambda i, f: (i, 0)),
                  pl.BlockSpec((tm, D), lambda i, f: (i, 0)),
                  pl.BlockSpec((tm, tf), lambda i, f: (i, f)),
                  pl.BlockSpec((tm, tf), lambda i, f: (i, f)),
                  pl.BlockSpec((None, 1, D), lambda i, f: (0, 0, 0))] + _ffn_weight_specs(fs, D, ents) + s_in,
        out_specs=[pl.BlockSpec((tm, D), lambda i, f: (i, 0)),
                   pl.BlockSpec((tm, tf), lambda i, f: (i, f)),
                   pl.BlockSpec((tm, tf), lambda i, f: (i, f)),
                   pl.BlockSpec((tm, tf), lambda i, f: (i, f)),
                   pl.BlockSpec((tm, D), lambda i, f: (i, 0)),
                   pl.BlockSpec((1, D), lambda i, f: (0, 0))] + s_out,
        out_shape=[jax.ShapeDtypeStruct((L, D), F32), jax.ShapeDtypeStruct((L, F), BF16),
                   jax.ShapeDtypeStruct((L, F), BF16), jax.ShapeDtypeStruct((L, F), BF16),
                   jax.ShapeDtypeStruct((L, D), BF16), jax.ShapeDtypeStruct((1, D), F32)] + s_shape,
        scratch_shapes=[pltpu.VMEM((tm, D), F32)] + s_sems,
        compiler_params=_params("arbitrary", "arbitrary"))(dh, h, gs, us, gam, wpiece, wpiece, wpiece, *s_args)


def _mm_tn(a, b, name):
    L, M = a.shape
    N = b.shape[1]
    tm = _pick(M, (1408, 1024, 512))
    tn = _pick(N, (1408, 1024, 512))
    tk = _pick(L, (2080, 640, 128))

    def kern(a_ref, b_ref, o_ref):
        @pl.when(pl.program_id(2) == 0)
        def _():
            o_ref[...] = jnp.zeros(o_ref.shape, F32)

        o_ref[...] += _dot_tn(a_ref[...], b_ref[...])

    return pl.pallas_call(
        kern, name=name, grid=(M // tm, N // tn, L // tk),
        in_specs=[pl.BlockSpec((tk, tm), lambda i, j, k: (k, i)),
                  pl.BlockSpec((tk, tn), lambda i, j, k: (k, j))],
        out_specs=pl.BlockSpec((tm, tn), lambda i, j, k: (i, j)),
        out_shape=jax.ShapeDtypeStruct((M, N), F32),
        compiler_params=_params("arbitrary", "arbitrary", "arbitrary"))(a, b)


def _pool_counts(pos, w):
    return jnp.clip(pos - (FRONT_PAD - 1), 1, w).astype(F32)


def _pool_forward_values(i, tm, h, hprev, gamma, D):
    cg = D // len(POOL_WINDOWS)
    hext = jnp.concatenate([hprev, h], axis=0)
    uext, xh, r = _rms(hext, gamma)
    pos = i * tm + lax.broadcasted_iota(jnp.int32, (tm, 1), 0)
    pooled = []
    for gi, w in enumerate(POOL_WINDOWS):
        s = uext[:, gi * cg:(gi + 1) * cg]
        span = 1
        while span < w:
            s = s + pltpu.roll(s, span, 0)
            span *= 2
        s = s[HALO:]
        pooled.append(s / _pool_counts(pos, w) - uext[HALO:, gi * cg:(gi + 1) * cg])
    return uext, xh[HALO:], r[HALO:], pooled


def _pool_fwd(h, gam, w, scale, l):
    L, D = h.shape
    cg = D // len(POOL_WINDOWS)

    def body(i, n, tm, ht, hprev, gamma, wv, sc):
        _, _, _, pooled = _pool_forward_values(i, tm, ht, hprev, gamma, D)
        ys = [_dot(pooled[gi], wv[gi]) for gi in range(len(POOL_WINDOWS))]
        y = jnp.concatenate(ys, axis=1) * sc
        return [ht + y], []

    del cg
    return _row_call("pool_fwd", body, L, [(h, "tile"), (h, "prev")], [gam, w, scale], [(D, F32)], [])[0]


def _pool_bwd(dy, h, gam, w, scale):
    L, D = h.shape
    ng = len(POOL_WINDOWS)
    cg = D // ng

    def body(i, n, tm, ht, hprev, dyt, dynext, gamma, wv, sc):
        _, xh, r, pooled = _pool_forward_values(i, tm, ht, hprev, gamma, D)
        dynext = jnp.where(i == n - 1, jnp.zeros_like(dynext), dynext)
        dyext = jnp.concatenate([dyt, dynext], axis=0) * sc
        pos_ext = i * tm + lax.broadcasted_iota(jnp.int32, (tm + HALO, 1), 0)
        dws, dscs, dus = [], [], []
        for gi, wd in enumerate(POOL_WINDOWS):
            cols = slice(gi * cg, (gi + 1) * cg)
            pb = pooled[gi].astype(BF16)
            ypre = _dot(pb, wv[gi])
            dscs.append(jnp.sum(dyt[:, cols] * ypre, axis=0, keepdims=True))
            dws.append(_dot_tn(pb, dyext[:tm, cols])[None])
            dp = _dot_nt(dyext[:, cols], wv[gi])
            s = dp / _pool_counts(pos_ext, wd)
            span = 1
            while span < wd:
                s = s + pltpu.roll(s, tm + HALO - span, 0)
                span *= 2
            dus.append(s[:tm] - dp[:tm])
        du = jnp.concatenate(dus, axis=1)
        pos = pos_ext[:tm]
        du = jnp.where(pos >= FRONT_PAD, du, 0.0)
        dx, dgam = _rms_bwd(du, xh, r, gamma)
        return [dyt + dx], [jnp.concatenate(dws, axis=0), jnp.concatenate(dscs, axis=1), dgam]

    return _row_call("pool_bwd", body, L, [(h, "tile"), (h, "prev"), (dy, "tile"), (dy, "next")],
                     [gam, w, scale], [(D, F32)], [(ng, cg, cg), (1, D), (1, D)])


def _kv_fwd(h, tabs, g1, wdkv, g2, wuk, wuv):
    L, D = h.shape
    hw = N_HEADS * HEAD_LANES

    def body(i, n, tm, ht, ck, s1, s2, g1v, wdkv_v, g2v, wuk_v, wuv_v):
        xkv, _, _ = _rms(ht, g1v)
        ckr = _dot(xkv, wdkv_v)
        ckv, _, _ = _rms(ckr[:, :KV_RANK], g2v)
        krope = _rope(ckr[:, KV_RANK:], ck, s1, s2)
        pos = i * tm + lax.broadcasted_iota(jnp.int32, (tm, HEAD_LANES), 0)
        lane = lax.broadcasted_iota(jnp.int32, (tm, HEAD_LANES), 1)
        krope = jnp.where((pos < FRONT_PAD) & (lane == BIAS_LANE), NEG, krope)
        ones = ((lax.broadcasted_iota(jnp.int32, (1, hw), 1) & (HEAD_LANES - 1)) == ONES_LANE).astype(F32)
        return [_dot(ckv, wuk_v), krope, _dot(ckv, wuv_v) + ones, ckr], []

    ck, s1, s2 = tabs["ck"], tabs["s1"], tabs["s2"]
    return _row_call("kv_fwd", body, L, [(h, "tile"), (ck, "tile"), (s1, "tile"), (s2, "tile")],
                     [g1, wdkv, g2, wuk, wuv],
                     [(hw, BF16), (HEAD_LANES, BF16), (hw, BF16), (KV_RANK + HEAD_LANES, F32)], [])


def _kv_bwd(dh, h, ckr, dks, dvs, tabs, g1, wdkv, g2, wuk, wuv):
    L, D = h.shape
    hw = N_HEADS * HEAD_LANES
    nl = len(dks)

    def body(i, n, tm, *vals):
        dht, ht, ckr_t = vals[:3]
        dk = sum(vals[3:3 + nl][1:], vals[3])
        dv = sum(vals[3 + nl:3 + 2 * nl][1:], vals[3 + nl])
        ck, s1, s2, g1v, wdkv_v, g2v, wuk_v, wuv_v = vals[3 + 2 * nl:]
        xkv, xh1, r1 = _rms(ht, g1v)
        ckv, xh2, r2 = _rms(ckr_t[:, :KV_RANK], g2v)
        dckv = _dot_nt(dk, wuk_v) + _dot_nt(dv, wuv_v)
        dlat, dg2 = _rms_bwd(dckv, xh2, r2, g2v)
        dkr = dk[:, :HEAD_LANES]
        for hd in range(1, N_HEADS):
            dkr = dkr + dk[:, hd * HEAD_LANES:(hd + 1) * HEAD_LANES]
        dckr = jnp.concatenate([dlat, _rope_t(dkr, ck, s1, s2)], axis=1)
        dx, dg1 = _rms_bwd(_dot_nt(dckr, wdkv_v), xh1, r1, g1v)
        return [dht + dx], [_dot_tn(xkv, dckr), _dot_tn(ckv, dk), _dot_tn(ckv, dv), dg1, dg2]

    row_ins = [(dh, "tile"), (h, "tile"), (ckr, "tile")] + [(a, "tile") for a in dks + dvs]
    row_ins += [(tabs[k], "tile") for k in ("ck", "s1", "s2")]
    return _row_call("kv_bwd", body, L, row_ins, [g1, wdkv, g2, wuk, wuv], [(D, F32)],
                     [(D, KV_RANK + HEAD_LANES), (KV_RANK, hw), (KV_RANK, hw), (1, D), (1, KV_RANK)])


def _q_fwd(h, tabs, g, wdq, gq, wuq):
    L, D = h.shape
    hw = N_HEADS * HEAD_LANES

    def body(i, n, tm, ht, cq_t, s1, s2, gv, wdq_v, gqv, wuq_v):
        u, _, _ = _rms(ht, gv)
        cqp = _dot(u, wdq_v)
        cq, _, _ = _rms(cqp, gqv)
        qp = _dot(cq, wuq_v)
        bias = (lax.broadcasted_iota(jnp.int32, (1, HEAD_LANES), 1) == BIAS_LANE).astype(F32)
        q = [_rope(qp[:, hd * HEAD_LANES:(hd + 1) * HEAD_LANES], cq_t, s1, s2) * (SM_SCALE * LOG2E) + bias
             for hd in range(N_HEADS)]
        return [jnp.concatenate(q, axis=1), cqp], []

    return _row_call("q_fwd", body, L, [(h, "tile")] + [(tabs[k], "tile") for k in ("cq", "s1", "s2")],
                     [g, wdq, gq, wuq], [(hw, BF16), (Q_RANK, F32)], [])


def _q_bwd(dh, h, cqp, dq, tabs, g, wdq, gq, wuq):
    L, D = h.shape
    hw = N_HEADS * HEAD_LANES

    def body(i, n, tm, dht, ht, cqp_t, dq_t, cq_t, s1, s2, gv, wdq_v, gqv, wuq_v):
        u, xh1, r1 = _rms(ht, gv)
        cq, xh2, r2 = _rms(cqp_t, gqv)
        dqp = jnp.concatenate([_rope_t(dq_t[:, hd * HEAD_LANES:(hd + 1) * HEAD_LANES], cq_t, s1, s2)
                               for hd in range(N_HEADS)], axis=1)
        dcqp, dgq = _rms_bwd(_dot_nt(dqp, wuq_v), xh2, r2, gqv)
        dx, dg = _rms_bwd(_dot_nt(dcqp, wdq_v), xh1, r1, gv)
        return [dht + dx], [_dot_tn(u, dcqp), _dot_tn(cq, dqp), dg, dgq]

    row_ins = [(dh, "tile"), (h, "tile"), (cqp, "tile"), (dq, "tile")]
    row_ins += [(tabs[k], "tile") for k in ("cq", "s1", "s2")]
    return _row_call("q_bwd", body, L, row_ins, [g, wdq, gq, wuq], [(D, F32)],
                     [(D, Q_RANK), (Q_RANK, hw), (1, D), (1, Q_RANK)])


def _oproj_fwd(h, o, wo):
    L, D = h.shape

    def body(i, n, tm, ht, ot, wov):
        return [ht + _dot(ot, wov)], []

    return _row_call("oproj_fwd", body, L, [(h, "tile"), (o, "tile")], [wo], [(D, F32)], [])[0]


def _oproj_bwd(dh, o, wo):
    L, D = dh.shape
    hw = N_HEADS * HEAD_LANES

    def body(i, n, tm, dht, ot, wov):
        do = _dot_nt(dht, wov)
        prod = do * ot.astype(F32)
        delta = [jnp.broadcast_to(jnp.sum(prod[:, hd * HEAD_LANES:(hd + 1) * HEAD_LANES], axis=-1, keepdims=True),
                                  (tm, HEAD_LANES)) for hd in range(N_HEADS)]
        return [do, jnp.concatenate(delta, axis=1)], [_dot_tn(ot, dht)]

    return _row_call("oproj_bwd", body, L, [(dh, "tile"), (o, "tile")], [wo], [(hw, BF16), (hw, F32)], [(hw, D)])


def _causal(t, keys_first=False):
    qpos = lax.broadcasted_iota(jnp.int32, (t, t), 1 if keys_first else 0)
    kpos = lax.broadcasted_iota(jnp.int32, (t, t), 0 if keys_first else 1)
    return (kpos >> CHUNK_SHIFT) <= (qpos >> CHUNK_SHIFT)


SM_SCALE = 1.0 / math.sqrt(QK_NOPE + QK_ROPE)


def _pairs(n, key_major):
    if key_major:
        order = [(i, j) for j in range(n) for i in range(j, n)]
    else:
        order = [(i, j) for i in range(n) for j in range(i + 1)]
    return (jnp.array([p[0] for p in order], jnp.int32), jnp.array([p[1] for p in order], jnp.int32))


def _attn_fwd(q, kn, kr, v):
    L = q.shape[0]
    hw = N_HEADS * HEAD_LANES
    t = _row_tile(L)
    it, jt = _pairs(L // t, key_major=False)

    def kern(it_ref, jt_ref, q_ref, kn_ref, kr_ref, v_ref, o_ref, lse_ref, m_s, acc_s):
        step = pl.program_id(1)
        i, j = it_ref[step], jt_ref[step]

        @pl.when(j == 0)
        def _():
            m_s[...] = jnp.full(m_s.shape, NEG, F32)
            acc_s[...] = jnp.zeros(acc_s.shape, F32)

        def update(diagonal):
            k = kn_ref[...] + kr_ref[...]
            s = _dot_nt(q_ref[...], k)
            if diagonal:
                s = jnp.where(_causal(t), s, NEG)
            m_prev = m_s[:, :1]
            m_new = jnp.maximum(m_prev, jnp.max(s, axis=-1, keepdims=True))
            p = jnp.exp2(s - m_new)
            acc_s[...] = jnp.exp2(m_prev - m_new) * acc_s[...] + _dot(p, v_ref[...])
            m_s[...] = jnp.broadcast_to(m_new, m_s.shape)

        @pl.when(j < i)
        def _():
            update(False)

        @pl.when(j == i)
        def _():
            update(True)
            acc = acc_s[...]
            total = acc[:, ONES_LANE:ONES_LANE + 1]
            o_ref[...] = (acc / total).astype(BF16)
            lse_ref[...] = m_s[...] + jnp.log2(jnp.broadcast_to(total, m_s.shape))

    qmap = lambda h, s, it, jt: (it[s], h)
    kmap = lambda h, s, it, jt: (jt[s], h)
    blk = (t, HEAD_LANES)
    return pl.pallas_call(
        kern, name="attn_fwd",
        grid_spec=pltpu.PrefetchScalarGridSpec(
            num_scalar_prefetch=2, grid=(N_HEADS, it.shape[0]),
            in_specs=[pl.BlockSpec(blk, qmap), pl.BlockSpec(blk, kmap),
                      pl.BlockSpec(blk, lambda h, s, it, jt: (jt[s], 0)), pl.BlockSpec(blk, kmap)],
            out_specs=[pl.BlockSpec(blk, qmap), pl.BlockSpec(blk, qmap)],
            scratch_shapes=[pltpu.VMEM(blk, F32)] * 2),
        out_shape=[jax.ShapeDtypeStruct((L, hw), BF16), jax.ShapeDtypeStruct((L, hw), F32)],
        compiler_params=_params("arbitrary", "arbitrary"))(it, jt, q, kn, kr, v)


def _attn_bwd(q, kn, kr, v, do, lse, delta):
    L = q.shape[0]
    hw = N_HEADS * HEAD_LANES
    t = _row_tile(L)
    it, jt = _pairs(L // t, key_major=True)

    def kern(it_ref, jt_ref, q_ref, kn_ref, kr_ref, v_ref, do_ref, lse_ref, dl_ref, dq_ref, dk_ref, dv_ref):
        step = pl.program_id(1)
        i, j = it_ref[step], jt_ref[step]

        @pl.when(step == 0)
        def _():
            dq_ref[...] = jnp.zeros(dq_ref.shape, F32)

        @pl.when(i == j)
        def _():
            dk_ref[...] = jnp.zeros(dk_ref.shape, F32)
            dv_ref[...] = jnp.zeros(dv_ref.shape, F32)

        def update(diagonal):
            k = kn_ref[...] + kr_ref[...]
            qv, dov = q_ref[...], do_ref[...]
            s = _dot_nt(k, qv)
            if diagonal:
                s = jnp.where(_causal(t, keys_first=True), s, NEG)
            p = jnp.exp2(s - lse_ref[...])
            dp = _dot_nt(v_ref[...], dov)
            dz = (p * (dp - dl_ref[...])).astype(BF16)
            dv_ref[...] += _dot(p, dov)
            dk_ref[...] += _dot(dz, qv) * (1.0 / LOG2E)
            rows = pl.ds(pl.multiple_of(i * t, t), t)
            dq_ref[rows, :] += _dot_tn(dz, k) * SM_SCALE

        @pl.when(j < i)
        def _():
            update(False)

        @pl.when(j == i)
        def _():
            update(True)

    qmap = lambda h, s, it, jt: (it[s], h)
    kmap = lambda h, s, it, jt: (jt[s], h)
    rowmap = lambda h, s, it, jt: (h, 0, it[s])
    per_head_rows = lambda a: a[:, ::HEAD_LANES].T.reshape(N_HEADS, 1, L)
    blk = (t, HEAD_LANES)
    return pl.pallas_call(
        kern, name="attn_bwd",
        grid_spec=pltpu.PrefetchScalarGridSpec(
            num_scalar_prefetch=2, grid=(N_HEADS, it.shape[0]),
            in_specs=[pl.BlockSpec(blk, qmap), pl.BlockSpec(blk, kmap),
                      pl.BlockSpec(blk, lambda h, s, it, jt: (jt[s], 0)), pl.BlockSpec(blk, kmap),
                      pl.BlockSpec(blk, qmap), pl.BlockSpec((None, 1, t), rowmap), pl.BlockSpec((None, 1, t), rowmap)],
            out_specs=[pl.BlockSpec((L, HEAD_LANES), lambda h, s, it, jt: (0, h)),
                       pl.BlockSpec(blk, kmap), pl.BlockSpec(blk, kmap)]),
        out_shape=[jax.ShapeDtypeStruct((L, hw), F32)] * 3,
        compiler_params=_params("arbitrary", "arbitrary"))(it, jt, q, kn, kr, v, do, per_head_rows(lse),
                                                            per_head_rows(delta))


def _head(h, target, g):
    L, D = h.shape

    def body(i, n, tm, ht, tt, gv):
        y, xh, r = _rms(ht, gv)
        pos = i * tm + lax.broadcasted_iota(jnp.int32, (tm, 1), 0)
        e = jnp.where(pos >= SEQ_START, y - tt, 0.0)
        loss = 0.5 * jnp.sum(jnp.mean(e * e, axis=-1, keepdims=True), axis=0, keepdims=True)
        dx, dg = _rms_bwd(e / D, xh, r, gv)
        return [dx], [jnp.broadcast_to(loss, (1, 128)), dg]

    return _row_call("loss_head", body, L, [(h, "tile"), (target, "tile")], [g], [(D, F32)], [(1, 128), (1, D)])


def _coords():
    return lax.axis_index("x"), lax.axis_index("y"), lax.axis_index("c")


def _my_index():
    mx, my, mc = _coords()
    return 4 * mx + 2 * my + mc


def _gather_side(x):
    R, W = x.shape

    def copies(x_ref, out_ref, send_sems, recv_sems):
        mx, my, mc = _coords()
        me, sibling = (mx, my, mc), (mx, my, 1 - mc)
        chips = [(1 - mx, my), (mx, 1 - my), (1 - mx, 1 - my)]

        def slot(px, py, pc):
            return out_ref.at[4 * px + 2 * py + pc]

        def copy(k, block, to, src=None):
            return pltpu.make_async_remote_copy(
                src_ref=slot(*block) if src is None else src, dst_ref=slot(*block),
                send_sem=send_sems.at[k], recv_sem=recv_sems.at[k], device_id=to, device_id_type=MESH)

        first = [copy(0, me, sibling, src=x_ref)]
        first += [copy(1 + n, me, (*chip, mc), src=x_ref) for n, chip in enumerate(chips)]
        passed = [copy(4 + n, (*chip, mc), sibling) for n, chip in enumerate(chips)]
        landed = [copy(1 + n, (*chip, mc), me) for n, chip in enumerate(chips)]
        from_sibling = [copy(0, sibling, me)] + [copy(4 + n, (*chip, 1 - mc), me) for n, chip in enumerate(chips)]
        return first, passed, landed, from_sibling

    def start(ins, outs, sems):
        for cp in copies(ins[0], outs[0], *sems)[0]:
            cp.start()

    def finish(ins, outs, sems):
        first, passed, landed, from_sibling = copies(ins[0], outs[0], *sems)
        for arrived, onward in zip(landed, passed):
            arrived.wait_recv()
            onward.start()
        for cp in from_sibling:
            cp.wait_recv()
        for cp in first + passed:
            cp.wait_send()

    return _Side([x], [jax.ShapeDtypeStruct((N_DEV, R, W), x.dtype)],
                 [pltpu.SemaphoreType.DMA((7,)), pltpu.SemaphoreType.DMA((7,))], start, finish)


def _with_own(gathered, x):
    return lax.dynamic_update_slice(gathered, x[None], (_my_index(), 0, 0))


def _chip_side(parts):
    n_arr = len(parts)

    def copies(p_refs, out_refs, send_sems, recv_sems):
        mx, my, mc = _coords()
        chips = [(1 - mx, my), (mx, 1 - my), (1 - mx, 1 - my)]
        return [pltpu.make_async_remote_copy(
            src_ref=p_ref.at[2 * cx + cy], dst_ref=out_ref.at[n], send_sem=send_sems.at[3 * a + n],
            recv_sem=recv_sems.at[3 * a + n], device_id=(cx, cy, mc), device_id_type=MESH)
            for a, (p_ref, out_ref) in enumerate(zip(p_refs, out_refs)) for n, (cx, cy) in enumerate(chips)]

    def start(ins, outs, sems):
        for cp in copies(ins, outs, *sems):
            cp.start()

    def finish(ins, outs, sems):
        cps = copies(ins, outs, *sems)
        for cp in cps:
            cp.wait_recv()
        for cp in cps:
            cp.wait_send()

    return _Side(list(parts), [jax.ShapeDtypeStruct((3,) + p.shape[1:], p.dtype) for p in parts],
                 [pltpu.SemaphoreType.DMA((3 * n_arr,)), pltpu.SemaphoreType.DMA((3 * n_arr,))], start, finish)


def _run_side(side, name):
    def kern(*refs):
        n_in, n_out = len(side.ins), len(side.out_shapes)
        ins, outs, sems = refs[:n_in], refs[n_in:n_in + n_out], refs[n_in + n_out:]
        side.start(ins, outs, sems)
        side.finish(ins, outs, sems)

    return pl.pallas_call(kern, name=name, in_specs=[ANY] * len(side.ins), out_specs=[ANY] * len(side.out_shapes),
                          out_shape=list(side.out_shapes), scratch_shapes=list(side.sems))(*side.ins)


def _sibling_side(arrs):
    n_arr = len(arrs)

    def copies(g_refs, out_refs, send_sems, recv_sems):
        mx, my, mc = _coords()
        return [pltpu.make_async_remote_copy(
            src_ref=g_ref.at[n, 1 - mc], dst_ref=out_ref.at[n], send_sem=send_sems.at[N_CHIPS * a + n],
            recv_sem=recv_sems.at[N_CHIPS * a + n], device_id=(mx, my, 1 - mc), device_id_type=MESH)
            for a, (g_ref, out_ref) in enumerate(zip(g_refs, out_refs)) for n in range(N_CHIPS)]

    def start(ins, outs, sems):
        for cp in copies(ins, outs, *sems):
            cp.start()

    def finish(ins, outs, sems):
        cps = copies(ins, outs, *sems)
        for cp in cps:
            cp.wait_recv()
        for cp in cps:
            cp.wait_send()

    return _Side(list(arrs), [jax.ShapeDtypeStruct((N_CHIPS,) + g.shape[2:], g.dtype) for g in arrs],
                 [pltpu.SemaphoreType.DMA((N_CHIPS * n_arr,)), pltpu.SemaphoreType.DMA((N_CHIPS * n_arr,))],
                 start, finish)


def _add_own(owns, sel, others, name):
    n_arr = len(owns)
    R, W = owns[0].shape[-2:]
    tr = _pick(R, (PACK_ROW_MULT, 176, 64, 8))
    first_phase = owns[0].ndim == 4
    if first_phase:
        n = owns[0].shape[0]
        grid = (n, R // tr)
        in_specs = ([pl.BlockSpec((None, None, tr, W), lambda b, i, sel: (b, sel[0], i, 0))] * n_arr
                    + [pl.BlockSpec((None, tr, W), lambda b, i, sel: (b, i, 0))] * n_arr)
        out_specs = [pl.BlockSpec((None, tr, W), lambda b, i, sel: (b, i, 0))] * (2 * n_arr)
        out_shape = [jax.ShapeDtypeStruct((n, R, W), F32)] * n_arr + [jax.ShapeDtypeStruct((n, R, W), BF16)] * n_arr

        def kern(sel_ref, *refs):
            for a in range(n_arr):
                acc = refs[a][...] + refs[n_arr + a][...]
                refs[2 * n_arr + a][...] = acc
                refs[3 * n_arr + a][...] = acc.astype(BF16)
    else:
        k = others[0].shape[0]
        grid = (1, R // tr)
        in_specs = ([pl.BlockSpec((None, tr, W), lambda b, i, sel: (sel[0], i, 0))] * n_arr
                    + [pl.BlockSpec((k, tr, W), lambda b, i, sel: (0, i, 0))] * n_arr)
        out_specs = [pl.BlockSpec((tr, W), lambda b, i, sel: (i, 0))] * n_arr
        out_shape = [jax.ShapeDtypeStruct((R, W), F32)] * n_arr

        def kern(sel_ref, *refs):
            for a in range(n_arr):
                acc = refs[a][...]
                for m in range(k):
                    acc = acc + refs[n_arr + a][m].astype(F32)
                refs[2 * n_arr + a][...] = acc

    outs = pl.pallas_call(
        kern, name=name,
        grid_spec=pltpu.PrefetchScalarGridSpec(num_scalar_prefetch=1, grid=grid, in_specs=in_specs,
                                               out_specs=out_specs),
        out_shape=out_shape, compiler_params=_params("arbitrary", "arbitrary"))(sel, *owns, *others)
    return (outs[:n_arr], outs[n_arr:]) if first_phase else outs


def _sum_lead(x, name):
    n, R, W = x.shape
    tr = _pick(R, (PACK_ROW_MULT, 8))

    def kern(x_ref, o_ref):
        acc = x_ref[0]
        for k in range(1, n):
            acc = acc + x_ref[k]
        o_ref[...] = acc

    return pl.pallas_call(
        kern, name=name, grid=(R // tr,),
        in_specs=[pl.BlockSpec((n, tr, W), lambda i: (0, i, 0))],
        out_specs=pl.BlockSpec((tr, W), lambda i: (i, 0)),
        out_shape=jax.ShapeDtypeStruct((R, W), F32), compiler_params=_params("arbitrary"))(x)


def _adamw(w, g, m, v):
    shape = w.shape
    cols = shape[-1]
    rows = w.size // cols
    tr = _pick(rows, (512, 352, 256, 128))
    if rows * cols * 4 <= (1 << 20):
        tr = rows

    def kern(w_ref, g_ref, m_ref, v_ref, d_ref, mo_ref, vo_ref):
        gv = g_ref[...]
        mn = ADAM_B1 * m_ref[...] + (1.0 - ADAM_B1) * gv
        vn = ADAM_B2 * v_ref[...] + (1.0 - ADAM_B2) * (gv * gv)
        m_hat = mn / (1.0 - ADAM_B1 ** ADAM_STEP)
        v_hat = vn / (1.0 - ADAM_B2 ** ADAM_STEP)
        d_ref[...] = -ADAM_LR * (m_hat / (jnp.sqrt(v_hat) + ADAM_EPS) + ADAM_WD * w_ref[...])
        mo_ref[...] = mn
        vo_ref[...] = vn

    spec = pl.BlockSpec((tr, cols), lambda i: (i, 0))
    outs = pl.pallas_call(
        kern, name="adamw", grid=(rows // tr,), in_specs=[spec] * 4, out_specs=[spec] * 3,
        out_shape=[jax.ShapeDtypeStruct((rows, cols), F32)] * 3, compiler_params=_params("arbitrary"),
    )(*[a.reshape(rows, cols) for a in (w, g, m, v)])
    return [o.reshape(shape) for o in outs]


def _pack(arrs, n_lead, row_mult, width, total_mult=PACK_ROW_MULT):
    parts, total = [], 0
    for n, a in enumerate(arrs):
        lead = a.shape[:n_lead]
        flat = a.reshape(lead + (-1,))
        size = flat.shape[-1]
        rows = -(-size // (width * row_mult)) * row_mult
        if n == len(arrs) - 1:
            rows += -(total + rows) % total_mult
        total += rows
        if rows * width > size:
            flat = jnp.concatenate([flat, jnp.zeros(lead + (rows * width - size,), flat.dtype)], axis=n_lead)
        parts.append(flat.reshape(lead + (rows, width)))
    return jnp.concatenate(parts, axis=n_lead)


def _unpack(pack, shapes, n_lead, row_mult):
    outs, row = [], 0
    lead = pack.shape[:n_lead]
    width = pack.shape[-1]
    for shp in shapes:
        size = math.prod(shp)
        rows = -(-size // (width * row_mult)) * row_mult
        blk = lax.slice_in_dim(pack, row, row + rows, axis=n_lead)
        outs.append(blk.reshape(lead + (-1,))[..., :size].reshape(lead + tuple(shp)))
        row += rows
    return outs


def _to_words(a):
    return lax.bitcast_convert_type(a, BF16)


def _from_words(a):
    return lax.bitcast_convert_type(a, F32)


def _pad_axis(a, axis, size):
    pads = [(0, 0)] * a.ndim
    pads[axis] = (0, size - a.shape[axis])
    return jnp.pad(a, pads)


def _dense(name, s):
    if name.endswith("w_gate") or name.endswith("w_up"):
        _, nl, d, fs = s.shape
        return s.transpose(1, 2, 0, 3).reshape(nl, d, N_DEV * fs)
    if name.endswith("w_down"):
        _, nl, fs, d = s.shape
        return s.transpose(1, 0, 2, 3).reshape(nl, N_DEV * fs, d)
    if name == "pool_w":
        _, nl, ng, r, cg = s.shape
        return s.transpose(1, 2, 0, 3, 4).reshape(nl, ng, cg, cg)
    if name == "w_dkv":
        w = s.reshape(-1, s.shape[2])
        z = lambda n: jnp.zeros((w.shape[0], n), w.dtype)
        return jnp.concatenate([w[:, :KV_RANK], z(ROPE_LANE0), w[:, KV_RANK:],
                                z(HEAD_LANES - ROPE_LANE0 - QK_ROPE)], axis=1)
    if name in ("w_uk", "w_uv"):
        return _pad_axis(s.transpose(1, 0, 2), 2, HEAD_LANES).reshape(KV_RANK, N_HEADS * HEAD_LANES)
    if name == "w_dq":
        _, nl, ds, r = s.shape
        return s.transpose(1, 0, 2, 3).reshape(nl, N_DEV * ds, r)
    if name == "w_uq":
        nl = s.shape[1]
        return _pad_axis(s.transpose(1, 2, 0, 3), 3, HEAD_LANES).reshape(nl, Q_RANK, N_HEADS * HEAD_LANES)
    if name == "w_o":
        _, nl, k, dc = s.shape
        w = s.transpose(1, 2, 0, 3).reshape(nl, N_HEADS, V_HEAD, N_DEV * dc)
        return _pad_axis(w, 2, HEAD_LANES).reshape(nl, N_HEADS * HEAD_LANES, N_DEV * dc)
    if name in ("meta_tokens", "pool_scale"):
        r, dc = s.shape[1:]
        return s.transpose(1, 0, 2).reshape(r, N_DEV * dc)
    raise ValueError(name)


def _shards(name, g):
    if name.endswith("w_gate") or name.endswith("w_up"):
        nl, d, f = g.shape
        return g.reshape(nl, d, N_DEV, f // N_DEV).transpose(2, 0, 1, 3)
    if name.endswith("w_down"):
        nl, f, d = g.shape
        return g.reshape(nl, N_DEV, f // N_DEV, d).transpose(1, 0, 2, 3)
    if name == "pool_w":
        nl, ng, cg, _ = g.shape
        return g.reshape(nl, ng, N_DEV, cg // N_DEV, cg).transpose(2, 0, 1, 3, 4)
    if name == "w_dkv":
        w = jnp.concatenate([g[:, :KV_RANK], g[:, KV_RANK + ROPE_LANE0:KV_RANK + ROPE_LANE0 + QK_ROPE]], axis=1)
        return w.reshape(N_DEV, -1, KV_RANK + QK_ROPE)
    if name in ("w_uk", "w_uv"):
        return g.reshape(KV_RANK, N_HEADS, HEAD_LANES)[:, :, :V_HEAD].transpose(1, 0, 2)
    if name == "w_dq":
        nl, d, r = g.shape
        return g.reshape(nl, N_DEV, d // N_DEV, r).transpose(1, 0, 2, 3)
    if name == "w_uq":
        nl = g.shape[0]
        return g.reshape(nl, Q_RANK, N_HEADS, HEAD_LANES)[..., :QK_NOPE + QK_ROPE].transpose(2, 0, 1, 3)
    if name == "w_o":
        nl, _, d = g.shape
        w = g.reshape(nl, N_HEADS, HEAD_LANES, d)[:, :, :V_HEAD].reshape(nl, N_HEADS * V_HEAD, N_DEV, d // N_DEV)
        return w.transpose(2, 0, 1, 3)
    if name in ("meta_tokens", "pool_scale"):
        r, d = g.shape
        return g.reshape(r, N_DEV, d // N_DEV).transpose(1, 0, 2)
    raise ValueError(name)


def _rope_tables(L):
    pos = jnp.maximum(jnp.arange(L) - FRONT_PAD, 0).astype(F32)
    inv = 1.0 / (ROPE_THETA ** (jnp.arange(0, QK_ROPE, 2, dtype=F32) / QK_ROPE))
    ang = pos[:, None] * inv[None, :]
    cos, sin = jnp.cos(ang), jnp.sin(ang)
    half = QK_ROPE // 2
    z = lambda n: jnp.zeros((L, n), F32)
    tail = z(HEAD_LANES - ROPE_LANE0 - QK_ROPE)
    return {
        "cq": jnp.concatenate([jnp.ones((L, ROPE_LANE0), F32), cos, cos, tail], axis=1),
        "ck": jnp.concatenate([z(ROPE_LANE0), cos, cos, tail], axis=1),
        "s1": jnp.concatenate([z(ROPE_LANE0), -sin, z(half), tail], axis=1),
        "s2": jnp.concatenate([z(ROPE_LANE0), z(half), sin, tail], axis=1),
    }


def kernel(x, meta_tokens, ffn1_norm, ffn1_w_gate, ffn1_w_up, ffn1_w_down, mix_norm, ffn2_norm, ffn2_w_gate, ffn2_w_up, ffn2_w_down, pool_w, pool_scale, kv_in_norm, w_dkv, kv_latent_norm, w_uk, w_uv, w_dq, q_latent_norm, w_uq, w_o, final_norm, loss_target, m_meta_tokens, m_ffn1_norm, m_ffn1_w_gate, m_ffn1_w_up, m_ffn1_w_down, m_mix_norm, m_ffn2_norm, m_ffn2_w_gate, m_ffn2_w_up, m_ffn2_w_down, m_pool_w, m_pool_scale, m_kv_in_norm, m_w_dkv, m_kv_latent_norm, m_w_uk, m_w_uv, m_w_dq, m_q_latent_norm, m_w_uq, m_w_o, m_final_norm, v_meta_tokens, v_ffn1_norm, v_ffn1_w_gate, v_ffn1_w_up, v_ffn1_w_down, v_mix_norm, v_ffn2_norm, v_ffn2_w_gate, v_ffn2_w_up, v_ffn2_w_down, v_pool_w, v_pool_scale, v_kv_in_norm, v_w_dkv, v_kv_latent_norm, v_w_uk, v_w_uv, v_w_dq, v_q_latent_norm, v_w_uq, v_w_o, v_final_norm):
    args = dict(locals())
    W = {n: args[n] for n in WEIGHTS}
    M = {n: args["m_" + n] for n in WEIGHTS}
    V = {n: args["v_" + n] for n in WEIGHTS}
    seq, D = x.shape[1], x.shape[2]
    L = SEQ_START + seq

    fs = ffn1_w_down.shape[1]

    ffns = [(l, which) for l in range(DEPTH) for which in (1, 2)]
    ffn_names = lambda which: FFN_WEIGHTS[3 * (which - 1):3 * which]

    def weight_piece(l, which):
        parts = [(W[n][l] if n.endswith("w_down") else W[n][l].T).astype(BF16) for n in ffn_names(which)]
        if (l, which) == ffns[0]:
            parts += [_to_words(W[n]) if n in SHARDED_F32 else W[n].astype(BF16) for n in SMALL_SHARDED]
        return _pack(parts, 0, 16, D, 16)

    pieces = {k: weight_piece(*k) for k in ffns}
    full = {}
    P = {}

    def take_piece(k, gathered):
        full[k] = _with_own(gathered, pieces[k])
        if k == ffns[0]:
            shapes = [(fs, D)] * 3 + [W[n].shape + ((2,) if n in SHARDED_F32 else ()) for n in SMALL_SHARDED]
            for n, s in zip(SMALL_SHARDED, _unpack(full[k], shapes, 1, 16)[3:]):
                P[n] = _dense(n, _from_words(s) if n in SHARDED_F32 else s)

    take_piece(ffns[0], _run_side(_gather_side(pieces[ffns[0]]), "all_gather_first")[0])
    norm3 = lambda a: a.reshape(a.shape[0], 1, a.shape[-1])
    row = lambda a: a.reshape(1, -1)
    g_ffn, g_mix = {1: norm3(ffn1_norm), 2: norm3(ffn2_norm)}, mix_norm
    ffn_entries = (0, 1, 2)

    def ffn_forward(h, l, which):
        at = ffns.index((l, which))
        nxt = ffns[at + 1] if at + 1 < len(ffns) else None
        side = None if nxt is None else _gather_side(pieces[nxt])
        outs = _ffn_fwd(h, g_ffn[which][l:l + 1], full[(l, which)], ffn_entries, fs, side)
        if nxt is not None:
            take_piece(nxt, outs[4])
        return outs[:4]

    h = jnp.concatenate([jnp.zeros((FRONT_PAD, D), F32), P["meta_tokens"], x[0]], axis=0)
    target = jnp.concatenate([jnp.zeros((SEQ_START, D), F32), loss_target[0]], axis=0)
    tabs = _rope_tables(L)
    saved = []
    kv = None
    for l in range(DEPTH):
        s = {"h1": h}
        h, s["xn1"], s["g1"], s["u1"] = ffn_forward(h, l, 1)
        s["hm"] = h
        if l < N_POOL_LAYERS:
            h = _pool_fwd(h, row(g_mix[l]), P["pool_w"][l], row(P["pool_scale"][l]), l)
        else:
            j = l - N_POOL_LAYERS
            s["q"], s["cqp"] = _q_fwd(h, tabs, row(g_mix[l]), P["w_dq"][j], row(q_latent_norm[j]), P["w_uq"][j])
            s["o"], s["lse"] = _attn_fwd(s["q"], kv["kn"], kv["kr"], kv["v"])
            h = _oproj_fwd(h, s["o"], P["w_o"][j])
        s["h2"] = h
        h, s["xn2"], s["g2"], s["u2"] = ffn_forward(h, l, 2)
        if l == N_POOL_LAYERS - 1:
            kv = {"h": h}
            kv["kn"], kv["kr"], kv["v"], kv["ckr"] = _kv_fwd(
                h, tabs, row(kv_in_norm), P["w_dkv"], row(kv_latent_norm), P["w_uk"], P["w_uv"])
        saved.append(s)
    dh, loss_row, d_final = _head(h, target, row(final_norm))
    loss = lax.psum(loss_row[0, 0], ("x", "y", "c"))

    G = {}
    stack = {n: [None] * DEPTH for n in ("ffn1_norm", "ffn1_w_gate", "ffn1_w_up", "ffn1_w_down", "mix_norm",
                                         "ffn2_norm", "ffn2_w_gate", "ffn2_w_up", "ffn2_w_down")}
    pool_dw, pool_ds = [None] * N_POOL_LAYERS, [None] * N_POOL_LAYERS
    mla = {n: [None] * (DEPTH - N_POOL_LAYERS) for n in ("w_dq", "w_uq", "w_o", "q_latent_norm")}
    dks, dvs = [], []
    my_core = lax.axis_index("c").astype(jnp.int32).reshape(1)
    my_chip = (2 * lax.axis_index("x") + lax.axis_index("y")).astype(jnp.int32).reshape(1)
    layer_grads = {n: [None] * DEPTH for n in FFN_WEIGHTS}
    grads = {}

    by_owner = lambda g: g.reshape((N_CHIPS, 2, g.shape[0] // N_DEV) + g.shape[1:])

    def reduce_cores(arrs, others):
        return _add_own(arrs, my_core, others, "sum_cores")

    def finish_layer(l, partials, from_chips):
        for n, g in zip(FFN_WEIGHTS, _add_own(partials, my_chip, from_chips, "sum_chips")):
            layer_grads[n][l] = g if n.endswith("w_down") else g.T

    def ffn_backward(dh, s, which, l, side=None):
        outs = _ffn_bwd(dh, s["h%d" % which], s["g%d" % which], s["u%d" % which], g_ffn[which][l:l + 1],
                        full[(l, which)], ffn_entries, fs, side)
        dh, dg, du, act, dob, dgam = outs[:6]
        xn = s["xn%d" % which]
        stack["ffn%d_w_gate" % which][l] = _mm_tn(dg, xn, "ffn_dw")
        stack["ffn%d_w_up" % which][l] = _mm_tn(du, xn, "ffn_dw")
        stack["ffn%d_w_down" % which][l] = _mm_tn(act, dob, "ffn_dw")
        stack["ffn%d_norm" % which][l] = dgam
        return dh, outs[6:]

    arrs = None
    for l in reversed(range(DEPTH)):
        s = saved[l]
        if l == N_POOL_LAYERS - 1:
            dh, d_dkv, d_uk, d_uv, d_kvin, d_kvlat = _kv_bwd(
                dh, kv["h"], kv["ckr"], dks, dvs, tabs, row(kv_in_norm), P["w_dkv"], row(kv_latent_norm),
                P["w_uk"], P["w_uv"])
            G.update(w_dkv=d_dkv, w_uk=d_uk, w_uv=d_uv, kv_in_norm=d_kvin, kv_latent_norm=d_kvlat)
        if arrs is None:
            dh, _ = ffn_backward(dh, s, 2, l)
        else:
            dh, from_sibling = ffn_backward(dh, s, 2, l, _sibling_side(arrs))
            partials, partials_bf16 = reduce_cores(arrs, from_sibling)
        if l < N_POOL_LAYERS:
            dh, pool_dw[l], pool_ds[l], stack["mix_norm"][l] = _pool_bwd(
                dh, s["hm"], row(g_mix[l]), P["pool_w"][l], row(P["pool_scale"][l]))
        else:
            j = l - N_POOL_LAYERS
            do, delta_o, mla["w_o"][j] = _oproj_bwd(dh, s["o"], P["w_o"][j])
            dq, dk, dv = _attn_bwd(s["q"], kv["kn"], kv["kr"], kv["v"], do, s["lse"], delta_o)
            dks.append(dk)
            dvs.append(dv)
            dh, mla["w_dq"][j], mla["w_uq"][j], stack["mix_norm"][l], mla["q_latent_norm"][j] = _q_bwd(
                dh, s["hm"], s["cqp"], dq, tabs, row(g_mix[l]), P["w_dq"][j], row(q_latent_norm[j]), P["w_uq"][j])
        if arrs is None:
            dh, _ = ffn_backward(dh, s, 1, l)
        else:
            dh, from_chips = ffn_backward(dh, s, 1, l, _chip_side(partials_bf16))
            finish_layer(l + 1, partials, from_chips)
        arrs = [by_owner(stack[n][l]) for n in FFN_WEIGHTS]
    grad_x = dh[SEQ_START:][None]
    for n in ("ffn1_norm", "mix_norm", "ffn2_norm"):
        G[n] = jnp.concatenate(stack[n], axis=0)
    G["pool_w"] = jnp.stack(pool_dw)
    G["pool_scale"] = jnp.concatenate(pool_ds, axis=0)
    G["w_dq"], G["w_uq"], G["w_o"] = (jnp.stack(mla[n]) for n in ("w_dq", "w_uq", "w_o"))
    G["q_latent_norm"] = jnp.concatenate(mla["q_latent_norm"], axis=0)
    G["meta_tokens"] = dh[FRONT_PAD:SEQ_START]
    G["final_norm"] = d_final

    spack = _pack([_shards(n, G[n]) for n in SMALL_SHARDED], 1, 8, D)
    spack = spack.reshape((N_CHIPS, 2) + spack.shape[1:])
    others = _run_side(_sibling_side(arrs + [spack]), "sibling_exchange_last")
    partials, partials_bf16 = reduce_cores(arrs, others[:-1])
    small, small_bf16 = reduce_cores([spack], others[-1:])
    from_chips = _run_side(_chip_side(list(partials_bf16) + list(small_bf16)), "chip_exchange_last")
    finish_layer(0, partials, from_chips[:-1])
    small_mine = _add_own(small, my_chip, from_chips[-1:], "sum_chips")[0]
    grads.update(zip(SMALL_SHARDED, _unpack(small_mine, [W[n].shape for n in SMALL_SHARDED], 0, 8)))
    for n in FFN_WEIGHTS:
        grads[n] = jnp.stack(layer_grads[n])
    rep_shapes = [W[n].shape for n in REPLICATED]
    rpack = _pack([G[n].reshape(W[n].shape) for n in REPLICATED], 0, 8, D)
    everyones = _with_own(_run_side(_gather_side(rpack), "all_gather_norm_grads")[0], rpack)
    grads.update(zip(REPLICATED, _unpack(_sum_lead(everyones, "sum_devices"), rep_shapes, 0, 8)))

    delta, new_m, new_v = {}, {}, {}
    for n in WEIGHTS:
        delta[n], new_m[n], new_v[n] = _adamw(W[n], grads[n], M[n], V[n])
    return (loss, grad_x, *[grads[n] for n in WEIGHTS], *[delta[n] for n in WEIGHTS],
            *[new_m[n] for n in WEIGHTS], *[new_v[n] for n in WEIGHTS])
```

```python
import functools
import math

import jax
import jax.numpy as jnp
from jax import lax
from jax.experimental import pallas as pl
from jax.experimental.pallas import tpu as pltpu

F32 = jnp.float32
BF16 = jnp.bfloat16
MESH = pl.DeviceIdType.MESH

N_DEV = 8
N_CHIPS = 4
DEPTH = 4
N_POOL_LAYERS = 2
N_HEADS = 8
QK_NOPE = 64
QK_ROPE = 32
V_HEAD = 64
KV_RANK = 256
Q_RANK = 384
HEAD_LANES = 128
HEADS = "heads"
ROPE_LANE0 = QK_NOPE
BIAS_LANE = QK_NOPE + QK_ROPE
ONES_LANE = V_HEAD
LOG2E = math.log2(math.e)
N_META = 16
CHUNK_SHIFT = 6
FRONT_PAD = 112
SEQ_START = FRONT_PAD + N_META
HALO = 16
POOL_WINDOWS = (2, 4, 8, 16)
EPS = 1e-6
ROPE_THETA = 10000.0
NEG = -1e30
PACK_ROW_MULT = 256
VMEM_LIMIT = 56 * 1024 * 1024

ADAM_LR = 0.001
ADAM_B1 = 0.9
ADAM_B2 = 0.999
ADAM_EPS = 1e-08
ADAM_WD = 0.01
ADAM_STEP = 10

SHARDED = ["ffn1_w_gate", "ffn1_w_up", "ffn1_w_down", "ffn2_w_gate", "ffn2_w_up", "ffn2_w_down",
           "pool_w", "w_dkv", "w_uk", "w_uv", "w_dq", "w_uq", "w_o", "meta_tokens", "pool_scale"]
FFN_WEIGHTS = SHARDED[:6]
SMALL_SHARDED = SHARDED[6:]
SHARDED_F32 = ("meta_tokens", "pool_scale")
REPLICATED = ["ffn1_norm", "mix_norm", "ffn2_norm", "kv_in_norm", "kv_latent_norm", "q_latent_norm",
              "final_norm"]
WEIGHTS = ['meta_tokens', 'ffn1_norm', 'ffn1_w_gate', 'ffn1_w_up', 'ffn1_w_down', 'mix_norm', 'ffn2_norm',
           'ffn2_w_gate', 'ffn2_w_up', 'ffn2_w_down', 'pool_w', 'pool_scale', 'kv_in_norm', 'w_dkv',
           'kv_latent_norm', 'w_uk', 'w_uv', 'w_dq', 'q_latent_norm', 'w_uq', 'w_o', 'final_norm']


def _dot(a, b):
    return jnp.dot(a.astype(BF16), b.astype(BF16), preferred_element_type=F32)


def _dot_nt(a, b):
    return lax.dot_general(a.astype(BF16), b.astype(BF16), (((1,), (1,)), ((), ())),
                           preferred_element_type=F32)


def _dot_tn(a, b):
    return lax.dot_general(a.astype(BF16), b.astype(BF16), (((0,), (0,)), ((), ())),
                           preferred_element_type=F32)


def _sigmoid(x):
    return 1.0 / (1.0 + jnp.exp(-x))


def _rms(x, g):
    r = lax.rsqrt(jnp.mean(x * x, axis=-1, keepdims=True) + EPS)
    xh = x * r
    return xh * g, xh, r


def _rms_bwd(dy, xh, r, g):
    dxh = dy * g
    dx = r * (dxh - xh * jnp.mean(dxh * xh, axis=-1, keepdims=True))
    return dx, jnp.sum(dy * xh, axis=0, keepdims=True)


def _rope(x, c, s1, s2):
    return x * c + pltpu.roll(x, HEAD_LANES - QK_ROPE // 2, 1) * s1 + pltpu.roll(x, QK_ROPE // 2, 1) * s2


def _rope_t(d, c, s1, s2):
    return d * c + pltpu.roll(d * s1, QK_ROPE // 2, 1) + pltpu.roll(d * s2, HEAD_LANES - QK_ROPE // 2, 1)


def _params(*sem):
    return pltpu.CompilerParams(dimension_semantics=sem, vmem_limit_bytes=VMEM_LIMIT)


def _pick(n, candidates):
    for c in candidates:
        if n % c == 0:
            return c
    return n


def _row_tile(L):
    return _pick(L, (640, 128))


FF_OWNERS = 4


def _row_call(name, body, L, row_ins, full_ins, row_outs, acc_outs):
    tm = _row_tile(L)
    n = L // tm
    hb = tm // HALO
    nb = L // HALO
    per_head = (N_HEADS, tm, HEAD_LANES)
    in_specs, args = [], []
    for arr, kind in row_ins:
        c = arr.shape[-1]
        if kind == "tile":
            spec = pl.BlockSpec((tm, c), lambda i: (i, 0))
        elif kind == "heads":
            spec = pl.BlockSpec(per_head, lambda i: (0, i, 0))
        elif kind == "prev":
            spec = pl.BlockSpec((HALO, c), lambda i: (jnp.maximum(i * hb - 1, 0), 0))
        else:
            spec = pl.BlockSpec((HALO, c), lambda i: (jnp.minimum((i + 1) * hb, nb - 1), 0))
        in_specs.append(spec)
        args.append(arr)
    for arr in full_ins:
        in_specs.append(pl.BlockSpec(arr.shape, lambda i, nd=arr.ndim: (0,) * nd))
        args.append(arr)
    out_shape, out_specs = [], []
    for c, dt in row_outs:
        if c == HEADS:
            out_shape.append(jax.ShapeDtypeStruct((N_HEADS, L, HEAD_LANES), dt))
            out_specs.append(pl.BlockSpec(per_head, lambda i: (0, i, 0)))
        else:
            out_shape.append(jax.ShapeDtypeStruct((L, c), dt))
            out_specs.append(pl.BlockSpec((tm, c), lambda i: (i, 0)))
    for shp in acc_outs:
        out_shape.append(jax.ShapeDtypeStruct(shp, F32))
        out_specs.append(pl.BlockSpec(shp, lambda i, nd=len(shp): (0,) * nd))
    n_in, n_ro = len(args), len(row_outs)

    def kern(*refs):
        i = pl.program_id(0)
        vals = [jnp.concatenate([r[hd] for hd in range(N_HEADS)], axis=1) if kind == "heads" else r[...]
                for r, (_, kind) in zip(refs, row_ins)] + [r[...] for r in refs[len(row_ins):n_in]]
        ro, ao = body(i, n, tm, *vals)
        for r, v in zip(refs[n_in:n_in + n_ro], ro):
            if len(r.shape) == 3:
                for hd in range(N_HEADS):
                    r[hd] = v[:, hd * HEAD_LANES:(hd + 1) * HEAD_LANES].astype(r.dtype)
            else:
                r[...] = v.astype(r.dtype)
        acc_refs = refs[n_in + n_ro:]

        @pl.when(i == 0)
        def _():
            for r in acc_refs:
                r[...] = jnp.zeros(r.shape, r.dtype)

        for r, v in zip(acc_refs, ao):
            r[...] += v

    return pl.pallas_call(kern, name=name, grid=(n,), in_specs=in_specs, out_specs=out_specs,
                          out_shape=out_shape, compiler_params=_params("arbitrary"))(*args)


class _Side:
    def __init__(self, ins, out_shapes, sems, start, finish):
        self.ins, self.out_shapes, self.sems, self.start, self.finish = ins, out_shapes, sems, start, finish


ANY = pl.BlockSpec(memory_space=pl.ANY)


def _hosted(kern, n_in, n_out, n_scratch, side, is_first, is_last):
    if side is None:
        return kern
    ns_in, ns_out = len(side.ins), len(side.out_shapes)

    def wrapped(*refs):
        ins, refs = refs[:n_in], refs[n_in:]
        side_ins, refs = refs[:ns_in], refs[ns_in:]
        outs, refs = refs[:n_out], refs[n_out:]
        side_outs, refs = refs[:ns_out], refs[ns_out:]
        scratch, side_sems = refs[:n_scratch], refs[n_scratch:]

        @pl.when(is_first())
        def _():
            side.start(side_ins, side_outs, side_sems)

        kern(*ins, *outs, *scratch)

        @pl.when(is_last())
        def _():
            side.finish(side_ins, side_outs, side_sems)

    return wrapped


def _side_args(side):
    if side is None:
        return [], [], [], [], []
    return ([ANY] * len(side.ins), [ANY] * len(side.out_shapes), list(side.out_shapes), list(side.sems),
            list(side.ins))


def _ffn_weight_specs(fs, D, ents):
    return [pl.BlockSpec((FF_OWNERS, fs, D), lambda i, f, e=e: (f, e, 0)) for e in ents]


def _ffn_fwd(h, gam, wpiece, ents, fs, side=None):
    L, D = h.shape
    F = N_DEV * fs
    tm = _row_tile(L)
    tf = FF_OWNERS * fs
    nL, nF = L // tm, F // tf
    s_in, s_out, s_shape, s_sems, s_args = _side_args(side)

    def kern(h_ref, gam_ref, wg_ref, wu_ref, wd_ref, ho_ref, xn_ref, gs_ref, us_ref, acc):
        f = pl.program_id(1)

        @pl.when(f == 0)
        def _():
            xn, _, _ = _rms(h_ref[...], gam_ref[...])
            xn_ref[...] = xn.astype(BF16)
            acc[...] = jnp.zeros(acc.shape, F32)

        xnb = xn_ref[...]
        g = _dot_nt(xnb, wg_ref[...].reshape(tf, D))
        u = _dot_nt(xnb, wu_ref[...].reshape(tf, D))
        gs_ref[...] = g.astype(BF16)
        us_ref[...] = u.astype(BF16)
        acc[...] += _dot(g * _sigmoid(g) * u, wd_ref[...].reshape(tf, D))

        @pl.when(f == nF - 1)
        def _():
            ho_ref[...] = h_ref[...] + 0.5 * acc[...]

    first = lambda: (pl.program_id(0) == 0) & (pl.program_id(1) == 0)
    last = lambda: (pl.program_id(0) == nL - 1) & (pl.program_id(1) == nF - 1)
    return pl.pallas_call(
        _hosted(kern, 5, 4, 1, side, first, last),
        name="ffn_fwd" if side is None else "ffn_fwd_hosting", grid=(nL, nF),
        in_specs=[pl.BlockSpec((tm, D), lambda i, f: (i, 0)),
                  pl.BlockSpec((None, 1, D), lambda i, f: (0, 0, 0))] + _ffn_weight_specs(fs, D, ents) + s_in,
        out_specs=[pl.BlockSpec((tm, D), lambda i, f: (i, 0)),
                   pl.BlockSpec((tm, D), lambda i, f: (i, 0)),
                   pl.BlockSpec((tm, tf), lambda i, f: (i, f)),
                   pl.BlockSpec((tm, tf), lambda i, f: (i, f))] + s_out,
        out_shape=[jax.ShapeDtypeStruct((L, D), F32), jax.ShapeDtypeStruct((L, D), BF16),
                   jax.ShapeDtypeStruct((L, F), BF16), jax.ShapeDtypeStruct((L, F), BF16)] + s_shape,
        scratch_shapes=[pltpu.VMEM((tm, D), F32)] + s_sems,
        compiler_params=_params("arbitrary", "arbitrary"))(h, gam, wpiece, wpiece, wpiece, *s_args)


def _ffn_bwd(dh, h, gs, us, gam, wpiece, ents, fs, side=None):
    L, D = h.shape
    F = N_DEV * fs
    tm = _pick(L, (416, 128))
    tf = FF_OWNERS * fs
    nL, nF = L // tm, F // tf
    s_in, s_out, s_shape, s_sems, s_args = _side_args(side)

    def kern(dh_ref, h_ref, gs_ref, us_ref, gam_ref, wg_ref, wu_ref, wd_ref,
             dhi_ref, dg_ref, du_ref, a_ref, dob_ref, dgam_ref, dxn):
        i = pl.program_id(0)
        f = pl.program_id(1)

        @pl.when(f == 0)
        def _():
            dxn[...] = jnp.zeros(dxn.shape, F32)
            dob_ref[...] = (0.5 * dh_ref[...]).astype(BF16)

        @pl.when((f == 0) & (i == 0))
        def _():
            dgam_ref[...] = jnp.zeros(dgam_ref.shape, F32)

        g = gs_ref[...].astype(F32)
        u = us_ref[...].astype(F32)
        sg = _sigmoid(g)
        silu = g * sg
        da = _dot_nt(dob_ref[...], wd_ref[...].reshape(tf, D))
        a_ref[...] = (silu * u).astype(BF16)
        dgt = (da * u * (sg * (1.0 + g * (1.0 - sg)))).astype(BF16)
        dut = (da * silu).astype(BF16)
        dg_ref[...] = dgt
        du_ref[...] = dut
        dxn[...] += _dot(dgt, wg_ref[...].reshape(tf, D)) + _dot(dut, wu_ref[...].reshape(tf, D))

        @pl.when(f == nF - 1)
        def _():
            gamma = gam_ref[...]
            _, xh, r = _rms(h_ref[...], gamma)
            dx, dgam = _rms_bwd(dxn[...], xh, r, gamma)
            dhi_ref[...] = dh_ref[...] + dx
            dgam_ref[...] += dgam

    first = lambda: (pl.program_id(0) == 0) & (pl.program_id(1) == 0)
    last = lambda: (pl.program_id(0) == nL - 1) & (pl.program_id(1) == nF - 1)
    return pl.pallas_call(
        _hosted(kern, 8, 6, 1, side, first, last),
        name="ffn_bwd" if side is None else "ffn_bwd_hosting", grid=(nL, nF),
        in_specs=[pl.BlockSpec((tm, D), lambda i, f: (i, 0)),
                  pl.BlockSpec((tm, D), lambda i, f: (i, 0)),
                  pl.BlockSpec((tm, tf), lambda i, f: (i, f)),
                  pl.BlockSpec((tm, tf), lambda i, f: (i, f)),
                  pl.BlockSpec((None, 1, D), lambda i, f: (0, 0, 0))] + _ffn_weight_specs(fs, D, ents) + s_in,
        out_specs=[pl.BlockSpec((tm, D), lambda i, f: (i, 0)),
                   pl.BlockSpec((tm, tf), lambda i, f: (i, f)),
                   pl.BlockSpec((tm, tf), lambda i, f: (i, f)),
                   pl.BlockSpec((tm, tf), lambda i, f: (i, f)),
                   pl.BlockSpec((tm, D), lambda i, f: (i, 0)),
                   pl.BlockSpec((1, D), lambda i, f: (0, 0))] + s_out,
        out_shape=[jax.ShapeDtypeStruct((L, D), F32), jax.ShapeDtypeStruct((L, F), BF16),
                   jax.ShapeDtypeStruct((L, F), BF16), jax.ShapeDtypeStruct((L, F), BF16),
                   jax.ShapeDtypeStruct((L, D), BF16), jax.ShapeDtypeStruct((1, D), F32)] + s_shape,
        scratch_shapes=[pltpu.VMEM((tm, D), F32)] + s_sems,
        compiler_params=_params("arbitrary", "arbitrary"))(dh, h, gs, us, gam, wpiece, wpiece, wpiece, *s_args)


def _mm_tn(a, b, name):
    L, M = a.shape
    N = b.shape[1]
    tm = _pick(M, (1408, 1024, 512))
    tn = _pick(N, (1408, 1024, 512))
    tk = _pick(L, (2080, 640, 128))

    def kern(a_ref, b_ref, o_ref):
        @pl.when(pl.program_id(2) == 0)
        def _():
            o_ref[...] = jnp.zeros(o_ref.shape, F32)

        o_ref[...] += _dot_tn(a_ref[...], b_ref[...])

    return pl.pallas_call(
        kern, name=name, grid=(M // tm, N // tn, L // tk),
        in_specs=[pl.BlockSpec((tk, tm), lambda i, j, k: (k, i)),
                  pl.BlockSpec((tk, tn), lambda i, j, k: (k, j))],
        out_specs=pl.BlockSpec((tm, tn), lambda i, j, k: (i, j)),
        out_shape=jax.ShapeDtypeStruct((M, N), F32),
        compiler_params=_params("arbitrary", "arbitrary", "arbitrary"))(a, b)


def _pool_counts(pos, w):
    return jnp.clip(pos - (FRONT_PAD - 1), 1, w).astype(F32)


def _pool_forward_values(i, tm, h, hprev, gamma, D):
    cg = D // len(POOL_WINDOWS)
    hext = jnp.concatenate([hprev, h], axis=0)
    uext, xh, r = _rms(hext, gamma)
    pos = i * tm + lax.broadcasted_iota(jnp.int32, (tm, 1), 0)
    pooled = []
    for gi, w in enumerate(POOL_WINDOWS):
        s = uext[:, gi * cg:(gi + 1) * cg]
        span = 1
        while span < w:
            s = s + pltpu.roll(s, span, 0)
            span *= 2
        s = s[HALO:]
        pooled.append(s / _pool_counts(pos, w) - uext[HALO:, gi * cg:(gi + 1) * cg])
    return uext, xh[HALO:], r[HALO:], pooled


def _pool_fwd(h, gam, w, scale, l):
    L, D = h.shape
    cg = D // len(POOL_WINDOWS)

    def body(i, n, tm, ht, hprev, gamma, wv, sc):
        _, _, _, pooled = _pool_forward_values(i, tm, ht, hprev, gamma, D)
        ys = [_dot(pooled[gi], wv[gi]) for gi in range(len(POOL_WINDOWS))]
        y = jnp.concatenate(ys, axis=1) * sc
        return [ht + y], []

    del cg
    return _row_call("pool_fwd", body, L, [(h, "tile"), (h, "prev")], [gam, w, scale], [(D, F32)], [])[0]


def _pool_bwd(dy, h, gam, w, scale):
    L, D = h.shape
    ng = len(POOL_WINDOWS)
    cg = D // ng

    def body(i, n, tm, ht, hprev, dyt, dynext, gamma, wv, sc):
        _, xh, r, pooled = _pool_forward_values(i, tm, ht, hprev, gamma, D)
        dynext = jnp.where(i == n - 1, jnp.zeros_like(dynext), dynext)
        dyext = jnp.concatenate([dyt, dynext], axis=0) * sc
        pos_ext = i * tm + lax.broadcasted_iota(jnp.int32, (tm + HALO, 1), 0)
        dws, dscs, dus = [], [], []
        for gi, wd in enumerate(POOL_WINDOWS):
            cols = slice(gi * cg, (gi + 1) * cg)
            pb = pooled[gi].astype(BF16)
            ypre = _dot(pb, wv[gi])
            dscs.append(jnp.sum(dyt[:, cols] * ypre, axis=0, keepdims=True))
            dws.append(_dot_tn(pb, dyext[:tm, cols])[None])
            dp = _dot_nt(dyext[:, cols], wv[gi])
            s = dp / _pool_counts(pos_ext, wd)
            span = 1
            while span < wd:
                s = s + pltpu.roll(s, tm + HALO - span, 0)
                span *= 2
            dus.append(s[:tm] - dp[:tm])
        du = jnp.concatenate(dus, axis=1)
        pos = pos_ext[:tm]
        du = jnp.where(pos >= FRONT_PAD, du, 0.0)
        dx, dgam = _rms_bwd(du, xh, r, gamma)
        return [dyt + dx], [jnp.concatenate(dws, axis=0), jnp.concatenate(dscs, axis=1), dgam]

    return _row_call("pool_bwd", body, L, [(h, "tile"), (h, "prev"), (dy, "tile"), (dy, "next")],
                     [gam, w, scale], [(D, F32)], [(ng, cg, cg), (1, D), (1, D)])


def _kv_fwd(h, tabs, g1, wdkv, g2, wuk, wuv):
    L, D = h.shape
    hw = N_HEADS * HEAD_LANES

    def body(i, n, tm, ht, ck, s1, s2, g1v, wdkv_v, g2v, wuk_v, wuv_v):
        xkv, _, _ = _rms(ht, g1v)
        ckr = _dot(xkv, wdkv_v)
        ckv, _, _ = _rms(ckr[:, :KV_RANK], g2v)
        krope = _rope(ckr[:, KV_RANK:], ck, s1, s2)
        pos = i * tm + lax.broadcasted_iota(jnp.int32, (tm, HEAD_LANES), 0)
        lane = lax.broadcasted_iota(jnp.int32, (tm, HEAD_LANES), 1)
        krope = jnp.where((pos < FRONT_PAD) & (lane == BIAS_LANE), NEG, krope)
        ones = ((lax.broadcasted_iota(jnp.int32, (1, hw), 1) & (HEAD_LANES - 1)) == ONES_LANE).astype(F32)
        return [_dot(ckv, wuk_v), krope, _dot(ckv, wuv_v) + ones, ckr], []

    ck, s1, s2 = tabs["ck"], tabs["s1"], tabs["s2"]
    return _row_call("kv_fwd", body, L, [(h, "tile"), (ck, "tile"), (s1, "tile"), (s2, "tile")],
                     [g1, wdkv, g2, wuk, wuv],
                     [(HEADS, BF16), (HEAD_LANES, BF16), (HEADS, BF16), (KV_RANK + HEAD_LANES, F32)], [])


def _kv_bwd(dh, h, ckr, dks, dvs, tabs, g1, wdkv, g2, wuk, wuv):
    L, D = h.shape
    hw = N_HEADS * HEAD_LANES
    nl = len(dks)

    def body(i, n, tm, *vals):
        dht, ht, ckr_t = vals[:3]
        dk = sum(vals[3:3 + nl][1:], vals[3])
        dv = sum(vals[3 + nl:3 + 2 * nl][1:], vals[3 + nl])
        ck, s1, s2, g1v, wdkv_v, g2v, wuk_v, wuv_v = vals[3 + 2 * nl:]
        xkv, xh1, r1 = _rms(ht, g1v)
        ckv, xh2, r2 = _rms(ckr_t[:, :KV_RANK], g2v)
        dckv = _dot_nt(dk, wuk_v) + _dot_nt(dv, wuv_v)
        dlat, dg2 = _rms_bwd(dckv, xh2, r2, g2v)
        dkr = dk[:, :HEAD_LANES]
        for hd in range(1, N_HEADS):
            dkr = dkr + dk[:, hd * HEAD_LANES:(hd + 1) * HEAD_LANES]
        dckr = jnp.concatenate([dlat, _rope_t(dkr, ck, s1, s2)], axis=1)
        dx, dg1 = _rms_bwd(_dot_nt(dckr, wdkv_v), xh1, r1, g1v)
        return [dht + dx], [_dot_tn(xkv, dckr), _dot_tn(ckv, dk), _dot_tn(ckv, dv), dg1, dg2]

    row_ins = [(dh, "tile"), (h, "tile"), (ckr, "tile")] + [(a, "heads") for a in dks + dvs]
    row_ins += [(tabs[k], "tile") for k in ("ck", "s1", "s2")]
    return _row_call("kv_bwd", body, L, row_ins, [g1, wdkv, g2, wuk, wuv], [(D, F32)],
                     [(D, KV_RANK + HEAD_LANES), (KV_RANK, hw), (KV_RANK, hw), (1, D), (1, KV_RANK)])


def _q_fwd(h, tabs, g, wdq, gq, wuq):
    L, D = h.shape
    hw = N_HEADS * HEAD_LANES

    def body(i, n, tm, ht, cq_t, s1, s2, gv, wdq_v, gqv, wuq_v):
        u, _, _ = _rms(ht, gv)
        cqp = _dot(u, wdq_v)
        cq, _, _ = _rms(cqp, gqv)
        qp = _dot(cq, wuq_v)
        bias = (lax.broadcasted_iota(jnp.int32, (1, HEAD_LANES), 1) == BIAS_LANE).astype(F32)
        q = [_rope(qp[:, hd * HEAD_LANES:(hd + 1) * HEAD_LANES], cq_t, s1, s2) * (SM_SCALE * LOG2E) + bias
             for hd in range(N_HEADS)]
        return [jnp.concatenate(q, axis=1), cqp], []

    return _row_call("q_fwd", body, L, [(h, "tile")] + [(tabs[k], "tile") for k in ("cq", "s1", "s2")],
                     [g, wdq, gq, wuq], [(HEADS, BF16), (Q_RANK, F32)], [])


def _q_bwd(dh, h, cqp, dq, tabs, g, wdq, gq, wuq):
    L, D = h.shape
    hw = N_HEADS * HEAD_LANES

    def body(i, n, tm, dht, ht, cqp_t, dq_t, cq_t, s1, s2, gv, wdq_v, gqv, wuq_v):
        u, xh1, r1 = _rms(ht, gv)
        cq, xh2, r2 = _rms(cqp_t, gqv)
        dqp = jnp.concatenate([_rope_t(dq_t[:, hd * HEAD_LANES:(hd + 1) * HEAD_LANES], cq_t, s1, s2)
                               for hd in range(N_HEADS)], axis=1)
        dcqp, dgq = _rms_bwd(_dot_nt(dqp, wuq_v), xh2, r2, gqv)
        dx, dg = _rms_bwd(_dot_nt(dcqp, wdq_v), xh1, r1, gv)
        return [dht + dx], [_dot_tn(u, dcqp), _dot_tn(cq, dqp), dg, dgq]

    row_ins = [(dh, "tile"), (h, "tile"), (cqp, "tile"), (dq, "heads")]
    row_ins += [(tabs[k], "tile") for k in ("cq", "s1", "s2")]
    return _row_call("q_bwd", body, L, row_ins, [g, wdq, gq, wuq], [(D, F32)],
                     [(D, Q_RANK), (Q_RANK, hw), (1, D), (1, Q_RANK)])


def _oproj_fwd(h, o, wo):
    L, D = h.shape

    def body(i, n, tm, ht, ot, wov):
        return [ht + _dot(ot, wov)], []

    return _row_call("oproj_fwd", body, L, [(h, "tile"), (o, "heads")], [wo], [(D, F32)], [])[0]


def _oproj_bwd(dh, o, wo):
    L, D = dh.shape
    hw = N_HEADS * HEAD_LANES

    def body(i, n, tm, dht, ot, wov):
        do = _dot_nt(dht, wov)
        prod = do * ot.astype(F32)
        delta = [jnp.broadcast_to(jnp.sum(prod[:, hd * HEAD_LANES:(hd + 1) * HEAD_LANES], axis=-1, keepdims=True),
                                  (tm, HEAD_LANES)) for hd in range(N_HEADS)]
        return [do, jnp.concatenate(delta, axis=1)], [_dot_tn(ot, dht)]

    return _row_call("oproj_bwd", body, L, [(dh, "tile"), (o, "heads")], [wo], [(HEADS, BF16), (HEADS, F32)],
                     [(hw, D)])


def _causal(t, keys_first=False):
    qpos = lax.broadcasted_iota(jnp.int32, (t, t), 1 if keys_first else 0)
    kpos = lax.broadcasted_iota(jnp.int32, (t, t), 0 if keys_first else 1)
    return (kpos >> CHUNK_SHIFT) <= (qpos >> CHUNK_SHIFT)


SM_SCALE = 1.0 / math.sqrt(QK_NOPE + QK_ROPE)


def _pairs(n, key_major):
    if key_major:
        order = [(i, j) for j in range(n) for i in range(j, n)]
    else:
        order = [(i, j) for i in range(n) for j in range(i + 1)]
    return (jnp.array([p[0] for p in order], jnp.int32), jnp.array([p[1] for p in order], jnp.int32))


def _attn_fwd(q, kn, kr, v):
    L = q.shape[1]
    hw = N_HEADS * HEAD_LANES
    t = _row_tile(L)
    it, jt = _pairs(L // t, key_major=False)

    def kern(it_ref, jt_ref, q_ref, kn_ref, kr_ref, v_ref, o_ref, lse_ref, m_s, acc_s):
        step = pl.program_id(1)
        i, j = it_ref[step], jt_ref[step]

        @pl.when(j == 0)
        def _():
            m_s[...] = jnp.full(m_s.shape, NEG, F32)
            acc_s[...] = jnp.zeros(acc_s.shape, F32)

        def update(diagonal):
            k = kn_ref[...] + kr_ref[...]
            s = _dot_nt(q_ref[...], k)
            if diagonal:
                s = jnp.where(_causal(t), s, NEG)
            m_prev = m_s[:, :1]
            m_new = jnp.maximum(m_prev, jnp.max(s, axis=-1, keepdims=True))
            p = jnp.exp2(s - m_new)
            acc_s[...] = jnp.exp2(m_prev - m_new) * acc_s[...] + _dot(p, v_ref[...])
            m_s[...] = jnp.broadcast_to(m_new, m_s.shape)

        @pl.when(j < i)
        def _():
            update(False)

        @pl.when(j == i)
        def _():
            update(True)
            acc = acc_s[...]
            total = acc[:, ONES_LANE:ONES_LANE + 1]
            o_ref[...] = (acc / total).astype(BF16)
            lse_ref[...] = m_s[...] + jnp.log2(jnp.broadcast_to(total, m_s.shape))

    qmap = lambda h, s, it, jt: (h, it[s], 0)
    kmap = lambda h, s, it, jt: (h, jt[s], 0)
    blk = (t, HEAD_LANES)
    hblk = (None, t, HEAD_LANES)
    return pl.pallas_call(
        kern, name="attn_fwd",
        grid_spec=pltpu.PrefetchScalarGridSpec(
            num_scalar_prefetch=2, grid=(N_HEADS, it.shape[0]),
            in_specs=[pl.BlockSpec(hblk, qmap), pl.BlockSpec(hblk, kmap),
                      pl.BlockSpec(blk, lambda h, s, it, jt: (jt[s], 0)), pl.BlockSpec(hblk, kmap)],
            out_specs=[pl.BlockSpec(hblk, qmap), pl.BlockSpec(hblk, qmap)],
            scratch_shapes=[pltpu.VMEM(blk, F32)] * 2),
        out_shape=[jax.ShapeDtypeStruct((N_HEADS, L, HEAD_LANES), BF16),
                   jax.ShapeDtypeStruct((N_HEADS, L, HEAD_LANES), F32)],
        compiler_params=_params("arbitrary", "arbitrary"))(it, jt, q, kn, kr, v)


def _attn_bwd(q, kn, kr, v, do, lse, delta):
    L = q.shape[1]
    hw = N_HEADS * HEAD_LANES
    t = _row_tile(L)
    it, jt = _pairs(L // t, key_major=True)

    def kern(it_ref, jt_ref, q_ref, kn_ref, kr_ref, v_ref, do_ref, lse_ref, dl_ref, dq_ref, dk_ref, dv_ref):
        step = pl.program_id(1)
        i, j = it_ref[step], jt_ref[step]

        @pl.when(step == 0)
        def _():
            dq_ref[...] = jnp.zeros(dq_ref.shape, F32)

        @pl.when(i == j)
        def _():
            dk_ref[...] = jnp.zeros(dk_ref.shape, F32)
            dv_ref[...] = jnp.zeros(dv_ref.shape, F32)

        def update(diagonal):
            k = kn_ref[...] + kr_ref[...]
            qv, dov = q_ref[...], do_ref[...]
            s = _dot_nt(k, qv)
            if diagonal:
                s = jnp.where(_causal(t, keys_first=True), s, NEG)
            p = jnp.exp2(s - lse_ref[...])
            dp = _dot_nt(v_ref[...], dov)
            dz = (p * (dp - dl_ref[...])).astype(BF16)
            dv_ref[...] += _dot(p, dov)
            dk_ref[...] += _dot(dz, qv) * (1.0 / LOG2E)
            rows = pl.ds(pl.multiple_of(i * t, t), t)
            dq_ref[rows, :] += _dot_tn(dz, k) * SM_SCALE

        @pl.when(j < i)
        def _():
            update(False)

        @pl.when(j == i)
        def _():
            update(True)

    qmap = lambda h, s, it, jt: (h, it[s], 0)
    kmap = lambda h, s, it, jt: (h, jt[s], 0)
    rowmap = lambda h, s, it, jt: (h, 0, it[s])
    per_head_rows = lambda a: a[:, :, 0].reshape(N_HEADS, 1, L)
    blk = (t, HEAD_LANES)
    hblk = (None, t, HEAD_LANES)
    return pl.pallas_call(
        kern, name="attn_bwd",
        grid_spec=pltpu.PrefetchScalarGridSpec(
            num_scalar_prefetch=2, grid=(N_HEADS, it.shape[0]),
            in_specs=[pl.BlockSpec(hblk, qmap), pl.BlockSpec(hblk, kmap),
                      pl.BlockSpec(blk, lambda h, s, it, jt: (jt[s], 0)), pl.BlockSpec(hblk, kmap),
                      pl.BlockSpec(hblk, qmap), pl.BlockSpec((None, 1, t), rowmap), pl.BlockSpec((None, 1, t), rowmap)],
            out_specs=[pl.BlockSpec((None, L, HEAD_LANES), lambda h, s, it, jt: (h, 0, 0)),
                       pl.BlockSpec(hblk, kmap), pl.BlockSpec(hblk, kmap)]),
        out_shape=[jax.ShapeDtypeStruct((N_HEADS, L, HEAD_LANES), F32)] * 3,
        compiler_params=_params("arbitrary", "arbitrary"))(it, jt, q, kn, kr, v, do, per_head_rows(lse),
                                                            per_head_rows(delta))


def _head(h, target, g):
    L, D = h.shape

    def body(i, n, tm, ht, tt, gv):
        y, xh, r = _rms(ht, gv)
        pos = i * tm + lax.broadcasted_iota(jnp.int32, (tm, 1), 0)
        e = jnp.where(pos >= SEQ_START, y - tt, 0.0)
        loss = 0.5 * jnp.sum(jnp.mean(e * e, axis=-1, keepdims=True), axis=0, keepdims=True)
        dx, dg = _rms_bwd(e / D, xh, r, gv)
        return [dx], [jnp.broadcast_to(loss, (1, 128)), dg]

    return _row_call("loss_head", body, L, [(h, "tile"), (target, "tile")], [g], [(D, F32)], [(1, 128), (1, D)])


def _coords():
    return lax.axis_index("x"), lax.axis_index("y"), lax.axis_index("c")


def _my_index():
    mx, my, mc = _coords()
    return 4 * mx + 2 * my + mc


def _gather_side(x):
    R, W = x.shape

    def copies(x_ref, out_ref, send_sems, recv_sems):
        mx, my, mc = _coords()
        me, sibling = (mx, my, mc), (mx, my, 1 - mc)
        chips = [(1 - mx, my), (mx, 1 - my), (1 - mx, 1 - my)]

        def slot(px, py, pc):
            return out_ref.at[4 * px + 2 * py + pc]

        def copy(k, block, to, src=None):
            return pltpu.make_async_remote_copy(
                src_ref=slot(*block) if src is None else src, dst_ref=slot(*block),
                send_sem=send_sems.at[k], recv_sem=recv_sems.at[k], device_id=to, device_id_type=MESH)

        first = [copy(0, me, sibling, src=x_ref)]
        first += [copy(1 + n, me, (*chip, mc), src=x_ref) for n, chip in enumerate(chips)]
        passed = [copy(4 + n, (*chip, mc), sibling) for n, chip in enumerate(chips)]
        landed = [copy(1 + n, (*chip, mc), me) for n, chip in enumerate(chips)]
        from_sibling = [copy(0, sibling, me)] + [copy(4 + n, (*chip, 1 - mc), me) for n, chip in enumerate(chips)]
        return first, passed, landed, from_sibling

    def start(ins, outs, sems):
        for cp in copies(ins[0], outs[0], *sems)[0]:
            cp.start()

    def finish(ins, outs, sems):
        first, passed, landed, from_sibling = copies(ins[0], outs[0], *sems)
        for arrived, onward in zip(landed, passed):
            arrived.wait_recv()
            onward.start()
        for cp in from_sibling:
            cp.wait_recv()
        for cp in first + passed:
            cp.wait_send()

    return _Side([x], [jax.ShapeDtypeStruct((N_DEV, R, W), x.dtype)],
                 [pltpu.SemaphoreType.DMA((7,)), pltpu.SemaphoreType.DMA((7,))], start, finish)


def _with_own(gathered, x):
    return lax.dynamic_update_slice(gathered, x[None], (_my_index(), 0, 0))


def _chip_side(parts):
    n_arr = len(parts)

    def copies(p_refs, out_refs, send_sems, recv_sems):
        mx, my, mc = _coords()
        chips = [(1 - mx, my), (mx, 1 - my), (1 - mx, 1 - my)]
        return [pltpu.make_async_remote_copy(
            src_ref=p_ref.at[2 * cx + cy], dst_ref=out_ref.at[n], send_sem=send_sems.at[3 * a + n],
            recv_sem=recv_sems.at[3 * a + n], device_id=(cx, cy, mc), device_id_type=MESH)
            for a, (p_ref, out_ref) in enumerate(zip(p_refs, out_refs)) for n, (cx, cy) in enumerate(chips)]

    def start(ins, outs, sems):
        for cp in copies(ins, outs, *sems):
            cp.start()

    def finish(ins, outs, sems):
        cps = copies(ins, outs, *sems)
        for cp in cps:
            cp.wait_recv()
        for cp in cps:
            cp.wait_send()

    return _Side(list(parts), [jax.ShapeDtypeStruct((3,) + p.shape[1:], p.dtype) for p in parts],
                 [pltpu.SemaphoreType.DMA((3 * n_arr,)), pltpu.SemaphoreType.DMA((3 * n_arr,))], start, finish)


def _run_side(side, name):
    def kern(*refs):
        n_in, n_out = len(side.ins), len(side.out_shapes)
        ins, outs, sems = refs[:n_in], refs[n_in:n_in + n_out], refs[n_in + n_out:]
        side.start(ins, outs, sems)
        side.finish(ins, outs, sems)

    return pl.pallas_call(kern, name=name, in_specs=[ANY] * len(side.ins), out_specs=[ANY] * len(side.out_shapes),
                          out_shape=list(side.out_shapes), scratch_shapes=list(side.sems))(*side.ins)


def _sibling_side(arrs):
    n_arr = len(arrs)

    def copies(g_refs, out_refs, send_sems, recv_sems):
        mx, my, mc = _coords()
        return [pltpu.make_async_remote_copy(
            src_ref=g_ref.at[n, 1 - mc], dst_ref=out_ref.at[n], send_sem=send_sems.at[N_CHIPS * a + n],
            recv_sem=recv_sems.at[N_CHIPS * a + n], device_id=(mx, my, 1 - mc), device_id_type=MESH)
            for a, (g_ref, out_ref) in enumerate(zip(g_refs, out_refs)) for n in range(N_CHIPS)]

    def start(ins, outs, sems):
        for cp in copies(ins, outs, *sems):
            cp.start()

    def finish(ins, outs, sems):
        cps = copies(ins, outs, *sems)
        for cp in cps:
            cp.wait_recv()
        for cp in cps:
            cp.wait_send()

    return _Side(list(arrs), [jax.ShapeDtypeStruct((N_CHIPS,) + g.shape[2:], g.dtype) for g in arrs],
                 [pltpu.SemaphoreType.DMA((N_CHIPS * n_arr,)), pltpu.SemaphoreType.DMA((N_CHIPS * n_arr,))],
                 start, finish)


def _add_own(owns, sel, others, name):
    n_arr = len(owns)
    R, W = owns[0].shape[-2:]
    tr = _pick(R, (PACK_ROW_MULT, 176, 64, 8))
    first_phase = owns[0].ndim == 4
    if first_phase:
        n = owns[0].shape[0]
        grid = (n, R // tr)
        in_specs = ([pl.BlockSpec((None, None, tr, W), lambda b, i, sel: (b, sel[0], i, 0))] * n_arr
                    + [pl.BlockSpec((None, tr, W), lambda b, i, sel: (b, i, 0))] * n_arr)
        out_specs = [pl.BlockSpec((None, tr, W), lambda b, i, sel: (b, i, 0))] * (2 * n_arr)
        out_shape = [jax.ShapeDtypeStruct((n, R, W), F32)] * n_arr + [jax.ShapeDtypeStruct((n, R, W), BF16)] * n_arr

        def kern(sel_ref, *refs):
            for a in range(n_arr):
                acc = refs[a][...] + refs[n_arr + a][...]
                refs[2 * n_arr + a][...] = acc
                refs[3 * n_arr + a][...] = acc.astype(BF16)
    else:
        k = others[0].shape[0]
        grid = (1, R // tr)
        in_specs = ([pl.BlockSpec((None, tr, W), lambda b, i, sel: (sel[0], i, 0))] * n_arr
                    + [pl.BlockSpec((k, tr, W), lambda b, i, sel: (0, i, 0))] * n_arr)
        out_specs = [pl.BlockSpec((tr, W), lambda b, i, sel: (i, 0))] * n_arr
        out_shape = [jax.ShapeDtypeStruct((R, W), F32)] * n_arr

        def kern(sel_ref, *refs):
            for a in range(n_arr):
                acc = refs[a][...]
                for m in range(k):
                    acc = acc + refs[n_arr + a][m].astype(F32)
                refs[2 * n_arr + a][...] = acc

    outs = pl.pallas_call(
        kern, name=name,
        grid_spec=pltpu.PrefetchScalarGridSpec(num_scalar_prefetch=1, grid=grid, in_specs=in_specs,
                                               out_specs=out_specs),
        out_shape=out_shape, compiler_params=_params("arbitrary", "arbitrary"))(sel, *owns, *others)
    return (outs[:n_arr], outs[n_arr:]) if first_phase else outs


def _sum_lead(x, name):
    n, R, W = x.shape
    tr = _pick(R, (PACK_ROW_MULT, 8))

    def kern(x_ref, o_ref):
        acc = x_ref[0]
        for k in range(1, n):
            acc = acc + x_ref[k]
        o_ref[...] = acc

    return pl.pallas_call(
        kern, name=name, grid=(R // tr,),
        in_specs=[pl.BlockSpec((n, tr, W), lambda i: (0, i, 0))],
        out_specs=pl.BlockSpec((tr, W), lambda i: (i, 0)),
        out_shape=jax.ShapeDtypeStruct((R, W), F32), compiler_params=_params("arbitrary"))(x)


def _adamw(w, g, m, v):
    shape = w.shape
    cols = shape[-1]
    rows = w.size // cols
    tr = _pick(rows, (512, 352, 256, 128))
    if rows * cols * 4 <= (1 << 20):
        tr = rows

    def kern(w_ref, g_ref, m_ref, v_ref, d_ref, mo_ref, vo_ref):
        gv = g_ref[...]
        mn = ADAM_B1 * m_ref[...] + (1.0 - ADAM_B1) * gv
        vn = ADAM_B2 * v_ref[...] + (1.0 - ADAM_B2) * (gv * gv)
        m_hat = mn / (1.0 - ADAM_B1 ** ADAM_STEP)
        v_hat = vn / (1.0 - ADAM_B2 ** ADAM_STEP)
        d_ref[...] = -ADAM_LR * (m_hat / (jnp.sqrt(v_hat) + ADAM_EPS) + ADAM_WD * w_ref[...])
        mo_ref[...] = mn
        vo_ref[...] = vn

    spec = pl.BlockSpec((tr, cols), lambda i: (i, 0))
    outs = pl.pallas_call(
        kern, name="adamw", grid=(rows // tr,), in_specs=[spec] * 4, out_specs=[spec] * 3,
        out_shape=[jax.ShapeDtypeStruct((rows, cols), F32)] * 3, compiler_params=_params("arbitrary"),
    )(*[a.reshape(rows, cols) for a in (w, g, m, v)])
    return [o.reshape(shape) for o in outs]


def _pack(arrs, n_lead, row_mult, width, total_mult=PACK_ROW_MULT):
    parts, total = [], 0
    for n, a in enumerate(arrs):
        lead = a.shape[:n_lead]
        flat = a.reshape(lead + (-1,))
        size = flat.shape[-1]
        rows = -(-size // (width * row_mult)) * row_mult
        if n == len(arrs) - 1:
            rows += -(total + rows) % total_mult
        total += rows
        if rows * width > size:
            flat = jnp.concatenate([flat, jnp.zeros(lead + (rows * width - size,), flat.dtype)], axis=n_lead)
        parts.append(flat.reshape(lead + (rows, width)))
    return jnp.concatenate(parts, axis=n_lead)


def _unpack(pack, shapes, n_lead, row_mult):
    outs, row = [], 0
    lead = pack.shape[:n_lead]
    width = pack.shape[-1]
    for shp in shapes:
        size = math.prod(shp)
        rows = -(-size // (width * row_mult)) * row_mult
        blk = lax.slice_in_dim(pack, row, row + rows, axis=n_lead)
        outs.append(blk.reshape(lead + (-1,))[..., :size].reshape(lead + tuple(shp)))
        row += rows
    return outs


def _to_words(a):
    return lax.bitcast_convert_type(a, BF16)


def _from_words(a):
    return lax.bitcast_convert_type(a, F32)


def _pad_axis(a, axis, size):
    pads = [(0, 0)] * a.ndim
    pads[axis] = (0, size - a.shape[axis])
    return jnp.pad(a, pads)


def _dense(name, s):
    if name.endswith("w_gate") or name.endswith("w_up"):
        _, nl, d, fs = s.shape
        return s.transpose(1, 2, 0, 3).reshape(nl, d, N_DEV * fs)
    if name.endswith("w_down"):
        _, nl, fs, d = s.shape
        return s.transpose(1, 0, 2, 3).reshape(nl, N_DEV * fs, d)
    if name == "pool_w":
        _, nl, ng, r, cg = s.shape
        return s.transpose(1, 2, 0, 3, 4).reshape(nl, ng, cg, cg)
    if name == "w_dkv":
        w = s.reshape(-1, s.shape[2])
        z = lambda n: jnp.zeros((w.shape[0], n), w.dtype)
        return jnp.concatenate([w[:, :KV_RANK], z(ROPE_LANE0), w[:, KV_RANK:],
                                z(HEAD_LANES - ROPE_LANE0 - QK_ROPE)], axis=1)
    if name in ("w_uk", "w_uv"):
        return _pad_axis(s.transpose(1, 0, 2), 2, HEAD_LANES).reshape(KV_RANK, N_HEADS * HEAD_LANES)
    if name == "w_dq":
        _, nl, ds, r = s.shape
        return s.transpose(1, 0, 2, 3).reshape(nl, N_DEV * ds, r)
    if name == "w_uq":
        nl = s.shape[1]
        return _pad_axis(s.transpose(1, 2, 0, 3), 3, HEAD_LANES).reshape(nl, Q_RANK, N_HEADS * HEAD_LANES)
    if name == "w_o":
        _, nl, k, dc = s.shape
        w = s.transpose(1, 2, 0, 3).reshape(nl, N_HEADS, V_HEAD, N_DEV * dc)
        return _pad_axis(w, 2, HEAD_LANES).reshape(nl, N_HEADS * HEAD_LANES, N_DEV * dc)
    if name in ("meta_tokens", "pool_scale"):
        r, dc = s.shape[1:]
        return s.transpose(1, 0, 2).reshape(r, N_DEV * dc)
    raise ValueError(name)


def _shards(name, g):
    if name.endswith("w_gate") or name.endswith("w_up"):
        nl, d, f = g.shape
        return g.reshape(nl, d, N_DEV, f // N_DEV).transpose(2, 0, 1, 3)
    if name.endswith("w_down"):
        nl, f, d = g.shape
        return g.reshape(nl, N_DEV, f // N_DEV, d).transpose(1, 0, 2, 3)
    if name == "pool_w":
        nl, ng, cg, _ = g.shape
        return g.reshape(nl, ng, N_DEV, cg // N_DEV, cg).transpose(2, 0, 1, 3, 4)
    if name == "w_dkv":
        w = jnp.concatenate([g[:, :KV_RANK], g[:, KV_RANK + ROPE_LANE0:KV_RANK + ROPE_LANE0 + QK_ROPE]], axis=1)
        return w.reshape(N_DEV, -1, KV_RANK + QK_ROPE)
    if name in ("w_uk", "w_uv"):
        return g.reshape(KV_RANK, N_HEADS, HEAD_LANES)[:, :, :V_HEAD].transpose(1, 0, 2)
    if name == "w_dq":
        nl, d, r = g.shape
        return g.reshape(nl, N_DEV, d // N_DEV, r).transpose(1, 0, 2, 3)
    if name == "w_uq":
        nl = g.shape[0]
        return g.reshape(nl, Q_RANK, N_HEADS, HEAD_LANES)[..., :QK_NOPE + QK_ROPE].transpose(2, 0, 1, 3)
    if name == "w_o":
        nl, _, d = g.shape
        w = g.reshape(nl, N_HEADS, HEAD_LANES, d)[:, :, :V_HEAD].reshape(nl, N_HEADS * V_HEAD, N_DEV, d // N_DEV)
        return w.transpose(2, 0, 1, 3)
    if name in ("meta_tokens", "pool_scale"):
        r, d = g.shape
        return g.reshape(r, N_DEV, d // N_DEV).transpose(1, 0, 2)
    raise ValueError(name)


def _rope_tables(L):
    pos = jnp.maximum(jnp.arange(L) - FRONT_PAD, 0).astype(F32)
    inv = 1.0 / (ROPE_THETA ** (jnp.arange(0, QK_ROPE, 2, dtype=F32) / QK_ROPE))
    ang = pos[:, None] * inv[None, :]
    cos, sin = jnp.cos(ang), jnp.sin(ang)
    half = QK_ROPE // 2
    z = lambda n: jnp.zeros((L, n), F32)
    tail = z(HEAD_LANES - ROPE_LANE0 - QK_ROPE)
    return {
        "cq": jnp.concatenate([jnp.ones((L, ROPE_LANE0), F32), cos, cos, tail], axis=1),
        "ck": jnp.concatenate([z(ROPE_LANE0), cos, cos, tail], axis=1),
        "s1": jnp.concatenate([z(ROPE_LANE0), -sin, z(half), tail], axis=1),
        "s2": jnp.concatenate([z(ROPE_LANE0), z(half), sin, tail], axis=1),
    }


def kernel(x, meta_tokens, ffn1_norm, ffn1_w_gate, ffn1_w_up, ffn1_w_down, mix_norm, ffn2_norm, ffn2_w_gate, ffn2_w_up, ffn2_w_down, pool_w, pool_scale, kv_in_norm, w_dkv, kv_latent_norm, w_uk, w_uv, w_dq, q_latent_norm, w_uq, w_o, final_norm, loss_target, m_meta_tokens, m_ffn1_norm, m_ffn1_w_gate, m_ffn1_w_up, m_ffn1_w_down, m_mix_norm, m_ffn2_norm, m_ffn2_w_gate, m_ffn2_w_up, m_ffn2_w_down, m_pool_w, m_pool_scale, m_kv_in_norm, m_w_dkv, m_kv_latent_norm, m_w_uk, m_w_uv, m_w_dq, m_q_latent_norm, m_w_uq, m_w_o, m_final_norm, v_meta_tokens, v_ffn1_norm, v_ffn1_w_gate, v_ffn1_w_up, v_ffn1_w_down, v_mix_norm, v_ffn2_norm, v_ffn2_w_gate, v_ffn2_w_up, v_ffn2_w_down, v_pool_w, v_pool_scale, v_kv_in_norm, v_w_dkv, v_kv_latent_norm, v_w_uk, v_w_uv, v_w_dq, v_q_latent_norm, v_w_uq, v_w_o, v_final_norm):
    args = dict(locals())
    W = {n: args[n] for n in WEIGHTS}
    M = {n: args["m_" + n] for n in WEIGHTS}
    V = {n: args["v_" + n] for n in WEIGHTS}
    seq, D = x.shape[1], x.shape[2]
    L = SEQ_START + seq

    fs = ffn1_w_down.shape[1]

    ffns = [(l, which) for l in range(DEPTH) for which in (1, 2)]
    ffn_names = lambda which: FFN_WEIGHTS[3 * (which - 1):3 * which]

    def weight_piece(l, which):
        parts = [(W[n][l] if n.endswith("w_down") else W[n][l].T).astype(BF16) for n in ffn_names(which)]
        if (l, which) == ffns[0]:
            parts += [_to_words(W[n]) if n in SHARDED_F32 else W[n].astype(BF16) for n in SMALL_SHARDED]
        return _pack(parts, 0, 16, D, 16)

    pieces = {k: weight_piece(*k) for k in ffns}
    full = {}
    P = {}

    def take_piece(k, gathered):
        full[k] = _with_own(gathered, pieces[k])
        if k == ffns[0]:
            shapes = [(fs, D)] * 3 + [W[n].shape + ((2,) if n in SHARDED_F32 else ()) for n in SMALL_SHARDED]
            for n, s in zip(SMALL_SHARDED, _unpack(full[k], shapes, 1, 16)[3:]):
                P[n] = _dense(n, _from_words(s) if n in SHARDED_F32 else s)

    take_piece(ffns[0], _run_side(_gather_side(pieces[ffns[0]]), "all_gather_first")[0])
    norm3 = lambda a: a.reshape(a.shape[0], 1, a.shape[-1])
    row = lambda a: a.reshape(1, -1)
    g_ffn, g_mix = {1: norm3(ffn1_norm), 2: norm3(ffn2_norm)}, mix_norm
    ffn_entries = (0, 1, 2)

    def ffn_forward(h, l, which):
        at = ffns.index((l, which))
        nxt = ffns[at + 1] if at + 1 < len(ffns) else None
        side = None if nxt is None else _gather_side(pieces[nxt])
        outs = _ffn_fwd(h, g_ffn[which][l:l + 1], full[(l, which)], ffn_entries, fs, side)
        if nxt is not None:
            take_piece(nxt, outs[4])
        return outs[:4]

    h = jnp.concatenate([jnp.zeros((FRONT_PAD, D), F32), P["meta_tokens"], x[0]], axis=0)
    target = jnp.concatenate([jnp.zeros((SEQ_START, D), F32), loss_target[0]], axis=0)
    tabs = _rope_tables(L)
    saved = []
    kv = None
    for l in range(DEPTH):
        s = {"h1": h}
        h, s["xn1"], s["g1"], s["u1"] = ffn_forward(h, l, 1)
        s["hm"] = h
        if l < N_POOL_LAYERS:
            h = _pool_fwd(h, row(g_mix[l]), P["pool_w"][l], row(P["pool_scale"][l]), l)
        else:
            j = l - N_POOL_LAYERS
            s["q"], s["cqp"] = _q_fwd(h, tabs, row(g_mix[l]), P["w_dq"][j], row(q_latent_norm[j]), P["w_uq"][j])
            s["o"], s["lse"] = _attn_fwd(s["q"], kv["kn"], kv["kr"], kv["v"])
            h = _oproj_fwd(h, s["o"], P["w_o"][j])
        s["h2"] = h
        h, s["xn2"], s["g2"], s["u2"] = ffn_forward(h, l, 2)
        if l == N_POOL_LAYERS - 1:
            kv = {"h": h}
            kv["kn"], kv["kr"], kv["v"], kv["ckr"] = _kv_fwd(
                h, tabs, row(kv_in_norm), P["w_dkv"], row(kv_latent_norm), P["w_uk"], P["w_uv"])
        saved.append(s)
    dh, loss_row, d_final = _head(h, target, row(final_norm))
    loss = lax.psum(loss_row[0, 0], ("x", "y", "c"))

    G = {}
    stack = {n: [None] * DEPTH for n in ("ffn1_norm", "ffn1_w_gate", "ffn1_w_up", "ffn1_w_down", "mix_norm",
                                         "ffn2_norm", "ffn2_w_gate", "ffn2_w_up", "ffn2_w_down")}
    pool_dw, pool_ds = [None] * N_POOL_LAYERS, [None] * N_POOL_LAYERS
    mla = {n: [None] * (DEPTH - N_POOL_LAYERS) for n in ("w_dq", "w_uq", "w_o", "q_latent_norm")}
    dks, dvs = [], []
    my_core = lax.axis_index("c").astype(jnp.int32).reshape(1)
    my_chip = (2 * lax.axis_index("x") + lax.axis_index("y")).astype(jnp.int32).reshape(1)
    layer_grads = {n: [None] * DEPTH for n in FFN_WEIGHTS}
    grads = {}

    by_owner = lambda g: g.reshape((N_CHIPS, 2, g.shape[0] // N_DEV) + g.shape[1:])

    def reduce_cores(arrs, others):
        return _add_own(arrs, my_core, others, "sum_cores")

    def finish_layer(l, partials, from_chips):
        for n, g in zip(FFN_WEIGHTS, _add_own(partials, my_chip, from_chips, "sum_chips")):
            layer_grads[n][l] = g if n.endswith("w_down") else g.T

    def ffn_backward(dh, s, which, l, side=None):
        outs = _ffn_bwd(dh, s["h%d" % which], s["g%d" % which], s["u%d" % which], g_ffn[which][l:l + 1],
                        full[(l, which)], ffn_entries, fs, side)
        dh, dg, du, act, dob, dgam = outs[:6]
        xn = s["xn%d" % which]
        stack["ffn%d_w_gate" % which][l] = _mm_tn(dg, xn, "ffn_dw")
        stack["ffn%d_w_up" % which][l] = _mm_tn(du, xn, "ffn_dw")
        stack["ffn%d_w_down" % which][l] = _mm_tn(act, dob, "ffn_dw")
        stack["ffn%d_norm" % which][l] = dgam
        return dh, outs[6:]

    arrs = None
    for l in reversed(range(DEPTH)):
        s = saved[l]
        if l == N_POOL_LAYERS - 1:
            dh, d_dkv, d_uk, d_uv, d_kvin, d_kvlat = _kv_bwd(
                dh, kv["h"], kv["ckr"], dks, dvs, tabs, row(kv_in_norm), P["w_dkv"], row(kv_latent_norm),
                P["w_uk"], P["w_uv"])
            G.update(w_dkv=d_dkv, w_uk=d_uk, w_uv=d_uv, kv_in_norm=d_kvin, kv_latent_norm=d_kvlat)
        if arrs is None:
            dh, _ = ffn_backward(dh, s, 2, l)
        else:
            dh, from_sibling = ffn_backward(dh, s, 2, l, _sibling_side(arrs))
            partials, partials_bf16 = reduce_cores(arrs, from_sibling)
        if l < N_POOL_LAYERS:
            dh, pool_dw[l], pool_ds[l], stack["mix_norm"][l] = _pool_bwd(
                dh, s["hm"], row(g_mix[l]), P["pool_w"][l], row(P["pool_scale"][l]))
        else:
            j = l - N_POOL_LAYERS
            do, delta_o, mla["w_o"][j] = _oproj_bwd(dh, s["o"], P["w_o"][j])
            dq, dk, dv = _attn_bwd(s["q"], kv["kn"], kv["kr"], kv["v"], do, s["lse"], delta_o)
            dks.append(dk)
            dvs.append(dv)
            dh, mla["w_dq"][j], mla["w_uq"][j], stack["mix_norm"][l], mla["q_latent_norm"][j] = _q_bwd(
                dh, s["hm"], s["cqp"], dq, tabs, row(g_mix[l]), P["w_dq"][j], row(q_latent_norm[j]), P["w_uq"][j])
        if arrs is None:
            dh, _ = ffn_backward(dh, s, 1, l)
        else:
            dh, from_chips = ffn_backward(dh, s, 1, l, _chip_side(partials_bf16))
            finish_layer(l + 1, partials, from_chips)
        arrs = [by_owner(stack[n][l]) for n in FFN_WEIGHTS]
    grad_x = dh[SEQ_START:][None]
    for n in ("ffn1_norm", "mix_norm", "ffn2_norm"):
        G[n] = jnp.concatenate(stack[n], axis=0)
    G["pool_w"] = jnp.stack(pool_dw)
    G["pool_scale"] = jnp.concatenate(pool_ds, axis=0)
    G["w_dq"], G["w_uq"], G["w_o"] = (jnp.stack(mla[n]) for n in ("w_dq", "w_uq", "w_o"))
    G["q_latent_norm"] = jnp.concatenate(mla["q_latent_norm"], axis=0)
    G["meta_tokens"] = dh[FRONT_PAD:SEQ_START]
    G["final_norm"] = d_final

    spack = _pack([_shards(n, G[n]) for n in SMALL_SHARDED], 1, 8, D)
    spack = spack.reshape((N_CHIPS, 2) + spack.shape[1:])
    others = _run_side(_sibling_side(arrs + [spack]), "sibling_exchange_last")
    partials, partials_bf16 = reduce_cores(arrs, others[:-1])
    small, small_bf16 = reduce_cores([spack], others[-1:])
    from_chips = _run_side(_chip_side(list(partials_bf16) + list(small_bf16)), "chip_exchange_last")
    finish_layer(0, partials, from_chips[:-1])
    small_mine = _add_own(small, my_chip, from_chips[-1:], "sum_chips")[0]
    grads.update(zip(SMALL_SHARDED, _unpack(small_mine, [W[n].shape for n in SMALL_SHARDED], 0, 8)))
    for n in FFN_WEIGHTS:
        grads[n] = jnp.stack(layer_grads[n])
    rep_shapes = [W[n].shape for n in REPLICATED]
    rpack = _pack([G[n].reshape(W[n].shape) for n in REPLICATED], 0, 8, D)
    everyones = _with_own(_run_side(_gather_side(rpack), "all_gather_norm_grads")[0], rpack)
    grads.update(zip(REPLICATED, _unpack(_sum_lead(everyones, "sum_devices"), rep_shapes, 0, 8)))

    delta, new_m, new_v = {}, {}, {}
    for n in WEIGHTS:
        delta[n], new_m[n], new_v[n] = _adamw(W[n], grads[n], M[n], V[n])
    return (loss, grad_x, *[grads[n] for n in WEIGHTS], *[delta[n] for n in WEIGHTS],
            *[new_m[n] for n in WEIGHTS], *[new_v[n] for n in WEIGHTS])
```

```python
import functools
import math

import jax
import jax.numpy as jnp
from jax import lax
from jax.experimental import pallas as pl
from jax.experimental.pallas import tpu as pltpu

F32 = jnp.float32
BF16 = jnp.bfloat16
MESH = pl.DeviceIdType.MESH

N_DEV = 8
N_CHIPS = 4
DEPTH = 4
N_POOL_LAYERS = 2
N_HEADS = 8
QK_NOPE = 64
QK_ROPE = 32
V_HEAD = 64
KV_RANK = 256
Q_RANK = 384
HEAD_LANES = 128
HEADS = "heads"
ROPE_LANE0 = QK_NOPE
BIAS_LANE = QK_NOPE + QK_ROPE
ONES_LANE = V_HEAD
LOG2E = math.log2(math.e)
N_META = 16
CHUNK_SHIFT = 6
FRONT_PAD = 112
SEQ_START = FRONT_PAD + N_META
HALO = 16
POOL_WINDOWS = (2, 4, 8, 16)
EPS = 1e-6
ROPE_THETA = 10000.0
NEG = -1e30
PACK_ROW_MULT = 256
VMEM_LIMIT = 56 * 1024 * 1024

ADAM_LR = 0.001
ADAM_B1 = 0.9
ADAM_B2 = 0.999
ADAM_EPS = 1e-08
ADAM_WD = 0.01
ADAM_STEP = 10

SHARDED = ["ffn1_w_gate", "ffn1_w_up", "ffn1_w_down", "ffn2_w_gate", "ffn2_w_up", "ffn2_w_down",
           "pool_w", "w_dkv", "w_uk", "w_uv", "w_dq", "w_uq", "w_o", "meta_tokens", "pool_scale"]
FFN_WEIGHTS = SHARDED[:6]
SMALL_SHARDED = SHARDED[6:]
SHARDED_F32 = ("meta_tokens", "pool_scale")
REPLICATED = ["ffn1_norm", "mix_norm", "ffn2_norm", "kv_in_norm", "kv_latent_norm", "q_latent_norm",
              "final_norm"]
WEIGHTS = ['meta_tokens', 'ffn1_norm', 'ffn1_w_gate', 'ffn1_w_up', 'ffn1_w_down', 'mix_norm', 'ffn2_norm',
           'ffn2_w_gate', 'ffn2_w_up', 'ffn2_w_down', 'pool_w', 'pool_scale', 'kv_in_norm', 'w_dkv',
           'kv_latent_norm', 'w_uk', 'w_uv', 'w_dq', 'q_latent_norm', 'w_uq', 'w_o', 'final_norm']


def _dot(a, b):
    return jnp.dot(a.astype(BF16), b.astype(BF16), preferred_element_type=F32)


def _dot_nt(a, b):
    return lax.dot_general(a.astype(BF16), b.astype(BF16), (((1,), (1,)), ((), ())),
                           preferred_element_type=F32)


def _dot_tn(a, b):
    return lax.dot_general(a.astype(BF16), b.astype(BF16), (((0,), (0,)), ((), ())),
                           preferred_element_type=F32)


def _sigmoid(x):
    return 1.0 / (1.0 + jnp.exp(-x))


def _rms(x, g):
    r = lax.rsqrt(jnp.mean(x * x, axis=-1, keepdims=True) + EPS)
    xh = x * r
    return xh * g, xh, r


def _rms_bwd(dy, xh, r, g):
    dxh = dy * g
    dx = r * (dxh - xh * jnp.mean(dxh * xh, axis=-1, keepdims=True))
    return dx, jnp.sum(dy * xh, axis=0, keepdims=True)


def _rope(x, c, s1, s2):
    return x * c + pltpu.roll(x, HEAD_LANES - QK_ROPE // 2, 1) * s1 + pltpu.roll(x, QK_ROPE // 2, 1) * s2


def _rope_t(d, c, s1, s2):
    return d * c + pltpu.roll(d * s1, QK_ROPE // 2, 1) + pltpu.roll(d * s2, HEAD_LANES - QK_ROPE // 2, 1)


def _params(*sem):
    return pltpu.CompilerParams(dimension_semantics=sem, vmem_limit_bytes=VMEM_LIMIT)


def _pick(n, candidates):
    for c in candidates:
        if n % c == 0:
            return c
    return n


def _row_tile(L):
    return _pick(L, (640, 128))


FF_OWNERS = 4


def _row_call(name, body, L, row_ins, full_ins, row_outs, acc_outs):
    tm = _row_tile(L)
    n = L // tm
    hb = tm // HALO
    nb = L // HALO
    per_head = (N_HEADS, tm, HEAD_LANES)
    in_specs, args = [], []
    for arr, kind in row_ins:
        c = arr.shape[-1]
        if kind == "tile":
            spec = pl.BlockSpec((tm, c), lambda i: (i, 0))
        elif kind == "heads":
            spec = pl.BlockSpec(per_head, lambda i: (0, i, 0))
        elif kind == "prev":
            spec = pl.BlockSpec((HALO, c), lambda i: (jnp.maximum(i * hb - 1, 0), 0))
        else:
            spec = pl.BlockSpec((HALO, c), lambda i: (jnp.minimum((i + 1) * hb, nb - 1), 0))
        in_specs.append(spec)
        args.append(arr)
    for arr in full_ins:
        in_specs.append(pl.BlockSpec(arr.shape, lambda i, nd=arr.ndim: (0,) * nd))
        args.append(arr)
    out_shape, out_specs = [], []
    for c, dt in row_outs:
        if c == HEADS:
            out_shape.append(jax.ShapeDtypeStruct((N_HEADS, L, HEAD_LANES), dt))
            out_specs.append(pl.BlockSpec(per_head, lambda i: (0, i, 0)))
        else:
            out_shape.append(jax.ShapeDtypeStruct((L, c), dt))
            out_specs.append(pl.BlockSpec((tm, c), lambda i: (i, 0)))
    for shp in acc_outs:
        out_shape.append(jax.ShapeDtypeStruct(shp, F32))
        out_specs.append(pl.BlockSpec(shp, lambda i, nd=len(shp): (0,) * nd))
    n_in, n_ro = len(args), len(row_outs)

    def kern(*refs):
        i = pl.program_id(0)
        vals = [jnp.concatenate([r[hd] for hd in range(N_HEADS)], axis=1) if kind == "heads" else r[...]
                for r, (_, kind) in zip(refs, row_ins)] + [r[...] for r in refs[len(row_ins):n_in]]
        ro, ao = body(i, n, tm, *vals)
        for r, v in zip(refs[n_in:n_in + n_ro], ro):
            if len(r.shape) == 3:
                for hd in range(N_HEADS):
                    r[hd] = v[:, hd * HEAD_LANES:(hd + 1) * HEAD_LANES].astype(r.dtype)
            else:
                r[...] = v.astype(r.dtype)
        acc_refs = refs[n_in + n_ro:]

        @pl.when(i == 0)
        def _():
            for r in acc_refs:
                r[...] = jnp.zeros(r.shape, r.dtype)

        for r, v in zip(acc_refs, ao):
            r[...] += v

    return pl.pallas_call(kern, name=name, grid=(n,), in_specs=in_specs, out_specs=out_specs,
                          out_shape=out_shape, compiler_params=_params("arbitrary"))(*args)


class _Side:
    def __init__(self, ins, out_shapes, sems, start, finish, onward=None):
        self.ins, self.out_shapes, self.sems, self.start, self.finish = ins, out_shapes, sems, start, finish
        self.onward = onward


ANY = pl.BlockSpec(memory_space=pl.ANY)


def _hosted(kern, n_in, n_out, n_scratch, side, is_first, is_late, is_last):
    if side is None:
        return kern
    ns_in, ns_out = len(side.ins), len(side.out_shapes)

    def wrapped(*refs):
        ins, refs = refs[:n_in], refs[n_in:]
        side_ins, refs = refs[:ns_in], refs[ns_in:]
        outs, refs = refs[:n_out], refs[n_out:]
        side_outs, refs = refs[:ns_out], refs[ns_out:]
        scratch, side_sems = refs[:n_scratch], refs[n_scratch:]

        @pl.when(is_first())
        def _():
            side.start(side_ins, side_outs, side_sems)

        kern(*ins, *outs, *scratch)

        if side.onward is not None:
            @pl.when(is_late())
            def _():
                side.onward(side_ins, side_outs, side_sems)

        @pl.when(is_last())
        def _():
            side.finish(side_ins, side_outs, side_sems)

    return wrapped


def _side_args(side):
    if side is None:
        return [], [], [], [], []
    return ([ANY] * len(side.ins), [ANY] * len(side.out_shapes), list(side.out_shapes), list(side.sems),
            list(side.ins))


def _ffn_weight_specs(fs, D, ents):
    return [pl.BlockSpec((FF_OWNERS, fs, D), lambda i, f, e=e: (f, e, 0)) for e in ents]


def _ffn_fwd(h, gam, wpiece, ents, fs, side=None):
    L, D = h.shape
    F = N_DEV * fs
    tm = _row_tile(L)
    tf = FF_OWNERS * fs
    nL, nF = L // tm, F // tf
    s_in, s_out, s_shape, s_sems, s_args = _side_args(side)

    def kern(h_ref, gam_ref, wg_ref, wu_ref, wd_ref, ho_ref, xn_ref, gs_ref, us_ref, acc):
        f = pl.program_id(1)

        @pl.when(f == 0)
        def _():
            xn, _, _ = _rms(h_ref[...], gam_ref[...])
            xn_ref[...] = xn.astype(BF16)
            acc[...] = jnp.zeros(acc.shape, F32)

        xnb = xn_ref[...]
        g = _dot_nt(xnb, wg_ref[...].reshape(tf, D))
        u = _dot_nt(xnb, wu_ref[...].reshape(tf, D))
        gs_ref[...] = g.astype(BF16)
        us_ref[...] = u.astype(BF16)
        acc[...] += _dot(g * _sigmoid(g) * u, wd_ref[...].reshape(tf, D))

        @pl.when(f == nF - 1)
        def _():
            ho_ref[...] = h_ref[...] + 0.5 * acc[...]

    first = lambda: (pl.program_id(0) == 0) & (pl.program_id(1) == 0)
    last = lambda: (pl.program_id(0) == nL - 1) & (pl.program_id(1) == nF - 1)
    late = lambda: (pl.program_id(0) == (2 * nL) // 3) & (pl.program_id(1) == 0)
    return pl.pallas_call(
        _hosted(kern, 5, 4, 1, side, first, late, last),
        name="ffn_fwd" if side is None else "ffn_fwd_hosting", grid=(nL, nF),
        in_specs=[pl.BlockSpec((tm, D), lambda i, f: (i, 0)),
                  pl.BlockSpec((None, 1, D), lambda i, f: (0, 0, 0))] + _ffn_weight_specs(fs, D, ents) + s_in,
        out_specs=[pl.BlockSpec((tm, D), lambda i, f: (i, 0)),
                   pl.BlockSpec((tm, D), lambda i, f: (i, 0)),
                   pl.BlockSpec((tm, tf), lambda i, f: (i, f)),
                   pl.BlockSpec((tm, tf), lambda i, f: (i, f))] + s_out,
        out_shape=[jax.ShapeDtypeStruct((L, D), F32), jax.ShapeDtypeStruct((L, D), BF16),
                   jax.ShapeDtypeStruct((L, F), BF16), jax.ShapeDtypeStruct((L, F), BF16)] + s_shape,
        scratch_shapes=[pltpu.VMEM((tm, D), F32)] + s_sems,
        compiler_params=_params("arbitrary", "arbitrary"))(h, gam, wpiece, wpiece, wpiece, *s_args)


def _ffn_bwd(dh, h, gs, us, gam, wpiece, ents, fs, side=None):
    L, D = h.shape
    F = N_DEV * fs
    tm = _pick(L, (416, 128))
    tf = FF_OWNERS * fs
    nL, nF = L // tm, F // tf
    s_in, s_out, s_shape, s_sems, s_args = _side_args(side)

    def kern(dh_ref, h_ref, gs_ref, us_ref, gam_ref, wg_ref, wu_ref, wd_ref,
             dhi_ref, dg_ref, du_ref, a_ref, dob_ref, dgam_ref, dxn):
        i = pl.program_id(0)
        f = pl.program_id(1)

        @pl.when(f == 0)
        def _():
            dxn[...] = jnp.zeros(dxn.shape, F32)
            dob_ref[...] = (0.5 * dh_ref[...]).astype(BF16)

        @pl.when((f == 0) & (i == 0))
        def _():
            dgam_ref[...] = jnp.zeros(dgam_ref.shape, F32)

        g = gs_ref[...].astype(F32)
        u = us_ref[...].astype(F32)
        sg = _sigmoid(g)
        silu = g * sg
        da = _dot_nt(dob_ref[...], wd_ref[...].reshape(tf, D))
        a_ref[...] = (silu * u).astype(BF16)
        dgt = (da * u * (sg * (1.0 + g * (1.0 - sg)))).astype(BF16)
        dut = (da * silu).astype(BF16)
        dg_ref[...] = dgt
        du_ref[...] = dut
        dxn[...] += _dot(dgt, wg_ref[...].reshape(tf, D)) + _dot(dut, wu_ref[...].reshape(tf, D))

        @pl.when(f == nF - 1)
        def _():
            gamma = gam_ref[...]
            _, xh, r = _rms(h_ref[...], gamma)
            dx, dgam = _rms_bwd(dxn[...], xh, r, gamma)
            dhi_ref[...] = dh_ref[...] + dx
            dgam_ref[...] += dgam

    first = lambda: (pl.program_id(0) == 0) & (pl.program_id(1) == 0)
    last = lambda: (pl.program_id(0) == nL - 1) & (pl.program_id(1) == nF - 1)
    late = lambda: (pl.program_id(0) == (2 * nL) // 3) & (pl.program_id(1) == 0)
    return pl.pallas_call(
        _hosted(kern, 8, 6, 1, side, first, late, last),
        name="ffn_bwd" if side is None else "ffn_bwd_hosting", grid=(nL, nF),
        in_specs=[pl.BlockSpec((tm, D), lambda i, f: (i, 0)),
                  pl.BlockSpec((tm, D), lambda i, f: (i, 0)),
                  pl.BlockSpec((tm, tf), lambda i, f: (i, f)),
                  pl.BlockSpec((tm, tf), lambda i, f: (i, f)),
                  pl.BlockSpec((None, 1, D), lambda i, f: (0, 0, 0))] + _ffn_weight_specs(fs, D, ents) + s_in,
        out_specs=[pl.BlockSpec((tm, D), lambda i, f: (i, 0)),
                   pl.BlockSpec((tm, tf), lambda i, f: (i, f)),
                   pl.BlockSpec((tm, tf), lambda i, f: (i, f)),
                   pl.BlockSpec((tm, tf), lambda i, f: (i, f)),
                   pl.BlockSpec((tm, D), lambda i, f: (i, 0)),
                   pl.BlockSpec((1, D), lambda i, f: (0, 0))] + s_out,
        out_shape=[jax.ShapeDtypeStruct((L, D), F32), jax.ShapeDtypeStruct((L, F), BF16),
                   jax.ShapeDtypeStruct((L, F), BF16), jax.ShapeDtypeStruct((L, F), BF16),
                   jax.ShapeDtypeStruct((L, D), BF16), jax.ShapeDtypeStruct((1, D), F32)] + s_shape,
        scratch_shapes=[pltpu.VMEM((tm, D), F32)] + s_sems,
        compiler_params=_params("arbitrary", "arbitrary"))(dh, h, gs, us, gam, wpiece, wpiece, wpiece, *s_args)


def _mm_tn(a, b, name):
    L, M = a.shape
    N = b.shape[1]
    tm = _pick(M, (1408, 1024, 512))
    tn = _pick(N, (1408, 1024, 512))
    tk = _pick(L, (2080, 640, 128))

    def kern(a_ref, b_ref, o_ref):
        @pl.when(pl.program_id(2) == 0)
        def _():
            o_ref[...] = jnp.zeros(o_ref.shape, F32)

        o_ref[...] += _dot_tn(a_ref[...], b_ref[...])

    return pl.pallas_call(
        kern, name=name, grid=(M // tm, N // tn, L // tk),
        in_specs=[pl.BlockSpec((tk, tm), lambda i, j, k: (k, i)),
                  pl.BlockSpec((tk, tn), lambda i, j, k: (k, j))],
        out_specs=pl.BlockSpec((tm, tn), lambda i, j, k: (i, j)),
        out_shape=jax.ShapeDtypeStruct((M, N), F32),
        compiler_params=_params("arbitrary", "arbitrary", "arbitrary"))(a, b)


def _pool_counts(pos, w):
    return jnp.clip(pos - (FRONT_PAD - 1), 1, w).astype(F32)


def _pool_forward_values(i, tm, h, hprev, gamma, D):
    cg = D // len(POOL_WINDOWS)
    hext = jnp.concatenate([hprev, h], axis=0)
    uext, xh, r = _rms(hext, gamma)
    pos = i * tm + lax.broadcasted_iota(jnp.int32, (tm, 1), 0)
    pooled = []
    for gi, w in enumerate(POOL_WINDOWS):
        s = uext[:, gi * cg:(gi + 1) * cg]
        span = 1
        while span < w:
            s = s + pltpu.roll(s, span, 0)
            span *= 2
        s = s[HALO:]
        pooled.append(s / _pool_counts(pos, w) - uext[HALO:, gi * cg:(gi + 1) * cg])
    return uext, xh[HALO:], r[HALO:], pooled


def _pool_fwd(h, gam, w, scale, l):
    L, D = h.shape
    cg = D // len(POOL_WINDOWS)

    def body(i, n, tm, ht, hprev, gamma, wv, sc):
        _, _, _, pooled = _pool_forward_values(i, tm, ht, hprev, gamma, D)
        ys = [_dot(pooled[gi], wv[gi]) for gi in range(len(POOL_WINDOWS))]
        y = jnp.concatenate(ys, axis=1) * sc
        return [ht + y], []

    del cg
    return _row_call("pool_fwd", body, L, [(h, "tile"), (h, "prev")], [gam, w, scale], [(D, F32)], [])[0]


def _pool_bwd(dy, h, gam, w, scale):
    L, D = h.shape
    ng = len(POOL_WINDOWS)
    cg = D // ng

    def body(i, n, tm, ht, hprev, dyt, dynext, gamma, wv, sc):
        _, xh, r, pooled = _pool_forward_values(i, tm, ht, hprev, gamma, D)
        dynext = jnp.where(i == n - 1, jnp.zeros_like(dynext), dynext)
        dyext = jnp.concatenate([dyt, dynext], axis=0) * sc
        pos_ext = i * tm + lax.broadcasted_iota(jnp.int32, (tm + HALO, 1), 0)
        dws, dscs, dus = [], [], []
        for gi, wd in enumerate(POOL_WINDOWS):
            cols = slice(gi * cg, (gi + 1) * cg)
            pb = pooled[gi].astype(BF16)
            ypre = _dot(pb, wv[gi])
            dscs.append(jnp.sum(dyt[:, cols] * ypre, axis=0, keepdims=True))
            dws.append(_dot_tn(pb, dyext[:tm, cols])[None])
            dp = _dot_nt(dyext[:, cols], wv[gi])
            s = dp / _pool_counts(pos_ext, wd)
            span = 1
            while span < wd:
                s = s + pltpu.roll(s, tm + HALO - span, 0)
                span *= 2
            dus.append(s[:tm] - dp[:tm])
        du = jnp.concatenate(dus, axis=1)
        pos = pos_ext[:tm]
        du = jnp.where(pos >= FRONT_PAD, du, 0.0)
        dx, dgam = _rms_bwd(du, xh, r, gamma)
        return [dyt + dx], [jnp.concatenate(dws, axis=0), jnp.concatenate(dscs, axis=1), dgam]

    return _row_call("pool_bwd", body, L, [(h, "tile"), (h, "prev"), (dy, "tile"), (dy, "next")],
                     [gam, w, scale], [(D, F32)], [(ng, cg, cg), (1, D), (1, D)])


def _kv_fwd(h, tabs, g1, wdkv, g2, wuk, wuv):
    L, D = h.shape
    hw = N_HEADS * HEAD_LANES

    def body(i, n, tm, ht, ck, s1, s2, g1v, wdkv_v, g2v, wuk_v, wuv_v):
        xkv, _, _ = _rms(ht, g1v)
        ckr = _dot(xkv, wdkv_v)
        ckv, _, _ = _rms(ckr[:, :KV_RANK], g2v)
        krope = _rope(ckr[:, KV_RANK:], ck, s1, s2)
        pos = i * tm + lax.broadcasted_iota(jnp.int32, (tm, HEAD_LANES), 0)
        lane = lax.broadcasted_iota(jnp.int32, (tm, HEAD_LANES), 1)
        krope = jnp.where((pos < FRONT_PAD) & (lane == BIAS_LANE), NEG, krope)
        ones = ((lax.broadcasted_iota(jnp.int32, (1, hw), 1) & (HEAD_LANES - 1)) == ONES_LANE).astype(F32)
        return [_dot(ckv, wuk_v), krope, _dot(ckv, wuv_v) + ones, ckr], []

    ck, s1, s2 = tabs["ck"], tabs["s1"], tabs["s2"]
    return _row_call("kv_fwd", body, L, [(h, "tile"), (ck, "tile"), (s1, "tile"), (s2, "tile")],
                     [g1, wdkv, g2, wuk, wuv],
                     [(HEADS, BF16), (HEAD_LANES, BF16), (HEADS, BF16), (KV_RANK + HEAD_LANES, F32)], [])


def _kv_bwd(dh, h, ckr, dks, dvs, tabs, g1, wdkv, g2, wuk, wuv):
    L, D = h.shape
    hw = N_HEADS * HEAD_LANES
    nl = len(dks)

    def body(i, n, tm, *vals):
        dht, ht, ckr_t = vals[:3]
        dk = sum(vals[3:3 + nl][1:], vals[3])
        dv = sum(vals[3 + nl:3 + 2 * nl][1:], vals[3 + nl])
        ck, s1, s2, g1v, wdkv_v, g2v, wuk_v, wuv_v = vals[3 + 2 * nl:]
        xkv, xh1, r1 = _rms(ht, g1v)
        ckv, xh2, r2 = _rms(ckr_t[:, :KV_RANK], g2v)
        dckv = _dot_nt(dk, wuk_v) + _dot_nt(dv, wuv_v)
        dlat, dg2 = _rms_bwd(dckv, xh2, r2, g2v)
        dkr = dk[:, :HEAD_LANES]
        for hd in range(1, N_HEADS):
            dkr = dkr + dk[:, hd * HEAD_LANES:(hd + 1) * HEAD_LANES]
        dckr = jnp.concatenate([dlat, _rope_t(dkr, ck, s1, s2)], axis=1)
        dx, dg1 = _rms_bwd(_dot_nt(dckr, wdkv_v), xh1, r1, g1v)
        return [dht + dx], [_dot_tn(xkv, dckr), _dot_tn(ckv, dk), _dot_tn(ckv, dv), dg1, dg2]

    row_ins = [(dh, "tile"), (h, "tile"), (ckr, "tile")] + [(a, "heads") for a in dks + dvs]
    row_ins += [(tabs[k], "tile") for k in ("ck", "s1", "s2")]
    return _row_call("kv_bwd", body, L, row_ins, [g1, wdkv, g2, wuk, wuv], [(D, F32)],
                     [(D, KV_RANK + HEAD_LANES), (KV_RANK, hw), (KV_RANK, hw), (1, D), (1, KV_RANK)])


def _q_fwd(h, tabs, g, wdq, gq, wuq):
    L, D = h.shape
    hw = N_HEADS * HEAD_LANES

    def body(i, n, tm, ht, cq_t, s1, s2, gv, wdq_v, gqv, wuq_v):
        u, _, _ = _rms(ht, gv)
        cqp = _dot(u, wdq_v)
        cq, _, _ = _rms(cqp, gqv)
        qp = _dot(cq, wuq_v)
        bias = (lax.broadcasted_iota(jnp.int32, (1, HEAD_LANES), 1) == BIAS_LANE).astype(F32)
        q = [_rope(qp[:, hd * HEAD_LANES:(hd + 1) * HEAD_LANES], cq_t, s1, s2) * (SM_SCALE * LOG2E) + bias
             for hd in range(N_HEADS)]
        return [jnp.concatenate(q, axis=1), cqp], []

    return _row_call("q_fwd", body, L, [(h, "tile")] + [(tabs[k], "tile") for k in ("cq", "s1", "s2")],
                     [g, wdq, gq, wuq], [(HEADS, BF16), (Q_RANK, F32)], [])


def _q_bwd(dh, h, cqp, dq, tabs, g, wdq, gq, wuq):
    L, D = h.shape
    hw = N_HEADS * HEAD_LANES

    def body(i, n, tm, dht, ht, cqp_t, dq_t, cq_t, s1, s2, gv, wdq_v, gqv, wuq_v):
        u, xh1, r1 = _rms(ht, gv)
        cq, xh2, r2 = _rms(cqp_t, gqv)
        dqp = jnp.concatenate([_rope_t(dq_t[:, hd * HEAD_LANES:(hd + 1) * HEAD_LANES], cq_t, s1, s2)
                               for hd in range(N_HEADS)], axis=1)
        dcqp, dgq = _rms_bwd(_dot_nt(dqp, wuq_v), xh2, r2, gqv)
        dx, dg = _rms_bwd(_dot_nt(dcqp, wdq_v), xh1, r1, gv)
        return [dht + dx], [_dot_tn(u, dcqp), _dot_tn(cq, dqp), dg, dgq]

    row_ins = [(dh, "tile"), (h, "tile"), (cqp, "tile"), (dq, "heads")]
    row_ins += [(tabs[k], "tile") for k in ("cq", "s1", "s2")]
    return _row_call("q_bwd", body, L, row_ins, [g, wdq, gq, wuq], [(D, F32)],
                     [(D, Q_RANK), (Q_RANK, hw), (1, D), (1, Q_RANK)])


def _oproj_fwd(h, o, wo):
    L, D = h.shape

    def body(i, n, tm, ht, ot, wov):
        return [ht + _dot(ot, wov)], []

    return _row_call("oproj_fwd", body, L, [(h, "tile"), (o, "heads")], [wo], [(D, F32)], [])[0]


def _oproj_bwd(dh, o, wo):
    L, D = dh.shape
    hw = N_HEADS * HEAD_LANES

    def body(i, n, tm, dht, ot, wov):
        do = _dot_nt(dht, wov)
        prod = do * ot.astype(F32)
        delta = [jnp.broadcast_to(jnp.sum(prod[:, hd * HEAD_LANES:(hd + 1) * HEAD_LANES], axis=-1, keepdims=True),
                                  (tm, HEAD_LANES)) for hd in range(N_HEADS)]
        return [do, jnp.concatenate(delta, axis=1)], [_dot_tn(ot, dht)]

    return _row_call("oproj_bwd", body, L, [(dh, "tile"), (o, "heads")], [wo], [(HEADS, BF16), (HEADS, F32)],
                     [(hw, D)])


def _causal(t, keys_first=False):
    qpos = lax.broadcasted_iota(jnp.int32, (t, t), 1 if keys_first else 0)
    kpos = lax.broadcasted_iota(jnp.int32, (t, t), 0 if keys_first else 1)
    return (kpos >> CHUNK_SHIFT) <= (qpos >> CHUNK_SHIFT)


SM_SCALE = 1.0 / math.sqrt(QK_NOPE + QK_ROPE)


def _pairs(n, key_major):
    if key_major:
        order = [(i, j) for j in range(n) for i in range(j, n)]
    else:
        order = [(i, j) for i in range(n) for j in range(i + 1)]
    return (jnp.array([p[0] for p in order], jnp.int32), jnp.array([p[1] for p in order], jnp.int32))


def _attn_fwd(q, kn, kr, v):
    L = q.shape[1]
    hw = N_HEADS * HEAD_LANES
    t = _row_tile(L)
    it, jt = _pairs(L // t, key_major=False)

    def kern(it_ref, jt_ref, q_ref, kn_ref, kr_ref, v_ref, o_ref, lse_ref, m_s, acc_s):
        step = pl.program_id(1)
        i, j = it_ref[step], jt_ref[step]

        @pl.when(j == 0)
        def _():
            m_s[...] = jnp.full(m_s.shape, NEG, F32)
            acc_s[...] = jnp.zeros(acc_s.shape, F32)

        def update(diagonal):
            k = kn_ref[...] + kr_ref[...]
            s = _dot_nt(q_ref[...], k)
            if diagonal:
                s = jnp.where(_causal(t), s, NEG)
            m_prev = m_s[:, :1]
            m_new = jnp.maximum(m_prev, jnp.max(s, axis=-1, keepdims=True))
            p = jnp.exp2(s - m_new)
            acc_s[...] = jnp.exp2(m_prev - m_new) * acc_s[...] + _dot(p, v_ref[...])
            m_s[...] = jnp.broadcast_to(m_new, m_s.shape)

        @pl.when(j < i)
        def _():
            update(False)

        @pl.when(j == i)
        def _():
            update(True)
            acc = acc_s[...]
            total = acc[:, ONES_LANE:ONES_LANE + 1]
            o_ref[...] = (acc / total).astype(BF16)
            lse_ref[...] = m_s[...] + jnp.log2(jnp.broadcast_to(total, m_s.shape))

    qmap = lambda h, s, it, jt: (h, it[s], 0)
    kmap = lambda h, s, it, jt: (h, jt[s], 0)
    blk = (t, HEAD_LANES)
    hblk = (None, t, HEAD_LANES)
    return pl.pallas_call(
        kern, name="attn_fwd",
        grid_spec=pltpu.PrefetchScalarGridSpec(
            num_scalar_prefetch=2, grid=(N_HEADS, it.shape[0]),
            in_specs=[pl.BlockSpec(hblk, qmap), pl.BlockSpec(hblk, kmap),
                      pl.BlockSpec(blk, lambda h, s, it, jt: (jt[s], 0)), pl.BlockSpec(hblk, kmap)],
            out_specs=[pl.BlockSpec(hblk, qmap), pl.BlockSpec(hblk, qmap)],
            scratch_shapes=[pltpu.VMEM(blk, F32)] * 2),
        out_shape=[jax.ShapeDtypeStruct((N_HEADS, L, HEAD_LANES), BF16),
                   jax.ShapeDtypeStruct((N_HEADS, L, HEAD_LANES), F32)],
        compiler_params=_params("arbitrary", "arbitrary"))(it, jt, q, kn, kr, v)


def _attn_bwd(q, kn, kr, v, do, lse, delta):
    L = q.shape[1]
    hw = N_HEADS * HEAD_LANES
    t = _row_tile(L)
    it, jt = _pairs(L // t, key_major=True)

    def kern(it_ref, jt_ref, q_ref, kn_ref, kr_ref, v_ref, do_ref, lse_ref, dl_ref, dq_ref, dk_ref, dv_ref):
        step = pl.program_id(1)
        i, j = it_ref[step], jt_ref[step]

        @pl.when(step == 0)
        def _():
            dq_ref[...] = jnp.zeros(dq_ref.shape, F32)

        @pl.when(i == j)
        def _():
            dk_ref[...] = jnp.zeros(dk_ref.shape, F32)
            dv_ref[...] = jnp.zeros(dv_ref.shape, F32)

        def update(diagonal):
            k = kn_ref[...] + kr_ref[...]
            qv, dov = q_ref[...], do_ref[...]
            s = _dot_nt(k, qv)
            if diagonal:
                s = jnp.where(_causal(t, keys_first=True), s, NEG)
            p = jnp.exp2(s - lse_ref[...])
            dp = _dot_nt(v_ref[...], dov)
            dz = (p * (dp - dl_ref[...])).astype(BF16)
            dv_ref[...] += _dot(p, dov)
            dk_ref[...] += _dot(dz, qv) * (1.0 / LOG2E)
            rows = pl.ds(pl.multiple_of(i * t, t), t)
            dq_ref[rows, :] += _dot_tn(dz, k) * SM_SCALE

        @pl.when(j < i)
        def _():
            update(False)

        @pl.when(j == i)
        def _():
            update(True)

    qmap = lambda h, s, it, jt: (h, it[s], 0)
    kmap = lambda h, s, it, jt: (h, jt[s], 0)
    rowmap = lambda h, s, it, jt: (h, 0, it[s])
    per_head_rows = lambda a: a[:, :, 0].reshape(N_HEADS, 1, L)
    blk = (t, HEAD_LANES)
    hblk = (None, t, HEAD_LANES)
    return pl.pallas_call(
        kern, name="attn_bwd",
        grid_spec=pltpu.PrefetchScalarGridSpec(
            num_scalar_prefetch=2, grid=(N_HEADS, it.shape[0]),
            in_specs=[pl.BlockSpec(hblk, qmap), pl.BlockSpec(hblk, kmap),
                      pl.BlockSpec(blk, lambda h, s, it, jt: (jt[s], 0)), pl.BlockSpec(hblk, kmap),
                      pl.BlockSpec(hblk, qmap), pl.BlockSpec((None, 1, t), rowmap), pl.BlockSpec((None, 1, t), rowmap)],
            out_specs=[pl.BlockSpec((None, L, HEAD_LANES), lambda h, s, it, jt: (h, 0, 0)),
                       pl.BlockSpec(hblk, kmap), pl.BlockSpec(hblk, kmap)]),
        out_shape=[jax.ShapeDtypeStruct((N_HEADS, L, HEAD_LANES), F32)] * 3,
        compiler_params=_params("arbitrary", "arbitrary"))(it, jt, q, kn, kr, v, do, per_head_rows(lse),
                                                            per_head_rows(delta))


def _head(h, target, g):
    L, D = h.shape

    def body(i, n, tm, ht, tt, gv):
        y, xh, r = _rms(ht, gv)
        pos = i * tm + lax.broadcasted_iota(jnp.int32, (tm, 1), 0)
        e = jnp.where(pos >= SEQ_START, y - tt, 0.0)
        loss = 0.5 * jnp.sum(jnp.mean(e * e, axis=-1, keepdims=True), axis=0, keepdims=True)
        dx, dg = _rms_bwd(e / D, xh, r, gv)
        return [dx], [jnp.broadcast_to(loss, (1, 128)), dg]

    return _row_call("loss_head", body, L, [(h, "tile"), (target, "tile")], [g], [(D, F32)], [(1, 128), (1, D)])


def _coords():
    return lax.axis_index("x"), lax.axis_index("y"), lax.axis_index("c")


def _my_index():
    mx, my, mc = _coords()
    return 4 * mx + 2 * my + mc


def _gather_side(x):
    R, W = x.shape

    def copies(x_ref, out_ref, send_sems, recv_sems):
        mx, my, mc = _coords()
        me, sibling = (mx, my, mc), (mx, my, 1 - mc)
        chips = [(1 - mx, my), (mx, 1 - my), (1 - mx, 1 - my)]

        def slot(px, py, pc):
            return out_ref.at[4 * px + 2 * py + pc]

        def copy(k, block, to, src=None):
            return pltpu.make_async_remote_copy(
                src_ref=slot(*block) if src is None else src, dst_ref=slot(*block),
                send_sem=send_sems.at[k], recv_sem=recv_sems.at[k], device_id=to, device_id_type=MESH)

        first = [copy(0, me, sibling, src=x_ref)]
        first += [copy(1 + n, me, (*chip, mc), src=x_ref) for n, chip in enumerate(chips)]
        passed = [copy(4 + n, (*chip, mc), sibling) for n, chip in enumerate(chips)]
        landed = [copy(1 + n, (*chip, mc), me) for n, chip in enumerate(chips)]
        from_sibling = [copy(0, sibling, me)] + [copy(4 + n, (*chip, 1 - mc), me) for n, chip in enumerate(chips)]
        return first, passed, landed, from_sibling

    def start(ins, outs, sems):
        for cp in copies(ins[0], outs[0], *sems)[0]:
            cp.start()

    def onward(ins, outs, sems):
        _, passed, landed, _ = copies(ins[0], outs[0], *sems)
        for arrived, on in zip(landed, passed):
            arrived.wait_recv()
            on.start()

    def finish(ins, outs, sems):
        first, passed, _, from_sibling = copies(ins[0], outs[0], *sems)
        for cp in from_sibling:
            cp.wait_recv()
        for cp in first + passed:
            cp.wait_send()

    return _Side([x], [jax.ShapeDtypeStruct((N_DEV, R, W), x.dtype)],
                 [pltpu.SemaphoreType.DMA((7,)), pltpu.SemaphoreType.DMA((7,))], start, finish, onward)


def _with_own(gathered, x):
    return lax.dynamic_update_slice(gathered, x[None], (_my_index(), 0, 0))


def _chip_side(parts):
    n_arr = len(parts)

    def copies(p_refs, out_refs, send_sems, recv_sems):
        mx, my, mc = _coords()
        chips = [(1 - mx, my), (mx, 1 - my), (1 - mx, 1 - my)]
        return [pltpu.make_async_remote_copy(
            src_ref=p_ref.at[2 * cx + cy], dst_ref=out_ref.at[n], send_sem=send_sems.at[3 * a + n],
            recv_sem=recv_sems.at[3 * a + n], device_id=(cx, cy, mc), device_id_type=MESH)
            for a, (p_ref, out_ref) in enumerate(zip(p_refs, out_refs)) for n, (cx, cy) in enumerate(chips)]

    def start(ins, outs, sems):
        for cp in copies(ins, outs, *sems):
            cp.start()

    def finish(ins, outs, sems):
        cps = copies(ins, outs, *sems)
        for cp in cps:
            cp.wait_recv()
        for cp in cps:
            cp.wait_send()

    return _Side(list(parts), [jax.ShapeDtypeStruct((3,) + p.shape[1:], p.dtype) for p in parts],
                 [pltpu.SemaphoreType.DMA((3 * n_arr,)), pltpu.SemaphoreType.DMA((3 * n_arr,))], start, finish)


def _run_side(side, name):
    def kern(*refs):
        n_in, n_out = len(side.ins), len(side.out_shapes)
        ins, outs, sems = refs[:n_in], refs[n_in:n_in + n_out], refs[n_in + n_out:]
        side.start(ins, outs, sems)
        if side.onward is not None:
            side.onward(ins, outs, sems)
        side.finish(ins, outs, sems)

    return pl.pallas_call(kern, name=name, in_specs=[ANY] * len(side.ins), out_specs=[ANY] * len(side.out_shapes),
                          out_shape=list(side.out_shapes), scratch_shapes=list(side.sems))(*side.ins)


def _sibling_side(arrs):
    n_arr = len(arrs)

    def copies(g_refs, out_refs, send_sems, recv_sems):
        mx, my, mc = _coords()
        return [pltpu.make_async_remote_copy(
            src_ref=g_ref.at[n, 1 - mc], dst_ref=out_ref.at[n], send_sem=send_sems.at[N_CHIPS * a + n],
            recv_sem=recv_sems.at[N_CHIPS * a + n], device_id=(mx, my, 1 - mc), device_id_type=MESH)
            for a, (g_ref, out_ref) in enumerate(zip(g_refs, out_refs)) for n in range(N_CHIPS)]

    def start(ins, outs, sems):
        for cp in copies(ins, outs, *sems):
            cp.start()

    def finish(ins, outs, sems):
        cps = copies(ins, outs, *sems)
        for cp in cps:
            cp.wait_recv()
        for cp in cps:
            cp.wait_send()

    return _Side(list(arrs), [jax.ShapeDtypeStruct((N_CHIPS,) + g.shape[2:], g.dtype) for g in arrs],
                 [pltpu.SemaphoreType.DMA((N_CHIPS * n_arr,)), pltpu.SemaphoreType.DMA((N_CHIPS * n_arr,))],
                 start, finish)


def _add_own(owns, sel, others, name):
    n_arr = len(owns)
    R, W = owns[0].shape[-2:]
    tr = _pick(R, (PACK_ROW_MULT, 176, 64, 8))
    first_phase = owns[0].ndim == 4
    if first_phase:
        n = owns[0].shape[0]
        grid = (n, R // tr)
        in_specs = ([pl.BlockSpec((None, None, tr, W), lambda b, i, sel: (b, sel[0], i, 0))] * n_arr
                    + [pl.BlockSpec((None, tr, W), lambda b, i, sel: (b, i, 0))] * n_arr)
        out_specs = [pl.BlockSpec((None, tr, W), lambda b, i, sel: (b, i, 0))] * (2 * n_arr)
        out_shape = [jax.ShapeDtypeStruct((n, R, W), F32)] * n_arr + [jax.ShapeDtypeStruct((n, R, W), BF16)] * n_arr

        def kern(sel_ref, *refs):
            for a in range(n_arr):
                acc = refs[a][...] + refs[n_arr + a][...]
                refs[2 * n_arr + a][...] = acc
                refs[3 * n_arr + a][...] = acc.astype(BF16)
    else:
        k = others[0].shape[0]
        grid = (1, R // tr)
        in_specs = ([pl.BlockSpec((None, tr, W), lambda b, i, sel: (sel[0], i, 0))] * n_arr
                    + [pl.BlockSpec((k, tr, W), lambda b, i, sel: (0, i, 0))] * n_arr)
        out_specs = [pl.BlockSpec((tr, W), lambda b, i, sel: (i, 0))] * n_arr
        out_shape = [jax.ShapeDtypeStruct((R, W), F32)] * n_arr

        def kern(sel_ref, *refs):
            for a in range(n_arr):
                acc = refs[a][...]
                for m in range(k):
                    acc = acc + refs[n_arr + a][m].astype(F32)
                refs[2 * n_arr + a][...] = acc

    outs = pl.pallas_call(
        kern, name=name,
        grid_spec=pltpu.PrefetchScalarGridSpec(num_scalar_prefetch=1, grid=grid, in_specs=in_specs,
                                               out_specs=out_specs),
        out_shape=out_shape, compiler_params=_params("arbitrary", "arbitrary"))(sel, *owns, *others)
    return (outs[:n_arr], outs[n_arr:]) if first_phase else outs


def _sum_lead(x, name):
    n, R, W = x.shape
    tr = _pick(R, (PACK_ROW_MULT, 8))

    def kern(x_ref, o_ref):
        acc = x_ref[0]
        for k in range(1, n):
            acc = acc + x_ref[k]
        o_ref[...] = acc

    return pl.pallas_call(
        kern, name=name, grid=(R // tr,),
        in_specs=[pl.BlockSpec((n, tr, W), lambda i: (0, i, 0))],
        out_specs=pl.BlockSpec((tr, W), lambda i: (i, 0)),
        out_shape=jax.ShapeDtypeStruct((R, W), F32), compiler_params=_params("arbitrary"))(x)


def _adamw(w, g, m, v):
    shape = w.shape
    cols = shape[-1]
    rows = w.size // cols
    tr = _pick(rows, (512, 352, 256, 128))
    if rows * cols * 4 <= (1 << 20):
        tr = rows

    def kern(w_ref, g_ref, m_ref, v_ref, d_ref, mo_ref, vo_ref):
        gv = g_ref[...]
        mn = ADAM_B1 * m_ref[...] + (1.0 - ADAM_B1) * gv
        vn = ADAM_B2 * v_ref[...] + (1.0 - ADAM_B2) * (gv * gv)
        m_hat = mn / (1.0 - ADAM_B1 ** ADAM_STEP)
        v_hat = vn / (1.0 - ADAM_B2 ** ADAM_STEP)
        d_ref[...] = -ADAM_LR * (m_hat / (jnp.sqrt(v_hat) + ADAM_EPS) + ADAM_WD * w_ref[...])
        mo_ref[...] = mn
        vo_ref[...] = vn

    spec = pl.BlockSpec((tr, cols), lambda i: (i, 0))
    outs = pl.pallas_call(
        kern, name="adamw", grid=(rows // tr,), in_specs=[spec] * 4, out_specs=[spec] * 3,
        out_shape=[jax.ShapeDtypeStruct((rows, cols), F32)] * 3, compiler_params=_params("arbitrary"),
    )(*[a.reshape(rows, cols) for a in (w, g, m, v)])
    return [o.reshape(shape) for o in outs]


def _pack(arrs, n_lead, row_mult, width, total_mult=PACK_ROW_MULT):
    parts, total = [], 0
    for n, a in enumerate(arrs):
        lead = a.shape[:n_lead]
        flat = a.reshape(lead + (-1,))
        size = flat.shape[-1]
        rows = -(-size // (width * row_mult)) * row_mult
        if n == len(arrs) - 1:
            rows += -(total + rows) % total_mult
        total += rows
        if rows * width > size:
            flat = jnp.concatenate([flat, jnp.zeros(lead + (rows * width - size,), flat.dtype)], axis=n_lead)
        parts.append(flat.reshape(lead + (rows, width)))
    return jnp.concatenate(parts, axis=n_lead)


def _unpack(pack, shapes, n_lead, row_mult):
    outs, row = [], 0
    lead = pack.shape[:n_lead]
    width = pack.shape[-1]
    for shp in shapes:
        size = math.prod(shp)
        rows = -(-size // (width * row_mult)) * row_mult
        blk = lax.slice_in_dim(pack, row, row + rows, axis=n_lead)
        outs.append(blk.reshape(lead + (-1,))[..., :size].reshape(lead + tuple(shp)))
        row += rows
    return outs


def _to_words(a):
    return lax.bitcast_convert_type(a, BF16)


def _from_words(a):
    return lax.bitcast_convert_type(a, F32)


def _pad_axis(a, axis, size):
    pads = [(0, 0)] * a.ndim
    pads[axis] = (0, size - a.shape[axis])
    return jnp.pad(a, pads)


def _dense(name, s):
    if name.endswith("w_gate") or name.endswith("w_up"):
        _, nl, d, fs = s.shape
        return s.transpose(1, 2, 0, 3).reshape(nl, d, N_DEV * fs)
    if name.endswith("w_down"):
        _, nl, fs, d = s.shape
        return s.transpose(1, 0, 2, 3).reshape(nl, N_DEV * fs, d)
    if name == "pool_w":
        _, nl, ng, r, cg = s.shape
        return s.transpose(1, 2, 0, 3, 4).reshape(nl, ng, cg, cg)
    if name == "w_dkv":
        w = s.reshape(-1, s.shape[2])
        z = lambda n: jnp.zeros((w.shape[0], n), w.dtype)
        return jnp.concatenate([w[:, :KV_RANK], z(ROPE_LANE0), w[:, KV_RANK:],
                                z(HEAD_LANES - ROPE_LANE0 - QK_ROPE)], axis=1)
    if name in ("w_uk", "w_uv"):
        return _pad_axis(s.transpose(1, 0, 2), 2, HEAD_LANES).reshape(KV_RANK, N_HEADS * HEAD_LANES)
    if name == "w_dq":
        _, nl, ds, r = s.shape
        return s.transpose(1, 0, 2, 3).reshape(nl, N_DEV * ds, r)
    if name == "w_uq":
        nl = s.shape[1]
        return _pad_axis(s.transpose(1, 2, 0, 3), 3, HEAD_LANES).reshape(nl, Q_RANK, N_HEADS * HEAD_LANES)
    if name == "w_o":
        _, nl, k, dc = s.shape
        w = s.transpose(1, 2, 0, 3).reshape(nl, N_HEADS, V_HEAD, N_DEV * dc)
        return _pad_axis(w, 2, HEAD_LANES).reshape(nl, N_HEADS * HEAD_LANES, N_DEV * dc)
    if name in ("meta_tokens", "pool_scale"):
        r, dc = s.shape[1:]
        return s.transpose(1, 0, 2).reshape(r, N_DEV * dc)
    raise ValueError(name)


def _shards(name, g):
    if name.endswith("w_gate") or name.endswith("w_up"):
        nl, d, f = g.shape
        return g.reshape(nl, d, N_DEV, f // N_DEV).transpose(2, 0, 1, 3)
    if name.endswith("w_down"):
        nl, f, d = g.shape
        return g.reshape(nl, N_DEV, f // N_DEV, d).transpose(1, 0, 2, 3)
    if name == "pool_w":
        nl, ng, cg, _ = g.shape
        return g.reshape(nl, ng, N_DEV, cg // N_DEV, cg).transpose(2, 0, 1, 3, 4)
    if name == "w_dkv":
        w = jnp.concatenate([g[:, :KV_RANK], g[:, KV_RANK + ROPE_LANE0:KV_RANK + ROPE_LANE0 + QK_ROPE]], axis=1)
        return w.reshape(N_DEV, -1, KV_RANK + QK_ROPE)
    if name in ("w_uk", "w_uv"):
        return g.reshape(KV_RANK, N_HEADS, HEAD_LANES)[:, :, :V_HEAD].transpose(1, 0, 2)
    if name == "w_dq":
        nl, d, r = g.shape
        return g.reshape(nl, N_DEV, d // N_DEV, r).transpose(1, 0, 2, 3)
    if name == "w_uq":
        nl = g.shape[0]
        return g.reshape(nl, Q_RANK, N_HEADS, HEAD_LANES)[..., :QK_NOPE + QK_ROPE].transpose(2, 0, 1, 3)
    if name == "w_o":
        nl, _, d = g.shape
        w = g.reshape(nl, N_HEADS, HEAD_LANES, d)[:, :, :V_HEAD].reshape(nl, N_HEADS * V_HEAD, N_DEV, d // N_DEV)
        return w.transpose(2, 0, 1, 3)
    if name in ("meta_tokens", "pool_scale"):
        r, d = g.shape
        return g.reshape(r, N_DEV, d // N_DEV).transpose(1, 0, 2)
    raise ValueError(name)


def _rope_tables(L):
    pos = jnp.maximum(jnp.arange(L) - FRONT_PAD, 0).astype(F32)
    inv = 1.0 / (ROPE_THETA ** (jnp.arange(0, QK_ROPE, 2, dtype=F32) / QK_ROPE))
    ang = pos[:, None] * inv[None, :]
    cos, sin = jnp.cos(ang), jnp.sin(ang)
    half = QK_ROPE // 2
    z = lambda n: jnp.zeros((L, n), F32)
    tail = z(HEAD_LANES - ROPE_LANE0 - QK_ROPE)
    return {
        "cq": jnp.concatenate([jnp.ones((L, ROPE_LANE0), F32), cos, cos, tail], axis=1),
        "ck": jnp.concatenate([z(ROPE_LANE0), cos, cos, tail], axis=1),
        "s1": jnp.concatenate([z(ROPE_LANE0), -sin, z(half), tail], axis=1),
        "s2": jnp.concatenate([z(ROPE_LANE0), z(half), sin, tail], axis=1),
    }


def kernel(x, meta_tokens, ffn1_norm, ffn1_w_gate, ffn1_w_up, ffn1_w_down, mix_norm, ffn2_norm, ffn2_w_gate, ffn2_w_up, ffn2_w_down, pool_w, pool_scale, kv_in_norm, w_dkv, kv_latent_norm, w_uk, w_uv, w_dq, q_latent_norm, w_uq, w_o, final_norm, loss_target, m_meta_tokens, m_ffn1_norm, m_ffn1_w_gate, m_ffn1_w_up, m_ffn1_w_down, m_mix_norm, m_ffn2_norm, m_ffn2_w_gate, m_ffn2_w_up, m_ffn2_w_down, m_pool_w, m_pool_scale, m_kv_in_norm, m_w_dkv, m_kv_latent_norm, m_w_uk, m_w_uv, m_w_dq, m_q_latent_norm, m_w_uq, m_w_o, m_final_norm, v_meta_tokens, v_ffn1_norm, v_ffn1_w_gate, v_ffn1_w_up, v_ffn1_w_down, v_mix_norm, v_ffn2_norm, v_ffn2_w_gate, v_ffn2_w_up, v_ffn2_w_down, v_pool_w, v_pool_scale, v_kv_in_norm, v_w_dkv, v_kv_latent_norm, v_w_uk, v_w_uv, v_w_dq, v_q_latent_norm, v_w_uq, v_w_o, v_final_norm):
    args = dict(locals())
    W = {n: args[n] for n in WEIGHTS}
    M = {n: args["m_" + n] for n in WEIGHTS}
    V = {n: args["v_" + n] for n in WEIGHTS}
    seq, D = x.shape[1], x.shape[2]
    L = SEQ_START + seq

    fs = ffn1_w_down.shape[1]

    ffns = [(l, which) for l in range(DEPTH) for which in (1, 2)]
    ffn_names = lambda which: FFN_WEIGHTS[3 * (which - 1):3 * which]

    extras = {ffns[0]: ["meta_tokens"], ffns[1]: [n for n in SMALL_SHARDED if n != "meta_tokens"]}

    def weight_piece(l, which):
        parts = [(W[n][l] if n.endswith("w_down") else W[n][l].T).astype(BF16) for n in ffn_names(which)]
        parts += [_to_words(W[n]) if n in SHARDED_F32 else W[n].astype(BF16) for n in extras.get((l, which), [])]
        return _pack(parts, 0, 16, D, 16)

    pieces = {k: weight_piece(*k) for k in ffns}
    full = {}
    P = {}

    def take_piece(k, gathered):
        full[k] = _with_own(gathered, pieces[k])
        names = extras.get(k, [])
        shapes = [(fs, D)] * 3 + [W[n].shape + ((2,) if n in SHARDED_F32 else ()) for n in names]
        for n, s in zip(names, _unpack(full[k], shapes, 1, 16)[3:]):
            P[n] = _dense(n, _from_words(s) if n in SHARDED_F32 else s)

    take_piece(ffns[0], _run_side(_gather_side(pieces[ffns[0]]), "all_gather_first")[0])
    norm3 = lambda a: a.reshape(a.shape[0], 1, a.shape[-1])
    row = lambda a: a.reshape(1, -1)
    g_ffn, g_mix = {1: norm3(ffn1_norm), 2: norm3(ffn2_norm)}, mix_norm
    ffn_entries = (0, 1, 2)

    def ffn_forward(h, l, which):
        at = ffns.index((l, which))
        nxt = ffns[at + 1] if at + 1 < len(ffns) else None
        side = None if nxt is None else _gather_side(pieces[nxt])
        outs = _ffn_fwd(h, g_ffn[which][l:l + 1], full[(l, which)], ffn_entries, fs, side)
        if nxt is not None:
            take_piece(nxt, outs[4])
        return outs[:4]

    h = jnp.concatenate([jnp.zeros((FRONT_PAD, D), F32), P["meta_tokens"], x[0]], axis=0)
    target = jnp.concatenate([jnp.zeros((SEQ_START, D), F32), loss_target[0]], axis=0)
    tabs = _rope_tables(L)
    saved = []
    kv = None
    for l in range(DEPTH):
        s = {"h1": h}
        h, s["xn1"], s["g1"], s["u1"] = ffn_forward(h, l, 1)
        s["hm"] = h
        if l < N_POOL_LAYERS:
            h = _pool_fwd(h, row(g_mix[l]), P["pool_w"][l], row(P["pool_scale"][l]), l)
        else:
            j = l - N_POOL_LAYERS
            s["q"], s["cqp"] = _q_fwd(h, tabs, row(g_mix[l]), P["w_dq"][j], row(q_latent_norm[j]), P["w_uq"][j])
            s["o"], s["lse"] = _attn_fwd(s["q"], kv["kn"], kv["kr"], kv["v"])
            h = _oproj_fwd(h, s["o"], P["w_o"][j])
        s["h2"] = h
        h, s["xn2"], s["g2"], s["u2"] = ffn_forward(h, l, 2)
        if l == N_POOL_LAYERS - 1:
            kv = {"h": h}
            kv["kn"], kv["kr"], kv["v"], kv["ckr"] = _kv_fwd(
                h, tabs, row(kv_in_norm), P["w_dkv"], row(kv_latent_norm), P["w_uk"], P["w_uv"])
        saved.append(s)
    dh, loss_row, d_final = _head(h, target, row(final_norm))
    loss = lax.psum(loss_row[0, 0], ("x", "y", "c"))

    G = {}
    stack = {n: [None] * DEPTH for n in ("ffn1_norm", "ffn1_w_gate", "ffn1_w_up", "ffn1_w_down", "mix_norm",
                                         "ffn2_norm", "ffn2_w_gate", "ffn2_w_up", "ffn2_w_down")}
    pool_dw, pool_ds = [None] * N_POOL_LAYERS, [None] * N_POOL_LAYERS
    mla = {n: [None] * (DEPTH - N_POOL_LAYERS) for n in ("w_dq", "w_uq", "w_o", "q_latent_norm")}
    dks, dvs = [], []
    my_core = lax.axis_index("c").astype(jnp.int32).reshape(1)
    my_chip = (2 * lax.axis_index("x") + lax.axis_index("y")).astype(jnp.int32).reshape(1)
    layer_grads = {n: [None] * DEPTH for n in FFN_WEIGHTS}
    grads = {}

    by_owner = lambda g: g.reshape((N_CHIPS, 2, g.shape[0] // N_DEV) + g.shape[1:])

    def reduce_cores(arrs, others):
        return _add_own(arrs, my_core, others, "sum_cores")

    def finish_layer(l, partials, from_chips):
        for n, g in zip(FFN_WEIGHTS, _add_own(partials, my_chip, from_chips, "sum_chips")):
            layer_grads[n][l] = g if n.endswith("w_down") else g.T

    def ffn_backward(dh, s, which, l, side=None):
        outs = _ffn_bwd(dh, s["h%d" % which], s["g%d" % which], s["u%d" % which], g_ffn[which][l:l + 1],
                        full[(l, which)], ffn_entries, fs, side)
        dh, dg, du, act, dob, dgam = outs[:6]
        xn = s["xn%d" % which]
        stack["ffn%d_w_gate" % which][l] = _mm_tn(dg, xn, "ffn_dw")
        stack["ffn%d_w_up" % which][l] = _mm_tn(du, xn, "ffn_dw")
        stack["ffn%d_w_down" % which][l] = _mm_tn(act, dob, "ffn_dw")
        stack["ffn%d_norm" % which][l] = dgam
        return dh, outs[6:]

    arrs = None
    for l in reversed(range(DEPTH)):
        s = saved[l]
        if l == N_POOL_LAYERS - 1:
            dh, d_dkv, d_uk, d_uv, d_kvin, d_kvlat = _kv_bwd(
                dh, kv["h"], kv["ckr"], dks, dvs, tabs, row(kv_in_norm), P["w_dkv"], row(kv_latent_norm),
                P["w_uk"], P["w_uv"])
            G.update(w_dkv=d_dkv, w_uk=d_uk, w_uv=d_uv, kv_in_norm=d_kvin, kv_latent_norm=d_kvlat)
        if arrs is None:
            dh, _ = ffn_backward(dh, s, 2, l)
        else:
            dh, from_sibling = ffn_backward(dh, s, 2, l, _sibling_side(arrs))
            partials, partials_bf16 = reduce_cores(arrs, from_sibling)
        if l < N_POOL_LAYERS:
            dh, pool_dw[l], pool_ds[l], stack["mix_norm"][l] = _pool_bwd(
                dh, s["hm"], row(g_mix[l]), P["pool_w"][l], row(P["pool_scale"][l]))
        else:
            j = l - N_POOL_LAYERS
            do, delta_o, mla["w_o"][j] = _oproj_bwd(dh, s["o"], P["w_o"][j])
            dq, dk, dv = _attn_bwd(s["q"], kv["kn"], kv["kr"], kv["v"], do, s["lse"], delta_o)
            dks.append(dk)
            dvs.append(dv)
            dh, mla["w_dq"][j], mla["w_uq"][j], stack["mix_norm"][l], mla["q_latent_norm"][j] = _q_bwd(
                dh, s["hm"], s["cqp"], dq, tabs, row(g_mix[l]), P["w_dq"][j], row(q_latent_norm[j]), P["w_uq"][j])
        if arrs is None:
            dh, _ = ffn_backward(dh, s, 1, l)
        else:
            dh, from_chips = ffn_backward(dh, s, 1, l, _chip_side(partials_bf16))
            finish_layer(l + 1, partials, from_chips)
        arrs = [by_owner(stack[n][l]) for n in FFN_WEIGHTS]
    grad_x = dh[SEQ_START:][None]
    for n in ("ffn1_norm", "mix_norm", "ffn2_norm"):
        G[n] = jnp.concatenate(stack[n], axis=0)
    G["pool_w"] = jnp.stack(pool_dw)
    G["pool_scale"] = jnp.concatenate(pool_ds, axis=0)
    G["w_dq"], G["w_uq"], G["w_o"] = (jnp.stack(mla[n]) for n in ("w_dq", "w_uq", "w_o"))
    G["q_latent_norm"] = jnp.concatenate(mla["q_latent_norm"], axis=0)
    G["meta_tokens"] = dh[FRONT_PAD:SEQ_START]
    G["final_norm"] = d_final

    spack = _pack([_shards(n, G[n]) for n in SMALL_SHARDED], 1, 8, D)
    spack = spack.reshape((N_CHIPS, 2) + spack.shape[1:])
    others = _run_side(_sibling_side(arrs + [spack]), "sibling_exchange_last")
    partials, partials_bf16 = reduce_cores(arrs, others[:-1])
    small, small_bf16 = reduce_cores([spack], others[-1:])
    from_chips = _run_side(_chip_side(list(partials_bf16) + list(small_bf16)), "chip_exchange_last")
    finish_layer(0, partials, from_chips[:-1])
    small_mine = _add_own(small, my_chip, from_chips[-1:], "sum_chips")[0]
    grads.update(zip(SMALL_SHARDED, _unpack(small_mine, [W[n].shape for n in SMALL_SHARDED], 0, 8)))
    for n in FFN_WEIGHTS:
        grads[n] = jnp.stack(layer_grads[n])
    rep_shapes = [W[n].shape for n in REPLICATED]
    rpack = _pack([G[n].reshape(W[n].shape) for n in REPLICATED], 0, 8, D)
    everyones = _with_own(_run_side(_gather_side(rpack), "all_gather_norm_grads")[0], rpack)
    grads.update(zip(REPLICATED, _unpack(_sum_lead(everyones, "sum_devices"), rep_shapes, 0, 8)))

    delta, new_m, new_v = {}, {}, {}
    for n in WEIGHTS:
        delta[n], new_m[n], new_v[n] = _adamw(W[n], grads[n], M[n], V[n])
    return (loss, grad_x, *[grads[n] for n in WEIGHTS], *[delta[n] for n in WEIGHTS],
            *[new_m[n] for n in WEIGHTS], *[new_v[n] for n in WEIGHTS])
```

```python
import functools
import math

import jax
import jax.numpy as jnp
from jax import lax
from jax.experimental import pallas as pl
from jax.experimental.pallas import tpu as pltpu

F32 = jnp.float32
BF16 = jnp.bfloat16
MESH = pl.DeviceIdType.MESH

N_DEV = 8
N_CHIPS = 4
DEPTH = 4
N_POOL_LAYERS = 2
N_HEADS = 8
QK_NOPE = 64
QK_ROPE = 32
V_HEAD = 64
KV_RANK = 256
Q_RANK = 384
HEAD_LANES = 128
HEADS = "heads"
ROPE_LANE0 = QK_NOPE
BIAS_LANE = QK_NOPE + QK_ROPE
ONES_LANE = V_HEAD
LOG2E = math.log2(math.e)
N_META = 16
CHUNK_SHIFT = 6
FRONT_PAD = 112
SEQ_START = FRONT_PAD + N_META
HALO = 16
POOL_WINDOWS = (2, 4, 8, 16)
EPS = 1e-6
ROPE_THETA = 10000.0
NEG = -1e30
PACK_ROW_MULT = 256
VMEM_LIMIT = 56 * 1024 * 1024

ADAM_LR = 0.001
ADAM_B1 = 0.9
ADAM_B2 = 0.999
ADAM_EPS = 1e-08
ADAM_WD = 0.01
ADAM_STEP = 10

SHARDED = ["ffn1_w_gate", "ffn1_w_up", "ffn1_w_down", "ffn2_w_gate", "ffn2_w_up", "ffn2_w_down",
           "pool_w", "w_dkv", "w_uk", "w_uv", "w_dq", "w_uq", "w_o", "meta_tokens", "pool_scale"]
FFN_WEIGHTS = SHARDED[:6]
SMALL_SHARDED = SHARDED[6:]
SHARDED_F32 = ("meta_tokens", "pool_scale")
REPLICATED = ["ffn1_norm", "mix_norm", "ffn2_norm", "kv_in_norm", "kv_latent_norm", "q_latent_norm",
              "final_norm"]
WEIGHTS = ['meta_tokens', 'ffn1_norm', 'ffn1_w_gate', 'ffn1_w_up', 'ffn1_w_down', 'mix_norm', 'ffn2_norm',
           'ffn2_w_gate', 'ffn2_w_up', 'ffn2_w_down', 'pool_w', 'pool_scale', 'kv_in_norm', 'w_dkv',
           'kv_latent_norm', 'w_uk', 'w_uv', 'w_dq', 'q_latent_norm', 'w_uq', 'w_o', 'final_norm']


def _dot(a, b):
    return jnp.dot(a.astype(BF16), b.astype(BF16), preferred_element_type=F32)


def _dot_nt(a, b):
    return lax.dot_general(a.astype(BF16), b.astype(BF16), (((1,), (1,)), ((), ())),
                           preferred_element_type=F32)


def _dot_tn(a, b):
    return lax.dot_general(a.astype(BF16), b.astype(BF16), (((0,), (0,)), ((), ())),
                           preferred_element_type=F32)


def _sigmoid(x):
    return 1.0 / (1.0 + jnp.exp(-x))


def _rms(x, g):
    r = lax.rsqrt(jnp.mean(x * x, axis=-1, keepdims=True) + EPS)
    xh = x * r
    return xh * g, xh, r


def _rms_bwd(dy, xh, r, g):
    dxh = dy * g
    dx = r * (dxh - xh * jnp.mean(dxh * xh, axis=-1, keepdims=True))
    return dx, jnp.sum(dy * xh, axis=0, keepdims=True)


def _rope(x, c, s1, s2):
    return x * c + pltpu.roll(x, HEAD_LANES - QK_ROPE // 2, 1) * s1 + pltpu.roll(x, QK_ROPE // 2, 1) * s2


def _rope_t(d, c, s1, s2):
    return d * c + pltpu.roll(d * s1, QK_ROPE // 2, 1) + pltpu.roll(d * s2, HEAD_LANES - QK_ROPE // 2, 1)


def _params(*sem):
    return pltpu.CompilerParams(dimension_semantics=sem, vmem_limit_bytes=VMEM_LIMIT)


def _pick(n, candidates):
    for c in candidates:
        if n % c == 0:
            return c
    return n


def _row_tile(L):
    return _pick(L, (640, 128))


FF_OWNERS = 4


def _row_call(name, body, L, row_ins, full_ins, row_outs, acc_outs):
    tm = _row_tile(L)
    n = L // tm
    hb = tm // HALO
    nb = L // HALO
    per_head = (N_HEADS, tm, HEAD_LANES)
    in_specs, args = [], []
    for arr, kind in row_ins:
        c = arr.shape[-1]
        if kind == "tile":
            spec = pl.BlockSpec((tm, c), lambda i: (i, 0))
        elif kind == "heads":
            spec = pl.BlockSpec(per_head, lambda i: (0, i, 0))
        elif kind == "prev":
            spec = pl.BlockSpec((HALO, c), lambda i: (jnp.maximum(i * hb - 1, 0), 0))
        else:
            spec = pl.BlockSpec((HALO, c), lambda i: (jnp.minimum((i + 1) * hb, nb - 1), 0))
        in_specs.append(spec)
        args.append(arr)
    for arr in full_ins:
        in_specs.append(pl.BlockSpec(arr.shape, lambda i, nd=arr.ndim: (0,) * nd))
        args.append(arr)
    out_shape, out_specs = [], []
    for c, dt in row_outs:
        if c == HEADS:
            out_shape.append(jax.ShapeDtypeStruct((N_HEADS, L, HEAD_LANES), dt))
            out_specs.append(pl.BlockSpec(per_head, lambda i: (0, i, 0)))
        else:
            out_shape.append(jax.ShapeDtypeStruct((L, c), dt))
            out_specs.append(pl.BlockSpec((tm, c), lambda i: (i, 0)))
    for shp in acc_outs:
        out_shape.append(jax.ShapeDtypeStruct(shp, F32))
        out_specs.append(pl.BlockSpec(shp, lambda i, nd=len(shp): (0,) * nd))
    n_in, n_ro = len(args), len(row_outs)

    def kern(*refs):
        i = pl.program_id(0)
        vals = [jnp.concatenate([r[hd] for hd in range(N_HEADS)], axis=1) if kind == "heads" else r[...]
                for r, (_, kind) in zip(refs, row_ins)] + [r[...] for r in refs[len(row_ins):n_in]]
        ro, ao = body(i, n, tm, *vals)
        for r, v in zip(refs[n_in:n_in + n_ro], ro):
            if len(r.shape) == 3:
                for hd in range(N_HEADS):
                    r[hd] = v[:, hd * HEAD_LANES:(hd + 1) * HEAD_LANES].astype(r.dtype)
            else:
                r[...] = v.astype(r.dtype)
        acc_refs = refs[n_in + n_ro:]

        @pl.when(i == 0)
        def _():
            for r in acc_refs:
                r[...] = jnp.zeros(r.shape, r.dtype)

        for r, v in zip(acc_refs, ao):
            r[...] += v

    return pl.pallas_call(kern, name=name, grid=(n,), in_specs=in_specs, out_specs=out_specs,
                          out_shape=out_shape, compiler_params=_params("arbitrary"))(*args)


class _Side:
    def __init__(self, ins, out_shapes, sems, start, finish, onward=None):
        self.ins, self.out_shapes, self.sems, self.start, self.finish = ins, out_shapes, sems, start, finish
        self.onward = onward


ANY = pl.BlockSpec(memory_space=pl.ANY)


def _hosted(kern, n_in, n_out, n_scratch, side, is_first, is_late, is_last):
    if side is None:
        return kern
    ns_in, ns_out = len(side.ins), len(side.out_shapes)

    def wrapped(*refs):
        ins, refs = refs[:n_in], refs[n_in:]
        side_ins, refs = refs[:ns_in], refs[ns_in:]
        outs, refs = refs[:n_out], refs[n_out:]
        side_outs, refs = refs[:ns_out], refs[ns_out:]
        scratch, side_sems = refs[:n_scratch], refs[n_scratch:]

        @pl.when(is_first())
        def _():
            side.start(side_ins, side_outs, side_sems)

        kern(*ins, *outs, *scratch)

        if side.onward is not None:
            @pl.when(is_late())
            def _():
                side.onward(side_ins, side_outs, side_sems)

        @pl.when(is_last())
        def _():
            side.finish(side_ins, side_outs, side_sems)

    return wrapped


def _side_args(side):
    if side is None:
        return [], [], [], [], []
    return ([ANY] * len(side.ins), [ANY] * len(side.out_shapes), list(side.out_shapes), list(side.sems),
            list(side.ins))


def _ffn_weight_specs(fs, D, ents):
    return [pl.BlockSpec((FF_OWNERS, fs, D), lambda i, f, e=e: (f, e, 0)) for e in ents]


def _ffn_fwd(h, gam, wpiece, ents, fs, side=None):
    L, D = h.shape
    F = N_DEV * fs
    tm = _row_tile(L)
    tf = FF_OWNERS * fs
    nL, nF = L // tm, F // tf
    s_in, s_out, s_shape, s_sems, s_args = _side_args(side)

    def kern(h_ref, gam_ref, wg_ref, wu_ref, wd_ref, ho_ref, xn_ref, gs_ref, us_ref, acc):
        f = pl.program_id(1)

        @pl.when(f == 0)
        def _():
            xn, _, _ = _rms(h_ref[...], gam_ref[...])
            xn_ref[...] = xn.astype(BF16)
            acc[...] = jnp.zeros(acc.shape, F32)

        xnb = xn_ref[...]
        g = _dot_nt(xnb, wg_ref[...].reshape(tf, D))
        u = _dot_nt(xnb, wu_ref[...].reshape(tf, D))
        gs_ref[...] = g.astype(BF16)
        us_ref[...] = u.astype(BF16)
        acc[...] += _dot(g * _sigmoid(g) * u, wd_ref[...].reshape(tf, D))

        @pl.when(f == nF - 1)
        def _():
            ho_ref[...] = h_ref[...] + 0.5 * acc[...]

    first = lambda: (pl.program_id(0) == 0) & (pl.program_id(1) == 0)
    last = lambda: (pl.program_id(0) == nL - 1) & (pl.program_id(1) == nF - 1)
    late = lambda: (pl.program_id(0) == (2 * nL) // 3) & (pl.program_id(1) == 0)
    return pl.pallas_call(
        _hosted(kern, 5, 4, 1, side, first, late, last),
        name="ffn_fwd" if side is None else "ffn_fwd_hosting", grid=(nL, nF),
        in_specs=[pl.BlockSpec((tm, D), lambda i, f: (i, 0)),
                  pl.BlockSpec((None, 1, D), lambda i, f: (0, 0, 0))] + _ffn_weight_specs(fs, D, ents) + s_in,
        out_specs=[pl.BlockSpec((tm, D), lambda i, f: (i, 0)),
                   pl.BlockSpec((tm, D), lambda i, f: (i, 0)),
                   pl.BlockSpec((tm, tf), lambda i, f: (i, f)),
                   pl.BlockSpec((tm, tf), lambda i, f: (i, f))] + s_out,
        out_shape=[jax.ShapeDtypeStruct((L, D), F32), jax.ShapeDtypeStruct((L, D), BF16),
                   jax.ShapeDtypeStruct((L, F), BF16), jax.ShapeDtypeStruct((L, F), BF16)] + s_shape,
        scratch_shapes=[pltpu.VMEM((tm, D), F32)] + s_sems,
        compiler_params=_params("arbitrary", "arbitrary"))(h, gam, wpiece, wpiece, wpiece, *s_args)


def _ffn_bwd(dh, h, gs, us, gam, wpiece, ents, fs, side=None):
    L, D = h.shape
    F = N_DEV * fs
    tm = _pick(L, (416, 128))
    tf = FF_OWNERS * fs
    nL, nF = L // tm, F // tf
    s_in, s_out, s_shape, s_sems, s_args = _side_args(side)

    def kern(dh_ref, h_ref, gs_ref, us_ref, gam_ref, wg_ref, wu_ref, wd_ref,
             dhi_ref, dg_ref, du_ref, a_ref, dob_ref, dgam_ref, dxn):
        i = pl.program_id(0)
        f = pl.program_id(1)

        @pl.when(f == 0)
        def _():
            dxn[...] = jnp.zeros(dxn.shape, F32)
            dob_ref[...] = (0.5 * dh_ref[...]).astype(BF16)

        @pl.when((f == 0) & (i == 0))
        def _():
            dgam_ref[...] = jnp.zeros(dgam_ref.shape, F32)

        g = gs_ref[...].astype(F32)
        u = us_ref[...].astype(F32)
        sg = _sigmoid(g)
        silu = g * sg
        da = _dot_nt(dob_ref[...], wd_ref[...].reshape(tf, D))
        a_ref[...] = (silu * u).astype(BF16)
        dgt = (da * u * (sg * (1.0 + g * (1.0 - sg)))).astype(BF16)
        dut = (da * silu).astype(BF16)
        dg_ref[...] = dgt
        du_ref[...] = dut
        dxn[...] += _dot(dgt, wg_ref[...].reshape(tf, D)) + _dot(dut, wu_ref[...].reshape(tf, D))

        @pl.when(f == nF - 1)
        def _():
            gamma = gam_ref[...]
            _, xh, r = _rms(h_ref[...], gamma)
            dx, dgam = _rms_bwd(dxn[...], xh, r, gamma)
            dhi_ref[...] = dh_ref[...] + dx
            dgam_ref[...] += dgam

    first = lambda: (pl.program_id(0) == 0) & (pl.program_id(1) == 0)
    last = lambda: (pl.program_id(0) == nL - 1) & (pl.program_id(1) == nF - 1)
    late = lambda: (pl.program_id(0) == (2 * nL) // 3) & (pl.program_id(1) == 0)
    return pl.pallas_call(
        _hosted(kern, 8, 6, 1, side, first, late, last),
        name="ffn_bwd" if side is None else "ffn_bwd_hosting", grid=(nL, nF),
        in_specs=[pl.BlockSpec((tm, D), lambda i, f: (i, 0)),
                  pl.BlockSpec((tm, D), lambda i, f: (i, 0)),
                  pl.BlockSpec((tm, tf), lambda i, f: (i, f)),
                  pl.BlockSpec((tm, tf), lambda i, f: (i, f)),
                  pl.BlockSpec((None, 1, D), lambda i, f: (0, 0, 0))] + _ffn_weight_specs(fs, D, ents) + s_in,
        out_specs=[pl.BlockSpec((tm, D), lambda i, f: (i, 0)),
                   pl.BlockSpec((tm, tf), lambda i, f: (i, f)),
                   pl.BlockSpec((tm, tf), lambda i, f: (i, f)),
                   pl.BlockSpec((tm, tf), lambda i, f: (i, f)),
                   pl.BlockSpec((tm, D), lambda i, f: (i, 0)),
                   pl.BlockSpec((1, D), lambda i, f: (0, 0))] + s_out,
        out_shape=[jax.ShapeDtypeStruct((L, D), F32), jax.ShapeDtypeStruct((L, F), BF16),
                   jax.ShapeDtypeStruct((L, F), BF16), jax.ShapeDtypeStruct((L, F), BF16),
                   jax.ShapeDtypeStruct((L, D), BF16), jax.ShapeDtypeStruct((1, D), F32)] + s_shape,
        scratch_shapes=[pltpu.VMEM((tm, D), F32)] + s_sems,
        compiler_params=_params("arbitrary", "arbitrary"))(dh, h, gs, us, gam, wpiece, wpiece, wpiece, *s_args)


def _mm_tn(a, b, name):
    L, M = a.shape
    N = b.shape[1]
    tm = _pick(M, (1408, 1024, 512))
    tn = _pick(N, (1408, 1024, 512))
    tk = _pick(L, (2080, 640, 128))

    def kern(a_ref, b_ref, o_ref):
        @pl.when(pl.program_id(2) == 0)
        def _():
            o_ref[...] = jnp.zeros(o_ref.shape, F32)

        o_ref[...] += _dot_tn(a_ref[...], b_ref[...])

    return pl.pallas_call(
        kern, name=name, grid=(M // tm, N // tn, L // tk),
        in_specs=[pl.BlockSpec((tk, tm), lambda i, j, k: (k, i)),
                  pl.BlockSpec((tk, tn), lambda i, j, k: (k, j))],
        out_specs=pl.BlockSpec((tm, tn), lambda i, j, k: (i, j)),
        out_shape=jax.ShapeDtypeStruct((M, N), F32),
        compiler_params=_params("arbitrary", "arbitrary", "arbitrary"))(a, b)


def _pool_counts(pos, w):
    return jnp.clip(pos - (FRONT_PAD - 1), 1, w).astype(F32)


def _pool_forward_values(i, tm, h, hprev, gamma, D):
    cg = D // len(POOL_WINDOWS)
    hext = jnp.concatenate([hprev, h], axis=0)
    uext, xh, r = _rms(hext, gamma)
    pos = i * tm + lax.broadcasted_iota(jnp.int32, (tm, 1), 0)
    pooled = []
    for gi, w in enumerate(POOL_WINDOWS):
        s = uext[:, gi * cg:(gi + 1) * cg]
        span = 1
        while span < w:
            s = s + pltpu.roll(s, span, 0)
            span *= 2
        s = s[HALO:]
        pooled.append(s / _pool_counts(pos, w) - uext[HALO:, gi * cg:(gi + 1) * cg])
    return uext, xh[HALO:], r[HALO:], pooled


def _pool_fwd(h, gam, w, scale, l):
    L, D = h.shape
    cg = D // len(POOL_WINDOWS)

    def body(i, n, tm, ht, hprev, gamma, wv, sc):
        _, _, _, pooled = _pool_forward_values(i, tm, ht, hprev, gamma, D)
        ys = [_dot(pooled[gi], wv[gi]) for gi in range(len(POOL_WINDOWS))]
        y = jnp.concatenate(ys, axis=1) * sc
        return [ht + y], []

    del cg
    return _row_call("pool_fwd", body, L, [(h, "tile"), (h, "prev")], [gam, w, scale], [(D, F32)], [])[0]


def _pool_bwd(dy, h, gam, w, scale):
    L, D = h.shape
    ng = len(POOL_WINDOWS)
    cg = D // ng

    def body(i, n, tm, ht, hprev, dyt, dynext, gamma, wv, sc):
        _, xh, r, pooled = _pool_forward_values(i, tm, ht, hprev, gamma, D)
        dynext = jnp.where(i == n - 1, jnp.zeros_like(dynext), dynext)
        dyext = jnp.concatenate([dyt, dynext], axis=0) * sc
        pos_ext = i * tm + lax.broadcasted_iota(jnp.int32, (tm + HALO, 1), 0)
        dws, dscs, dus = [], [], []
        for gi, wd in enumerate(POOL_WINDOWS):
            cols = slice(gi * cg, (gi + 1) * cg)
            pb = pooled[gi].astype(BF16)
            ypre = _dot(pb, wv[gi])
            dscs.append(jnp.sum(dyt[:, cols] * ypre, axis=0, keepdims=True))
            dws.append(_dot_tn(pb, dyext[:tm, cols])[None])
            dp = _dot_nt(dyext[:, cols], wv[gi])
            s = dp / _pool_counts(pos_ext, wd)
            span = 1
            while span < wd:
                s = s + pltpu.roll(s, tm + HALO - span, 0)
                span *= 2
            dus.append(s[:tm] - dp[:tm])
        du = jnp.concatenate(dus, axis=1)
        pos = pos_ext[:tm]
        du = jnp.where(pos >= FRONT_PAD, du, 0.0)
        dx, dgam = _rms_bwd(du, xh, r, gamma)
        return [dyt + dx], [jnp.concatenate(dws, axis=0), jnp.concatenate(dscs, axis=1), dgam]

    return _row_call("pool_bwd", body, L, [(h, "tile"), (h, "prev"), (dy, "tile"), (dy, "next")],
                     [gam, w, scale], [(D, F32)], [(ng, cg, cg), (1, D), (1, D)])


def _kv_fwd(h, tabs, g1, wdkv, g2, wuk, wuv):
    L, D = h.shape
    hw = N_HEADS * HEAD_LANES

    def body(i, n, tm, ht, ck, s1, s2, g1v, wdkv_v, g2v, wuk_v, wuv_v):
        xkv, _, _ = _rms(ht, g1v)
        ckr = _dot(xkv, wdkv_v)
        ckv, _, _ = _rms(ckr[:, :KV_RANK], g2v)
        krope = _rope(ckr[:, KV_RANK:], ck, s1, s2)
        pos = i * tm + lax.broadcasted_iota(jnp.int32, (tm, HEAD_LANES), 0)
        lane = lax.broadcasted_iota(jnp.int32, (tm, HEAD_LANES), 1)
        krope = jnp.where((pos < FRONT_PAD) & (lane == BIAS_LANE), NEG, krope)
        ones = ((lax.broadcasted_iota(jnp.int32, (1, hw), 1) & (HEAD_LANES - 1)) == ONES_LANE).astype(F32)
        return [_dot(ckv, wuk_v), krope, _dot(ckv, wuv_v) + ones, ckr], []

    ck, s1, s2 = tabs["ck"], tabs["s1"], tabs["s2"]
    return _row_call("kv_fwd", body, L, [(h, "tile"), (ck, "tile"), (s1, "tile"), (s2, "tile")],
                     [g1, wdkv, g2, wuk, wuv],
                     [(HEADS, BF16), (HEAD_LANES, BF16), (HEADS, BF16), (KV_RANK + HEAD_LANES, F32)], [])


def _kv_bwd(dh, h, ckr, dks, dvs, tabs, g1, wdkv, g2, wuk, wuv):
    L, D = h.shape
    hw = N_HEADS * HEAD_LANES
    nl = len(dks)

    def body(i, n, tm, *vals):
        dht, ht, ckr_t = vals[:3]
        dk = sum(vals[3:3 + nl][1:], vals[3])
        dv = sum(vals[3 + nl:3 + 2 * nl][1:], vals[3 + nl])
        ck, s1, s2, g1v, wdkv_v, g2v, wuk_v, wuv_v = vals[3 + 2 * nl:]
        xkv, xh1, r1 = _rms(ht, g1v)
        ckv, xh2, r2 = _rms(ckr_t[:, :KV_RANK], g2v)
        dckv = _dot_nt(dk, wuk_v) + _dot_nt(dv, wuv_v)
        dlat, dg2 = _rms_bwd(dckv, xh2, r2, g2v)
        dkr = dk[:, :HEAD_LANES]
        for hd in range(1, N_HEADS):
            dkr = dkr + dk[:, hd * HEAD_LANES:(hd + 1) * HEAD_LANES]
        dckr = jnp.concatenate([dlat, _rope_t(dkr, ck, s1, s2)], axis=1)
        dx, dg1 = _rms_bwd(_dot_nt(dckr, wdkv_v), xh1, r1, g1v)
        return [dht + dx], [_dot_tn(xkv, dckr), _dot_tn(ckv, dk), _dot_tn(ckv, dv), dg1, dg2]

    row_ins = [(dh, "tile"), (h, "tile"), (ckr, "tile")] + [(a, "heads") for a in dks + dvs]
    row_ins += [(tabs[k], "tile") for k in ("ck", "s1", "s2")]
    return _row_call("kv_bwd", body, L, row_ins, [g1, wdkv, g2, wuk, wuv], [(D, F32)],
                     [(D, KV_RANK + HEAD_LANES), (KV_RANK, hw), (KV_RANK, hw), (1, D), (1, KV_RANK)])


def _q_fwd(h, tabs, g, wdq, gq, wuq):
    L, D = h.shape
    hw = N_HEADS * HEAD_LANES

    def body(i, n, tm, ht, cq_t, s1, s2, gv, wdq_v, gqv, wuq_v):
        u, _, _ = _rms(ht, gv)
        cqp = _dot(u, wdq_v)
        cq, _, _ = _rms(cqp, gqv)
        qp = _dot(cq, wuq_v)
        bias = (lax.broadcasted_iota(jnp.int32, (1, HEAD_LANES), 1) == BIAS_LANE).astype(F32)
        q = [_rope(qp[:, hd * HEAD_LANES:(hd + 1) * HEAD_LANES], cq_t, s1, s2) * (SM_SCALE * LOG2E) + bias
             for hd in range(N_HEADS)]
        return [jnp.concatenate(q, axis=1), cqp], []

    return _row_call("q_fwd", body, L, [(h, "tile")] + [(tabs[k], "tile") for k in ("cq", "s1", "s2")],
                     [g, wdq, gq, wuq], [(HEADS, BF16), (Q_RANK, F32)], [])


def _q_bwd(dh, h, cqp, dq, tabs, g, wdq, gq, wuq):
    L, D = h.shape
    hw = N_HEADS * HEAD_LANES

    def body(i, n, tm, dht, ht, cqp_t, dq_t, cq_t, s1, s2, gv, wdq_v, gqv, wuq_v):
        u, xh1, r1 = _rms(ht, gv)
        cq, xh2, r2 = _rms(cqp_t, gqv)
        dqp = jnp.concatenate([_rope_t(dq_t[:, hd * HEAD_LANES:(hd + 1) * HEAD_LANES], cq_t, s1, s2)
                               for hd in range(N_HEADS)], axis=1)
        dcqp, dgq = _rms_bwd(_dot_nt(dqp, wuq_v), xh2, r2, gqv)
        dx, dg = _rms_bwd(_dot_nt(dcqp, wdq_v), xh1, r1, gv)
        return [dht + dx], [_dot_tn(u, dcqp), _dot_tn(cq, dqp), dg, dgq]

    row_ins = [(dh, "tile"), (h, "tile"), (cqp, "tile"), (dq, "heads")]
    row_ins += [(tabs[k], "tile") for k in ("cq", "s1", "s2")]
    return _row_call("q_bwd", body, L, row_ins, [g, wdq, gq, wuq], [(D, F32)],
                     [(D, Q_RANK), (Q_RANK, hw), (1, D), (1, Q_RANK)])


def _oproj_fwd(h, o, wo):
    L, D = h.shape

    def body(i, n, tm, ht, ot, wov):
        return [ht + _dot(ot, wov)], []

    return _row_call("oproj_fwd", body, L, [(h, "tile"), (o, "heads")], [wo], [(D, F32)], [])[0]


def _oproj_bwd(dh, o, wo):
    L, D = dh.shape
    hw = N_HEADS * HEAD_LANES

    def body(i, n, tm, dht, ot, wov):
        do = _dot_nt(dht, wov)
        prod = do * ot.astype(F32)
        delta = [jnp.broadcast_to(jnp.sum(prod[:, hd * HEAD_LANES:(hd + 1) * HEAD_LANES], axis=-1, keepdims=True),
                                  (tm, HEAD_LANES)) for hd in range(N_HEADS)]
        return [do, jnp.concatenate(delta, axis=1)], [_dot_tn(ot, dht)]

    return _row_call("oproj_bwd", body, L, [(dh, "tile"), (o, "heads")], [wo], [(HEADS, BF16), (HEADS, F32)],
                     [(hw, D)])


def _causal(t, keys_first=False):
    qpos = lax.broadcasted_iota(jnp.int32, (t, t), 1 if keys_first else 0)
    kpos = lax.broadcasted_iota(jnp.int32, (t, t), 0 if keys_first else 1)
    return (kpos >> CHUNK_SHIFT) <= (qpos >> CHUNK_SHIFT)


SM_SCALE = 1.0 / math.sqrt(QK_NOPE + QK_ROPE)


def _pairs(n, key_major):
    if key_major:
        order = [(i, j) for j in range(n) for i in range(j, n)]
    else:
        order = [(i, j) for i in range(n) for j in range(i + 1)]
    return (jnp.array([p[0] for p in order], jnp.int32), jnp.array([p[1] for p in order], jnp.int32))


def _attn_fwd(q, kn, kr, v):
    L = q.shape[1]
    hw = N_HEADS * HEAD_LANES
    t = _row_tile(L)
    it, jt = _pairs(L // t, key_major=False)

    def kern(it_ref, jt_ref, q_ref, kn_ref, kr_ref, v_ref, o_ref, lse_ref, m_s, acc_s):
        step = pl.program_id(1)
        i, j = it_ref[step], jt_ref[step]

        @pl.when(j == 0)
        def _():
            m_s[...] = jnp.full(m_s.shape, NEG, F32)
            acc_s[...] = jnp.zeros(acc_s.shape, F32)

        def update(diagonal):
            k = kn_ref[...] + kr_ref[...]
            s = _dot_nt(q_ref[...], k)
            if diagonal:
                s = jnp.where(_causal(t), s, NEG)
            m_prev = m_s[:, :1]
            m_new = jnp.maximum(m_prev, jnp.max(s, axis=-1, keepdims=True))
            p = jnp.exp2(s - m_new)
            acc_s[...] = jnp.exp2(m_prev - m_new) * acc_s[...] + _dot(p, v_ref[...])
            m_s[...] = jnp.broadcast_to(m_new, m_s.shape)

        @pl.when(j < i)
        def _():
            update(False)

        @pl.when(j == i)
        def _():
            update(True)
            acc = acc_s[...]
            total = acc[:, ONES_LANE:ONES_LANE + 1]
            o_ref[...] = (acc / total).astype(BF16)
            lse_ref[...] = m_s[...] + jnp.log2(jnp.broadcast_to(total, m_s.shape))

    qmap = lambda h, s, it, jt: (h, it[s], 0)
    kmap = lambda h, s, it, jt: (h, jt[s], 0)
    blk = (t, HEAD_LANES)
    hblk = (None, t, HEAD_LANES)
    return pl.pallas_call(
        kern, name="attn_fwd",
        grid_spec=pltpu.PrefetchScalarGridSpec(
            num_scalar_prefetch=2, grid=(N_HEADS, it.shape[0]),
            in_specs=[pl.BlockSpec(hblk, qmap), pl.BlockSpec(hblk, kmap),
                      pl.BlockSpec(blk, lambda h, s, it, jt: (jt[s], 0)), pl.BlockSpec(hblk, kmap)],
            out_specs=[pl.BlockSpec(hblk, qmap), pl.BlockSpec(hblk, qmap)],
            scratch_shapes=[pltpu.VMEM(blk, F32)] * 2),
        out_shape=[jax.ShapeDtypeStruct((N_HEADS, L, HEAD_LANES), BF16),
                   jax.ShapeDtypeStruct((N_HEADS, L, HEAD_LANES), F32)],
        compiler_params=_params("arbitrary", "arbitrary"))(it, jt, q, kn, kr, v)


def _attn_bwd(q, kn, kr, v, do, lse, delta):
    L = q.shape[1]
    hw = N_HEADS * HEAD_LANES
    t = _row_tile(L)
    it, jt = _pairs(L // t, key_major=True)

    def kern(it_ref, jt_ref, q_ref, kn_ref, kr_ref, v_ref, do_ref, lse_ref, dl_ref, dq_ref, dk_ref, dv_ref):
        step = pl.program_id(1)
        i, j = it_ref[step], jt_ref[step]

        @pl.when(step == 0)
        def _():
            dq_ref[...] = jnp.zeros(dq_ref.shape, F32)

        @pl.when(i == j)
        def _():
            dk_ref[...] = jnp.zeros(dk_ref.shape, F32)
            dv_ref[...] = jnp.zeros(dv_ref.shape, F32)

        def update(diagonal):
            k = kn_ref[...] + kr_ref[...]
            qv, dov = q_ref[...], do_ref[...]
            s = _dot_nt(k, qv)
            if diagonal:
                s = jnp.where(_causal(t, keys_first=True), s, NEG)
            p = jnp.exp2(s - lse_ref[...])
            dp = _dot_nt(v_ref[...], dov)
            dz = (p * (dp - dl_ref[...])).astype(BF16)
            dv_ref[...] += _dot(p, dov)
            dk_ref[...] += _dot(dz, qv) * (1.0 / LOG2E)
            rows = pl.ds(pl.multiple_of(i * t, t), t)
            dq_ref[rows, :] += _dot_tn(dz, k) * SM_SCALE

        @pl.when(j < i)
        def _():
            update(False)

        @pl.when(j == i)
        def _():
            update(True)

    qmap = lambda h, s, it, jt: (h, it[s], 0)
    kmap = lambda h, s, it, jt: (h, jt[s], 0)
    rowmap = lambda h, s, it, jt: (h, 0, it[s])
    per_head_rows = lambda a: a[:, :, 0].reshape(N_HEADS, 1, L)
    blk = (t, HEAD_LANES)
    hblk = (None, t, HEAD_LANES)
    return pl.pallas_call(
        kern, name="attn_bwd",
        grid_spec=pltpu.PrefetchScalarGridSpec(
            num_scalar_prefetch=2, grid=(N_HEADS, it.shape[0]),
            in_specs=[pl.BlockSpec(hblk, qmap), pl.BlockSpec(hblk, kmap),
                      pl.BlockSpec(blk, lambda h, s, it, jt: (jt[s], 0)), pl.BlockSpec(hblk, kmap),
                      pl.BlockSpec(hblk, qmap), pl.BlockSpec((None, 1, t), rowmap), pl.BlockSpec((None, 1, t), rowmap)],
            out_specs=[pl.BlockSpec((None, L, HEAD_LANES), lambda h, s, it, jt: (h, 0, 0)),
                       pl.BlockSpec(hblk, kmap), pl.BlockSpec(hblk, kmap)]),
        out_shape=[jax.ShapeDtypeStruct((N_HEADS, L, HEAD_LANES), F32)] * 3,
        compiler_params=_params("arbitrary", "arbitrary"))(it, jt, q, kn, kr, v, do, per_head_rows(lse),
                                                            per_head_rows(delta))


def _head(h, target, g):
    L, D = h.shape

    def body(i, n, tm, ht, tt, gv):
        y, xh, r = _rms(ht, gv)
        pos = i * tm + lax.broadcasted_iota(jnp.int32, (tm, 1), 0)
        e = jnp.where(pos >= SEQ_START, y - tt, 0.0)
        loss = 0.5 * jnp.sum(jnp.mean(e * e, axis=-1, keepdims=True), axis=0, keepdims=True)
        dx, dg = _rms_bwd(e / D, xh, r, gv)
        return [dx], [jnp.broadcast_to(loss, (1, 128)), dg]

    return _row_call("loss_head", body, L, [(h, "tile"), (target, "tile")], [g], [(D, F32)], [(1, 128), (1, D)])


def _coords():
    return lax.axis_index("x"), lax.axis_index("y"), lax.axis_index("c")


def _my_index():
    mx, my, mc = _coords()
    return 4 * mx + 2 * my + mc


def _gather_side(x):
    R, W = x.shape

    def copies(x_ref, out_ref, send_sems, recv_sems):
        mx, my, mc = _coords()
        me, sibling = (mx, my, mc), (mx, my, 1 - mc)
        chips = [(1 - mx, my), (mx, 1 - my), (1 - mx, 1 - my)]

        def slot(px, py, pc):
            return out_ref.at[4 * px + 2 * py + pc]

        def copy(k, block, to, src=None):
            return pltpu.make_async_remote_copy(
                src_ref=slot(*block) if src is None else src, dst_ref=slot(*block),
                send_sem=send_sems.at[k], recv_sem=recv_sems.at[k], device_id=to, device_id_type=MESH)

        first = [copy(0, me, sibling, src=x_ref)]
        first += [copy(1 + n, me, (*chip, mc), src=x_ref) for n, chip in enumerate(chips)]
        passed = [copy(4 + n, (*chip, mc), sibling) for n, chip in enumerate(chips)]
        landed = [copy(1 + n, (*chip, mc), me) for n, chip in enumerate(chips)]
        from_sibling = [copy(0, sibling, me)] + [copy(4 + n, (*chip, 1 - mc), me) for n, chip in enumerate(chips)]
        return first, passed, landed, from_sibling

    def start(ins, outs, sems):
        for cp in copies(ins[0], outs[0], *sems)[0]:
            cp.start()

    def onward(ins, outs, sems):
        _, passed, landed, _ = copies(ins[0], outs[0], *sems)
        for arrived, on in zip(landed, passed):
            arrived.wait_recv()
            on.start()

    def finish(ins, outs, sems):
        first, passed, _, from_sibling = copies(ins[0], outs[0], *sems)
        for cp in from_sibling:
            cp.wait_recv()
        for cp in first + passed:
            cp.wait_send()

    return _Side([x], [jax.ShapeDtypeStruct((N_DEV, R, W), x.dtype)],
                 [pltpu.SemaphoreType.DMA((7,)), pltpu.SemaphoreType.DMA((7,))], start, finish, onward)


def _with_own(gathered, x):
    return lax.dynamic_update_slice(gathered, x[None], (_my_index(), 0, 0))


def _chip_side(parts):
    n_arr = len(parts)

    def copies(p_refs, out_refs, send_sems, recv_sems):
        mx, my, mc = _coords()
        chips = [(1 - mx, my), (mx, 1 - my), (1 - mx, 1 - my)]
        return [pltpu.make_async_remote_copy(
            src_ref=p_ref.at[2 * cx + cy], dst_ref=out_ref.at[n], send_sem=send_sems.at[3 * a + n],
            recv_sem=recv_sems.at[3 * a + n], device_id=(cx, cy, mc), device_id_type=MESH)
            for a, (p_ref, out_ref) in enumerate(zip(p_refs, out_refs)) for n, (cx, cy) in enumerate(chips)]

    def start(ins, outs, sems):
        for cp in copies(ins, outs, *sems):
            cp.start()

    def finish(ins, outs, sems):
        cps = copies(ins, outs, *sems)
        for cp in cps:
            cp.wait_recv()
        for cp in cps:
            cp.wait_send()

    return _Side(list(parts), [jax.ShapeDtypeStruct((3,) + p.shape[1:], p.dtype) for p in parts],
                 [pltpu.SemaphoreType.DMA((3 * n_arr,)), pltpu.SemaphoreType.DMA((3 * n_arr,))], start, finish)


def _join(a, b):
    n_in, n_out, n_sem = len(a.ins), len(a.out_shapes), len(a.sems)

    def both(f_a, f_b):
        def run(ins, outs, sems):
            if f_a is not None:
                f_a(ins[:n_in], outs[:n_out], sems[:n_sem])
            if f_b is not None:
                f_b(ins[n_in:], outs[n_out:], sems[n_sem:])
        return run

    onward = both(a.onward, b.onward) if (a.onward is not None or b.onward is not None) else None
    return _Side(a.ins + b.ins, a.out_shapes + b.out_shapes, a.sems + b.sems, both(a.start, b.start),
                 both(a.finish, b.finish), onward)


def _run_side(side, name):
    def kern(*refs):
        n_in, n_out = len(side.ins), len(side.out_shapes)
        ins, outs, sems = refs[:n_in], refs[n_in:n_in + n_out], refs[n_in + n_out:]
        side.start(ins, outs, sems)
        if side.onward is not None:
            side.onward(ins, outs, sems)
        side.finish(ins, outs, sems)

    return pl.pallas_call(kern, name=name, in_specs=[ANY] * len(side.ins), out_specs=[ANY] * len(side.out_shapes),
                          out_shape=list(side.out_shapes), scratch_shapes=list(side.sems))(*side.ins)


def _sibling_side(arrs):
    n_arr = len(arrs)

    def copies(g_refs, out_refs, send_sems, recv_sems):
        mx, my, mc = _coords()
        return [pltpu.make_async_remote_copy(
            src_ref=g_ref.at[n, 1 - mc], dst_ref=out_ref.at[n], send_sem=send_sems.at[N_CHIPS * a + n],
            recv_sem=recv_sems.at[N_CHIPS * a + n], device_id=(mx, my, 1 - mc), device_id_type=MESH)
            for a, (g_ref, out_ref) in enumerate(zip(g_refs, out_refs)) for n in range(N_CHIPS)]

    def start(ins, outs, sems):
        for cp in copies(ins, outs, *sems):
            cp.start()

    def finish(ins, outs, sems):
        cps = copies(ins, outs, *sems)
        for cp in cps:
            cp.wait_recv()
        for cp in cps:
            cp.wait_send()

    return _Side(list(arrs), [jax.ShapeDtypeStruct((N_CHIPS,) + g.shape[2:], g.dtype) for g in arrs],
                 [pltpu.SemaphoreType.DMA((N_CHIPS * n_arr,)), pltpu.SemaphoreType.DMA((N_CHIPS * n_arr,))],
                 start, finish)


def _add_own(owns, sel, others, name):
    n_arr = len(owns)
    R, W = owns[0].shape[-2:]
    tr = _pick(R, (PACK_ROW_MULT, 176, 64, 8))
    first_phase = owns[0].ndim == 4
    if first_phase:
        n = owns[0].shape[0]
        grid = (n, R // tr)
        in_specs = ([pl.BlockSpec((None, None, tr, W), lambda b, i, sel: (b, sel[0], i, 0))] * n_arr
                    + [pl.BlockSpec((None, tr, W), lambda b, i, sel: (b, i, 0))] * n_arr)
        out_specs = [pl.BlockSpec((None, tr, W), lambda b, i, sel: (b, i, 0))] * (2 * n_arr)
        out_shape = [jax.ShapeDtypeStruct((n, R, W), F32)] * n_arr + [jax.ShapeDtypeStruct((n, R, W), BF16)] * n_arr

        def kern(sel_ref, *refs):
            for a in range(n_arr):
                acc = refs[a][...] + refs[n_arr + a][...]
                refs[2 * n_arr + a][...] = acc
                refs[3 * n_arr + a][...] = acc.astype(BF16)
    else:
        k = others[0].shape[0]
        grid = (1, R // tr)
        in_specs = ([pl.BlockSpec((None, tr, W), lambda b, i, sel: (sel[0], i, 0))] * n_arr
                    + [pl.BlockSpec((k, tr, W), lambda b, i, sel: (0, i, 0))] * n_arr)
        out_specs = [pl.BlockSpec((tr, W), lambda b, i, sel: (i, 0))] * n_arr
        out_shape = [jax.ShapeDtypeStruct((R, W), F32)] * n_arr

        def kern(sel_ref, *refs):
            for a in range(n_arr):
                acc = refs[a][...]
                for m in range(k):
                    acc = acc + refs[n_arr + a][m].astype(F32)
                refs[2 * n_arr + a][...] = acc

    outs = pl.pallas_call(
        kern, name=name,
        grid_spec=pltpu.PrefetchScalarGridSpec(num_scalar_prefetch=1, grid=grid, in_specs=in_specs,
                                               out_specs=out_specs),
        out_shape=out_shape, compiler_params=_params("arbitrary", "arbitrary"))(sel, *owns, *others)
    return (outs[:n_arr], outs[n_arr:]) if first_phase else outs


def _sum_lead(x, name):
    n, R, W = x.shape
    tr = _pick(R, (PACK_ROW_MULT, 8))

    def kern(x_ref, o_ref):
        acc = x_ref[0]
        for k in range(1, n):
            acc = acc + x_ref[k]
        o_ref[...] = acc

    return pl.pallas_call(
        kern, name=name, grid=(R // tr,),
        in_specs=[pl.BlockSpec((n, tr, W), lambda i: (0, i, 0))],
        out_specs=pl.BlockSpec((tr, W), lambda i: (i, 0)),
        out_shape=jax.ShapeDtypeStruct((R, W), F32), compiler_params=_params("arbitrary"))(x)


def _adamw(w, g, m, v):
    shape = w.shape
    cols = shape[-1]
    rows = w.size // cols
    tr = _pick(rows, (512, 352, 256, 128))
    if rows * cols * 4 <= (1 << 20):
        tr = rows

    def kern(w_ref, g_ref, m_ref, v_ref, d_ref, mo_ref, vo_ref):
        gv = g_ref[...]
        mn = ADAM_B1 * m_ref[...] + (1.0 - ADAM_B1) * gv
        vn = ADAM_B2 * v_ref[...] + (1.0 - ADAM_B2) * (gv * gv)
        m_hat = mn / (1.0 - ADAM_B1 ** ADAM_STEP)
        v_hat = vn / (1.0 - ADAM_B2 ** ADAM_STEP)
        d_ref[...] = -ADAM_LR * (m_hat / (jnp.sqrt(v_hat) + ADAM_EPS) + ADAM_WD * w_ref[...])
        mo_ref[...] = mn
        vo_ref[...] = vn

    spec = pl.BlockSpec((tr, cols), lambda i: (i, 0))
    outs = pl.pallas_call(
        kern, name="adamw", grid=(rows // tr,), in_specs=[spec] * 4, out_specs=[spec] * 3,
        out_shape=[jax.ShapeDtypeStruct((rows, cols), F32)] * 3, compiler_params=_params("arbitrary"),
    )(*[a.reshape(rows, cols) for a in (w, g, m, v)])
    return [o.reshape(shape) for o in outs]


def _pack(arrs, n_lead, row_mult, width, total_mult=PACK_ROW_MULT):
    parts, total = [], 0
    for n, a in enumerate(arrs):
        lead = a.shape[:n_lead]
        flat = a.reshape(lead + (-1,))
        size = flat.shape[-1]
        rows = -(-size // (width * row_mult)) * row_mult
        if n == len(arrs) - 1:
            rows += -(total + rows) % total_mult
        total += rows
        if rows * width > size:
            flat = jnp.concatenate([flat, jnp.zeros(lead + (rows * width - size,), flat.dtype)], axis=n_lead)
        parts.append(flat.reshape(lead + (rows, width)))
    return jnp.concatenate(parts, axis=n_lead)


def _unpack(pack, shapes, n_lead, row_mult):
    outs, row = [], 0
    lead = pack.shape[:n_lead]
    width = pack.shape[-1]
    for shp in shapes:
        size = math.prod(shp)
        rows = -(-size // (width * row_mult)) * row_mult
        blk = lax.slice_in_dim(pack, row, row + rows, axis=n_lead)
        outs.append(blk.reshape(lead + (-1,))[..., :size].reshape(lead + tuple(shp)))
        row += rows
    return outs


def _to_words(a):
    return lax.bitcast_convert_type(a, BF16)


def _from_words(a):
    return lax.bitcast_convert_type(a, F32)


def _pad_axis(a, axis, size):
    pads = [(0, 0)] * a.ndim
    pads[axis] = (0, size - a.shape[axis])
    return jnp.pad(a, pads)


def _dense(name, s):
    if name.endswith("w_gate") or name.endswith("w_up"):
        _, nl, d, fs = s.shape
        return s.transpose(1, 2, 0, 3).reshape(nl, d, N_DEV * fs)
    if name.endswith("w_down"):
        _, nl, fs, d = s.shape
        return s.transpose(1, 0, 2, 3).reshape(nl, N_DEV * fs, d)
    if name == "pool_w":
        _, nl, ng, r, cg = s.shape
        return s.transpose(1, 2, 0, 3, 4).reshape(nl, ng, cg, cg)
    if name == "w_dkv":
        w = s.reshape(-1, s.shape[2])
        z = lambda n: jnp.zeros((w.shape[0], n), w.dtype)
        return jnp.concatenate([w[:, :KV_RANK], z(ROPE_LANE0), w[:, KV_RANK:],
                                z(HEAD_LANES - ROPE_LANE0 - QK_ROPE)], axis=1)
    if name in ("w_uk", "w_uv"):
        return _pad_axis(s.transpose(1, 0, 2), 2, HEAD_LANES).reshape(KV_RANK, N_HEADS * HEAD_LANES)
    if name == "w_dq":
        _, nl, ds, r = s.shape
        return s.transpose(1, 0, 2, 3).reshape(nl, N_DEV * ds, r)
    if name == "w_uq":
        nl = s.shape[1]
        return _pad_axis(s.transpose(1, 2, 0, 3), 3, HEAD_LANES).reshape(nl, Q_RANK, N_HEADS * HEAD_LANES)
    if name == "w_o":
        _, nl, k, dc = s.shape
        w = s.transpose(1, 2, 0, 3).reshape(nl, N_HEADS, V_HEAD, N_DEV * dc)
        return _pad_axis(w, 2, HEAD_LANES).reshape(nl, N_HEADS * HEAD_LANES, N_DEV * dc)
    if name in ("meta_tokens", "pool_scale"):
        r, dc = s.shape[1:]
        return s.transpose(1, 0, 2).reshape(r, N_DEV * dc)
    raise ValueError(name)


def _shards(name, g):
    if name.endswith("w_gate") or name.endswith("w_up"):
        nl, d, f = g.shape
        return g.reshape(nl, d, N_DEV, f // N_DEV).transpose(2, 0, 1, 3)
    if name.endswith("w_down"):
        nl, f, d = g.shape
        return g.reshape(nl, N_DEV, f // N_DEV, d).transpose(1, 0, 2, 3)
    if name == "pool_w":
        nl, ng, cg, _ = g.shape
        return g.reshape(nl, ng, N_DEV, cg // N_DEV, cg).transpose(2, 0, 1, 3, 4)
    if name == "w_dkv":
        w = jnp.concatenate([g[:, :KV_RANK], g[:, KV_RANK + ROPE_LANE0:KV_RANK + ROPE_LANE0 + QK_ROPE]], axis=1)
        return w.reshape(N_DEV, -1, KV_RANK + QK_ROPE)
    if name in ("w_uk", "w_uv"):
        return g.reshape(KV_RANK, N_HEADS, HEAD_LANES)[:, :, :V_HEAD].transpose(1, 0, 2)
    if name == "w_dq":
        nl, d, r = g.shape
        return g.reshape(nl, N_DEV, d // N_DEV, r).transpose(1, 0, 2, 3)
    if name == "w_uq":
        nl = g.shape[0]
        return g.reshape(nl, Q_RANK, N_HEADS, HEAD_LANES)[..., :QK_NOPE + QK_ROPE].transpose(2, 0, 1, 3)
    if name == "w_o":
        nl, _, d = g.shape
        w = g.reshape(nl, N_HEADS, HEAD_LANES, d)[:, :, :V_HEAD].reshape(nl, N_HEADS * V_HEAD, N_DEV, d // N_DEV)
        return w.transpose(2, 0, 1, 3)
    if name in ("meta_tokens", "pool_scale"):
        r, d = g.shape
        return g.reshape(r, N_DEV, d // N_DEV).transpose(1, 0, 2)
    raise ValueError(name)


def _rope_tables(L):
    pos = jnp.maximum(jnp.arange(L) - FRONT_PAD, 0).astype(F32)
    inv = 1.0 / (ROPE_THETA ** (jnp.arange(0, QK_ROPE, 2, dtype=F32) / QK_ROPE))
    ang = pos[:, None] * inv[None, :]
    cos, sin = jnp.cos(ang), jnp.sin(ang)
    half = QK_ROPE // 2
    z = lambda n: jnp.zeros((L, n), F32)
    tail = z(HEAD_LANES - ROPE_LANE0 - QK_ROPE)
    return {
        "cq": jnp.concatenate([jnp.ones((L, ROPE_LANE0), F32), cos, cos, tail], axis=1),
        "ck": jnp.concatenate([z(ROPE_LANE0), cos, cos, tail], axis=1),
        "s1": jnp.concatenate([z(ROPE_LANE0), -sin, z(half), tail], axis=1),
        "s2": jnp.concatenate([z(ROPE_LANE0), z(half), sin, tail], axis=1),
    }


def kernel(x, meta_tokens, ffn1_norm, ffn1_w_gate, ffn1_w_up, ffn1_w_down, mix_norm, ffn2_norm, ffn2_w_gate, ffn2_w_up, ffn2_w_down, pool_w, pool_scale, kv_in_norm, w_dkv, kv_latent_norm, w_uk, w_uv, w_dq, q_latent_norm, w_uq, w_o, final_norm, loss_target, m_meta_tokens, m_ffn1_norm, m_ffn1_w_gate, m_ffn1_w_up, m_ffn1_w_down, m_mix_norm, m_ffn2_norm, m_ffn2_w_gate, m_ffn2_w_up, m_ffn2_w_down, m_pool_w, m_pool_scale, m_kv_in_norm, m_w_dkv, m_kv_latent_norm, m_w_uk, m_w_uv, m_w_dq, m_q_latent_norm, m_w_uq, m_w_o, m_final_norm, v_meta_tokens, v_ffn1_norm, v_ffn1_w_gate, v_ffn1_w_up, v_ffn1_w_down, v_mix_norm, v_ffn2_norm, v_ffn2_w_gate, v_ffn2_w_up, v_ffn2_w_down, v_pool_w, v_pool_scale, v_kv_in_norm, v_w_dkv, v_kv_latent_norm, v_w_uk, v_w_uv, v_w_dq, v_q_latent_norm, v_w_uq, v_w_o, v_final_norm):
    args = dict(locals())
    W = {n: args[n] for n in WEIGHTS}
    M = {n: args["m_" + n] for n in WEIGHTS}
    V = {n: args["v_" + n] for n in WEIGHTS}
    seq, D = x.shape[1], x.shape[2]
    L = SEQ_START + seq

    fs = ffn1_w_down.shape[1]

    ffns = [(l, which) for l in range(DEPTH) for which in (1, 2)]
    ffn_names = lambda which: FFN_WEIGHTS[3 * (which - 1):3 * which]

    extras = {ffns[0]: ["meta_tokens"], ffns[1]: [n for n in SMALL_SHARDED if n != "meta_tokens"]}

    def weight_piece(l, which):
        parts = [(W[n][l] if n.endswith("w_down") else W[n][l].T).astype(BF16) for n in ffn_names(which)]
        parts += [_to_words(W[n]) if n in SHARDED_F32 else W[n].astype(BF16) for n in extras.get((l, which), [])]
        return _pack(parts, 0, 16, D, 16)

    pieces = {k: weight_piece(*k) for k in ffns}
    full = {}
    P = {}

    def take_piece(k, gathered):
        full[k] = _with_own(gathered, pieces[k])
        names = extras.get(k, [])
        shapes = [(fs, D)] * 3 + [W[n].shape + ((2,) if n in SHARDED_F32 else ()) for n in names]
        for n, s in zip(names, _unpack(full[k], shapes, 1, 16)[3:]):
            P[n] = _dense(n, _from_words(s) if n in SHARDED_F32 else s)

    take_piece(ffns[0], _run_side(_gather_side(pieces[ffns[0]]), "all_gather_first")[0])
    norm3 = lambda a: a.reshape(a.shape[0], 1, a.shape[-1])
    row = lambda a: a.reshape(1, -1)
    g_ffn, g_mix = {1: norm3(ffn1_norm), 2: norm3(ffn2_norm)}, mix_norm
    ffn_entries = (0, 1, 2)

    def ffn_forward(h, l, which):
        at = ffns.index((l, which))
        nxt = ffns[at + 1] if at + 1 < len(ffns) else None
        side = None if nxt is None else _gather_side(pieces[nxt])
        outs = _ffn_fwd(h, g_ffn[which][l:l + 1], full[(l, which)], ffn_entries, fs, side)
        if nxt is not None:
            take_piece(nxt, outs[4])
        return outs[:4]

    h = jnp.concatenate([jnp.zeros((FRONT_PAD, D), F32), P["meta_tokens"], x[0]], axis=0)
    target = jnp.concatenate([jnp.zeros((SEQ_START, D), F32), loss_target[0]], axis=0)
    tabs = _rope_tables(L)
    saved = []
    kv = None
    for l in range(DEPTH):
        s = {"h1": h}
        h, s["xn1"], s["g1"], s["u1"] = ffn_forward(h, l, 1)
        s["hm"] = h
        if l < N_POOL_LAYERS:
            h = _pool_fwd(h, row(g_mix[l]), P["pool_w"][l], row(P["pool_scale"][l]), l)
        else:
            j = l - N_POOL_LAYERS
            s["q"], s["cqp"] = _q_fwd(h, tabs, row(g_mix[l]), P["w_dq"][j], row(q_latent_norm[j]), P["w_uq"][j])
            s["o"], s["lse"] = _attn_fwd(s["q"], kv["kn"], kv["kr"], kv["v"])
            h = _oproj_fwd(h, s["o"], P["w_o"][j])
        s["h2"] = h
        h, s["xn2"], s["g2"], s["u2"] = ffn_forward(h, l, 2)
        if l == N_POOL_LAYERS - 1:
            kv = {"h": h}
            kv["kn"], kv["kr"], kv["v"], kv["ckr"] = _kv_fwd(
                h, tabs, row(kv_in_norm), P["w_dkv"], row(kv_latent_norm), P["w_uk"], P["w_uv"])
        saved.append(s)
    dh, loss_row, d_final = _head(h, target, row(final_norm))
    loss = lax.psum(loss_row[0, 0], ("x", "y", "c"))

    G = {}
    stack = {n: [None] * DEPTH for n in ("ffn1_norm", "ffn1_w_gate", "ffn1_w_up", "ffn1_w_down", "mix_norm",
                                         "ffn2_norm", "ffn2_w_gate", "ffn2_w_up", "ffn2_w_down")}
    pool_dw, pool_ds = [None] * N_POOL_LAYERS, [None] * N_POOL_LAYERS
    mla = {n: [None] * (DEPTH - N_POOL_LAYERS) for n in ("w_dq", "w_uq", "w_o", "q_latent_norm")}
    dks, dvs = [], []
    my_core = lax.axis_index("c").astype(jnp.int32).reshape(1)
    my_chip = (2 * lax.axis_index("x") + lax.axis_index("y")).astype(jnp.int32).reshape(1)
    layer_grads = {n: [None] * DEPTH for n in FFN_WEIGHTS}
    grads = {}

    by_owner = lambda g: g.reshape((N_CHIPS, 2, g.shape[0] // N_DEV) + g.shape[1:])

    def reduce_cores(arrs, others):
        return _add_own(arrs, my_core, others, "sum_cores")

    def finish_layer(l, partials, from_chips):
        for n, g in zip(FFN_WEIGHTS, _add_own(partials, my_chip, from_chips, "sum_chips")):
            layer_grads[n][l] = g if n.endswith("w_down") else g.T

    def ffn_backward(dh, s, which, l, side=None):
        outs = _ffn_bwd(dh, s["h%d" % which], s["g%d" % which], s["u%d" % which], g_ffn[which][l:l + 1],
                        full[(l, which)], ffn_entries, fs, side)
        dh, dg, du, act, dob, dgam = outs[:6]
        xn = s["xn%d" % which]
        stack["ffn%d_w_gate" % which][l] = _mm_tn(dg, xn, "ffn_dw")
        stack["ffn%d_w_up" % which][l] = _mm_tn(du, xn, "ffn_dw")
        stack["ffn%d_w_down" % which][l] = _mm_tn(act, dob, "ffn_dw")
        stack["ffn%d_norm" % which][l] = dgam
        return dh, outs[6:]

    arrs = None
    for l in reversed(range(DEPTH)):
        s = saved[l]
        if l == N_POOL_LAYERS - 1:
            dh, d_dkv, d_uk, d_uv, d_kvin, d_kvlat = _kv_bwd(
                dh, kv["h"], kv["ckr"], dks, dvs, tabs, row(kv_in_norm), P["w_dkv"], row(kv_latent_norm),
                P["w_uk"], P["w_uv"])
            G.update(w_dkv=d_dkv, w_uk=d_uk, w_uv=d_uv, kv_in_norm=d_kvin, kv_latent_norm=d_kvlat)
        if arrs is None:
            dh, _ = ffn_backward(dh, s, 2, l)
        else:
            dh, from_sibling = ffn_backward(dh, s, 2, l, _sibling_side(arrs))
            partials, partials_bf16 = reduce_cores(arrs, from_sibling)
        if l < N_POOL_LAYERS:
            dh, pool_dw[l], pool_ds[l], stack["mix_norm"][l] = _pool_bwd(
                dh, s["hm"], row(g_mix[l]), P["pool_w"][l], row(P["pool_scale"][l]))
        else:
            j = l - N_POOL_LAYERS
            do, delta_o, mla["w_o"][j] = _oproj_bwd(dh, s["o"], P["w_o"][j])
            dq, dk, dv = _attn_bwd(s["q"], kv["kn"], kv["kr"], kv["v"], do, s["lse"], delta_o)
            dks.append(dk)
            dvs.append(dv)
            dh, mla["w_dq"][j], mla["w_uq"][j], stack["mix_norm"][l], mla["q_latent_norm"][j] = _q_bwd(
                dh, s["hm"], s["cqp"], dq, tabs, row(g_mix[l]), P["w_dq"][j], row(q_latent_norm[j]), P["w_uq"][j])
        if arrs is None:
            dh, _ = ffn_backward(dh, s, 1, l)
        else:
            dh, from_chips = ffn_backward(dh, s, 1, l, _chip_side(partials_bf16))
            finish_layer(l + 1, partials, from_chips)
        arrs = [by_owner(stack[n][l]) for n in FFN_WEIGHTS]
    grad_x = dh[SEQ_START:][None]
    for n in ("ffn1_norm", "mix_norm", "ffn2_norm"):
        G[n] = jnp.concatenate(stack[n], axis=0)
    G["pool_w"] = jnp.stack(pool_dw)
    G["pool_scale"] = jnp.concatenate(pool_ds, axis=0)
    G["w_dq"], G["w_uq"], G["w_o"] = (jnp.stack(mla[n]) for n in ("w_dq", "w_uq", "w_o"))
    G["q_latent_norm"] = jnp.concatenate(mla["q_latent_norm"], axis=0)
    G["meta_tokens"] = dh[FRONT_PAD:SEQ_START]
    G["final_norm"] = d_final

    spack = _pack([_shards(n, G[n]) for n in SMALL_SHARDED], 1, 8, D)
    spack = spack.reshape((N_CHIPS, 2) + spack.shape[1:])
    rep_shapes = [W[n].shape for n in REPLICATED]
    rpack = _pack([G[n].reshape(W[n].shape) for n in REPLICATED], 0, 8, D)
    others = _run_side(_join(_sibling_side(arrs + [spack]), _gather_side(rpack)), "sibling_exchange_last")
    others, everyones = others[:-1], _with_own(others[-1], rpack)
    partials, partials_bf16 = reduce_cores(arrs, others[:-1])
    small, small_bf16 = reduce_cores([spack], others[-1:])
    from_chips = _run_side(_chip_side(list(partials_bf16) + list(small_bf16)), "chip_exchange_last")
    finish_layer(0, partials, from_chips[:-1])
    small_mine = _add_own(small, my_chip, from_chips[-1:], "sum_chips")[0]
    grads.update(zip(SMALL_SHARDED, _unpack(small_mine, [W[n].shape for n in SMALL_SHARDED], 0, 8)))
    for n in FFN_WEIGHTS:
        grads[n] = jnp.stack(layer_grads[n])
    grads.update(zip(REPLICATED, _unpack(_sum_lead(everyones, "sum_devices"), rep_shapes, 0, 8)))

    delta, new_m, new_v = {}, {}, {}
    for n in WEIGHTS:
        delta[n], new_m[n], new_v[n] = _adamw(W[n], grads[n], M[n], V[n])
    return (loss, grad_x, *[grads[n] for n in WEIGHTS], *[delta[n] for n in WEIGHTS],
            *[new_m[n] for n in WEIGHTS], *[new_v[n] for n in WEIGHTS])
```

```python
import functools
import math

import jax
import jax.numpy as jnp
from jax import lax
from jax.experimental import pallas as pl
from jax.experimental.pallas import tpu as pltpu

F32 = jnp.float32
BF16 = jnp.bfloat16
MESH = pl.DeviceIdType.MESH

N_DEV = 8
N_CHIPS = 4
DEPTH = 4
N_POOL_LAYERS = 2
N_HEADS = 8
QK_NOPE = 64
QK_ROPE = 32
V_HEAD = 64
KV_RANK = 256
Q_RANK = 384
HEAD_LANES = 128
HEADS = "heads"
ROPE_LANE0 = QK_NOPE
BIAS_LANE = QK_NOPE + QK_ROPE
ONES_LANE = V_HEAD
LOG2E = math.log2(math.e)
N_META = 16
CHUNK_SHIFT = 6
FRONT_PAD = 112
SEQ_START = FRONT_PAD + N_META
HALO = 16
POOL_WINDOWS = (2, 4, 8, 16)
EPS = 1e-6
ROPE_THETA = 10000.0
NEG = -1e30
PACK_ROW_MULT = 256
VMEM_LIMIT = 56 * 1024 * 1024

ADAM_LR = 0.001
ADAM_B1 = 0.9
ADAM_B2 = 0.999
ADAM_EPS = 1e-08
ADAM_WD = 0.01
ADAM_STEP = 10

SHARDED = ["ffn1_w_gate", "ffn1_w_up", "ffn1_w_down", "ffn2_w_gate", "ffn2_w_up", "ffn2_w_down",
           "pool_w", "w_dkv", "w_uk", "w_uv", "w_dq", "w_uq", "w_o", "meta_tokens", "pool_scale"]
FFN_WEIGHTS = SHARDED[:6]
SMALL_SHARDED = SHARDED[6:]
SHARDED_F32 = ("meta_tokens", "pool_scale")
REPLICATED = ["ffn1_norm", "mix_norm", "ffn2_norm", "kv_in_norm", "kv_latent_norm", "q_latent_norm",
              "final_norm"]
WEIGHTS = ['meta_tokens', 'ffn1_norm', 'ffn1_w_gate', 'ffn1_w_up', 'ffn1_w_down', 'mix_norm', 'ffn2_norm',
           'ffn2_w_gate', 'ffn2_w_up', 'ffn2_w_down', 'pool_w', 'pool_scale', 'kv_in_norm', 'w_dkv',
           'kv_latent_norm', 'w_uk', 'w_uv', 'w_dq', 'q_latent_norm', 'w_uq', 'w_o', 'final_norm']


def _dot(a, b):
    return jnp.dot(a.astype(BF16), b.astype(BF16), preferred_element_type=F32)


def _dot_nt(a, b):
    return lax.dot_general(a.astype(BF16), b.astype(BF16), (((1,), (1,)), ((), ())),
                           preferred_element_type=F32)


def _dot_tn(a, b):
    return lax.dot_general(a.astype(BF16), b.astype(BF16), (((0,), (0,)), ((), ())),
                           preferred_element_type=F32)


def _sigmoid(x):
    return 1.0 / (1.0 + jnp.exp(-x))


def _rms(x, g):
    r = lax.rsqrt(jnp.mean(x * x, axis=-1, keepdims=True) + EPS)
    xh = x * r
    return xh * g, xh, r


def _rms_bwd(dy, xh, r, g):
    dxh = dy * g
    dx = r * (dxh - xh * jnp.mean(dxh * xh, axis=-1, keepdims=True))
    return dx, jnp.sum(dy * xh, axis=0, keepdims=True)


def _rope(x, c, s1, s2):
    return x * c + pltpu.roll(x, HEAD_LANES - QK_ROPE // 2, 1) * s1 + pltpu.roll(x, QK_ROPE // 2, 1) * s2


def _rope_t(d, c, s1, s2):
    return d * c + pltpu.roll(d * s1, QK_ROPE // 2, 1) + pltpu.roll(d * s2, HEAD_LANES - QK_ROPE // 2, 1)


def _params(*sem):
    return pltpu.CompilerParams(dimension_semantics=sem, vmem_limit_bytes=VMEM_LIMIT)


def _pick(n, candidates):
    for c in candidates:
        if n % c == 0:
            return c
    return n


def _row_tile(L):
    return _pick(L, (640, 128))


FF_OWNERS = 4


def _row_call(name, body, L, row_ins, full_ins, row_outs, acc_outs):
    tm = _row_tile(L)
    n = L // tm
    hb = tm // HALO
    nb = L // HALO
    per_head = (N_HEADS, tm, HEAD_LANES)
    in_specs, args = [], []
    for arr, kind in row_ins:
        c = arr.shape[-1]
        if kind == "tile":
            spec = pl.BlockSpec((tm, c), lambda i: (i, 0))
        elif kind == "heads":
            spec = pl.BlockSpec(per_head, lambda i: (0, i, 0))
        elif kind == "prev":
            spec = pl.BlockSpec((HALO, c), lambda i: (jnp.maximum(i * hb - 1, 0), 0))
        else:
            spec = pl.BlockSpec((HALO, c), lambda i: (jnp.minimum((i + 1) * hb, nb - 1), 0))
        in_specs.append(spec)
        args.append(arr)
    for arr in full_ins:
        in_specs.append(pl.BlockSpec(arr.shape, lambda i, nd=arr.ndim: (0,) * nd))
        args.append(arr)
    out_shape, out_specs = [], []
    for c, dt in row_outs:
        if c == HEADS:
            out_shape.append(jax.ShapeDtypeStruct((N_HEADS, L, HEAD_LANES), dt))
            out_specs.append(pl.BlockSpec(per_head, lambda i: (0, i, 0)))
        else:
            out_shape.append(jax.ShapeDtypeStruct((L, c), dt))
            out_specs.append(pl.BlockSpec((tm, c), lambda i: (i, 0)))
    for shp in acc_outs:
        out_shape.append(jax.ShapeDtypeStruct(shp, F32))
        out_specs.append(pl.BlockSpec(shp, lambda i, nd=len(shp): (0,) * nd))
    n_in, n_ro = len(args), len(row_outs)

    def kern(*refs):
        i = pl.program_id(0)
        vals = [jnp.concatenate([r[hd] for hd in range(N_HEADS)], axis=1) if kind == "heads" else r[...]
                for r, (_, kind) in zip(refs, row_ins)] + [r[...] for r in refs[len(row_ins):n_in]]
        ro, ao = body(i, n, tm, *vals)
        for r, v in zip(refs[n_in:n_in + n_ro], ro):
            if len(r.shape) == 3:
                for hd in range(N_HEADS):
                    r[hd] = v[:, hd * HEAD_LANES:(hd + 1) * HEAD_LANES].astype(r.dtype)
            else:
                r[...] = v.astype(r.dtype)
        acc_refs = refs[n_in + n_ro:]

        @pl.when(i == 0)
        def _():
            for r in acc_refs:
                r[...] = jnp.zeros(r.shape, r.dtype)

        for r, v in zip(acc_refs, ao):
            r[...] += v

    return pl.pallas_call(kern, name=name, grid=(n,), in_specs=in_specs, out_specs=out_specs,
                          out_shape=out_shape, compiler_params=_params("arbitrary"))(*args)


class _Side:
    def __init__(self, ins, out_shapes, sems, start, finish, onward=None):
        self.ins, self.out_shapes, self.sems, self.start, self.finish = ins, out_shapes, sems, start, finish
        self.onward = onward


ANY = pl.BlockSpec(memory_space=pl.ANY)


def _hosted(kern, n_in, n_out, n_scratch, side, is_first, is_late, is_last):
    if side is None:
        return kern
    ns_in, ns_out = len(side.ins), len(side.out_shapes)

    def wrapped(*refs):
        ins, refs = refs[:n_in], refs[n_in:]
        side_ins, refs = refs[:ns_in], refs[ns_in:]
        outs, refs = refs[:n_out], refs[n_out:]
        side_outs, refs = refs[:ns_out], refs[ns_out:]
        scratch, side_sems = refs[:n_scratch], refs[n_scratch:]

        @pl.when(is_first())
        def _():
            side.start(side_ins, side_outs, side_sems)

        kern(*ins, *outs, *scratch)

        if side.onward is not None:
            @pl.when(is_late())
            def _():
                side.onward(side_ins, side_outs, side_sems)

        @pl.when(is_last())
        def _():
            side.finish(side_ins, side_outs, side_sems)

    return wrapped


def _side_args(side):
    if side is None:
        return [], [], [], [], []
    return ([ANY] * len(side.ins), [ANY] * len(side.out_shapes), list(side.out_shapes), list(side.sems),
            list(side.ins))


def _ffn_weight_specs(fs, D, ents):
    return [pl.BlockSpec((FF_OWNERS, fs, D), lambda i, f, e=e: (f, e, 0)) for e in ents]


def _ffn_fwd(h, gam, wpiece, ents, fs, side=None):
    L, D = h.shape
    F = N_DEV * fs
    tm = _row_tile(L)
    tf = FF_OWNERS * fs
    nL, nF = L // tm, F // tf
    s_in, s_out, s_shape, s_sems, s_args = _side_args(side)

    def kern(h_ref, gam_ref, wg_ref, wu_ref, wd_ref, ho_ref, xn_ref, gs_ref, us_ref, acc):
        f = pl.program_id(1)

        @pl.when(f == 0)
        def _():
            xn, _, _ = _rms(h_ref[...], gam_ref[...])
            xn_ref[...] = xn.astype(BF16)
            acc[...] = jnp.zeros(acc.shape, F32)

        xnb = xn_ref[...]
        g = _dot_nt(xnb, wg_ref[...].reshape(tf, D))
        u = _dot_nt(xnb, wu_ref[...].reshape(tf, D))
        gs_ref[...] = g.astype(BF16)
        us_ref[...] = u.astype(BF16)
        acc[...] += _dot(g * _sigmoid(g) * u, wd_ref[...].reshape(tf, D))

        @pl.when(f == nF - 1)
        def _():
            ho_ref[...] = h_ref[...] + 0.5 * acc[...]

    first = lambda: (pl.program_id(0) == 0) & (pl.program_id(1) == 0)
    last = lambda: (pl.program_id(0) == nL - 1) & (pl.program_id(1) == nF - 1)
    late = lambda: (pl.program_id(0) == (2 * nL) // 3) & (pl.program_id(1) == 0)
    return pl.pallas_call(
        _hosted(kern, 5, 4, 1, side, first, late, last),
        name="ffn_fwd" if side is None else "ffn_fwd_hosting", grid=(nL, nF),
        in_specs=[pl.BlockSpec((tm, D), lambda i, f: (i, 0)),
                  pl.BlockSpec((None, 1, D), lambda i, f: (0, 0, 0))] + _ffn_weight_specs(fs, D, ents) + s_in,
        out_specs=[pl.BlockSpec((tm, D), lambda i, f: (i, 0)),
                   pl.BlockSpec((tm, D), lambda i, f: (i, 0)),
                   pl.BlockSpec((tm, tf), lambda i, f: (i, f)),
                   pl.BlockSpec((tm, tf), lambda i, f: (i, f))] + s_out,
        out_shape=[jax.ShapeDtypeStruct((L, D), F32), jax.ShapeDtypeStruct((L, D), BF16),
                   jax.ShapeDtypeStruct((L, F), BF16), jax.ShapeDtypeStruct((L, F), BF16)] + s_shape,
        scratch_shapes=[pltpu.VMEM((tm, D), F32)] + s_sems,
        compiler_params=_params("arbitrary", "arbitrary"))(h, gam, wpiece, wpiece, wpiece, *s_args)


def _ffn_bwd(dh, h, gs, us, gam, wpiece, ents, fs, side=None):
    L, D = h.shape
    F = N_DEV * fs
    tm = _pick(L, (416, 128))
    tf = FF_OWNERS * fs
    nL, nF = L // tm, F // tf
    s_in, s_out, s_shape, s_sems, s_args = _side_args(side)

    def kern(dh_ref, h_ref, gs_ref, us_ref, gam_ref, wg_ref, wu_ref, wd_ref,
             dhi_ref, dg_ref, du_ref, a_ref, dob_ref, dgam_ref, dxn):
        i = pl.program_id(0)
        f = pl.program_id(1)

        @pl.when(f == 0)
        def _():
            dxn[...] = jnp.zeros(dxn.shape, F32)
            dob_ref[...] = (0.5 * dh_ref[...]).astype(BF16)

        @pl.when((f == 0) & (i == 0))
        def _():
            dgam_ref[...] = jnp.zeros(dgam_ref.shape, F32)

        g = gs_ref[...].astype(F32)
        u = us_ref[...].astype(F32)
        sg = _sigmoid(g)
        silu = g * sg
        da = _dot_nt(dob_ref[...], wd_ref[...].reshape(tf, D))
        a_ref[...] = (silu * u).astype(BF16)
        dgt = (da * u * (sg * (1.0 + g * (1.0 - sg)))).astype(BF16)
        dut = (da * silu).astype(BF16)
        dg_ref[...] = dgt
        du_ref[...] = dut
        dxn[...] += _dot(dgt, wg_ref[...].reshape(tf, D)) + _dot(dut, wu_ref[...].reshape(tf, D))

        @pl.when(f == nF - 1)
        def _():
            gamma = gam_ref[...]
            _, xh, r = _rms(h_ref[...], gamma)
            dx, dgam = _rms_bwd(dxn[...], xh, r, gamma)
            dhi_ref[...] = dh_ref[...] + dx
            dgam_ref[...] += dgam

    first = lambda: (pl.program_id(0) == 0) & (pl.program_id(1) == 0)
    last = lambda: (pl.program_id(0) == nL - 1) & (pl.program_id(1) == nF - 1)
    late = lambda: (pl.program_id(0) == (2 * nL) // 3) & (pl.program_id(1) == 0)
    return pl.pallas_call(
        _hosted(kern, 8, 6, 1, side, first, late, last),
        name="ffn_bwd" if side is None else "ffn_bwd_hosting", grid=(nL, nF),
        in_specs=[pl.BlockSpec((tm, D), lambda i, f: (i, 0)),
                  pl.BlockSpec((tm, D), lambda i, f: (i, 0)),
                  pl.BlockSpec((tm, tf), lambda i, f: (i, f)),
                  pl.BlockSpec((tm, tf), lambda i, f: (i, f)),
                  pl.BlockSpec((None, 1, D), lambda i, f: (0, 0, 0))] + _ffn_weight_specs(fs, D, ents) + s_in,
        out_specs=[pl.BlockSpec((tm, D), lambda i, f: (i, 0)),
                   pl.BlockSpec((tm, tf), lambda i, f: (i, f)),
                   pl.BlockSpec((tm, tf), lambda i, f: (i, f)),
                   pl.BlockSpec((tm, tf), lambda i, f: (i, f)),
                   pl.BlockSpec((tm, D), lambda i, f: (i, 0)),
                   pl.BlockSpec((1, D), lambda i, f: (0, 0))] + s_out,
        out_shape=[jax.ShapeDtypeStruct((L, D), F32), jax.ShapeDtypeStruct((L, F), BF16),
                   jax.ShapeDtypeStruct((L, F), BF16), jax.ShapeDtypeStruct((L, F), BF16),
                   jax.ShapeDtypeStruct((L, D), BF16), jax.ShapeDtypeStruct((1, D), F32)] + s_shape,
        scratch_shapes=[pltpu.VMEM((tm, D), F32)] + s_sems,
        compiler_params=_params("arbitrary", "arbitrary"))(dh, h, gs, us, gam, wpiece, wpiece, wpiece, *s_args)


def _mm_tn(a, b, name):
    L, M = a.shape
    N = b.shape[1]
    tm = _pick(M, (1408, 1024, 512))
    tn = _pick(N, (1408, 1024, 512))
    tk = _pick(L, (2080, 640, 128))
    nk = L // tk

    def kern(a_ref, b_ref, o_ref, ob_ref):
        @pl.when(pl.program_id(2) == 0)
        def _():
            o_ref[...] = jnp.zeros(o_ref.shape, F32)

        o_ref[...] += _dot_tn(a_ref[...], b_ref[...])

        @pl.when(pl.program_id(2) == nk - 1)
        def _():
            ob_ref[...] = o_ref[...].astype(BF16)

    return pl.pallas_call(
        kern, name=name, grid=(M // tm, N // tn, nk),
        in_specs=[pl.BlockSpec((tk, tm), lambda i, j, k: (k, i)),
                  pl.BlockSpec((tk, tn), lambda i, j, k: (k, j))],
        out_specs=[pl.BlockSpec((tm, tn), lambda i, j, k: (i, j))] * 2,
        out_shape=[jax.ShapeDtypeStruct((M, N), F32), jax.ShapeDtypeStruct((M, N), BF16)],
        compiler_params=_params("arbitrary", "arbitrary", "arbitrary"))(a, b)


def _pool_counts(pos, w):
    return jnp.clip(pos - (FRONT_PAD - 1), 1, w).astype(F32)


def _pool_forward_values(i, tm, h, hprev, gamma, D):
    cg = D // len(POOL_WINDOWS)
    hext = jnp.concatenate([hprev, h], axis=0)
    uext, xh, r = _rms(hext, gamma)
    pos = i * tm + lax.broadcasted_iota(jnp.int32, (tm, 1), 0)
    pooled = []
    for gi, w in enumerate(POOL_WINDOWS):
        s = uext[:, gi * cg:(gi + 1) * cg]
        span = 1
        while span < w:
            s = s + pltpu.roll(s, span, 0)
            span *= 2
        s = s[HALO:]
        pooled.append(s / _pool_counts(pos, w) - uext[HALO:, gi * cg:(gi + 1) * cg])
    return uext, xh[HALO:], r[HALO:], pooled


def _pool_fwd(h, gam, w, scale, l):
    L, D = h.shape
    cg = D // len(POOL_WINDOWS)

    def body(i, n, tm, ht, hprev, gamma, wv, sc):
        _, _, _, pooled = _pool_forward_values(i, tm, ht, hprev, gamma, D)
        ys = [_dot(pooled[gi], wv[gi]) for gi in range(len(POOL_WINDOWS))]
        y = jnp.concatenate(ys, axis=1) * sc
        return [ht + y], []

    del cg
    return _row_call("pool_fwd", body, L, [(h, "tile"), (h, "prev")], [gam, w, scale], [(D, F32)], [])[0]


def _pool_bwd(dy, h, gam, w, scale):
    L, D = h.shape
    ng = len(POOL_WINDOWS)
    cg = D // ng

    def body(i, n, tm, ht, hprev, dyt, dynext, gamma, wv, sc):
        _, xh, r, pooled = _pool_forward_values(i, tm, ht, hprev, gamma, D)
        dynext = jnp.where(i == n - 1, jnp.zeros_like(dynext), dynext)
        dyext = jnp.concatenate([dyt, dynext], axis=0) * sc
        pos_ext = i * tm + lax.broadcasted_iota(jnp.int32, (tm + HALO, 1), 0)
        dws, dscs, dus = [], [], []
        for gi, wd in enumerate(POOL_WINDOWS):
            cols = slice(gi * cg, (gi + 1) * cg)
            pb = pooled[gi].astype(BF16)
            ypre = _dot(pb, wv[gi])
            dscs.append(jnp.sum(dyt[:, cols] * ypre, axis=0, keepdims=True))
            dws.append(_dot_tn(pb, dyext[:tm, cols])[None])
            dp = _dot_nt(dyext[:, cols], wv[gi])
            s = dp / _pool_counts(pos_ext, wd)
            span = 1
            while span < wd:
                s = s + pltpu.roll(s, tm + HALO - span, 0)
                span *= 2
            dus.append(s[:tm] - dp[:tm])
        du = jnp.concatenate(dus, axis=1)
        pos = pos_ext[:tm]
        du = jnp.where(pos >= FRONT_PAD, du, 0.0)
        dx, dgam = _rms_bwd(du, xh, r, gamma)
        return [dyt + dx], [jnp.concatenate(dws, axis=0), jnp.concatenate(dscs, axis=1), dgam]

    return _row_call("pool_bwd", body, L, [(h, "tile"), (h, "prev"), (dy, "tile"), (dy, "next")],
                     [gam, w, scale], [(D, F32)], [(ng, cg, cg), (1, D), (1, D)])


def _kv_fwd(h, tabs, g1, wdkv, g2, wuk, wuv):
    L, D = h.shape
    hw = N_HEADS * HEAD_LANES

    def body(i, n, tm, ht, ck, s1, s2, g1v, wdkv_v, g2v, wuk_v, wuv_v):
        xkv, _, _ = _rms(ht, g1v)
        ckr = _dot(xkv, wdkv_v)
        ckv, _, _ = _rms(ckr[:, :KV_RANK], g2v)
        krope = _rope(ckr[:, KV_RANK:], ck, s1, s2)
        pos = i * tm + lax.broadcasted_iota(jnp.int32, (tm, HEAD_LANES), 0)
        lane = lax.broadcasted_iota(jnp.int32, (tm, HEAD_LANES), 1)
        krope = jnp.where((pos < FRONT_PAD) & (lane == BIAS_LANE), NEG, krope)
        ones = ((lax.broadcasted_iota(jnp.int32, (1, hw), 1) & (HEAD_LANES - 1)) == ONES_LANE).astype(F32)
        return [_dot(ckv, wuk_v), krope, _dot(ckv, wuv_v) + ones, ckr], []

    ck, s1, s2 = tabs["ck"], tabs["s1"], tabs["s2"]
    return _row_call("kv_fwd", body, L, [(h, "tile"), (ck, "tile"), (s1, "tile"), (s2, "tile")],
                     [g1, wdkv, g2, wuk, wuv],
                     [(HEADS, BF16), (HEAD_LANES, BF16), (HEADS, BF16), (KV_RANK + HEAD_LANES, F32)], [])


def _kv_bwd(dh, h, ckr, dks, dvs, tabs, g1, wdkv, g2, wuk, wuv):
    L, D = h.shape
    hw = N_HEADS * HEAD_LANES
    nl = len(dks)

    def body(i, n, tm, *vals):
        dht, ht, ckr_t = vals[:3]
        dk = sum(vals[3:3 + nl][1:], vals[3])
        dv = sum(vals[3 + nl:3 + 2 * nl][1:], vals[3 + nl])
        ck, s1, s2, g1v, wdkv_v, g2v, wuk_v, wuv_v = vals[3 + 2 * nl:]
        xkv, xh1, r1 = _rms(ht, g1v)
        ckv, xh2, r2 = _rms(ckr_t[:, :KV_RANK], g2v)
        dckv = _dot_nt(dk, wuk_v) + _dot_nt(dv, wuv_v)
        dlat, dg2 = _rms_bwd(dckv, xh2, r2, g2v)
        dkr = dk[:, :HEAD_LANES]
        for hd in range(1, N_HEADS):
            dkr = dkr + dk[:, hd * HEAD_LANES:(hd + 1) * HEAD_LANES]
        dckr = jnp.concatenate([dlat, _rope_t(dkr, ck, s1, s2)], axis=1)
        dx, dg1 = _rms_bwd(_dot_nt(dckr, wdkv_v), xh1, r1, g1v)
        return [dht + dx], [_dot_tn(xkv, dckr), _dot_tn(ckv, dk), _dot_tn(ckv, dv), dg1, dg2]

    row_ins = [(dh, "tile"), (h, "tile"), (ckr, "tile")] + [(a, "heads") for a in dks + dvs]
    row_ins += [(tabs[k], "tile") for k in ("ck", "s1", "s2")]
    return _row_call("kv_bwd", body, L, row_ins, [g1, wdkv, g2, wuk, wuv], [(D, F32)],
                     [(D, KV_RANK + HEAD_LANES), (KV_RANK, hw), (KV_RANK, hw), (1, D), (1, KV_RANK)])


def _q_fwd(h, tabs, g, wdq, gq, wuq):
    L, D = h.shape
    hw = N_HEADS * HEAD_LANES

    def body(i, n, tm, ht, cq_t, s1, s2, gv, wdq_v, gqv, wuq_v):
        u, _, _ = _rms(ht, gv)
        cqp = _dot(u, wdq_v)
        cq, _, _ = _rms(cqp, gqv)
        qp = _dot(cq, wuq_v)
        bias = (lax.broadcasted_iota(jnp.int32, (1, HEAD_LANES), 1) == BIAS_LANE).astype(F32)
        q = [_rope(qp[:, hd * HEAD_LANES:(hd + 1) * HEAD_LANES], cq_t, s1, s2) * (SM_SCALE * LOG2E) + bias
             for hd in range(N_HEADS)]
        return [jnp.concatenate(q, axis=1), cqp], []

    return _row_call("q_fwd", body, L, [(h, "tile")] + [(tabs[k], "tile") for k in ("cq", "s1", "s2")],
                     [g, wdq, gq, wuq], [(HEADS, BF16), (Q_RANK, F32)], [])


def _q_bwd(dh, h, cqp, dq, tabs, g, wdq, gq, wuq):
    L, D = h.shape
    hw = N_HEADS * HEAD_LANES

    def body(i, n, tm, dht, ht, cqp_t, dq_t, cq_t, s1, s2, gv, wdq_v, gqv, wuq_v):
        u, xh1, r1 = _rms(ht, gv)
        cq, xh2, r2 = _rms(cqp_t, gqv)
        dqp = jnp.concatenate([_rope_t(dq_t[:, hd * HEAD_LANES:(hd + 1) * HEAD_LANES], cq_t, s1, s2)
                               for hd in range(N_HEADS)], axis=1)
        dcqp, dgq = _rms_bwd(_dot_nt(dqp, wuq_v), xh2, r2, gqv)
        dx, dg = _rms_bwd(_dot_nt(dcqp, wdq_v), xh1, r1, gv)
        return [dht + dx], [_dot_tn(u, dcqp), _dot_tn(cq, dqp), dg, dgq]

    row_ins = [(dh, "tile"), (h, "tile"), (cqp, "tile"), (dq, "heads")]
    row_ins += [(tabs[k], "tile") for k in ("cq", "s1", "s2")]
    return _row_call("q_bwd", body, L, row_ins, [g, wdq, gq, wuq], [(D, F32)],
                     [(D, Q_RANK), (Q_RANK, hw), (1, D), (1, Q_RANK)])


def _oproj_fwd(h, o, wo):
    L, D = h.shape

    def body(i, n, tm, ht, ot, wov):
        return [ht + _dot(ot, wov)], []

    return _row_call("oproj_fwd", body, L, [(h, "tile"), (o, "heads")], [wo], [(D, F32)], [])[0]


def _oproj_bwd(dh, o, wo):
    L, D = dh.shape
    hw = N_HEADS * HEAD_LANES

    def body(i, n, tm, dht, ot, wov):
        do = _dot_nt(dht, wov)
        prod = do * ot.astype(F32)
        delta = [jnp.broadcast_to(jnp.sum(prod[:, hd * HEAD_LANES:(hd + 1) * HEAD_LANES], axis=-1, keepdims=True),
                                  (tm, HEAD_LANES)) for hd in range(N_HEADS)]
        return [do, jnp.concatenate(delta, axis=1)], [_dot_tn(ot, dht)]

    return _row_call("oproj_bwd", body, L, [(dh, "tile"), (o, "heads")], [wo], [(HEADS, BF16), (HEADS, F32)],
                     [(hw, D)])


def _causal(t, keys_first=False):
    qpos = lax.broadcasted_iota(jnp.int32, (t, t), 1 if keys_first else 0)
    kpos = lax.broadcasted_iota(jnp.int32, (t, t), 0 if keys_first else 1)
    return (kpos >> CHUNK_SHIFT) <= (qpos >> CHUNK_SHIFT)


SM_SCALE = 1.0 / math.sqrt(QK_NOPE + QK_ROPE)


def _pairs(n, key_major):
    if key_major:
        order = [(i, j) for j in range(n) for i in range(j, n)]
    else:
        order = [(i, j) for i in range(n) for j in range(i + 1)]
    return (jnp.array([p[0] for p in order], jnp.int32), jnp.array([p[1] for p in order], jnp.int32))


def _attn_fwd(q, kn, kr, v):
    L = q.shape[1]
    hw = N_HEADS * HEAD_LANES
    t = _row_tile(L)
    it, jt = _pairs(L // t, key_major=False)

    def kern(it_ref, jt_ref, q_ref, kn_ref, kr_ref, v_ref, o_ref, lse_ref, m_s, acc_s):
        step = pl.program_id(1)
        i, j = it_ref[step], jt_ref[step]

        @pl.when(j == 0)
        def _():
            m_s[...] = jnp.full(m_s.shape, NEG, F32)
            acc_s[...] = jnp.zeros(acc_s.shape, F32)

        def update(diagonal):
            k = kn_ref[...] + kr_ref[...]
            s = _dot_nt(q_ref[...], k)
            if diagonal:
                s = jnp.where(_causal(t), s, NEG)
            m_prev = m_s[:, :1]
            m_new = jnp.maximum(m_prev, jnp.max(s, axis=-1, keepdims=True))
            p = jnp.exp2(s - m_new)
            acc_s[...] = jnp.exp2(m_prev - m_new) * acc_s[...] + _dot(p, v_ref[...])
            m_s[...] = jnp.broadcast_to(m_new, m_s.shape)

        @pl.when(j < i)
        def _():
            update(False)

        @pl.when(j == i)
        def _():
            update(True)
            acc = acc_s[...]
            total = acc[:, ONES_LANE:ONES_LANE + 1]
            o_ref[...] = (acc / total).astype(BF16)
            lse_ref[...] = m_s[...] + jnp.log2(jnp.broadcast_to(total, m_s.shape))

    qmap = lambda h, s, it, jt: (h, it[s], 0)
    kmap = lambda h, s, it, jt: (h, jt[s], 0)
    blk = (t, HEAD_LANES)
    hblk = (None, t, HEAD_LANES)
    return pl.pallas_call(
        kern, name="attn_fwd",
        grid_spec=pltpu.PrefetchScalarGridSpec(
            num_scalar_prefetch=2, grid=(N_HEADS, it.shape[0]),
            in_specs=[pl.BlockSpec(hblk, qmap), pl.BlockSpec(hblk, kmap),
                      pl.BlockSpec(blk, lambda h, s, it, jt: (jt[s], 0)), pl.BlockSpec(hblk, kmap)],
            out_specs=[pl.BlockSpec(hblk, qmap), pl.BlockSpec(hblk, qmap)],
            scratch_shapes=[pltpu.VMEM(blk, F32)] * 2),
        out_shape=[jax.ShapeDtypeStruct((N_HEADS, L, HEAD_LANES), BF16),
                   jax.ShapeDtypeStruct((N_HEADS, L, HEAD_LANES), F32)],
        compiler_params=_params("arbitrary", "arbitrary"))(it, jt, q, kn, kr, v)


def _attn_bwd(q, kn, kr, v, do, lse, delta):
    L = q.shape[1]
    hw = N_HEADS * HEAD_LANES
    t = _row_tile(L)
    it, jt = _pairs(L // t, key_major=True)

    def kern(it_ref, jt_ref, q_ref, kn_ref, kr_ref, v_ref, do_ref, lse_ref, dl_ref, dq_ref, dk_ref, dv_ref):
        step = pl.program_id(1)
        i, j = it_ref[step], jt_ref[step]

        @pl.when(step == 0)
        def _():
            dq_ref[...] = jnp.zeros(dq_ref.shape, F32)

        @pl.when(i == j)
        def _():
            dk_ref[...] = jnp.zeros(dk_ref.shape, F32)
            dv_ref[...] = jnp.zeros(dv_ref.shape, F32)

        def update(diagonal):
            k = kn_ref[...] + kr_ref[...]
            qv, dov = q_ref[...], do_ref[...]
            s = _dot_nt(k, qv)
            if diagonal:
                s = jnp.where(_causal(t, keys_first=True), s, NEG)
            p = jnp.exp2(s - lse_ref[...])
            dp = _dot_nt(v_ref[...], dov)
            dz = (p * (dp - dl_ref[...])).astype(BF16)
            dv_ref[...] += _dot(p, dov)
            dk_ref[...] += _dot(dz, qv) * (1.0 / LOG2E)
            rows = pl.ds(pl.multiple_of(i * t, t), t)
            dq_ref[rows, :] += _dot_tn(dz, k) * SM_SCALE

        @pl.when(j < i)
        def _():
            update(False)

        @pl.when(j == i)
        def _():
            update(True)

    qmap = lambda h, s, it, jt: (h, it[s], 0)
    kmap = lambda h, s, it, jt: (h, jt[s], 0)
    rowmap = lambda h, s, it, jt: (h, 0, it[s])
    per_head_rows = lambda a: a[:, :, 0].reshape(N_HEADS, 1, L)
    blk = (t, HEAD_LANES)
    hblk = (None, t, HEAD_LANES)
    return pl.pallas_call(
        kern, name="attn_bwd",
        grid_spec=pltpu.PrefetchScalarGridSpec(
            num_scalar_prefetch=2, grid=(N_HEADS, it.shape[0]),
            in_specs=[pl.BlockSpec(hblk, qmap), pl.BlockSpec(hblk, kmap),
                      pl.BlockSpec(blk, lambda h, s, it, jt: (jt[s], 0)), pl.BlockSpec(hblk, kmap),
                      pl.BlockSpec(hblk, qmap), pl.BlockSpec((None, 1, t), rowmap), pl.BlockSpec((None, 1, t), rowmap)],
            out_specs=[pl.BlockSpec((None, L, HEAD_LANES), lambda h, s, it, jt: (h, 0, 0)),
                       pl.BlockSpec(hblk, kmap), pl.BlockSpec(hblk, kmap)]),
        out_shape=[jax.ShapeDtypeStruct((N_HEADS, L, HEAD_LANES), F32)] * 3,
        compiler_params=_params("arbitrary", "arbitrary"))(it, jt, q, kn, kr, v, do, per_head_rows(lse),
                                                            per_head_rows(delta))


def _head(h, target, g):
    L, D = h.shape

    def body(i, n, tm, ht, tt, gv):
        y, xh, r = _rms(ht, gv)
        pos = i * tm + lax.broadcasted_iota(jnp.int32, (tm, 1), 0)
        e = jnp.where(pos >= SEQ_START, y - tt, 0.0)
        loss = 0.5 * jnp.sum(jnp.mean(e * e, axis=-1, keepdims=True), axis=0, keepdims=True)
        dx, dg = _rms_bwd(e / D, xh, r, gv)
        return [dx], [jnp.broadcast_to(loss, (1, 128)), dg]

    return _row_call("loss_head", body, L, [(h, "tile"), (target, "tile")], [g], [(D, F32)], [(1, 128), (1, D)])


def _coords():
    return lax.axis_index("x"), lax.axis_index("y"), lax.axis_index("c")


def _my_index():
    mx, my, mc = _coords()
    return 4 * mx + 2 * my + mc


def _gather_side(x):
    R, W = x.shape

    def copies(x_ref, out_ref, send_sems, recv_sems):
        mx, my, mc = _coords()
        me, sibling = (mx, my, mc), (mx, my, 1 - mc)
        chips = [(1 - mx, my), (mx, 1 - my), (1 - mx, 1 - my)]

        def slot(px, py, pc):
            return out_ref.at[4 * px + 2 * py + pc]

        def copy(k, block, to, src=None):
            return pltpu.make_async_remote_copy(
                src_ref=slot(*block) if src is None else src, dst_ref=slot(*block),
                send_sem=send_sems.at[k], recv_sem=recv_sems.at[k], device_id=to, device_id_type=MESH)

        first = [copy(0, me, sibling, src=x_ref)]
        first += [copy(1 + n, me, (*chip, mc), src=x_ref) for n, chip in enumerate(chips)]
        passed = [copy(4 + n, (*chip, mc), sibling) for n, chip in enumerate(chips)]
        landed = [copy(1 + n, (*chip, mc), me) for n, chip in enumerate(chips)]
        from_sibling = [copy(0, sibling, me)] + [copy(4 + n, (*chip, 1 - mc), me) for n, chip in enumerate(chips)]
        return first, passed, landed, from_sibling

    def start(ins, outs, sems):
        for cp in copies(ins[0], outs[0], *sems)[0]:
            cp.start()

    def onward(ins, outs, sems):
        _, passed, landed, _ = copies(ins[0], outs[0], *sems)
        for arrived, on in zip(landed, passed):
            arrived.wait_recv()
            on.start()

    def finish(ins, outs, sems):
        first, passed, _, from_sibling = copies(ins[0], outs[0], *sems)
        for cp in from_sibling:
            cp.wait_recv()
        for cp in first + passed:
            cp.wait_send()

    return _Side([x], [jax.ShapeDtypeStruct((N_DEV, R, W), x.dtype)],
                 [pltpu.SemaphoreType.DMA((7,)), pltpu.SemaphoreType.DMA((7,))], start, finish, onward)


def _with_own(gathered, x):
    return lax.dynamic_update_slice(gathered, x[None], (_my_index(), 0, 0))


def _chip_side(parts):
    n_arr = len(parts)

    def copies(p_refs, out_refs, send_sems, recv_sems):
        mx, my, mc = _coords()
        chips = [(1 - mx, my), (mx, 1 - my), (1 - mx, 1 - my)]
        return [pltpu.make_async_remote_copy(
            src_ref=p_ref.at[2 * cx + cy], dst_ref=out_ref.at[n], send_sem=send_sems.at[3 * a + n],
            recv_sem=recv_sems.at[3 * a + n], device_id=(cx, cy, mc), device_id_type=MESH)
            for a, (p_ref, out_ref) in enumerate(zip(p_refs, out_refs)) for n, (cx, cy) in enumerate(chips)]

    def start(ins, outs, sems):
        for cp in copies(ins, outs, *sems):
            cp.start()

    def finish(ins, outs, sems):
        cps = copies(ins, outs, *sems)
        for cp in cps:
            cp.wait_recv()
        for cp in cps:
            cp.wait_send()

    return _Side(list(parts), [jax.ShapeDtypeStruct((3,) + p.shape[1:], p.dtype) for p in parts],
                 [pltpu.SemaphoreType.DMA((3 * n_arr,)), pltpu.SemaphoreType.DMA((3 * n_arr,))], start, finish)


def _join(a, b):
    n_in, n_out, n_sem = len(a.ins), len(a.out_shapes), len(a.sems)

    def both(f_a, f_b):
        def run(ins, outs, sems):
            if f_a is not None:
                f_a(ins[:n_in], outs[:n_out], sems[:n_sem])
            if f_b is not None:
                f_b(ins[n_in:], outs[n_out:], sems[n_sem:])
        return run

    onward = both(a.onward, b.onward) if (a.onward is not None or b.onward is not None) else None
    return _Side(a.ins + b.ins, a.out_shapes + b.out_shapes, a.sems + b.sems, both(a.start, b.start),
                 both(a.finish, b.finish), onward)


def _run_side(side, name):
    def kern(*refs):
        n_in, n_out = len(side.ins), len(side.out_shapes)
        ins, outs, sems = refs[:n_in], refs[n_in:n_in + n_out], refs[n_in + n_out:]
        side.start(ins, outs, sems)
        if side.onward is not None:
            side.onward(ins, outs, sems)
        side.finish(ins, outs, sems)

    return pl.pallas_call(kern, name=name, in_specs=[ANY] * len(side.ins), out_specs=[ANY] * len(side.out_shapes),
                          out_shape=list(side.out_shapes), scratch_shapes=list(side.sems))(*side.ins)


def _sibling_side(arrs):
    n_arr = len(arrs)

    def copies(g_refs, out_refs, send_sems, recv_sems):
        mx, my, mc = _coords()
        return [pltpu.make_async_remote_copy(
            src_ref=g_ref.at[n, 1 - mc], dst_ref=out_ref.at[n], send_sem=send_sems.at[N_CHIPS * a + n],
            recv_sem=recv_sems.at[N_CHIPS * a + n], device_id=(mx, my, 1 - mc), device_id_type=MESH)
            for a, (g_ref, out_ref) in enumerate(zip(g_refs, out_refs)) for n in range(N_CHIPS)]

    def start(ins, outs, sems):
        for cp in copies(ins, outs, *sems):
            cp.start()

    def finish(ins, outs, sems):
        cps = copies(ins, outs, *sems)
        for cp in cps:
            cp.wait_recv()
        for cp in cps:
            cp.wait_send()

    return _Side(list(arrs), [jax.ShapeDtypeStruct((N_CHIPS,) + g.shape[2:], g.dtype) for g in arrs],
                 [pltpu.SemaphoreType.DMA((N_CHIPS * n_arr,)), pltpu.SemaphoreType.DMA((N_CHIPS * n_arr,))],
                 start, finish)


def _add_own(owns, sel, others, name):
    n_arr = len(owns)
    R, W = owns[0].shape[-2:]
    tr = _pick(R, (PACK_ROW_MULT, 176, 64, 8))
    first_phase = owns[0].ndim == 4
    if first_phase:
        n = owns[0].shape[0]
        grid = (n, R // tr)
        in_specs = ([pl.BlockSpec((None, None, tr, W), lambda b, i, sel: (b, sel[0], i, 0))] * n_arr
                    + [pl.BlockSpec((None, tr, W), lambda b, i, sel: (b, i, 0))] * n_arr)
        out_specs = [pl.BlockSpec((None, tr, W), lambda b, i, sel: (b, i, 0))] * (2 * n_arr)
        out_shape = [jax.ShapeDtypeStruct((n, R, W), F32)] * n_arr + [jax.ShapeDtypeStruct((n, R, W), BF16)] * n_arr

        def kern(sel_ref, *refs):
            for a in range(n_arr):
                acc = refs[a][...] + refs[n_arr + a][...].astype(F32)
                refs[2 * n_arr + a][...] = acc
                refs[3 * n_arr + a][...] = acc.astype(BF16)
    else:
        k = others[0].shape[0]
        grid = (1, R // tr)
        in_specs = ([pl.BlockSpec((None, tr, W), lambda b, i, sel: (sel[0], i, 0))] * n_arr
                    + [pl.BlockSpec((k, tr, W), lambda b, i, sel: (0, i, 0))] * n_arr)
        out_specs = [pl.BlockSpec((tr, W), lambda b, i, sel: (i, 0))] * n_arr
        out_shape = [jax.ShapeDtypeStruct((R, W), F32)] * n_arr

        def kern(sel_ref, *refs):
            for a in range(n_arr):
                acc = refs[a][...]
                for m in range(k):
                    acc = acc + refs[n_arr + a][m].astype(F32)
                refs[2 * n_arr + a][...] = acc

    outs = pl.pallas_call(
        kern, name=name,
        grid_spec=pltpu.PrefetchScalarGridSpec(num_scalar_prefetch=1, grid=grid, in_specs=in_specs,
                                               out_specs=out_specs),
        out_shape=out_shape, compiler_params=_params("arbitrary", "arbitrary"))(sel, *owns, *others)
    return (outs[:n_arr], outs[n_arr:]) if first_phase else outs


def _sum_lead(x, name):
    n, R, W = x.shape
    tr = _pick(R, (PACK_ROW_MULT, 8))

    def kern(x_ref, o_ref):
        acc = x_ref[0]
        for k in range(1, n):
            acc = acc + x_ref[k]
        o_ref[...] = acc

    return pl.pallas_call(
        kern, name=name, grid=(R // tr,),
        in_specs=[pl.BlockSpec((n, tr, W), lambda i: (0, i, 0))],
        out_specs=pl.BlockSpec((tr, W), lambda i: (i, 0)),
        out_shape=jax.ShapeDtypeStruct((R, W), F32), compiler_params=_params("arbitrary"))(x)


def _adamw(w, g, m, v):
    shape = w.shape
    cols = shape[-1]
    rows = w.size // cols
    tr = _pick(rows, (512, 352, 256, 128))
    if rows * cols * 4 <= (1 << 20):
        tr = rows

    def kern(w_ref, g_ref, m_ref, v_ref, d_ref, mo_ref, vo_ref):
        gv = g_ref[...]
        mn = ADAM_B1 * m_ref[...] + (1.0 - ADAM_B1) * gv
        vn = ADAM_B2 * v_ref[...] + (1.0 - ADAM_B2) * (gv * gv)
        m_hat = mn / (1.0 - ADAM_B1 ** ADAM_STEP)
        v_hat = vn / (1.0 - ADAM_B2 ** ADAM_STEP)
        d_ref[...] = -ADAM_LR * (m_hat / (jnp.sqrt(v_hat) + ADAM_EPS) + ADAM_WD * w_ref[...])
        mo_ref[...] = mn
        vo_ref[...] = vn

    spec = pl.BlockSpec((tr, cols), lambda i: (i, 0))
    outs = pl.pallas_call(
        kern, name="adamw", grid=(rows // tr,), in_specs=[spec] * 4, out_specs=[spec] * 3,
        out_shape=[jax.ShapeDtypeStruct((rows, cols), F32)] * 3, compiler_params=_params("arbitrary"),
    )(*[a.reshape(rows, cols) for a in (w, g, m, v)])
    return [o.reshape(shape) for o in outs]


def _pack(arrs, n_lead, row_mult, width, total_mult=PACK_ROW_MULT):
    parts, total = [], 0
    for n, a in enumerate(arrs):
        lead = a.shape[:n_lead]
        flat = a.reshape(lead + (-1,))
        size = flat.shape[-1]
        rows = -(-size // (width * row_mult)) * row_mult
        if n == len(arrs) - 1:
            rows += -(total + rows) % total_mult
        total += rows
        if rows * width > size:
            flat = jnp.concatenate([flat, jnp.zeros(lead + (rows * width - size,), flat.dtype)], axis=n_lead)
        parts.append(flat.reshape(lead + (rows, width)))
    return jnp.concatenate(parts, axis=n_lead)


def _unpack(pack, shapes, n_lead, row_mult):
    outs, row = [], 0
    lead = pack.shape[:n_lead]
    width = pack.shape[-1]
    for shp in shapes:
        size = math.prod(shp)
        rows = -(-size // (width * row_mult)) * row_mult
        blk = lax.slice_in_dim(pack, row, row + rows, axis=n_lead)
        outs.append(blk.reshape(lead + (-1,))[..., :size].reshape(lead + tuple(shp)))
        row += rows
    return outs


def _to_words(a):
    return lax.bitcast_convert_type(a, BF16)


def _from_words(a):
    return lax.bitcast_convert_type(a, F32)


def _pad_axis(a, axis, size):
    pads = [(0, 0)] * a.ndim
    pads[axis] = (0, size - a.shape[axis])
    return jnp.pad(a, pads)


def _dense(name, s):
    if name.endswith("w_gate") or name.endswith("w_up"):
        _, nl, d, fs = s.shape
        return s.transpose(1, 2, 0, 3).reshape(nl, d, N_DEV * fs)
    if name.endswith("w_down"):
        _, nl, fs, d = s.shape
        return s.transpose(1, 0, 2, 3).reshape(nl, N_DEV * fs, d)
    if name == "pool_w":
        _, nl, ng, r, cg = s.shape
        return s.transpose(1, 2, 0, 3, 4).reshape(nl, ng, cg, cg)
    if name == "w_dkv":
        w = s.reshape(-1, s.shape[2])
        z = lambda n: jnp.zeros((w.shape[0], n), w.dtype)
        return jnp.concatenate([w[:, :KV_RANK], z(ROPE_LANE0), w[:, KV_RANK:],
                                z(HEAD_LANES - ROPE_LANE0 - QK_ROPE)], axis=1)
    if name in ("w_uk", "w_uv"):
        return _pad_axis(s.transpose(1, 0, 2), 2, HEAD_LANES).reshape(KV_RANK, N_HEADS * HEAD_LANES)
    if name == "w_dq":
        _, nl, ds, r = s.shape
        return s.transpose(1, 0, 2, 3).reshape(nl, N_DEV * ds, r)
    if name == "w_uq":
        nl = s.shape[1]
        return _pad_axis(s.transpose(1, 2, 0, 3), 3, HEAD_LANES).reshape(nl, Q_RANK, N_HEADS * HEAD_LANES)
    if name == "w_o":
        _, nl, k, dc = s.shape
        w = s.transpose(1, 2, 0, 3).reshape(nl, N_HEADS, V_HEAD, N_DEV * dc)
        return _pad_axis(w, 2, HEAD_LANES).reshape(nl, N_HEADS * HEAD_LANES, N_DEV * dc)
    if name in ("meta_tokens", "pool_scale"):
        r, dc = s.shape[1:]
        return s.transpose(1, 0, 2).reshape(r, N_DEV * dc)
    raise ValueError(name)


def _shards(name, g):
    if name.endswith("w_gate") or name.endswith("w_up"):
        nl, d, f = g.shape
        return g.reshape(nl, d, N_DEV, f // N_DEV).transpose(2, 0, 1, 3)
    if name.endswith("w_down"):
        nl, f, d = g.shape
        return g.reshape(nl, N_DEV, f // N_DEV, d).transpose(1, 0, 2, 3)
    if name == "pool_w":
        nl, ng, cg, _ = g.shape
        return g.reshape(nl, ng, N_DEV, cg // N_DEV, cg).transpose(2, 0, 1, 3, 4)
    if name == "w_dkv":
        w = jnp.concatenate([g[:, :KV_RANK], g[:, KV_RANK + ROPE_LANE0:KV_RANK + ROPE_LANE0 + QK_ROPE]], axis=1)
        return w.reshape(N_DEV, -1, KV_RANK + QK_ROPE)
    if name in ("w_uk", "w_uv"):
        return g.reshape(KV_RANK, N_HEADS, HEAD_LANES)[:, :, :V_HEAD].transpose(1, 0, 2)
    if name == "w_dq":
        nl, d, r = g.shape
        return g.reshape(nl, N_DEV, d // N_DEV, r).transpose(1, 0, 2, 3)
    if name == "w_uq":
        nl = g.shape[0]
        return g.reshape(nl, Q_RANK, N_HEADS, HEAD_LANES)[..., :QK_NOPE + QK_ROPE].transpose(2, 0, 1, 3)
    if name == "w_o":
        nl, _, d = g.shape
        w = g.reshape(nl, N_HEADS, HEAD_LANES, d)[:, :, :V_HEAD].reshape(nl, N_HEADS * V_HEAD, N_DEV, d // N_DEV)
        return w.transpose(2, 0, 1, 3)
    if name in ("meta_tokens", "pool_scale"):
        r, d = g.shape
        return g.reshape(r, N_DEV, d // N_DEV).transpose(1, 0, 2)
    raise ValueError(name)


def _rope_tables(L):
    pos = jnp.maximum(jnp.arange(L) - FRONT_PAD, 0).astype(F32)
    inv = 1.0 / (ROPE_THETA ** (jnp.arange(0, QK_ROPE, 2, dtype=F32) / QK_ROPE))
    ang = pos[:, None] * inv[None, :]
    cos, sin = jnp.cos(ang), jnp.sin(ang)
    half = QK_ROPE // 2
    z = lambda n: jnp.zeros((L, n), F32)
    tail = z(HEAD_LANES - ROPE_LANE0 - QK_ROPE)
    return {
        "cq": jnp.concatenate([jnp.ones((L, ROPE_LANE0), F32), cos, cos, tail], axis=1),
        "ck": jnp.concatenate([z(ROPE_LANE0), cos, cos, tail], axis=1),
        "s1": jnp.concatenate([z(ROPE_LANE0), -sin, z(half), tail], axis=1),
        "s2": jnp.concatenate([z(ROPE_LANE0), z(half), sin, tail], axis=1),
    }


def kernel(x, meta_tokens, ffn1_norm, ffn1_w_gate, ffn1_w_up, ffn1_w_down, mix_norm, ffn2_norm, ffn2_w_gate, ffn2_w_up, ffn2_w_down, pool_w, pool_scale, kv_in_norm, w_dkv, kv_latent_norm, w_uk, w_uv, w_dq, q_latent_norm, w_uq, w_o, final_norm, loss_target, m_meta_tokens, m_ffn1_norm, m_ffn1_w_gate, m_ffn1_w_up, m_ffn1_w_down, m_mix_norm, m_ffn2_norm, m_ffn2_w_gate, m_ffn2_w_up, m_ffn2_w_down, m_pool_w, m_pool_scale, m_kv_in_norm, m_w_dkv, m_kv_latent_norm, m_w_uk, m_w_uv, m_w_dq, m_q_latent_norm, m_w_uq, m_w_o, m_final_norm, v_meta_tokens, v_ffn1_norm, v_ffn1_w_gate, v_ffn1_w_up, v_ffn1_w_down, v_mix_norm, v_ffn2_norm, v_ffn2_w_gate, v_ffn2_w_up, v_ffn2_w_down, v_pool_w, v_pool_scale, v_kv_in_norm, v_w_dkv, v_kv_latent_norm, v_w_uk, v_w_uv, v_w_dq, v_q_latent_norm, v_w_uq, v_w_o, v_final_norm):
    args = dict(locals())
    W = {n: args[n] for n in WEIGHTS}
    M = {n: args["m_" + n] for n in WEIGHTS}
    V = {n: args["v_" + n] for n in WEIGHTS}
    seq, D = x.shape[1], x.shape[2]
    L = SEQ_START + seq

    fs = ffn1_w_down.shape[1]

    ffns = [(l, which) for l in range(DEPTH) for which in (1, 2)]
    ffn_names = lambda which: FFN_WEIGHTS[3 * (which - 1):3 * which]

    extras = {ffns[0]: ["meta_tokens"], ffns[1]: [n for n in SMALL_SHARDED if n != "meta_tokens"]}

    def weight_piece(l, which):
        parts = [(W[n][l] if n.endswith("w_down") else W[n][l].T).astype(BF16) for n in ffn_names(which)]
        parts += [_to_words(W[n]) if n in SHARDED_F32 else W[n].astype(BF16) for n in extras.get((l, which), [])]
        return _pack(parts, 0, 16, D, 16)

    pieces = {k: weight_piece(*k) for k in ffns}
    full = {}
    P = {}

    def take_piece(k, gathered):
        full[k] = _with_own(gathered, pieces[k])
        names = extras.get(k, [])
        shapes = [(fs, D)] * 3 + [W[n].shape + ((2,) if n in SHARDED_F32 else ()) for n in names]
        for n, s in zip(names, _unpack(full[k], shapes, 1, 16)[3:]):
            P[n] = _dense(n, _from_words(s) if n in SHARDED_F32 else s)

    take_piece(ffns[0], _run_side(_gather_side(pieces[ffns[0]]), "all_gather_first")[0])
    norm3 = lambda a: a.reshape(a.shape[0], 1, a.shape[-1])
    row = lambda a: a.reshape(1, -1)
    g_ffn, g_mix = {1: norm3(ffn1_norm), 2: norm3(ffn2_norm)}, mix_norm
    ffn_entries = (0, 1, 2)

    def ffn_forward(h, l, which):
        at = ffns.index((l, which))
        nxt = ffns[at + 1] if at + 1 < len(ffns) else None
        side = None if nxt is None else _gather_side(pieces[nxt])
        outs = _ffn_fwd(h, g_ffn[which][l:l + 1], full[(l, which)], ffn_entries, fs, side)
        if nxt is not None:
            take_piece(nxt, outs[4])
        return outs[:4]

    h = jnp.concatenate([jnp.zeros((FRONT_PAD, D), F32), P["meta_tokens"], x[0]], axis=0)
    target = jnp.concatenate([jnp.zeros((SEQ_START, D), F32), loss_target[0]], axis=0)
    tabs = _rope_tables(L)
    saved = []
    kv = None
    for l in range(DEPTH):
        s = {"h1": h}
        h, s["xn1"], s["g1"], s["u1"] = ffn_forward(h, l, 1)
        s["hm"] = h
        if l < N_POOL_LAYERS:
            h = _pool_fwd(h, row(g_mix[l]), P["pool_w"][l], row(P["pool_scale"][l]), l)
        else:
            j = l - N_POOL_LAYERS
            s["q"], s["cqp"] = _q_fwd(h, tabs, row(g_mix[l]), P["w_dq"][j], row(q_latent_norm[j]), P["w_uq"][j])
            s["o"], s["lse"] = _attn_fwd(s["q"], kv["kn"], kv["kr"], kv["v"])
            h = _oproj_fwd(h, s["o"], P["w_o"][j])
        s["h2"] = h
        h, s["xn2"], s["g2"], s["u2"] = ffn_forward(h, l, 2)
        if l == N_POOL_LAYERS - 1:
            kv = {"h": h}
            kv["kn"], kv["kr"], kv["v"], kv["ckr"] = _kv_fwd(
                h, tabs, row(kv_in_norm), P["w_dkv"], row(kv_latent_norm), P["w_uk"], P["w_uv"])
        saved.append(s)
    dh, loss_row, d_final = _head(h, target, row(final_norm))
    loss = lax.psum(loss_row[0, 0], ("x", "y", "c"))

    G = {}
    stack = {n: [None] * DEPTH for n in ("ffn1_norm", "ffn1_w_gate", "ffn1_w_up", "ffn1_w_down", "mix_norm",
                                         "ffn2_norm", "ffn2_w_gate", "ffn2_w_up", "ffn2_w_down")}
    narrow = {n: [None] * DEPTH for n in FFN_WEIGHTS}
    pool_dw, pool_ds = [None] * N_POOL_LAYERS, [None] * N_POOL_LAYERS
    mla = {n: [None] * (DEPTH - N_POOL_LAYERS) for n in ("w_dq", "w_uq", "w_o", "q_latent_norm")}
    dks, dvs = [], []
    my_core = lax.axis_index("c").astype(jnp.int32).reshape(1)
    my_chip = (2 * lax.axis_index("x") + lax.axis_index("y")).astype(jnp.int32).reshape(1)
    layer_grads = {n: [None] * DEPTH for n in FFN_WEIGHTS}
    grads = {}

    by_owner = lambda g: g.reshape((N_CHIPS, 2, g.shape[0] // N_DEV) + g.shape[1:])

    def reduce_cores(arrs, others):
        return _add_own(arrs, my_core, others, "sum_cores")

    def finish_layer(l, partials, from_chips):
        for n, g in zip(FFN_WEIGHTS, _add_own(partials, my_chip, from_chips, "sum_chips")):
            layer_grads[n][l] = g if n.endswith("w_down") else g.T

    def ffn_backward(dh, s, which, l, side=None):
        outs = _ffn_bwd(dh, s["h%d" % which], s["g%d" % which], s["u%d" % which], g_ffn[which][l:l + 1],
                        full[(l, which)], ffn_entries, fs, side)
        dh, dg, du, act, dob, dgam = outs[:6]
        xn = s["xn%d" % which]
        for name, (lhs, rhs) in zip(ffn_names(which), ((dg, xn), (du, xn), (act, dob))):
            stack[name][l], narrow[name][l] = _mm_tn(lhs, rhs, "ffn_dw")
        stack["ffn%d_norm" % which][l] = dgam
        return dh, outs[6:]

    arrs = None
    for l in reversed(range(DEPTH)):
        s = saved[l]
        if l == N_POOL_LAYERS - 1:
            dh, d_dkv, d_uk, d_uv, d_kvin, d_kvlat = _kv_bwd(
                dh, kv["h"], kv["ckr"], dks, dvs, tabs, row(kv_in_norm), P["w_dkv"], row(kv_latent_norm),
                P["w_uk"], P["w_uv"])
            G.update(w_dkv=d_dkv, w_uk=d_uk, w_uv=d_uv, kv_in_norm=d_kvin, kv_latent_norm=d_kvlat)
        if arrs is None:
            dh, _ = ffn_backward(dh, s, 2, l)
        else:
            dh, from_sibling = ffn_backward(dh, s, 2, l, _sibling_side(arrs_bf16))
            partials, partials_bf16 = reduce_cores(arrs, from_sibling)
        if l < N_POOL_LAYERS:
            dh, pool_dw[l], pool_ds[l], stack["mix_norm"][l] = _pool_bwd(
                dh, s["hm"], row(g_mix[l]), P["pool_w"][l], row(P["pool_scale"][l]))
        else:
            j = l - N_POOL_LAYERS
            do, delta_o, mla["w_o"][j] = _oproj_bwd(dh, s["o"], P["w_o"][j])
            dq, dk, dv = _attn_bwd(s["q"], kv["kn"], kv["kr"], kv["v"], do, s["lse"], delta_o)
            dks.append(dk)
            dvs.append(dv)
            dh, mla["w_dq"][j], mla["w_uq"][j], stack["mix_norm"][l], mla["q_latent_norm"][j] = _q_bwd(
                dh, s["hm"], s["cqp"], dq, tabs, row(g_mix[l]), P["w_dq"][j], row(q_latent_norm[j]), P["w_uq"][j])
        if arrs is None:
            dh, _ = ffn_backward(dh, s, 1, l)
        else:
            dh, from_chips = ffn_backward(dh, s, 1, l, _chip_side(partials_bf16))
            finish_layer(l + 1, partials, from_chips)
        arrs = [by_owner(stack[n][l]) for n in FFN_WEIGHTS]
        arrs_bf16 = [by_owner(narrow[n][l]) for n in FFN_WEIGHTS]
    grad_x = dh[SEQ_START:][None]
    for n in ("ffn1_norm", "mix_norm", "ffn2_norm"):
        G[n] = jnp.concatenate(stack[n], axis=0)
    G["pool_w"] = jnp.stack(pool_dw)
    G["pool_scale"] = jnp.concatenate(pool_ds, axis=0)
    G["w_dq"], G["w_uq"], G["w_o"] = (jnp.stack(mla[n]) for n in ("w_dq", "w_uq", "w_o"))
    G["q_latent_norm"] = jnp.concatenate(mla["q_latent_norm"], axis=0)
    G["meta_tokens"] = dh[FRONT_PAD:SEQ_START]
    G["final_norm"] = d_final

    spack = _pack([_shards(n, G[n]) for n in SMALL_SHARDED], 1, 8, D)
    spack = spack.reshape((N_CHIPS, 2) + spack.shape[1:])
    rep_shapes = [W[n].shape for n in REPLICATED]
    rpack = _pack([G[n].reshape(W[n].shape) for n in REPLICATED], 0, 8, D)
    others = _run_side(_join(_sibling_side(arrs_bf16 + [spack.astype(BF16)]), _gather_side(rpack)),
                       "sibling_exchange_last")
    others, everyones = others[:-1], _with_own(others[-1], rpack)
    partials, partials_bf16 = reduce_cores(arrs, others[:-1])
    small, small_bf16 = reduce_cores([spack], others[-1:])
    from_chips = _run_side(_chip_side(list(partials_bf16) + list(small_bf16)), "chip_exchange_last")
    finish_layer(0, partials, from_chips[:-1])
    small_mine = _add_own(small, my_chip, from_chips[-1:], "sum_chips")[0]
    grads.update(zip(SMALL_SHARDED, _unpack(small_mine, [W[n].shape for n in SMALL_SHARDED], 0, 8)))
    for n in FFN_WEIGHTS:
        grads[n] = jnp.stack(layer_grads[n])
    grads.update(zip(REPLICATED, _unpack(_sum_lead(everyones, "sum_devices"), rep_shapes, 0, 8)))

    delta, new_m, new_v = {}, {}, {}
    for n in WEIGHTS:
        delta[n], new_m[n], new_v[n] = _adamw(W[n], grads[n], M[n], V[n])
    return (loss, grad_x, *[grads[n] for n in WEIGHTS], *[delta[n] for n in WEIGHTS],
            *[new_m[n] for n in WEIGHTS], *[new_v[n] for n in WEIGHTS])
```

```python
import functools
import math

import jax
import jax.numpy as jnp
from jax import lax
from jax.experimental import pallas as pl
from jax.experimental.pallas import tpu as pltpu

F32 = jnp.float32
BF16 = jnp.bfloat16
MESH = pl.DeviceIdType.MESH

N_DEV = 8
N_CHIPS = 4
DEPTH = 4
N_POOL_LAYERS = 2
N_HEADS = 8
QK_NOPE = 64
QK_ROPE = 32
V_HEAD = 64
KV_RANK = 256
Q_RANK = 384
HEAD_LANES = 128
HEADS = "heads"
ROPE_LANE0 = QK_NOPE
BIAS_LANE = QK_NOPE + QK_ROPE
ONES_LANE = V_HEAD
LOG2E = math.log2(math.e)
N_META = 16
CHUNK_SHIFT = 6
FRONT_PAD = 112
SEQ_START = FRONT_PAD + N_META
HALO = 16
POOL_WINDOWS = (2, 4, 8, 16)
EPS = 1e-6
ROPE_THETA = 10000.0
NEG = -1e30
PACK_ROW_MULT = 256
VMEM_LIMIT = 56 * 1024 * 1024

ADAM_LR = 0.001
ADAM_B1 = 0.9
ADAM_B2 = 0.999
ADAM_EPS = 1e-08
ADAM_WD = 0.01
ADAM_STEP = 10

SHARDED = ["ffn1_w_gate", "ffn1_w_up", "ffn1_w_down", "ffn2_w_gate", "ffn2_w_up", "ffn2_w_down",
           "pool_w", "w_dkv", "w_uk", "w_uv", "w_dq", "w_uq", "w_o", "meta_tokens", "pool_scale"]
FFN_WEIGHTS = SHARDED[:6]
SMALL_SHARDED = SHARDED[6:]
SHARDED_F32 = ("meta_tokens", "pool_scale")
REPLICATED = ["ffn1_norm", "mix_norm", "ffn2_norm", "kv_in_norm", "kv_latent_norm", "q_latent_norm",
              "final_norm"]
WEIGHTS = ['meta_tokens', 'ffn1_norm', 'ffn1_w_gate', 'ffn1_w_up', 'ffn1_w_down', 'mix_norm', 'ffn2_norm',
           'ffn2_w_gate', 'ffn2_w_up', 'ffn2_w_down', 'pool_w', 'pool_scale', 'kv_in_norm', 'w_dkv',
           'kv_latent_norm', 'w_uk', 'w_uv', 'w_dq', 'q_latent_norm', 'w_uq', 'w_o', 'final_norm']


def _dot(a, b):
    return jnp.dot(a.astype(BF16), b.astype(BF16), preferred_element_type=F32)


def _dot_nt(a, b):
    return lax.dot_general(a.astype(BF16), b.astype(BF16), (((1,), (1,)), ((), ())),
                           preferred_element_type=F32)


def _dot_tn(a, b):
    return lax.dot_general(a.astype(BF16), b.astype(BF16), (((0,), (0,)), ((), ())),
                           preferred_element_type=F32)


def _sigmoid(x):
    return 1.0 / (1.0 + jnp.exp(-x))


def _rms(x, g):
    r = lax.rsqrt(jnp.mean(x * x, axis=-1, keepdims=True) + EPS)
    xh = x * r
    return xh * g, xh, r


def _rms_bwd(dy, xh, r, g):
    dxh = dy * g
    dx = r * (dxh - xh * jnp.mean(dxh * xh, axis=-1, keepdims=True))
    return dx, jnp.sum(dy * xh, axis=0, keepdims=True)


def _rope(x, c, s1, s2):
    return x * c + pltpu.roll(x, HEAD_LANES - QK_ROPE // 2, 1) * s1 + pltpu.roll(x, QK_ROPE // 2, 1) * s2


def _rope_t(d, c, s1, s2):
    return d * c + pltpu.roll(d * s1, QK_ROPE // 2, 1) + pltpu.roll(d * s2, HEAD_LANES - QK_ROPE // 2, 1)


def _params(*sem):
    return pltpu.CompilerParams(dimension_semantics=sem, vmem_limit_bytes=VMEM_LIMIT)


def _pick(n, candidates):
    for c in candidates:
        if n % c == 0:
            return c
    return n


def _row_tile(L):
    return _pick(L, (640, 128))


FF_OWNERS = 4


def _row_call(name, body, L, row_ins, full_ins, row_outs, acc_outs):
    tm = _row_tile(L)
    n = L // tm
    hb = tm // HALO
    nb = L // HALO
    per_head = (N_HEADS, tm, HEAD_LANES)
    in_specs, args = [], []
    for arr, kind in row_ins:
        c = arr.shape[-1]
        if kind == "tile":
            spec = pl.BlockSpec((tm, c), lambda i: (i, 0))
        elif kind == "heads":
            spec = pl.BlockSpec(per_head, lambda i: (0, i, 0))
        elif kind == "prev":
            spec = pl.BlockSpec((HALO, c), lambda i: (jnp.maximum(i * hb - 1, 0), 0))
        else:
            spec = pl.BlockSpec((HALO, c), lambda i: (jnp.minimum((i + 1) * hb, nb - 1), 0))
        in_specs.append(spec)
        args.append(arr)
    for arr in full_ins:
        in_specs.append(pl.BlockSpec(arr.shape, lambda i, nd=arr.ndim: (0,) * nd))
        args.append(arr)
    out_shape, out_specs = [], []
    for c, dt in row_outs:
        if c == HEADS:
            out_shape.append(jax.ShapeDtypeStruct((N_HEADS, L, HEAD_LANES), dt))
            out_specs.append(pl.BlockSpec(per_head, lambda i: (0, i, 0)))
        else:
            out_shape.append(jax.ShapeDtypeStruct((L, c), dt))
            out_specs.append(pl.BlockSpec((tm, c), lambda i: (i, 0)))
    for shp in acc_outs:
        out_shape.append(jax.ShapeDtypeStruct(shp, F32))
        out_specs.append(pl.BlockSpec(shp, lambda i, nd=len(shp): (0,) * nd))
    n_in, n_ro = len(args), len(row_outs)

    def kern(*refs):
        i = pl.program_id(0)
        vals = [jnp.concatenate([r[hd] for hd in range(N_HEADS)], axis=1) if kind == "heads" else r[...]
                for r, (_, kind) in zip(refs, row_ins)] + [r[...] for r in refs[len(row_ins):n_in]]
        ro, ao = body(i, n, tm, *vals)
        for r, v in zip(refs[n_in:n_in + n_ro], ro):
            if len(r.shape) == 3:
                for hd in range(N_HEADS):
                    r[hd] = v[:, hd * HEAD_LANES:(hd + 1) * HEAD_LANES].astype(r.dtype)
            else:
                r[...] = v.astype(r.dtype)
        acc_refs = refs[n_in + n_ro:]

        @pl.when(i == 0)
        def _():
            for r in acc_refs:
                r[...] = jnp.zeros(r.shape, r.dtype)

        for r, v in zip(acc_refs, ao):
            r[...] += v

    return pl.pallas_call(kern, name=name, grid=(n,), in_specs=in_specs, out_specs=out_specs,
                          out_shape=out_shape, compiler_params=_params("arbitrary"))(*args)


class _Side:
    def __init__(self, ins, out_shapes, sems, start, finish, onward=None):
        self.ins, self.out_shapes, self.sems, self.start, self.finish = ins, out_shapes, sems, start, finish
        self.onward = onward


ANY = pl.BlockSpec(memory_space=pl.ANY)


def _hosted(kern, n_in, n_out, n_scratch, side, is_first, is_late, is_last):
    if side is None:
        return kern
    ns_in, ns_out = len(side.ins), len(side.out_shapes)

    def wrapped(*refs):
        ins, refs = refs[:n_in], refs[n_in:]
        side_ins, refs = refs[:ns_in], refs[ns_in:]
        outs, refs = refs[:n_out], refs[n_out:]
        side_outs, refs = refs[:ns_out], refs[ns_out:]
        scratch, side_sems = refs[:n_scratch], refs[n_scratch:]

        @pl.when(is_first())
        def _():
            side.start(side_ins, side_outs, side_sems)

        kern(*ins, *outs, *scratch)

        if side.onward is not None:
            @pl.when(is_late())
            def _():
                side.onward(side_ins, side_outs, side_sems)

        @pl.when(is_last())
        def _():
            side.finish(side_ins, side_outs, side_sems)

    return wrapped


def _side_args(side):
    if side is None:
        return [], [], [], [], []
    return ([ANY] * len(side.ins), [ANY] * len(side.out_shapes), list(side.out_shapes), list(side.sems),
            list(side.ins))


def _ffn_weight_specs(fs, D, ents):
    return [pl.BlockSpec((FF_OWNERS, fs, D), lambda i, f, e=e: (f, e, 0)) for e in ents]


def _ffn_fwd(h, gam, wpiece, ents, fs, side=None):
    L, D = h.shape
    F = N_DEV * fs
    tm = _row_tile(L)
    tf = FF_OWNERS * fs
    nL, nF = L // tm, F // tf
    s_in, s_out, s_shape, s_sems, s_args = _side_args(side)

    def kern(h_ref, gam_ref, wg_ref, wu_ref, wd_ref, ho_ref, xn_ref, gs_ref, us_ref, acc):
        f = pl.program_id(1)

        @pl.when(f == 0)
        def _():
            xn, _, _ = _rms(h_ref[...], gam_ref[...])
            xn_ref[...] = xn.astype(BF16)
            acc[...] = jnp.zeros(acc.shape, F32)

        xnb = xn_ref[...]
        g = _dot_nt(xnb, wg_ref[...].reshape(tf, D))
        u = _dot_nt(xnb, wu_ref[...].reshape(tf, D))
        gs_ref[...] = g.astype(BF16)
        us_ref[...] = u.astype(BF16)
        acc[...] += _dot(g * _sigmoid(g) * u, wd_ref[...].reshape(tf, D))

        @pl.when(f == nF - 1)
        def _():
            ho_ref[...] = h_ref[...] + 0.5 * acc[...]

    first = lambda: (pl.program_id(0) == 0) & (pl.program_id(1) == 0)
    last = lambda: (pl.program_id(0) == nL - 1) & (pl.program_id(1) == nF - 1)
    late = lambda: (pl.program_id(0) == (2 * nL) // 3) & (pl.program_id(1) == 0)
    return pl.pallas_call(
        _hosted(kern, 5, 4, 1, side, first, late, last),
        name="ffn_fwd" if side is None else "ffn_fwd_hosting", grid=(nL, nF),
        in_specs=[pl.BlockSpec((tm, D), lambda i, f: (i, 0)),
                  pl.BlockSpec((None, 1, D), lambda i, f: (0, 0, 0))] + _ffn_weight_specs(fs, D, ents) + s_in,
        out_specs=[pl.BlockSpec((tm, D), lambda i, f: (i, 0)),
                   pl.BlockSpec((tm, D), lambda i, f: (i, 0)),
                   pl.BlockSpec((tm, tf), lambda i, f: (i, f)),
                   pl.BlockSpec((tm, tf), lambda i, f: (i, f))] + s_out,
        out_shape=[jax.ShapeDtypeStruct((L, D), F32), jax.ShapeDtypeStruct((L, D), BF16),
                   jax.ShapeDtypeStruct((L, F), BF16), jax.ShapeDtypeStruct((L, F), BF16)] + s_shape,
        scratch_shapes=[pltpu.VMEM((tm, D), F32)] + s_sems,
        compiler_params=_params("arbitrary", "arbitrary"))(h, gam, wpiece, wpiece, wpiece, *s_args)


def _ffn_bwd(dh, h, gs, us, gam, wpiece, ents, fs, side=None):
    L, D = h.shape
    F = N_DEV * fs
    tm = _pick(L, (416, 128))
    tf = FF_OWNERS * fs
    nL, nF = L // tm, F // tf
    s_in, s_out, s_shape, s_sems, s_args = _side_args(side)

    def kern(dh_ref, h_ref, gs_ref, us_ref, gam_ref, wg_ref, wu_ref, wd_ref,
             dhi_ref, dg_ref, du_ref, a_ref, dob_ref, dgam_ref, dxn):
        i = pl.program_id(0)
        f = pl.program_id(1)

        @pl.when(f == 0)
        def _():
            dxn[...] = jnp.zeros(dxn.shape, F32)
            dob_ref[...] = (0.5 * dh_ref[...]).astype(BF16)

        @pl.when((f == 0) & (i == 0))
        def _():
            dgam_ref[...] = jnp.zeros(dgam_ref.shape, F32)

        g = gs_ref[...].astype(F32)
        u = us_ref[...].astype(F32)
        sg = _sigmoid(g)
        silu = g * sg
        da = _dot_nt(dob_ref[...], wd_ref[...].reshape(tf, D))
        a_ref[...] = (silu * u).astype(BF16)
        dgt = (da * u * (sg * (1.0 + g * (1.0 - sg)))).astype(BF16)
        dut = (da * silu).astype(BF16)
        dg_ref[...] = dgt
        du_ref[...] = dut
        dxn[...] += _dot(dgt, wg_ref[...].reshape(tf, D)) + _dot(dut, wu_ref[...].reshape(tf, D))

        @pl.when(f == nF - 1)
        def _():
            gamma = gam_ref[...]
            _, xh, r = _rms(h_ref[...], gamma)
            dx, dgam = _rms_bwd(dxn[...], xh, r, gamma)
            dhi_ref[...] = dh_ref[...] + dx
            dgam_ref[...] += dgam

    first = lambda: (pl.program_id(0) == 0) & (pl.program_id(1) == 0)
    last = lambda: (pl.program_id(0) == nL - 1) & (pl.program_id(1) == nF - 1)
    late = lambda: (pl.program_id(0) == (2 * nL) // 3) & (pl.program_id(1) == 0)
    return pl.pallas_call(
        _hosted(kern, 8, 6, 1, side, first, late, last),
        name="ffn_bwd" if side is None else "ffn_bwd_hosting", grid=(nL, nF),
        in_specs=[pl.BlockSpec((tm, D), lambda i, f: (i, 0)),
                  pl.BlockSpec((tm, D), lambda i, f: (i, 0)),
                  pl.BlockSpec((tm, tf), lambda i, f: (i, f)),
                  pl.BlockSpec((tm, tf), lambda i, f: (i, f)),
                  pl.BlockSpec((None, 1, D), lambda i, f: (0, 0, 0))] + _ffn_weight_specs(fs, D, ents) + s_in,
        out_specs=[pl.BlockSpec((tm, D), lambda i, f: (i, 0)),
                   pl.BlockSpec((tm, tf), lambda i, f: (i, f)),
                   pl.BlockSpec((tm, tf), lambda i, f: (i, f)),
                   pl.BlockSpec((tm, tf), lambda i, f: (i, f)),
                   pl.BlockSpec((tm, D), lambda i, f: (i, 0)),
                   pl.BlockSpec((1, D), lambda i, f: (0, 0))] + s_out,
        out_shape=[jax.ShapeDtypeStruct((L, D), F32), jax.ShapeDtypeStruct((L, F), BF16),
                   jax.ShapeDtypeStruct((L, F), BF16), jax.ShapeDtypeStruct((L, F), BF16),
                   jax.ShapeDtypeStruct((L, D), BF16), jax.ShapeDtypeStruct((1, D), F32)] + s_shape,
        scratch_shapes=[pltpu.VMEM((tm, D), F32)] + s_sems,
        compiler_params=_params("arbitrary", "arbitrary"))(dh, h, gs, us, gam, wpiece, wpiece, wpiece, *s_args)


def _mm_tn(a, b, name):
    L, M = a.shape
    N = b.shape[1]
    tm = _pick(M, (1408, 1024, 512))
    tn = _pick(N, (1408, 1024, 512))
    tk = _pick(L, (2080, 640, 128))
    nk = L // tk

    def kern(a_ref, b_ref, o_ref, ob_ref):
        @pl.when(pl.program_id(2) == 0)
        def _():
            o_ref[...] = jnp.zeros(o_ref.shape, F32)

        o_ref[...] += _dot_tn(a_ref[...], b_ref[...])

        @pl.when(pl.program_id(2) == nk - 1)
        def _():
            ob_ref[...] = o_ref[...].astype(BF16)

    return pl.pallas_call(
        kern, name=name, grid=(M // tm, N // tn, nk),
        in_specs=[pl.BlockSpec((tk, tm), lambda i, j, k: (k, i)),
                  pl.BlockSpec((tk, tn), lambda i, j, k: (k, j))],
        out_specs=[pl.BlockSpec((tm, tn), lambda i, j, k: (i, j))] * 2,
        out_shape=[jax.ShapeDtypeStruct((M, N), F32), jax.ShapeDtypeStruct((M, N), BF16)],
        compiler_params=_params("arbitrary", "arbitrary", "arbitrary"))(a, b)


def _mm_tn_pair(a1, a2, b, name):
    L, M = a1.shape
    N = b.shape[1]
    tm = _pick(M, (1408, 1024, 512))
    tk = _pick(L, (640, 128))
    nk = L // tk

    def kern(a1_ref, a2_ref, b_ref, o1_ref, o2_ref, ob1_ref, ob2_ref):
        @pl.when(pl.program_id(1) == 0)
        def _():
            o1_ref[...] = jnp.zeros(o1_ref.shape, F32)
            o2_ref[...] = jnp.zeros(o2_ref.shape, F32)

        bv = b_ref[...]
        o1_ref[...] += _dot_tn(a1_ref[...], bv)
        o2_ref[...] += _dot_tn(a2_ref[...], bv)

        @pl.when(pl.program_id(1) == nk - 1)
        def _():
            ob1_ref[...] = o1_ref[...].astype(BF16)
            ob2_ref[...] = o2_ref[...].astype(BF16)

    a_spec = pl.BlockSpec((tk, tm), lambda i, k: (k, i))
    o_spec = pl.BlockSpec((tm, N), lambda i, k: (i, 0))
    return pl.pallas_call(
        kern, name=name, grid=(M // tm, nk),
        in_specs=[a_spec, a_spec, pl.BlockSpec((tk, N), lambda i, k: (k, 0))],
        out_specs=[o_spec] * 4,
        out_shape=[jax.ShapeDtypeStruct((M, N), F32)] * 2 + [jax.ShapeDtypeStruct((M, N), BF16)] * 2,
        compiler_params=_params("arbitrary", "arbitrary"))(a1, a2, b)


def _pool_counts(pos, w):
    return jnp.clip(pos - (FRONT_PAD - 1), 1, w).astype(F32)


def _pool_forward_values(i, tm, h, hprev, gamma, D):
    cg = D // len(POOL_WINDOWS)
    hext = jnp.concatenate([hprev, h], axis=0)
    uext, xh, r = _rms(hext, gamma)
    pos = i * tm + lax.broadcasted_iota(jnp.int32, (tm, 1), 0)
    pooled = []
    for gi, w in enumerate(POOL_WINDOWS):
        s = uext[:, gi * cg:(gi + 1) * cg]
        span = 1
        while span < w:
            s = s + pltpu.roll(s, span, 0)
            span *= 2
        s = s[HALO:]
        pooled.append(s / _pool_counts(pos, w) - uext[HALO:, gi * cg:(gi + 1) * cg])
    return uext, xh[HALO:], r[HALO:], pooled


def _pool_fwd(h, gam, w, scale, l):
    L, D = h.shape
    cg = D // len(POOL_WINDOWS)

    def body(i, n, tm, ht, hprev, gamma, wv, sc):
        _, _, _, pooled = _pool_forward_values(i, tm, ht, hprev, gamma, D)
        ys = [_dot(pooled[gi], wv[gi]) for gi in range(len(POOL_WINDOWS))]
        y = jnp.concatenate(ys, axis=1) * sc
        return [ht + y], []

    del cg
    return _row_call("pool_fwd", body, L, [(h, "tile"), (h, "prev")], [gam, w, scale], [(D, F32)], [])[0]


def _pool_bwd(dy, h, gam, w, scale):
    L, D = h.shape
    ng = len(POOL_WINDOWS)
    cg = D // ng

    def body(i, n, tm, ht, hprev, dyt, dynext, gamma, wv, sc):
        _, xh, r, pooled = _pool_forward_values(i, tm, ht, hprev, gamma, D)
        dynext = jnp.where(i == n - 1, jnp.zeros_like(dynext), dynext)
        dyext = jnp.concatenate([dyt, dynext], axis=0) * sc
        pos_ext = i * tm + lax.broadcasted_iota(jnp.int32, (tm + HALO, 1), 0)
        dws, dscs, dus = [], [], []
        for gi, wd in enumerate(POOL_WINDOWS):
            cols = slice(gi * cg, (gi + 1) * cg)
            pb = pooled[gi].astype(BF16)
            ypre = _dot(pb, wv[gi])
            dscs.append(jnp.sum(dyt[:, cols] * ypre, axis=0, keepdims=True))
            dws.append(_dot_tn(pb, dyext[:tm, cols])[None])
            dp = _dot_nt(dyext[:, cols], wv[gi])
            s = dp / _pool_counts(pos_ext, wd)
            span = 1
            while span < wd:
                s = s + pltpu.roll(s, tm + HALO - span, 0)
                span *= 2
            dus.append(s[:tm] - dp[:tm])
        du = jnp.concatenate(dus, axis=1)
        pos = pos_ext[:tm]
        du = jnp.where(pos >= FRONT_PAD, du, 0.0)
        dx, dgam = _rms_bwd(du, xh, r, gamma)
        return [dyt + dx], [jnp.concatenate(dws, axis=0), jnp.concatenate(dscs, axis=1), dgam]

    return _row_call("pool_bwd", body, L, [(h, "tile"), (h, "prev"), (dy, "tile"), (dy, "next")],
                     [gam, w, scale], [(D, F32)], [(ng, cg, cg), (1, D), (1, D)])


def _kv_fwd(h, tabs, g1, wdkv, g2, wuk, wuv):
    L, D = h.shape
    hw = N_HEADS * HEAD_LANES

    def body(i, n, tm, ht, ck, s1, s2, g1v, wdkv_v, g2v, wuk_v, wuv_v):
        xkv, _, _ = _rms(ht, g1v)
        ckr = _dot(xkv, wdkv_v)
        ckv, _, _ = _rms(ckr[:, :KV_RANK], g2v)
        krope = _rope(ckr[:, KV_RANK:], ck, s1, s2)
        pos = i * tm + lax.broadcasted_iota(jnp.int32, (tm, HEAD_LANES), 0)
        lane = lax.broadcasted_iota(jnp.int32, (tm, HEAD_LANES), 1)
        krope = jnp.where((pos < FRONT_PAD) & (lane == BIAS_LANE), NEG, krope)
        ones = ((lax.broadcasted_iota(jnp.int32, (1, hw), 1) & (HEAD_LANES - 1)) == ONES_LANE).astype(F32)
        return [_dot(ckv, wuk_v), krope, _dot(ckv, wuv_v) + ones, ckr], []

    ck, s1, s2 = tabs["ck"], tabs["s1"], tabs["s2"]
    return _row_call("kv_fwd", body, L, [(h, "tile"), (ck, "tile"), (s1, "tile"), (s2, "tile")],
                     [g1, wdkv, g2, wuk, wuv],
                     [(HEADS, BF16), (HEAD_LANES, BF16), (HEADS, BF16), (KV_RANK + HEAD_LANES, F32)], [])


def _kv_bwd(dh, h, ckr, dks, dvs, tabs, g1, wdkv, g2, wuk, wuv):
    L, D = h.shape
    hw = N_HEADS * HEAD_LANES
    nl = len(dks)

    def body(i, n, tm, *vals):
        dht, ht, ckr_t = vals[:3]
        dk = sum(vals[3:3 + nl][1:], vals[3])
        dv = sum(vals[3 + nl:3 + 2 * nl][1:], vals[3 + nl])
        ck, s1, s2, g1v, wdkv_v, g2v, wuk_v, wuv_v = vals[3 + 2 * nl:]
        xkv, xh1, r1 = _rms(ht, g1v)
        ckv, xh2, r2 = _rms(ckr_t[:, :KV_RANK], g2v)
        dckv = _dot_nt(dk, wuk_v) + _dot_nt(dv, wuv_v)
        dlat, dg2 = _rms_bwd(dckv, xh2, r2, g2v)
        dkr = dk[:, :HEAD_LANES]
        for hd in range(1, N_HEADS):
            dkr = dkr + dk[:, hd * HEAD_LANES:(hd + 1) * HEAD_LANES]
        dckr = jnp.concatenate([dlat, _rope_t(dkr, ck, s1, s2)], axis=1)
        dx, dg1 = _rms_bwd(_dot_nt(dckr, wdkv_v), xh1, r1, g1v)
        return [dht + dx], [_dot_tn(xkv, dckr), _dot_tn(ckv, dk), _dot_tn(ckv, dv), dg1, dg2]

    row_ins = [(dh, "tile"), (h, "tile"), (ckr, "tile")] + [(a, "heads") for a in dks + dvs]
    row_ins += [(tabs[k], "tile") for k in ("ck", "s1", "s2")]
    return _row_call("kv_bwd", body, L, row_ins, [g1, wdkv, g2, wuk, wuv], [(D, F32)],
                     [(D, KV_RANK + HEAD_LANES), (KV_RANK, hw), (KV_RANK, hw), (1, D), (1, KV_RANK)])


def _q_fwd(h, tabs, g, wdq, gq, wuq):
    L, D = h.shape
    hw = N_HEADS * HEAD_LANES

    def body(i, n, tm, ht, cq_t, s1, s2, gv, wdq_v, gqv, wuq_v):
        u, _, _ = _rms(ht, gv)
        cqp = _dot(u, wdq_v)
        cq, _, _ = _rms(cqp, gqv)
        qp = _dot(cq, wuq_v)
        bias = (lax.broadcasted_iota(jnp.int32, (1, HEAD_LANES), 1) == BIAS_LANE).astype(F32)
        q = [_rope(qp[:, hd * HEAD_LANES:(hd + 1) * HEAD_LANES], cq_t, s1, s2) * (SM_SCALE * LOG2E) + bias
             for hd in range(N_HEADS)]
        return [jnp.concatenate(q, axis=1), cqp], []

    return _row_call("q_fwd", body, L, [(h, "tile")] + [(tabs[k], "tile") for k in ("cq", "s1", "s2")],
                     [g, wdq, gq, wuq], [(HEADS, BF16), (Q_RANK, F32)], [])


def _q_bwd(dh, h, cqp, dq, tabs, g, wdq, gq, wuq):
    L, D = h.shape
    hw = N_HEADS * HEAD_LANES

    def body(i, n, tm, dht, ht, cqp_t, dq_t, cq_t, s1, s2, gv, wdq_v, gqv, wuq_v):
        u, xh1, r1 = _rms(ht, gv)
        cq, xh2, r2 = _rms(cqp_t, gqv)
        dqp = jnp.concatenate([_rope_t(dq_t[:, hd * HEAD_LANES:(hd + 1) * HEAD_LANES], cq_t, s1, s2)
                               for hd in range(N_HEADS)], axis=1)
        dcqp, dgq = _rms_bwd(_dot_nt(dqp, wuq_v), xh2, r2, gqv)
        dx, dg = _rms_bwd(_dot_nt(dcqp, wdq_v), xh1, r1, gv)
        return [dht + dx], [_dot_tn(u, dcqp), _dot_tn(cq, dqp), dg, dgq]

    row_ins = [(dh, "tile"), (h, "tile"), (cqp, "tile"), (dq, "heads")]
    row_ins += [(tabs[k], "tile") for k in ("cq", "s1", "s2")]
    return _row_call("q_bwd", body, L, row_ins, [g, wdq, gq, wuq], [(D, F32)],
                     [(D, Q_RANK), (Q_RANK, hw), (1, D), (1, Q_RANK)])


def _oproj_fwd(h, o, wo):
    L, D = h.shape

    def body(i, n, tm, ht, ot, wov):
        return [ht + _dot(ot, wov)], []

    return _row_call("oproj_fwd", body, L, [(h, "tile"), (o, "heads")], [wo], [(D, F32)], [])[0]


def _oproj_bwd(dh, o, wo):
    L, D = dh.shape
    hw = N_HEADS * HEAD_LANES

    def body(i, n, tm, dht, ot, wov):
        do = _dot_nt(dht, wov)
        prod = do * ot.astype(F32)
        delta = [jnp.broadcast_to(jnp.sum(prod[:, hd * HEAD_LANES:(hd + 1) * HEAD_LANES], axis=-1, keepdims=True),
                                  (tm, HEAD_LANES)) for hd in range(N_HEADS)]
        return [do, jnp.concatenate(delta, axis=1)], [_dot_tn(ot, dht)]

    return _row_call("oproj_bwd", body, L, [(dh, "tile"), (o, "heads")], [wo], [(HEADS, BF16), (HEADS, F32)],
                     [(hw, D)])


def _causal(t, keys_first=False):
    qpos = lax.broadcasted_iota(jnp.int32, (t, t), 1 if keys_first else 0)
    kpos = lax.broadcasted_iota(jnp.int32, (t, t), 0 if keys_first else 1)
    return (kpos >> CHUNK_SHIFT) <= (qpos >> CHUNK_SHIFT)


SM_SCALE = 1.0 / math.sqrt(QK_NOPE + QK_ROPE)


def _pairs(n, key_major):
    if key_major:
        order = [(i, j) for j in range(n) for i in range(j, n)]
    else:
        order = [(i, j) for i in range(n) for j in range(i + 1)]
    return (jnp.array([p[0] for p in order], jnp.int32), jnp.array([p[1] for p in order], jnp.int32))


def _attn_fwd(q, kn, kr, v):
    L = q.shape[1]
    hw = N_HEADS * HEAD_LANES
    t = _row_tile(L)
    it, jt = _pairs(L // t, key_major=False)

    def kern(it_ref, jt_ref, q_ref, kn_ref, kr_ref, v_ref, o_ref, lse_ref, m_s, acc_s):
        step = pl.program_id(1)
        i, j = it_ref[step], jt_ref[step]

        @pl.when(j == 0)
        def _():
            m_s[...] = jnp.full(m_s.shape, NEG, F32)
            acc_s[...] = jnp.zeros(acc_s.shape, F32)

        def update(diagonal):
            k = kn_ref[...] + kr_ref[...]
            s = _dot_nt(q_ref[...], k)
            if diagonal:
                s = jnp.where(_causal(t), s, NEG)
            m_prev = m_s[:, :1]
            m_new = jnp.maximum(m_prev, jnp.max(s, axis=-1, keepdims=True))
            p = jnp.exp2(s - m_new)
            acc_s[...] = jnp.exp2(m_prev - m_new) * acc_s[...] + _dot(p, v_ref[...])
            m_s[...] = jnp.broadcast_to(m_new, m_s.shape)

        @pl.when(j < i)
        def _():
            update(False)

        @pl.when(j == i)
        def _():
            update(True)
            acc = acc_s[...]
            total = acc[:, ONES_LANE:ONES_LANE + 1]
            o_ref[...] = (acc / total).astype(BF16)
            lse_ref[...] = m_s[...] + jnp.log2(jnp.broadcast_to(total, m_s.shape))

    qmap = lambda h, s, it, jt: (h, it[s], 0)
    kmap = lambda h, s, it, jt: (h, jt[s], 0)
    blk = (t, HEAD_LANES)
    hblk = (None, t, HEAD_LANES)
    return pl.pallas_call(
        kern, name="attn_fwd",
        grid_spec=pltpu.PrefetchScalarGridSpec(
            num_scalar_prefetch=2, grid=(N_HEADS, it.shape[0]),
            in_specs=[pl.BlockSpec(hblk, qmap), pl.BlockSpec(hblk, kmap),
                      pl.BlockSpec(blk, lambda h, s, it, jt: (jt[s], 0)), pl.BlockSpec(hblk, kmap)],
            out_specs=[pl.BlockSpec(hblk, qmap), pl.BlockSpec(hblk, qmap)],
            scratch_shapes=[pltpu.VMEM(blk, F32)] * 2),
        out_shape=[jax.ShapeDtypeStruct((N_HEADS, L, HEAD_LANES), BF16),
                   jax.ShapeDtypeStruct((N_HEADS, L, HEAD_LANES), F32)],
        compiler_params=_params("arbitrary", "arbitrary"))(it, jt, q, kn, kr, v)


def _attn_bwd(q, kn, kr, v, do, lse, delta):
    L = q.shape[1]
    hw = N_HEADS * HEAD_LANES
    t = _row_tile(L)
    it, jt = _pairs(L // t, key_major=True)

    def kern(it_ref, jt_ref, q_ref, kn_ref, kr_ref, v_ref, do_ref, lse_ref, dl_ref, dq_ref, dk_ref, dv_ref):
        step = pl.program_id(1)
        i, j = it_ref[step], jt_ref[step]

        @pl.when(step == 0)
        def _():
            dq_ref[...] = jnp.zeros(dq_ref.shape, F32)

        @pl.when(i == j)
        def _():
            dk_ref[...] = jnp.zeros(dk_ref.shape, F32)
            dv_ref[...] = jnp.zeros(dv_ref.shape, F32)

        def update(diagonal):
            k = kn_ref[...] + kr_ref[...]
            qv, dov = q_ref[...], do_ref[...]
            s = _dot_nt(k, qv)
            if diagonal:
                s = jnp.where(_causal(t, keys_first=True), s, NEG)
            p = jnp.exp2(s - lse_ref[...])
            dp = _dot_nt(v_ref[...], dov)
            dz = (p * (dp - dl_ref[...])).astype(BF16)
            dv_ref[...] += _dot(p, dov)
            dk_ref[...] += _dot(dz, qv) * (1.0 / LOG2E)
            rows = pl.ds(pl.multiple_of(i * t, t), t)
            dq_ref[rows, :] += _dot_tn(dz, k) * SM_SCALE

        @pl.when(j < i)
        def _():
            update(False)

        @pl.when(j == i)
        def _():
            update(True)

    qmap = lambda h, s, it, jt: (h, it[s], 0)
    kmap = lambda h, s, it, jt: (h, jt[s], 0)
    rowmap = lambda h, s, it, jt: (h, 0, it[s])
    per_head_rows = lambda a: a[:, :, 0].reshape(N_HEADS, 1, L)
    blk = (t, HEAD_LANES)
    hblk = (None, t, HEAD_LANES)
    return pl.pallas_call(
        kern, name="attn_bwd",
        grid_spec=pltpu.PrefetchScalarGridSpec(
            num_scalar_prefetch=2, grid=(N_HEADS, it.shape[0]),
            in_specs=[pl.BlockSpec(hblk, qmap), pl.BlockSpec(hblk, kmap),
                      pl.BlockSpec(blk, lambda h, s, it, jt: (jt[s], 0)), pl.BlockSpec(hblk, kmap),
                      pl.BlockSpec(hblk, qmap), pl.BlockSpec((None, 1, t), rowmap), pl.BlockSpec((None, 1, t), rowmap)],
            out_specs=[pl.BlockSpec((None, L, HEAD_LANES), lambda h, s, it, jt: (h, 0, 0)),
                       pl.BlockSpec(hblk, kmap), pl.BlockSpec(hblk, kmap)]),
        out_shape=[jax.ShapeDtypeStruct((N_HEADS, L, HEAD_LANES), F32)] * 3,
        compiler_params=_params("arbitrary", "arbitrary"))(it, jt, q, kn, kr, v, do, per_head_rows(lse),
                                                            per_head_rows(delta))


def _head(h, target, g):
    L, D = h.shape

    def body(i, n, tm, ht, tt, gv):
        y, xh, r = _rms(ht, gv)
        pos = i * tm + lax.broadcasted_iota(jnp.int32, (tm, 1), 0)
        e = jnp.where(pos >= SEQ_START, y - tt, 0.0)
        loss = 0.5 * jnp.sum(jnp.mean(e * e, axis=-1, keepdims=True), axis=0, keepdims=True)
        dx, dg = _rms_bwd(e / D, xh, r, gv)
        return [dx], [jnp.broadcast_to(loss, (1, 128)), dg]

    return _row_call("loss_head", body, L, [(h, "tile"), (target, "tile")], [g], [(D, F32)], [(1, 128), (1, D)])


def _coords():
    return lax.axis_index("x"), lax.axis_index("y"), lax.axis_index("c")


def _my_index():
    mx, my, mc = _coords()
    return 4 * mx + 2 * my + mc


def _gather_side(x):
    R, W = x.shape

    def copies(x_ref, out_ref, send_sems, recv_sems):
        mx, my, mc = _coords()
        me, sibling = (mx, my, mc), (mx, my, 1 - mc)
        chips = [(1 - mx, my), (mx, 1 - my), (1 - mx, 1 - my)]

        def slot(px, py, pc):
            return out_ref.at[4 * px + 2 * py + pc]

        def copy(k, block, to, src=None):
            return pltpu.make_async_remote_copy(
                src_ref=slot(*block) if src is None else src, dst_ref=slot(*block),
                send_sem=send_sems.at[k], recv_sem=recv_sems.at[k], device_id=to, device_id_type=MESH)

        first = [copy(0, me, sibling, src=x_ref)]
        first += [copy(1 + n, me, (*chip, mc), src=x_ref) for n, chip in enumerate(chips)]
        passed = [copy(4 + n, (*chip, mc), sibling) for n, chip in enumerate(chips)]
        landed = [copy(1 + n, (*chip, mc), me) for n, chip in enumerate(chips)]
        from_sibling = [copy(0, sibling, me)] + [copy(4 + n, (*chip, 1 - mc), me) for n, chip in enumerate(chips)]
        return first, passed, landed, from_sibling

    def start(ins, outs, sems):
        for cp in copies(ins[0], outs[0], *sems)[0]:
            cp.start()

    def onward(ins, outs, sems):
        _, passed, landed, _ = copies(ins[0], outs[0], *sems)
        for arrived, on in zip(landed, passed):
            arrived.wait_recv()
            on.start()

    def finish(ins, outs, sems):
        first, passed, _, from_sibling = copies(ins[0], outs[0], *sems)
        for cp in from_sibling:
            cp.wait_recv()
        for cp in first + passed:
            cp.wait_send()

    return _Side([x], [jax.ShapeDtypeStruct((N_DEV, R, W), x.dtype)],
                 [pltpu.SemaphoreType.DMA((7,)), pltpu.SemaphoreType.DMA((7,))], start, finish, onward)


def _with_own(gathered, x):
    return lax.dynamic_update_slice(gathered, x[None], (_my_index(), 0, 0))


def _chip_side(parts):
    n_arr = len(parts)

    def copies(p_refs, out_refs, send_sems, recv_sems):
        mx, my, mc = _coords()
        chips = [(1 - mx, my), (mx, 1 - my), (1 - mx, 1 - my)]
        return [pltpu.make_async_remote_copy(
            src_ref=p_ref.at[2 * cx + cy], dst_ref=out_ref.at[n], send_sem=send_sems.at[3 * a + n],
            recv_sem=recv_sems.at[3 * a + n], device_id=(cx, cy, mc), device_id_type=MESH)
            for a, (p_ref, out_ref) in enumerate(zip(p_refs, out_refs)) for n, (cx, cy) in enumerate(chips)]

    def start(ins, outs, sems):
        for cp in copies(ins, outs, *sems):
            cp.start()

    def finish(ins, outs, sems):
        cps = copies(ins, outs, *sems)
        for cp in cps:
            cp.wait_recv()
        for cp in cps:
            cp.wait_send()

    return _Side(list(parts), [jax.ShapeDtypeStruct((3,) + p.shape[1:], p.dtype) for p in parts],
                 [pltpu.SemaphoreType.DMA((3 * n_arr,)), pltpu.SemaphoreType.DMA((3 * n_arr,))], start, finish)


def _join(a, b):
    n_in, n_out, n_sem = len(a.ins), len(a.out_shapes), len(a.sems)

    def both(f_a, f_b):
        def run(ins, outs, sems):
            if f_a is not None:
                f_a(ins[:n_in], outs[:n_out], sems[:n_sem])
            if f_b is not None:
                f_b(ins[n_in:], outs[n_out:], sems[n_sem:])
        return run

    onward = both(a.onward, b.onward) if (a.onward is not None or b.onward is not None) else None
    return _Side(a.ins + b.ins, a.out_shapes + b.out_shapes, a.sems + b.sems, both(a.start, b.start),
                 both(a.finish, b.finish), onward)


def _run_side(side, name):
    def kern(*refs):
        n_in, n_out = len(side.ins), len(side.out_shapes)
        ins, outs, sems = refs[:n_in], refs[n_in:n_in + n_out], refs[n_in + n_out:]
        side.start(ins, outs, sems)
        if side.onward is not None:
            side.onward(ins, outs, sems)
        side.finish(ins, outs, sems)

    return pl.pallas_call(kern, name=name, in_specs=[ANY] * len(side.ins), out_specs=[ANY] * len(side.out_shapes),
                          out_shape=list(side.out_shapes), scratch_shapes=list(side.sems))(*side.ins)


def _sibling_side(arrs):
    n_arr = len(arrs)

    def copies(g_refs, out_refs, send_sems, recv_sems):
        mx, my, mc = _coords()
        return [pltpu.make_async_remote_copy(
            src_ref=g_ref.at[n, 1 - mc], dst_ref=out_ref.at[n], send_sem=send_sems.at[N_CHIPS * a + n],
            recv_sem=recv_sems.at[N_CHIPS * a + n], device_id=(mx, my, 1 - mc), device_id_type=MESH)
            for a, (g_ref, out_ref) in enumerate(zip(g_refs, out_refs)) for n in range(N_CHIPS)]

    def start(ins, outs, sems):
        for cp in copies(ins, outs, *sems):
            cp.start()

    def finish(ins, outs, sems):
        cps = copies(ins, outs, *sems)
        for cp in cps:
            cp.wait_recv()
        for cp in cps:
            cp.wait_send()

    return _Side(list(arrs), [jax.ShapeDtypeStruct((N_CHIPS,) + g.shape[2:], g.dtype) for g in arrs],
                 [pltpu.SemaphoreType.DMA((N_CHIPS * n_arr,)), pltpu.SemaphoreType.DMA((N_CHIPS * n_arr,))],
                 start, finish)


def _add_own(owns, sel, others, name):
    n_arr = len(owns)
    R, W = owns[0].shape[-2:]
    tr = _pick(R, (PACK_ROW_MULT, 176, 64, 8))
    first_phase = owns[0].ndim == 4
    if first_phase:
        n = owns[0].shape[0]
        grid = (n, R // tr)
        in_specs = ([pl.BlockSpec((None, None, tr, W), lambda b, i, sel: (b, sel[0], i, 0))] * n_arr
                    + [pl.BlockSpec((None, tr, W), lambda b, i, sel: (b, i, 0))] * n_arr)
        out_specs = [pl.BlockSpec((None, tr, W), lambda b, i, sel: (b, i, 0))] * (2 * n_arr)
        out_shape = [jax.ShapeDtypeStruct((n, R, W), F32)] * n_arr + [jax.ShapeDtypeStruct((n, R, W), BF16)] * n_arr

        def kern(sel_ref, *refs):
            for a in range(n_arr):
                acc = refs[a][...] + refs[n_arr + a][...].astype(F32)
                refs[2 * n_arr + a][...] = acc
                refs[3 * n_arr + a][...] = acc.astype(BF16)
    else:
        k = others[0].shape[0]
        grid = (1, R // tr)
        in_specs = ([pl.BlockSpec((None, tr, W), lambda b, i, sel: (sel[0], i, 0))] * n_arr
                    + [pl.BlockSpec((k, tr, W), lambda b, i, sel: (0, i, 0))] * n_arr)
        out_specs = [pl.BlockSpec((tr, W), lambda b, i, sel: (i, 0))] * n_arr
        out_shape = [jax.ShapeDtypeStruct((R, W), F32)] * n_arr

        def kern(sel_ref, *refs):
            for a in range(n_arr):
                acc = refs[a][...]
                for m in range(k):
                    acc = acc + refs[n_arr + a][m].astype(F32)
                refs[2 * n_arr + a][...] = acc

    outs = pl.pallas_call(
        kern, name=name,
        grid_spec=pltpu.PrefetchScalarGridSpec(num_scalar_prefetch=1, grid=grid, in_specs=in_specs,
                                               out_specs=out_specs),
        out_shape=out_shape, compiler_params=_params("arbitrary", "arbitrary"))(sel, *owns, *others)
    return (outs[:n_arr], outs[n_arr:]) if first_phase else outs


def _sum_lead(x, name):
    n, R, W = x.shape
    tr = _pick(R, (PACK_ROW_MULT, 8))

    def kern(x_ref, o_ref):
        acc = x_ref[0]
        for k in range(1, n):
            acc = acc + x_ref[k]
        o_ref[...] = acc

    return pl.pallas_call(
        kern, name=name, grid=(R // tr,),
        in_specs=[pl.BlockSpec((n, tr, W), lambda i: (0, i, 0))],
        out_specs=pl.BlockSpec((tr, W), lambda i: (i, 0)),
        out_shape=jax.ShapeDtypeStruct((R, W), F32), compiler_params=_params("arbitrary"))(x)


def _adamw(w, g, m, v):
    shape = w.shape
    cols = shape[-1]
    rows = w.size // cols
    tr = _pick(rows, (512, 352, 256, 128))
    if rows * cols * 4 <= (1 << 20):
        tr = rows

    def kern(w_ref, g_ref, m_ref, v_ref, d_ref, mo_ref, vo_ref):
        gv = g_ref[...]
        mn = ADAM_B1 * m_ref[...] + (1.0 - ADAM_B1) * gv
        vn = ADAM_B2 * v_ref[...] + (1.0 - ADAM_B2) * (gv * gv)
        m_hat = mn / (1.0 - ADAM_B1 ** ADAM_STEP)
        v_hat = vn / (1.0 - ADAM_B2 ** ADAM_STEP)
        d_ref[...] = -ADAM_LR * (m_hat / (jnp.sqrt(v_hat) + ADAM_EPS) + ADAM_WD * w_ref[...])
        mo_ref[...] = mn
        vo_ref[...] = vn

    spec = pl.BlockSpec((tr, cols), lambda i: (i, 0))
    outs = pl.pallas_call(
        kern, name="adamw", grid=(rows // tr,), in_specs=[spec] * 4, out_specs=[spec] * 3,
        out_shape=[jax.ShapeDtypeStruct((rows, cols), F32)] * 3, compiler_params=_params("arbitrary"),
    )(*[a.reshape(rows, cols) for a in (w, g, m, v)])
    return [o.reshape(shape) for o in outs]


def _pack(arrs, n_lead, row_mult, width, total_mult=PACK_ROW_MULT):
    parts, total = [], 0
    for n, a in enumerate(arrs):
        lead = a.shape[:n_lead]
        flat = a.reshape(lead + (-1,))
        size = flat.shape[-1]
        rows = -(-size // (width * row_mult)) * row_mult
        if n == len(arrs) - 1:
            rows += -(total + rows) % total_mult
        total += rows
        if rows * width > size:
            flat = jnp.concatenate([flat, jnp.zeros(lead + (rows * width - size,), flat.dtype)], axis=n_lead)
        parts.append(flat.reshape(lead + (rows, width)))
    return jnp.concatenate(parts, axis=n_lead)


def _unpack(pack, shapes, n_lead, row_mult):
    outs, row = [], 0
    lead = pack.shape[:n_lead]
    width = pack.shape[-1]
    for shp in shapes:
        size = math.prod(shp)
        rows = -(-size // (width * row_mult)) * row_mult
        blk = lax.slice_in_dim(pack, row, row + rows, axis=n_lead)
        outs.append(blk.reshape(lead + (-1,))[..., :size].reshape(lead + tuple(shp)))
        row += rows
    return outs


def _to_words(a):
    return lax.bitcast_convert_type(a, BF16)


def _from_words(a):
    return lax.bitcast_convert_type(a, F32)


def _pad_axis(a, axis, size):
    pads = [(0, 0)] * a.ndim
    pads[axis] = (0, size - a.shape[axis])
    return jnp.pad(a, pads)


def _dense(name, s):
    if name.endswith("w_gate") or name.endswith("w_up"):
        _, nl, d, fs = s.shape
        return s.transpose(1, 2, 0, 3).reshape(nl, d, N_DEV * fs)
    if name.endswith("w_down"):
        _, nl, fs, d = s.shape
        return s.transpose(1, 0, 2, 3).reshape(nl, N_DEV * fs, d)
    if name == "pool_w":
        _, nl, ng, r, cg = s.shape
        return s.transpose(1, 2, 0, 3, 4).reshape(nl, ng, cg, cg)
    if name == "w_dkv":
        w = s.reshape(-1, s.shape[2])
        z = lambda n: jnp.zeros((w.shape[0], n), w.dtype)
        return jnp.concatenate([w[:, :KV_RANK], z(ROPE_LANE0), w[:, KV_RANK:],
                                z(HEAD_LANES - ROPE_LANE0 - QK_ROPE)], axis=1)
    if name in ("w_uk", "w_uv"):
        return _pad_axis(s.transpose(1, 0, 2), 2, HEAD_LANES).reshape(KV_RANK, N_HEADS * HEAD_LANES)
    if name == "w_dq":
        _, nl, ds, r = s.shape
        return s.transpose(1, 0, 2, 3).reshape(nl, N_DEV * ds, r)
    if name == "w_uq":
        nl = s.shape[1]
        return _pad_axis(s.transpose(1, 2, 0, 3), 3, HEAD_LANES).reshape(nl, Q_RANK, N_HEADS * HEAD_LANES)
    if name == "w_o":
        _, nl, k, dc = s.shape
        w = s.transpose(1, 2, 0, 3).reshape(nl, N_HEADS, V_HEAD, N_DEV * dc)
        return _pad_axis(w, 2, HEAD_LANES).reshape(nl, N_HEADS * HEAD_LANES, N_DEV * dc)
    if name in ("meta_tokens", "pool_scale"):
        r, dc = s.shape[1:]
        return s.transpose(1, 0, 2).reshape(r, N_DEV * dc)
    raise ValueError(name)


def _shards(name, g):
    if name.endswith("w_gate") or name.endswith("w_up"):
        nl, d, f = g.shape
        return g.reshape(nl, d, N_DEV, f // N_DEV).transpose(2, 0, 1, 3)
    if name.endswith("w_down"):
        nl, f, d = g.shape
        return g.reshape(nl, N_DEV, f // N_DEV, d).transpose(1, 0, 2, 3)
    if name == "pool_w":
        nl, ng, cg, _ = g.shape
        return g.reshape(nl, ng, N_DEV, cg // N_DEV, cg).transpose(2, 0, 1, 3, 4)
    if name == "w_dkv":
        w = jnp.concatenate([g[:, :KV_RANK], g[:, KV_RANK + ROPE_LANE0:KV_RANK + ROPE_LANE0 + QK_ROPE]], axis=1)
        return w.reshape(N_DEV, -1, KV_RANK + QK_ROPE)
    if name in ("w_uk", "w_uv"):
        return g.reshape(KV_RANK, N_HEADS, HEAD_LANES)[:, :, :V_HEAD].transpose(1, 0, 2)
    if name == "w_dq":
        nl, d, r = g.shape
        return g.reshape(nl, N_DEV, d // N_DEV, r).transpose(1, 0, 2, 3)
    if name == "w_uq":
        nl = g.shape[0]
        return g.reshape(nl, Q_RANK, N_HEADS, HEAD_LANES)[..., :QK_NOPE + QK_ROPE].transpose(2, 0, 1, 3)
    if name == "w_o":
        nl, _, d = g.shape
        w = g.reshape(nl, N_HEADS, HEAD_LANES, d)[:, :, :V_HEAD].reshape(nl, N_HEADS * V_HEAD, N_DEV, d // N_DEV)
        return w.transpose(2, 0, 1, 3)
    if name in ("meta_tokens", "pool_scale"):
        r, d = g.shape
        return g.reshape(r, N_DEV, d // N_DEV).transpose(1, 0, 2)
    raise ValueError(name)


def _rope_tables(L):
    pos = jnp.maximum(jnp.arange(L) - FRONT_PAD, 0).astype(F32)
    inv = 1.0 / (ROPE_THETA ** (jnp.arange(0, QK_ROPE, 2, dtype=F32) / QK_ROPE))
    ang = pos[:, None] * inv[None, :]
    cos, sin = jnp.cos(ang), jnp.sin(ang)
    half = QK_ROPE // 2
    z = lambda n: jnp.zeros((L, n), F32)
    tail = z(HEAD_LANES - ROPE_LANE0 - QK_ROPE)
    return {
        "cq": jnp.concatenate([jnp.ones((L, ROPE_LANE0), F32), cos, cos, tail], axis=1),
        "ck": jnp.concatenate([z(ROPE_LANE0), cos, cos, tail], axis=1),
        "s1": jnp.concatenate([z(ROPE_LANE0), -sin, z(half), tail], axis=1),
        "s2": jnp.concatenate([z(ROPE_LANE0), z(half), sin, tail], axis=1),
    }


def kernel(x, meta_tokens, ffn1_norm, ffn1_w_gate, ffn1_w_up, ffn1_w_down, mix_norm, ffn2_norm, ffn2_w_gate, ffn2_w_up, ffn2_w_down, pool_w, pool_scale, kv_in_norm, w_dkv, kv_latent_norm, w_uk, w_uv, w_dq, q_latent_norm, w_uq, w_o, final_norm, loss_target, m_meta_tokens, m_ffn1_norm, m_ffn1_w_gate, m_ffn1_w_up, m_ffn1_w_down, m_mix_norm, m_ffn2_norm, m_ffn2_w_gate, m_ffn2_w_up, m_ffn2_w_down, m_pool_w, m_pool_scale, m_kv_in_norm, m_w_dkv, m_kv_latent_norm, m_w_uk, m_w_uv, m_w_dq, m_q_latent_norm, m_w_uq, m_w_o, m_final_norm, v_meta_tokens, v_ffn1_norm, v_ffn1_w_gate, v_ffn1_w_up, v_ffn1_w_down, v_mix_norm, v_ffn2_norm, v_ffn2_w_gate, v_ffn2_w_up, v_ffn2_w_down, v_pool_w, v_pool_scale, v_kv_in_norm, v_w_dkv, v_kv_latent_norm, v_w_uk, v_w_uv, v_w_dq, v_q_latent_norm, v_w_uq, v_w_o, v_final_norm):
    args = dict(locals())
    W = {n: args[n] for n in WEIGHTS}
    M = {n: args["m_" + n] for n in WEIGHTS}
    V = {n: args["v_" + n] for n in WEIGHTS}
    seq, D = x.shape[1], x.shape[2]
    L = SEQ_START + seq

    fs = ffn1_w_down.shape[1]

    ffns = [(l, which) for l in range(DEPTH) for which in (1, 2)]
    ffn_names = lambda which: FFN_WEIGHTS[3 * (which - 1):3 * which]

    extras = {ffns[0]: ["meta_tokens"], ffns[1]: [n for n in SMALL_SHARDED if n != "meta_tokens"]}

    def weight_piece(l, which):
        parts = [(W[n][l] if n.endswith("w_down") else W[n][l].T).astype(BF16) for n in ffn_names(which)]
        parts += [_to_words(W[n]) if n in SHARDED_F32 else W[n].astype(BF16) for n in extras.get((l, which), [])]
        return _pack(parts, 0, 16, D, 16)

    pieces = {k: weight_piece(*k) for k in ffns}
    full = {}
    P = {}

    def take_piece(k, gathered):
        full[k] = _with_own(gathered, pieces[k])
        names = extras.get(k, [])
        shapes = [(fs, D)] * 3 + [W[n].shape + ((2,) if n in SHARDED_F32 else ()) for n in names]
        for n, s in zip(names, _unpack(full[k], shapes, 1, 16)[3:]):
            P[n] = _dense(n, _from_words(s) if n in SHARDED_F32 else s)

    take_piece(ffns[0], _run_side(_gather_side(pieces[ffns[0]]), "all_gather_first")[0])
    norm3 = lambda a: a.reshape(a.shape[0], 1, a.shape[-1])
    row = lambda a: a.reshape(1, -1)
    g_ffn, g_mix = {1: norm3(ffn1_norm), 2: norm3(ffn2_norm)}, mix_norm
    ffn_entries = (0, 1, 2)

    def ffn_forward(h, l, which):
        at = ffns.index((l, which))
        nxt = ffns[at + 1] if at + 1 < len(ffns) else None
        side = None if nxt is None else _gather_side(pieces[nxt])
        outs = _ffn_fwd(h, g_ffn[which][l:l + 1], full[(l, which)], ffn_entries, fs, side)
        if nxt is not None:
            take_piece(nxt, outs[4])
        return outs[:4]

    h = jnp.concatenate([jnp.zeros((FRONT_PAD, D), F32), P["meta_tokens"], x[0]], axis=0)
    target = jnp.concatenate([jnp.zeros((SEQ_START, D), F32), loss_target[0]], axis=0)
    tabs = _rope_tables(L)
    saved = []
    kv = None
    for l in range(DEPTH):
        s = {"h1": h}
        h, s["xn1"], s["g1"], s["u1"] = ffn_forward(h, l, 1)
        s["hm"] = h
        if l < N_POOL_LAYERS:
            h = _pool_fwd(h, row(g_mix[l]), P["pool_w"][l], row(P["pool_scale"][l]), l)
        else:
            j = l - N_POOL_LAYERS
            s["q"], s["cqp"] = _q_fwd(h, tabs, row(g_mix[l]), P["w_dq"][j], row(q_latent_norm[j]), P["w_uq"][j])
            s["o"], s["lse"] = _attn_fwd(s["q"], kv["kn"], kv["kr"], kv["v"])
            h = _oproj_fwd(h, s["o"], P["w_o"][j])
        s["h2"] = h
        h, s["xn2"], s["g2"], s["u2"] = ffn_forward(h, l, 2)
        if l == N_POOL_LAYERS - 1:
            kv = {"h": h}
            kv["kn"], kv["kr"], kv["v"], kv["ckr"] = _kv_fwd(
                h, tabs, row(kv_in_norm), P["w_dkv"], row(kv_latent_norm), P["w_uk"], P["w_uv"])
        saved.append(s)
    dh, loss_row, d_final = _head(h, target, row(final_norm))
    loss = lax.psum(loss_row[0, 0], ("x", "y", "c"))

    G = {}
    stack = {n: [None] * DEPTH for n in ("ffn1_norm", "ffn1_w_gate", "ffn1_w_up", "ffn1_w_down", "mix_norm",
                                         "ffn2_norm", "ffn2_w_gate", "ffn2_w_up", "ffn2_w_down")}
    narrow = {n: [None] * DEPTH for n in FFN_WEIGHTS}
    pool_dw, pool_ds = [None] * N_POOL_LAYERS, [None] * N_POOL_LAYERS
    mla = {n: [None] * (DEPTH - N_POOL_LAYERS) for n in ("w_dq", "w_uq", "w_o", "q_latent_norm")}
    dks, dvs = [], []
    my_core = lax.axis_index("c").astype(jnp.int32).reshape(1)
    my_chip = (2 * lax.axis_index("x") + lax.axis_index("y")).astype(jnp.int32).reshape(1)
    layer_grads = {n: [None] * DEPTH for n in FFN_WEIGHTS}
    grads = {}

    by_owner = lambda g: g.reshape((N_CHIPS, 2, g.shape[0] // N_DEV) + g.shape[1:])

    def reduce_cores(arrs, others):
        return _add_own(arrs, my_core, others, "sum_cores")

    def finish_layer(l, partials, from_chips):
        for n, g in zip(FFN_WEIGHTS, _add_own(partials, my_chip, from_chips, "sum_chips")):
            layer_grads[n][l] = g if n.endswith("w_down") else g.T

    def ffn_backward(dh, s, which, l, side=None):
        outs = _ffn_bwd(dh, s["h%d" % which], s["g%d" % which], s["u%d" % which], g_ffn[which][l:l + 1],
                        full[(l, which)], ffn_entries, fs, side)
        dh, dg, du, act, dob, dgam = outs[:6]
        xn = s["xn%d" % which]
        gate, up, down = ffn_names(which)
        stack[gate][l], stack[up][l], narrow[gate][l], narrow[up][l] = _mm_tn_pair(dg, du, xn, "ffn_dw_in")
        stack[down][l], narrow[down][l] = _mm_tn(act, dob, "ffn_dw")
        stack["ffn%d_norm" % which][l] = dgam
        return dh, outs[6:]

    arrs = None
    for l in reversed(range(DEPTH)):
        s = saved[l]
        if l == N_POOL_LAYERS - 1:
            dh, d_dkv, d_uk, d_uv, d_kvin, d_kvlat = _kv_bwd(
                dh, kv["h"], kv["ckr"], dks, dvs, tabs, row(kv_in_norm), P["w_dkv"], row(kv_latent_norm),
                P["w_uk"], P["w_uv"])
            G.update(w_dkv=d_dkv, w_uk=d_uk, w_uv=d_uv, kv_in_norm=d_kvin, kv_latent_norm=d_kvlat)
        if arrs is None:
            dh, _ = ffn_backward(dh, s, 2, l)
        else:
            dh, from_sibling = ffn_backward(dh, s, 2, l, _sibling_side(arrs_bf16))
            partials, partials_bf16 = reduce_cores(arrs, from_sibling)
        if l < N_POOL_LAYERS:
            dh, pool_dw[l], pool_ds[l], stack["mix_norm"][l] = _pool_bwd(
                dh, s["hm"], row(g_mix[l]), P["pool_w"][l], row(P["pool_scale"][l]))
        else:
            j = l - N_POOL_LAYERS
            do, delta_o, mla["w_o"][j] = _oproj_bwd(dh, s["o"], P["w_o"][j])
            dq, dk, dv = _attn_bwd(s["q"], kv["kn"], kv["kr"], kv["v"], do, s["lse"], delta_o)
            dks.append(dk)
            dvs.append(dv)
            dh, mla["w_dq"][j], mla["w_uq"][j], stack["mix_norm"][l], mla["q_latent_norm"][j] = _q_bwd(
                dh, s["hm"], s["cqp"], dq, tabs, row(g_mix[l]), P["w_dq"][j], row(q_latent_norm[j]), P["w_uq"][j])
        if arrs is None:
            dh, _ = ffn_backward(dh, s, 1, l)
        else:
            dh, from_chips = ffn_backward(dh, s, 1, l, _chip_side(partials_bf16))
            finish_layer(l + 1, partials, from_chips)
        arrs = [by_owner(stack[n][l]) for n in FFN_WEIGHTS]
        arrs_bf16 = [by_owner(narrow[n][l]) for n in FFN_WEIGHTS]
    grad_x = dh[SEQ_START:][None]
    for n in ("ffn1_norm", "mix_norm", "ffn2_norm"):
        G[n] = jnp.concatenate(stack[n], axis=0)
    G["pool_w"] = jnp.stack(pool_dw)
    G["pool_scale"] = jnp.concatenate(pool_ds, axis=0)
    G["w_dq"], G["w_uq"], G["w_o"] = (jnp.stack(mla[n]) for n in ("w_dq", "w_uq", "w_o"))
    G["q_latent_norm"] = jnp.concatenate(mla["q_latent_norm"], axis=0)
    G["meta_tokens"] = dh[FRONT_PAD:SEQ_START]
    G["final_norm"] = d_final

    spack = _pack([_shards(n, G[n]) for n in SMALL_SHARDED], 1, 8, D)
    spack = spack.reshape((N_CHIPS, 2) + spack.shape[1:])
    rep_shapes = [W[n].shape for n in REPLICATED]
    rpack = _pack([G[n].reshape(W[n].shape) for n in REPLICATED], 0, 8, D)
    others = _run_side(_join(_sibling_side(arrs_bf16 + [spack.astype(BF16)]), _gather_side(rpack)),
                       "sibling_exchange_last")
    others, everyones = others[:-1], _with_own(others[-1], rpack)
    partials, partials_bf16 = reduce_cores(arrs, others[:-1])
    small, small_bf16 = reduce_cores([spack], others[-1:])
    from_chips = _run_side(_chip_side(list(partials_bf16) + list(small_bf16)), "chip_exchange_last")
    finish_layer(0, partials, from_chips[:-1])
    small_mine = _add_own(small, my_chip, from_chips[-1:], "sum_chips")[0]
    grads.update(zip(SMALL_SHARDED, _unpack(small_mine, [W[n].shape for n in SMALL_SHARDED], 0, 8)))
    for n in FFN_WEIGHTS:
        grads[n] = jnp.stack(layer_grads[n])
    grads.update(zip(REPLICATED, _unpack(_sum_lead(everyones, "sum_devices"), rep_shapes, 0, 8)))

    delta, new_m, new_v = {}, {}, {}
    for n in WEIGHTS:
        delta[n], new_m[n], new_v[n] = _adamw(W[n], grads[n], M[n], V[n])
    return (loss, grad_x, *[grads[n] for n in WEIGHTS], *[delta[n] for n in WEIGHTS],
            *[new_m[n] for n in WEIGHTS], *[new_v[n] for n in WEIGHTS])
```
